```python
import jax
import jax.numpy as jnp
from jax import lax
import numpy as np

D_MODEL = 1024
BATCH = 2
SEQ = 8192
DEPTH = 1

GRID_W = 64
CTX_LEN = 256
N_MOD = 9
NORM_EPS = 1e-6
D_FF = 2816

NA_HEADS = 8
NA_HEAD_DIM = 64
NA_WIDTH = NA_HEADS * NA_HEAD_DIM
NA_WIN_H = 8
NA_WIN_W = 16
NA_QC = 16
NA_KC = NA_QC + NA_WIN_W

RW_HEADS = 8
RW_HEAD_DIM = 64
RW_WIDTH = RW_HEADS * RW_HEAD_DIM
RW_DECAY_LORA = 64
RW_AAA_LORA = 64
RW_GATE_LORA = 128
RW_GN_EPS = 64e-5

OFF_RW = 3 * NA_WIDTH
RW_SHIFT_COLS = 3 * RW_WIDTH + 2 * RW_DECAY_LORA + 2 * RW_AAA_LORA + RW_GATE_LORA
OFF_GATE = OFF_RW + RW_SHIFT_COLS
IN_COLS = OFF_GATE + 2 * D_MODEL
RW_SPLITS = (RW_WIDTH, 2 * RW_WIDTH, 3 * RW_WIDTH, 3 * RW_WIDTH + 2 * RW_DECAY_LORA, 3 * RW_WIDTH + 2 * RW_DECAY_LORA + 2 * RW_AAA_LORA)

kernel_name = 'hybrid_na_rwkv7_macaron_dit_layer'


def rmsnorm(x, g):
    xf = x.astype(jnp.float32)
    y = xf * lax.rsqrt(jnp.mean(xf * xf, axis=-1, keepdims=True) + NORM_EPS)
    return (y * g.astype(jnp.float32)).astype(x.dtype)


def modulated_norm(x, g, mod, i):
    return rmsnorm(x, g) * (1 + mod[i + 1]) + mod[i]


def swiglu(h, wg, wu, wd):
    return (jax.nn.silu(h @ wg) * (h @ wu)) @ wd


def split_heads(t, h):
    return t.reshape(t.shape[:-1] + (h, t.shape[-1] // h))


def token_shift(f, mu_prev, mu_next):
    prev = jnp.pad(f[:, :-1], ((0, 0), (1, 0), (0, 0)))
    nxt = jnp.pad(f[:, 1:], ((0, 0), (0, 1), (0, 0)))
    return f + mu_prev * (prev - f) + mu_next * (nxt - f)


def na_attention(q, k, v, kc, vc, rpb):
    B, T, H, d = q.shape
    rows = T // GRID_W
    kr = min(NA_WIN_H, rows)
    ncb = GRID_W // NA_QC
    qcol = np.arange(GRID_W).reshape(ncb, NA_QC)
    cb = np.clip(np.arange(ncb) * NA_QC - NA_WIN_W // 2, 0, GRID_W - NA_KC)
    kcol = cb[:, None] + np.arange(NA_KC)
    s = np.clip(qcol - NA_WIN_W // 2, 0, GRID_W - NA_WIN_W)
    valid = (kcol[:, None, :] >= s[..., None]) & (kcol[:, None, :] < s[..., None] + NA_WIN_W)
    valid_full = np.broadcast_to(valid[:, :, None, :], (ncb, NA_QC, kr, NA_KC)).reshape(ncb, NA_QC, kr * NA_KC)
    dc_idx = np.clip(kcol[:, None, :] - qcol[..., None] + NA_WIN_W - 1, 0, 2 * NA_WIN_W - 2)
    scale = d ** -0.5
    kg = k.reshape(B, rows, GRID_W, H, d).transpose(0, 3, 1, 2, 4)
    vg = v.reshape(B, rows, GRID_W, H, d).transpose(0, 3, 1, 2, 4)
    qg = q.reshape(B, rows, ncb, NA_QC, H, d).transpose(1, 0, 4, 2, 3, 5)
    kc_t = kc.transpose(0, 2, 1, 3)
    vc_t = vc.transpose(0, 2, 1, 3)
    n_loc = kr * NA_KC

    def row_fn(args):
        i, q_i = args
        r0 = jnp.clip(i - kr // 2, 0, rows - kr)
        k_r = lax.dynamic_slice_in_dim(kg, r0, kr, axis=2)
        v_r = lax.dynamic_slice_in_dim(vg, r0, kr, axis=2)
        k_b = k_r[:, :, :, kcol].transpose(0, 1, 3, 2, 4, 5).reshape(B, H, ncb, n_loc, d)
        v_b = v_r[:, :, :, kcol].transpose(0, 1, 3, 2, 4, 5).reshape(B, H, ncb, n_loc, d)
        s_loc = jnp.einsum('bhmqd,bhmkd->bhmqk', q_i, k_b).astype(jnp.float32) * scale
        dr = r0 + jnp.arange(kr) - i + (NA_WIN_H - 1)
        bias = rpb[:, dr][:, :, dc_idx]
        bias = bias.transpose(0, 2, 3, 1, 4).reshape(H, ncb, NA_QC, n_loc).astype(jnp.float32)
        s_loc = jnp.where(valid_full, s_loc + bias, -jnp.inf)
        s_ctx = jnp.einsum('bhmqd,bhcd->bhmqc', q_i, kc_t).astype(jnp.float32) * scale
        p = jax.nn.softmax(jnp.concatenate([s_loc, s_ctx], axis=-1), axis=-1).astype(v.dtype)
        return (jnp.einsum('bhmqk,bhmkd->bhmqd', p[..., :n_loc], v_b)
                + jnp.einsum('bhmqc,bhcd->bhmqd', p[..., n_loc:], vc_t))

    o = lax.map(row_fn, (jnp.arange(rows), qg))
    return o.transpose(1, 0, 3, 4, 2, 5).reshape(B, T, H * d)


def context_attention(q, k, v):
    s = jnp.einsum('bqhd,bkhd->bhqk', q, k).astype(jnp.float32) * NA_HEAD_DIM ** -0.5
    p = jax.nn.softmax(s, axis=-1).astype(v.dtype)
    o = jnp.einsum('bhqk,bkhd->bqhd', p, v)
    return o.reshape(o.shape[:2] + (NA_WIDTH,))


def rwkv_prep(f, w0, w_up, a0, a_up, g_up, k_k, k_a):
    B, T, _ = f.shape
    f32 = jnp.float32
    r, k, v, wl, al, gl = jnp.split(f, RW_SPLITS, axis=-1)
    wl = wl.reshape(B, T, 2, RW_DECAY_LORA)
    al = al.reshape(B, T, 2, RW_AAA_LORA)
    w_log = -jax.nn.softplus(-(w0 + jnp.einsum('btdr,drc->btdc', jnp.tanh(wl), w_up)).astype(f32)) - 0.5
    decay = jnp.exp(-jnp.exp(w_log))
    a = jax.nn.sigmoid((a0 + jnp.einsum('btdr,drc->btdc', al, a_up)).astype(f32))
    g = jax.nn.sigmoid(gl) @ g_up
    kk = split_heads((k * k_k).astype(f32), RW_HEADS)
    kk = kk * lax.rsqrt(jnp.maximum(jnp.sum(kk * kk, axis=-1, keepdims=True), 1e-12))
    k_dir = k.astype(f32)[:, :, None, :] * (1 + (a - 1) * k_a.astype(f32))
    return (split_heads(r.astype(f32), RW_HEADS), split_heads(v.astype(f32), RW_HEADS), kk,
            split_heads(decay, RW_HEADS), split_heads(a, RW_HEADS), split_heads(k_dir, RW_HEADS), g)


def rwkv_scan(S0, r, decay, kk, a, k, v, reverse, emit):
    xs = tuple(jnp.swapaxes(t, 0, 1) for t in (r, decay, kk, kk * a, k, v))

    def step(S, inp):
        r_t, w_t, kk_t, b_t, k_t, v_t = inp
        sa = jnp.einsum('bhij,bhj->bhi', S, kk_t)
        S = S * w_t[:, :, None, :] - sa[..., None] * b_t[:, :, None, :] + v_t[..., None] * k_t[:, :, None, :]
        if emit:
            return S, jnp.einsum('bhij,bhj->bhi', S, r_t)
        return S, None

    S, ys = lax.scan(step, S0, xs, reverse=reverse)
    return S, (jnp.swapaxes(ys, 0, 1) if emit else None)


def rwkv_bidir(feats, s_f0, s_b0, emit):
    r, v, kk, decay, a, k_dir, _ = feats
    s_f, y_f = rwkv_scan(s_f0, r, decay[:, :, 0], kk, a[:, :, 0], k_dir[:, :, 0], v, False, emit)
    s_b, y_b = rwkv_scan(s_b0, r, decay[:, :, 1], kk, a[:, :, 1], k_dir[:, :, 1], v, True, emit)
    return s_f, s_b, (y_f + y_b if emit else None)


def rwkv_out(y, feats, r_k, ln_w, ln_b):
    r, v, _, _, _, k_dir, g = feats
    B, T = y.shape[:2]
    mean = jnp.mean(y, axis=-1, keepdims=True)
    var = jnp.mean(jnp.square(y - mean), axis=-1, keepdims=True)
    yn = ((y - mean) * lax.rsqrt(var + RW_GN_EPS)).reshape(B, T, RW_WIDTH) * ln_w + ln_b
    bonus = jnp.sum(r[:, :, None] * k_dir * r_k.astype(jnp.float32), axis=(2, 4))[..., None] * v
    return (yn + bonus.reshape(B, T, RW_WIDTH)) * g


def merge_branches(p, o_na, o_rw, b_gate, p_na, p_rw, w_out):
    gates = jax.nn.sigmoid(p[..., OFF_GATE:] + b_gate)
    g_na, g_rw = jnp.split(gates, 2, axis=-1)
    return (g_na * (o_na @ p_na) + g_rw * (o_rw @ p_rw)) @ w_out


def setup_inputs(seed: int = 0) -> dict:
    key = jax.random.key(seed)
    ks = jax.random.split(key, 40)
    f32 = jnp.float32
    L = DEPTH
    D = D_MODEL

    def nrm(k, shape, s):
        return s * jax.random.normal(k, shape, f32)

    w0_base = -6.0 + 5.0 * jnp.linspace(0.0, 1.0, RW_WIDTH, dtype=f32) ** 1.5
    return {
        'x': nrm(ks[0], (BATCH, SEQ, D), 1.0),
        'c': nrm(ks[1], (BATCH, D), 1.0),
        'ctx': nrm(ks[2], (BATCH, CTX_LEN, D), 1.0),
        'c_ctx': nrm(ks[3], (D,), 1.0),
        'w_ada': nrm(ks[4], (L, D, N_MOD * D), 0.5 * D ** -0.5),
        'b_ada': nrm(ks[5], (L, N_MOD * D), 0.02),
        'norm_ffn1': 1.0 + nrm(ks[6], (L, D), 0.05),
        'norm_mix': 1.0 + nrm(ks[7], (L, D), 0.05),
        'norm_ffn2': 1.0 + nrm(ks[8], (L, D), 0.05),
        'norm_final': 1.0 + nrm(ks[9], (D,), 0.05),
        'ffn1_wg': nrm(ks[10], (L, D, D_FF), D ** -0.5),
        'ffn1_wu': nrm(ks[11], (L, D, D_FF), D ** -0.5),
        'ffn1_wd': nrm(ks[12], (L, D_FF, D), D_FF ** -0.5),
        'ffn2_wg': nrm(ks[13], (L, D, D_FF), D ** -0.5),
        'ffn2_wu': nrm(ks[14], (L, D, D_FF), D ** -0.5),
        'ffn2_wd': nrm(ks[15], (L, D_FF, D), D_FF ** -0.5),
        'w_in': nrm(ks[16], (L, D, IN_COLS), D ** -0.5),
        'b_gate': nrm(ks[17], (L, 2 * D), 0.1),
        'na_rpb': nrm(ks[18], (L, NA_HEADS, 2 * NA_WIN_H - 1, 2 * NA_WIN_W - 1), 0.1),
        'rw_mu': jax.random.uniform(ks[19], (L, 2, RW_SHIFT_COLS), f32, 0.0, 0.5),
        'rw_w0': w0_base + nrm(ks[20], (L, 2, RW_WIDTH), 0.1),
        'rw_w_up': nrm(ks[21], (L, 2, RW_DECAY_LORA, RW_WIDTH), 0.1),
        'rw_a0': nrm(ks[22], (L, 2, RW_WIDTH), 0.1),
        'rw_a_up': nrm(ks[23], (L, 2, RW_AAA_LORA, RW_WIDTH), RW_AAA_LORA ** -0.5),
        'rw_g_up': nrm(ks[24], (L, RW_GATE_LORA, RW_WIDTH), RW_GATE_LORA ** -0.5),
        'rw_k_k': 0.85 + nrm(ks[25], (L, RW_WIDTH), 0.05),
        'rw_k_a': 1.0 + nrm(ks[26], (L, RW_WIDTH), 0.05),
        'rw_r_k': nrm(ks[27], (L, RW_HEADS, RW_HEAD_DIM), 0.1),
        'rw_ln_w': 1.0 + nrm(ks[28], (L, RW_WIDTH), 0.05),
        'rw_ln_b': nrm(ks[29], (L, RW_WIDTH), 0.02),
        'p_na': nrm(ks[30], (L, NA_WIDTH, D), NA_WIDTH ** -0.5),
        'p_rw': nrm(ks[31], (L, RW_WIDTH, D), RW_WIDTH ** -0.5),
        'w_out': nrm(ks[32], (L, D, D), D ** -0.5),
    }


def reference(x, c, ctx, c_ctx, w_ada, b_ada, norm_ffn1, norm_mix, norm_ffn2, norm_final,
              ffn1_wg, ffn1_wu, ffn1_wd, ffn2_wg, ffn2_wu, ffn2_wd, w_in, b_gate, na_rpb,
              rw_mu, rw_w0, rw_w_up, rw_a0, rw_a_up, rw_g_up, rw_k_k, rw_k_a, rw_r_k, rw_ln_w, rw_ln_b,
              p_na, p_rw, w_out):
    B = x.shape[0]
    xc = ctx
    for l in range(DEPTH):
        is_last = l == DEPTH - 1
        mod_x = (jax.nn.silu(c) @ w_ada[l] + b_ada[l]).reshape(B, N_MOD, D_MODEL).transpose(1, 0, 2)[:, :, None, :]
        mod_c = (jax.nn.silu(c_ctx) @ w_ada[l] + b_ada[l]).reshape(N_MOD, D_MODEL)

        x = x + 0.5 * mod_x[2] * swiglu(modulated_norm(x, norm_ffn1[l], mod_x, 0), ffn1_wg[l], ffn1_wu[l], ffn1_wd[l])
        xc = xc + 0.5 * mod_c[2] * swiglu(modulated_norm(xc, norm_ffn1[l], mod_c, 0), ffn1_wg[l], ffn1_wu[l], ffn1_wd[l])

        px = modulated_norm(x, norm_mix[l], mod_x, 3) @ w_in[l]
        pc = modulated_norm(xc, norm_mix[l], mod_c, 3) @ w_in[l]

        qx, kx, vx = [split_heads(t, NA_HEADS) for t in jnp.split(px[..., :OFF_RW], 3, axis=-1)]
        qc, kc, vc = [split_heads(t, NA_HEADS) for t in jnp.split(pc[..., :OFF_RW], 3, axis=-1)]
        o_na = na_attention(qx, kx, vx, kc, vc, na_rpb[l])

        rw_params = (rw_w0[l], rw_w_up[l], rw_a0[l], rw_a_up[l], rw_g_up[l], rw_k_k[l], rw_k_a[l])
        feats_x = rwkv_prep(token_shift(px[..., OFF_RW:OFF_GATE], rw_mu[l, 0], rw_mu[l, 1]), *rw_params)
        feats_c = rwkv_prep(token_shift(pc[..., OFF_RW:OFF_GATE], rw_mu[l, 0], rw_mu[l, 1]), *rw_params)
        s0 = jnp.zeros((B, RW_HEADS, RW_HEAD_DIM, RW_HEAD_DIM), jnp.float32)
        s_cf, s_cb, y_c = rwkv_bidir(feats_c, s0, s0, not is_last)
        _, _, y_x = rwkv_bidir(feats_x, s_cf, s_cb, True)
        o_rw = rwkv_out(y_x, feats_x, rw_r_k[l], rw_ln_w[l], rw_ln_b[l]).astype(x.dtype)

        x = x + mod_x[5] * merge_branches(px, o_na, o_rw, b_gate[l], p_na[l], p_rw[l], w_out[l])
        if not is_last:
            o_na_c = context_attention(qc, kc, vc)
            o_rw_c = rwkv_out(y_c, feats_c, rw_r_k[l], rw_ln_w[l], rw_ln_b[l]).astype(xc.dtype)
            xc = xc + mod_c[5] * merge_branches(pc, o_na_c, o_rw_c, b_gate[l], p_na[l], p_rw[l], w_out[l])
            xc = xc + 0.5 * mod_c[8] * swiglu(modulated_norm(xc, norm_ffn2[l], mod_c, 6), ffn2_wg[l], ffn2_wu[l], ffn2_wd[l])

        x = x + 0.5 * mod_x[8] * swiglu(modulated_norm(x, norm_ffn2[l], mod_x, 6), ffn2_wg[l], ffn2_wu[l], ffn2_wd[l])
    return rmsnorm(x, norm_final)
```

```python
import functools
import math

import numpy as np
import jax
import jax.numpy as jnp
from jax import lax
from jax.experimental import pallas as pl
from jax.experimental.pallas import tpu as pltpu

F32 = jnp.float32
BF16 = jnp.bfloat16

NORM_EPS = 1e-6
RW_GN_EPS = 64e-5
N_MOD = 9
GRID_W = 64
NA_HEADS = 8
NA_HEAD_DIM = 64
NA_WIN_H = 8
NA_WIN_W = 16
RW_HEADS = 8
RW_HEAD_DIM = 64
RW_LORA = 64
RW_GATE_LORA = 128

LANES = 128
VMEM_LIMIT = 56 * 1024 * 1024

TM = 512
TM_IN = 256
NA_ROWS = 8
RW_TILE = 256
RW_CHUNK = 64


def _cparams(n_axes):
    return pltpu.CompilerParams(dimension_semantics=("arbitrary",) * n_axes,
                                vmem_limit_bytes=VMEM_LIMIT)


def _resident(shape):
    nd = len(shape)
    return pl.BlockSpec(shape, lambda *_: (0,) * nd, pipeline_mode=pl.Buffered(1))


def _rmsnorm(x, g):
    return x * lax.rsqrt(jnp.mean(x * x, axis=-1, keepdims=True) + NORM_EPS) * g


def _split2(x):
    hi = x.astype(BF16)
    lo = (x - hi.astype(F32)).astype(BF16)
    return hi, lo


def _dot(a, b):
    return jnp.dot(a, b, preferred_element_type=F32)


def _dot_split_lhs(x, w_bf16, parts):
    acc = None
    rem = x
    for _ in range(parts):
        p = rem.astype(BF16)
        rem = rem - p.astype(F32)
        t = _dot(p, w_bf16)
        acc = t if acc is None else acc + t
    return acc


def _dot_split_rhs(w_bf16, x, parts):
    acc = None
    rem = x
    for _ in range(parts):
        p = rem.astype(BF16)
        rem = rem - p.astype(F32)
        t = _dot(w_bf16, p)
        acc = t if acc is None else acc + t
    return acc


def _dot3(a, b):
    ah, al = _split2(a)
    bh, bl = _split2(b)
    return _dot(ah, bh) + (_dot(ah, bl) + _dot(al, bh))


def _ada_kernel(c_ref, w_ref, b_ref, o_ref):
    c = c_ref[...]
    s = c * jax.nn.sigmoid(c)
    o_ref[...] = _dot3(s, w_ref[...]) + b_ref[...]


def _ada_mod(cs, w_ada, b_ada):
    d = cs.shape[1]
    nm = w_ada.shape[1] // d
    return pl.pallas_call(
        _ada_kernel,
        grid=(nm,),
        in_specs=[pl.BlockSpec((8, d), lambda j: (0, 0)),
                  pl.BlockSpec((d, d), lambda j: (0, j)),
                  pl.BlockSpec((1, d), lambda j: (0, j))],
        out_specs=pl.BlockSpec((8, d), lambda j: (0, j)),
        out_shape=jax.ShapeDtypeStruct((8, nm * d), F32),
        compiler_params=_cparams(1),
    )(cs, w_ada, b_ada.reshape(1, -1))


def _ffn_kernel(x_ref, mod_ref, g_ref, wg_ref, wu_ref, wd_ref, gf_ref, o_ref, *, mod0, f_chunk, final):
    x = x_ref[...]
    shift = mod_ref[0, mod0:mod0 + 1, :]
    scale = mod_ref[0, mod0 + 1:mod0 + 2, :]
    gate = mod_ref[0, mod0 + 2:mod0 + 3, :]
    h = (_rmsnorm(x, g_ref[...]) * (1.0 + scale) + shift).astype(BF16)
    d_ff = wg_ref.shape[1]
    acc = None
    for f0 in range(0, d_ff, f_chunk):
        gg = _dot(h, wg_ref[:, f0:f0 + f_chunk])
        uu = _dot(h, wu_ref[:, f0:f0 + f_chunk])
        a = (gg * jax.nn.sigmoid(gg) * uu).astype(BF16)
        t = _dot(a, wd_ref[f0:f0 + f_chunk, :])
        acc = t if acc is None else acc + t
    y = x + 0.5 * gate * acc
    if final:
        y = _rmsnorm(y, gf_ref[...])
    o_ref[...] = y


def _ffn(xs, mod, g, wg, wu, wd, gf, *, mod0, final, tiles_per_seq, n_seq):
    n, d = xs.shape
    d_ff = wg.shape[1]
    f_chunk = d_ff // 2 if (d_ff // 2) % LANES == 0 else d_ff
    seq_of = lambda i: jnp.minimum(i // tiles_per_seq, n_seq)
    return pl.pallas_call(
        functools.partial(_ffn_kernel, mod0=mod0, f_chunk=f_chunk, final=final),
        grid=(n // TM,),
        in_specs=[pl.BlockSpec((TM, d), lambda i: (i, 0)),
                  pl.BlockSpec((1, N_MOD, d), lambda i: (seq_of(i), 0, 0)),
                  _resident((1, d)), _resident((d, d_ff)), _resident((d, d_ff)), _resident((d_ff, d)),
                  _resident((1, d))],
        out_specs=pl.BlockSpec((TM, d), lambda i: (i, 0)),
        out_shape=jax.ShapeDtypeStruct((n, d), F32),
        compiler_params=_cparams(1),
    )(xs, mod, g, wg, wu, wd, gf)


def _inproj_kernel(x_ref, mod_ref, g_ref, w_ref, qkv_ref, rw_ref, gate_ref, *, c_rw, c_gate):
    x = x_ref[...]
    h = (_rmsnorm(x, g_ref[...]) * (1.0 + mod_ref[0, 4:5, :]) + mod_ref[0, 3:4, :]).astype(BF16)
    qkv_ref[...] = _dot(h, w_ref[:, :c_rw]).astype(BF16)
    rw_ref[...] = _dot(h, w_ref[:, c_rw:c_gate])
    gate_ref[...] = _dot(h, w_ref[:, c_gate:])


def _inproj(xs, mod, g, w_in, *, c_rw, c_gate, tiles_per_seq, n_seq):
    n, d = xs.shape
    cols = w_in.shape[1]
    seq_of = lambda i: jnp.minimum(i // tiles_per_seq, n_seq)
    return pl.pallas_call(
        functools.partial(_inproj_kernel, c_rw=c_rw, c_gate=c_gate),
        grid=(n // TM_IN,),
        in_specs=[pl.BlockSpec((TM_IN, d), lambda i: (i, 0)),
                  pl.BlockSpec((1, N_MOD, d), lambda i: (seq_of(i), 0, 0)),
                  _resident((1, d)), _resident((d, cols))],
        out_specs=[pl.BlockSpec((TM_IN, c_rw), lambda i: (i, 0)),
                   pl.BlockSpec((TM_IN, c_gate - c_rw), lambda i: (i, 0)),
                   pl.BlockSpec((TM_IN, cols - c_gate), lambda i: (i, 0))],
        out_shape=[jax.ShapeDtypeStruct((n, c_rw), BF16),
                   jax.ShapeDtypeStruct((n, c_gate - c_rw), F32),
                   jax.ShapeDtypeStruct((n, cols - c_gate), F32)],
        compiler_params=_cparams(1),
    )(xs, mod, g, w_in)


def _na_bias_table(rpb, rows):
    nj = rows // NA_ROWS
    kr = min(NA_WIN_H, rows)
    w = GRID_W
    tabs = []
    for j in (0, min(1, nj - 1), nj - 1):
        qi = np.arange(NA_ROWS)
        i_abs = NA_ROWS * j + qi
        r0 = np.clip(i_abs - kr // 2, 0, rows - kr)
        kl = np.arange(16)
        blk = 2 * j - 1 + kl // 4
        k_abs = 4 * blk + kl % 4
        blk_ok = (blk >= 0) & (blk < rows // 4)
        row_ok = blk_ok[None, :] & (k_abs[None, :] >= r0[:, None]) & (k_abs[None, :] < r0[:, None] + kr)
        dr = np.clip(k_abs[None, :] - i_abs[:, None] + (NA_WIN_H - 1), 0, 2 * NA_WIN_H - 2)
        qc = np.arange(w)
        kc = np.arange(w)
        s_c = np.clip(qc - NA_WIN_W // 2, 0, w - NA_WIN_W)
        col_ok = (kc[None, :] >= s_c[:, None]) & (kc[None, :] < s_c[:, None] + NA_WIN_W)
        dc = np.clip(kc[None, :] - qc[:, None] + NA_WIN_W - 1, 0, 2 * NA_WIN_W - 2)
        ok = row_ok[:, None, :, None] & col_ok[None, :, None, :]
        dr_f = np.broadcast_to(dr[:, None, :, None], ok.shape).reshape(NA_ROWS * w, 16 * w)
        dc_f = np.broadcast_to(dc[None, :, None, :], ok.shape).reshape(NA_ROWS * w, 16 * w)
        bias = rpb[:, dr_f, dc_f]
        tabs.append(jnp.where(ok.reshape(1, NA_ROWS * w, 16 * w), bias, -jnp.inf))
    return jnp.stack(tabs, axis=0).astype(F32)


def _na_kernel(q_ref, k0_ref, k1_ref, k2_ref, k3_ref, kc_ref, v0_ref, v1_ref, v2_ref, v3_ref, vc_ref,
               bias_ref, o_ref, kcat, vcat):
    blk = k0_ref.shape[0]
    for s, (kr, vr) in enumerate(((k0_ref, v0_ref), (k1_ref, v1_ref), (k2_ref, v2_ref), (k3_ref, v3_ref),
                                  (kc_ref, vc_ref))):
        kcat[s * blk:s * blk + kr.shape[0], :] = kr[...]
        vcat[s * blk:s * blk + vr.shape[0], :] = vr[...]
    n_loc = 4 * blk
    lane = lax.broadcasted_iota(jnp.int32, (1, LANES), 1)
    first = lane < NA_HEAD_DIM
    q = q_ref[...] * (NA_HEAD_DIM ** -0.5)
    k_all = kcat[...]
    v_all = vcat[...]
    outs = []
    for h in range(2):
        sel = first if h == 0 else jnp.logical_not(first)
        qh = jnp.where(sel, q, jnp.zeros_like(q))
        s_loc = lax.dot_general(qh, k_all[:n_loc], (((1,), (1,)), ((), ())), preferred_element_type=F32)
        s_loc = s_loc + bias_ref[0, h]
        s_ctx = lax.dot_general(qh, k_all[n_loc:], (((1,), (1,)), ((), ())), preferred_element_type=F32)
        m = jnp.maximum(jnp.max(s_loc, axis=-1, keepdims=True), jnp.max(s_ctx, axis=-1, keepdims=True))
        p_loc = jnp.exp(s_loc - m)
        p_ctx = jnp.exp(s_ctx - m)
        l = jnp.sum(p_loc, axis=-1, keepdims=True) + jnp.sum(p_ctx, axis=-1, keepdims=True)
        o = _dot(p_loc.astype(BF16), v_all[:n_loc]) + _dot(p_ctx.astype(BF16), v_all[n_loc:])
        outs.append(o / l)
    o_ref[...] = jnp.where(first, outs[0], outs[1]).astype(o_ref.dtype)


def _na_attention(qkv, bias, *, batch, seq, ctx_len):
    rows = seq // GRID_W
    nj = rows // NA_ROWS
    tq = NA_ROWS * GRID_W
    tk = 4 * GRID_W
    width = NA_HEADS * NA_HEAD_DIM
    n_pairs = width // LANES
    kblocks = seq // tk
    ctx_blk0 = (batch * seq) // ctx_len

    def kv_spec(slot, col0):
        def imap(hp, b, j):
            return (b * kblocks + jnp.clip(2 * j - 1 + slot, 0, kblocks - 1), col0 + hp)
        return pl.BlockSpec((tk, LANES), imap)

    def ctx_spec(col0):
        return pl.BlockSpec((ctx_len, LANES), lambda hp, b, j: (ctx_blk0 + b, col0 + hp))

    def case_of(j):
        return jnp.where(j == 0, 0, jnp.where(j == nj - 1, 2, 1))

    in_specs = ([pl.BlockSpec((tq, LANES), lambda hp, b, j: (b * nj + j, hp))]
                + [kv_spec(s, n_pairs) for s in range(4)] + [ctx_spec(n_pairs)]
                + [kv_spec(s, 2 * n_pairs) for s in range(4)] + [ctx_spec(2 * n_pairs)]
                + [pl.BlockSpec((1, 2, tq, 4 * tk), lambda hp, b, j: (case_of(j), hp, 0, 0))])
    return pl.pallas_call(
        _na_kernel,
        grid=(n_pairs, batch, nj),
        in_specs=in_specs,
        out_specs=pl.BlockSpec((tq, LANES), lambda hp, b, j: (b * nj + j, hp)),
        out_shape=jax.ShapeDtypeStruct((batch * seq, width), BF16),
        scratch_shapes=[pltpu.VMEM((4 * tk + ctx_len, LANES), BF16),
                        pltpu.VMEM((4 * tk + ctx_len, LANES), BF16)],
        compiler_params=_cparams(3),
    )(*([qkv] * 11), bias)


def _head_segsum(x, parts):
    r = lax.broadcasted_iota(jnp.int32, (LANES, LANES), 0) // RW_HEAD_DIM
    c = lax.broadcasted_iota(jnp.int32, (LANES, LANES), 1) // RW_HEAD_DIM
    e = jnp.where(r == c, 1.0, 0.0).astype(BF16)
    cols = [_dot_split_lhs(x[:, j:j + LANES], e, parts) for j in range(0, x.shape[1], LANES)]
    return jnp.concatenate(cols, axis=1)


def _stack_heads(x, first):
    z = jnp.zeros_like(x)
    return jnp.concatenate([jnp.where(first, x, z), jnp.where(first, z, x)], axis=0)


def _rwkv_kernel(p_ref, prev_ref, next_ref, mu_ref, par_ref, wup_ref, aup_ref, gup_ref, tri_ref,
                 strict_ref, incl_ref, y_ref, bon_ref, g_ref,
                 z_s, r_s, kk_s, kd_s, bb_s, v_s, ci_s, lw_s, *, reverse, n_ctx_tiles, n_x_tiles):
    n = pl.program_id(1)
    ta = p_ref.shape[0]
    width = r_s.shape[1]
    n_pairs = width // LANES
    c_len = RW_CHUNK

    @pl.when(n == 0)
    def _():
        z_s[...] = jnp.zeros_like(z_s)

    is_ctx = n < n_ctx_tiles
    t_ctx = (n_ctx_tiles - 1 - n) if reverse else n
    t_x = (n_x_tiles - 1 - (n - n_ctx_tiles)) if reverse else (n - n_ctx_tiles)
    tile = jnp.where(is_ctx, t_ctx, t_x)
    last_tile = jnp.where(is_ctx, n_ctx_tiles - 1, n_x_tiles - 1)

    p = p_ref[...]
    row = lax.broadcasted_iota(jnp.int32, (ta, 1), 0)
    prev_row = jnp.where(tile == 0, 0.0, prev_ref[7:8, :])
    next_row = jnp.where(tile == last_tile, 0.0, next_ref[0:1, :])
    p_dn = jnp.where(row == 0, prev_row, pltpu.roll(p, 1, 0))
    p_up = jnp.where(row == ta - 1, next_row, pltpu.roll(p, ta - 1, 0))
    f = p + mu_ref[0:1, :] * (p_dn - p) + mu_ref[1:2, :] * (p_up - p)

    r = f[:, 0:width]
    k = f[:, width:2 * width]
    v = f[:, 2 * width:3 * width]
    o = 3 * width
    wl = f[:, o:o + LANES]
    al = f[:, o + LANES:o + 2 * LANES]
    gl = f[:, o + 2 * LANES:o + 3 * LANES]
    w0 = par_ref[0:1, :]
    a0 = par_ref[1:2, :]
    k_k = par_ref[2:3, :]
    k_a = par_ref[3:4, :]
    r_k = par_ref[4:5, :]

    z = w0 + _dot3(jnp.tanh(wl), wup_ref[...])
    lw = -math.exp(-0.5) * jax.nn.sigmoid(z)
    a = jax.nn.sigmoid(a0 + _dot(al.astype(BF16), aup_ref[...]))
    g_ref[...] = _dot(jax.nn.sigmoid(gl).astype(BF16), gup_ref[...])
    kk = k * k_k
    kk = kk * lax.rsqrt(jnp.maximum(_head_segsum(kk * kk, 2), 1e-12))
    kd = k * (1.0 + (a - 1.0) * k_a)
    bon_ref[0] = _head_segsum(r * kd * r_k, 2) * v
    r_s[...] = r
    kk_s[...] = kk
    kd_s[...] = kd
    bb_s[...] = kk * a
    v_s[...] = v
    lw_s[...] = lw
    ci_s[...] = _dot_split_rhs(tri_ref[...], lw, 3)

    lane = lax.broadcasted_iota(jnp.int32, (1, LANES), 1)
    first = lane < RW_HEAD_DIM
    ri = lax.broadcasted_iota(jnp.int32, (LANES, LANES), 0)
    cj = lax.broadcasted_iota(jnp.int32, (LANES, LANES), 1)
    eye = ri == cj
    blockdiag = (ri // RW_HEAD_DIM) == (cj // RW_HEAD_DIM)
    strict = strict_ref[...] > 0.0
    incl = incl_ref[...] > 0.0
    n_chunks = ta // c_len
    order = range(n_chunks - 1, -1, -1) if reverse else range(n_chunks)
    for c in order:
        rows = slice(c * c_len, (c + 1) * c_len)
        t_last = c * c_len if reverse else (c + 1) * c_len - 1
        for pr in range(n_pairs):
            cols = slice(pr * LANES, (pr + 1) * LANES)
            ci = ci_s[rows, cols]
            ce = ci - lw_s[rows, cols]
            ct = ci_s[t_last:t_last + 1, cols]
            cm = 0.5 * ct
            rr = r_s[rows, cols]
            kkc = kk_s[rows, cols]
            kdc = kd_s[rows, cols]
            bbc = bb_s[rows, cols]
            vc = v_s[rows, cols]
            e_q = jnp.exp(ce - cm)
            e_k = jnp.exp(cm - ci)
            kk_t = kkc * e_q
            r_t = rr * jnp.exp(ci - cm)
            b_t = bbc * e_k
            k_t = kdc * e_k
            kk_abs = kkc * jnp.exp(ce)
            r_abs = rr * jnp.exp(ci)
            e_end = jnp.exp(ct - ci)
            k_bar = kdc * e_end
            b_bar = bbc * e_end
            gam = jnp.exp(ct)

            lhs = jnp.concatenate([kk_t, r_t], axis=0).astype(BF16)
            rhs = jnp.concatenate([_stack_heads(b_t, first), _stack_heads(k_t, first)], axis=0).astype(BF16)
            gm = lax.dot_general(lhs, rhs, (((1,), (1,)), ((), ())), preferred_element_type=F32)
            ab_w = jnp.where(strict, gm[:c_len, :LANES], 0.0)
            ak_w = jnp.where(strict, gm[:c_len, LANES:], 0.0)
            db_w = jnp.where(incl, gm[c_len:, :LANES], 0.0)
            dk_w = jnp.where(incl, gm[c_len:, LANES:], 0.0)

            l_bd = _stack_heads(ab_w, first)
            t_bd = jnp.where(eye, 1.0, 0.0) - l_bd
            pw = _dot(l_bd.astype(BF16), l_bd.astype(BF16))
            span = 2
            while span < c_len:
                t_bd = t_bd + _dot(t_bd.astype(BF16), pw.astype(BF16))
                span *= 2
                if span < c_len:
                    pw = _dot(pw.astype(BF16), pw.astype(BF16))
            t_w = (t_bd[:c_len] + t_bd[c_len:]).astype(BF16)

            v_st = _stack_heads(vc, first).astype(BF16)
            w1 = _dot(ak_w.astype(BF16), v_st)
            pq = _dot(t_w, jnp.concatenate([_stack_heads(kk_abs, first), _stack_heads(w1, first)],
                                           axis=1).astype(BF16))
            p_m = pq[:, :LANES]
            q_m = pq[:, LANES:]
            dpq = _dot(db_w.astype(BF16),
                       jnp.concatenate([_stack_heads(p_m, first), _stack_heads(q_m, first)], axis=1).astype(BF16))
            r_hat = r_abs - dpq[:, :LANES]
            y_loc = _dot(dk_w.astype(BF16), v_st) - dpq[:, LANES:]
            tn = (((0,), (0,)), ((), ()))
            bpq = lax.dot_general(b_bar.astype(BF16), pq.astype(BF16), tn, preferred_element_type=F32)
            kv = lax.dot_general(k_bar.astype(BF16), vc.astype(BF16), tn, preferred_element_type=F32)
            m_t = jnp.where(eye, jnp.broadcast_to(gam, (LANES, LANES)), 0.0) - bpq[:, :LANES]
            g_t = kv - bpq[:, LANES:]

            z_old = z_s[pr]
            y_ref[0, rows, cols] = _dot(r_hat.astype(BF16), z_old.astype(BF16)) + y_loc
            z_s[pr] = jnp.where(blockdiag, _dot3(m_t, z_old) + g_t, 0.0)


def _rwkv_stream(p_rw, mu, par, wup, aup, gup, *, reverse, batch, seq, ctx_len):
    n_tok, cols = p_rw.shape
    width = RW_HEADS * RW_HEAD_DIM
    ta = RW_TILE
    nct = ctx_len // ta
    nxt = seq // ta
    n_x = batch * seq
    c_len = RW_CHUNK

    def blk(b, n):
        is_ctx = n < nct
        t_ctx = (nct - 1 - n) if reverse else n
        t_x = (nxt - 1 - (n - nct)) if reverse else (n - nct)
        return jnp.where(is_ctx, batch * nxt + b * nct + t_ctx, b * nxt + t_x)

    def xblk(b, n):
        m = jnp.maximum(n, nct) - nct
        return b * nxt + ((nxt - 1 - m) if reverse else m)

    sub = ta // 8
    last8 = n_tok // 8 - 1
    idx = np.arange(ta)
    same_chunk = (idx[:, None] // c_len) == (idx[None, :] // c_len)
    causal = (idx[None, :] >= idx[:, None]) if reverse else (idx[None, :] <= idx[:, None])
    tri = jnp.asarray(same_chunk & causal, BF16)
    t_i = np.arange(c_len)[:, None]
    s_i = np.arange(LANES)[None, :] % RW_HEAD_DIM
    strict = jnp.asarray((s_i > t_i) if reverse else (s_i < t_i), F32)
    incl = jnp.asarray((s_i >= t_i) if reverse else (s_i <= t_i), F32)

    out_block = pl.BlockSpec((1, ta, width), lambda b, n: (0, xblk(b, n), 0))
    return pl.pallas_call(
        functools.partial(_rwkv_kernel, reverse=reverse, n_ctx_tiles=nct, n_x_tiles=nxt),
        grid=(batch, nct + nxt),
        in_specs=[pl.BlockSpec((ta, cols), lambda b, n: (blk(b, n), 0)),
                  pl.BlockSpec((8, cols), lambda b, n: (jnp.maximum(blk(b, n) * sub - 1, 0), 0)),
                  pl.BlockSpec((8, cols), lambda b, n: (jnp.minimum((blk(b, n) + 1) * sub, last8), 0)),
                  _resident(mu.shape), _resident(par.shape), _resident(wup.shape), _resident(aup.shape),
                  _resident(gup.shape), _resident(tri.shape), _resident(strict.shape), _resident(incl.shape)],
        out_specs=[out_block, out_block, pl.BlockSpec((ta, width), lambda b, n: (xblk(b, n), 0))],
        out_shape=[jax.ShapeDtypeStruct((1, n_x, width), F32), jax.ShapeDtypeStruct((1, n_x, width), F32),
                   jax.ShapeDtypeStruct((n_x, width), F32)],
        scratch_shapes=[pltpu.VMEM((width // LANES, LANES, LANES), F32)]
                       + [pltpu.VMEM((ta, width), F32)] * 7,
        compiler_params=_cparams(2),
    )(p_rw, p_rw, p_rw, mu, par, wup, aup, gup, tri, strict, incl)


def _merge_kernel(x_ref, mod_ref, ona_ref, yf_ref, yb_ref, bf_ref, bb_ref, g_ref, pg_ref, bg_ref, ln_ref,
                  pna_ref, prw_ref, wout_ref, o_ref):
    d = x_ref.shape[1]
    y = yf_ref[0] + yb_ref[0]
    inv_n = 1.0 / RW_HEAD_DIM
    mean = _head_segsum(y, 2) * inv_n
    yc = y - mean
    var = _head_segsum(yc * yc, 2) * inv_n
    yn = yc * lax.rsqrt(var + RW_GN_EPS) * ln_ref[0:1, :] + ln_ref[1:2, :]
    o_rw = ((yn + bf_ref[0] + bb_ref[0]) * g_ref[...]).astype(BF16)
    gates = jax.nn.sigmoid(pg_ref[...] + bg_ref[...])
    m = gates[:, :d] * _dot(ona_ref[...], pna_ref[...]) + gates[:, d:] * _dot(o_rw, prw_ref[...])
    o_ref[...] = x_ref[...] + mod_ref[0, 5:6, :] * _dot(m.astype(BF16), wout_ref[...])


def _merge(x1, mod, o_na, y_f, y_b, bon_f, bon_b, g, p_gate, b_gate, ln, p_na, p_rw, w_out, *, tiles_per_seq):
    n_x, d = o_na.shape[0], x1.shape[1]
    width = o_na.shape[1]
    tok = lambda i: (i, 0)
    tok3 = lambda i: (0, i, 0)
    return pl.pallas_call(
        _merge_kernel,
        grid=(n_x // TM,),
        in_specs=[pl.BlockSpec((TM, d), tok),
                  pl.BlockSpec((1, N_MOD, d), lambda i: (i // tiles_per_seq, 0, 0)),
                  pl.BlockSpec((TM, width), tok),
                  pl.BlockSpec((1, TM, width), tok3), pl.BlockSpec((1, TM, width), tok3),
                  pl.BlockSpec((1, TM, width), tok3), pl.BlockSpec((1, TM, width), tok3),
                  pl.BlockSpec((TM, width), tok),
                  pl.BlockSpec((TM, 2 * d), tok),
                  _resident((1, 2 * d)), _resident((2, width)),
                  _resident(p_na.shape), _resident(p_rw.shape), _resident(w_out.shape)],
        out_specs=pl.BlockSpec((TM, d), tok),
        out_shape=jax.ShapeDtypeStruct((n_x, d), F32),
        compiler_params=_cparams(1),
    )(x1, mod, o_na, y_f, y_b, bon_f, bon_b, g, p_gate, b_gate, ln, p_na, p_rw, w_out)


def _pad_lora(w_up, direction):
    z = jnp.zeros_like(w_up[0])
    return jnp.concatenate([w_up[0] if direction == 0 else z, w_up[1] if direction == 1 else z], axis=0)


def kernel(x, c, ctx, c_ctx, w_ada, b_ada, norm_ffn1, norm_mix, norm_ffn2, norm_final, ffn1_wg, ffn1_wu, ffn1_wd, ffn2_wg, ffn2_wu, ffn2_wd, w_in, b_gate, na_rpb, rw_mu, rw_w0, rw_w_up, rw_a0, rw_a_up, rw_g_up, rw_k_k, rw_k_a, rw_r_k, rw_ln_w, rw_ln_b, p_na, p_rw, w_out):
    batch, seq, d = x.shape
    ctx_len = ctx.shape[1]
    n_x = batch * seq
    na_width = NA_HEADS * NA_HEAD_DIM
    rw_width = RW_HEADS * RW_HEAD_DIM
    c_rw = 3 * na_width
    c_gate = c_rw + 3 * rw_width + 4 * RW_LORA + RW_GATE_LORA
    assert w_ada.shape[0] == 1, "single layer"
    assert seq % TM == 0 and (batch * ctx_len) % TM == 0 and ctx_len % RW_TILE == 0 and batch + 1 <= 8
    assert seq % (NA_ROWS * GRID_W) == 0 and seq // (NA_ROWS * GRID_W) >= 2 and ctx_len == 4 * GRID_W
    row = lambda t: t.reshape(1, -1)

    cs = jnp.concatenate([c, c_ctx[None], jnp.zeros((8 - batch - 1, d), F32)], axis=0)
    mod = _ada_mod(cs, w_ada[0], b_ada[0])[:batch + 1].reshape(batch + 1, N_MOD, d)

    xs = jnp.concatenate([x.reshape(n_x, d), ctx.reshape(batch * ctx_len, d)], axis=0)
    bf = lambda t: t.astype(BF16)

    x1 = _ffn(xs, mod, row(norm_ffn1[0]), bf(ffn1_wg[0]), bf(ffn1_wu[0]), bf(ffn1_wd[0]), row(norm_final),
              mod0=0, final=False, tiles_per_seq=seq // TM, n_seq=batch)
    qkv, p_rwkv, p_gate = _inproj(x1, mod, row(norm_mix[0]), bf(w_in[0]), c_rw=c_rw, c_gate=c_gate,
                                  tiles_per_seq=seq // TM_IN, n_seq=batch)

    bias = _na_bias_table(na_rpb[0], seq // GRID_W)
    o_na = _na_attention(qkv, bias, batch=batch, seq=seq, ctx_len=ctx_len)

    streams = []
    for direction in (0, 1):
        par = jnp.concatenate([row(rw_w0[0, direction]), row(rw_a0[0, direction]), row(rw_k_k[0]),
                               row(rw_k_a[0]), row(rw_r_k[0]), jnp.zeros((3, rw_width), F32)], axis=0)
        streams.append(_rwkv_stream(p_rwkv, rw_mu[0], par, _pad_lora(rw_w_up[0], direction),
                                    bf(_pad_lora(rw_a_up[0], direction)), bf(rw_g_up[0]),
                                    reverse=direction == 1, batch=batch, seq=seq, ctx_len=ctx_len))
    (y_f, bon_f, g), (y_b, bon_b, _) = streams

    ln = jnp.stack([rw_ln_w[0], rw_ln_b[0]], axis=0)
    x2 = _merge(x1, mod, o_na, y_f, y_b, bon_f, bon_b, g, p_gate, row(b_gate[0]), ln,
                bf(p_na[0]), bf(p_rw[0]), bf(w_out[0]), tiles_per_seq=seq // TM)
    out = _ffn(x2, mod, row(norm_ffn2[0]), bf(ffn2_wg[0]), bf(ffn2_wu[0]), bf(ffn2_wd[0]), row(norm_final),
               mod0=6, final=True, tiles_per_seq=seq // TM, n_seq=batch)
    return out.reshape(batch, seq, d)
```

```python
import functools
import math

import numpy as np
import jax
import jax.numpy as jnp
from jax import lax
from jax.experimental import pallas as pl
from jax.experimental.pallas import tpu as pltpu

F32 = jnp.float32
BF16 = jnp.bfloat16

NORM_EPS = 1e-6
RW_GN_EPS = 64e-5
N_MOD = 9
GRID_W = 64
NA_HEADS = 8
NA_HEAD_DIM = 64
NA_WIN_H = 8
NA_WIN_W = 16
RW_HEADS = 8
RW_HEAD_DIM = 64
RW_LORA = 64
RW_GATE_LORA = 128

LANES = 128
VMEM_LIMIT = 56 * 1024 * 1024

TM = 512
TM_IN = 256
NA_ROWS = 8
RW_TILE = 256
RW_CHUNK = 64


def _cparams(n_axes):
    return pltpu.CompilerParams(dimension_semantics=("arbitrary",) * n_axes,
                                vmem_limit_bytes=VMEM_LIMIT)


def _resident(shape):
    nd = len(shape)
    return pl.BlockSpec(shape, lambda *_: (0,) * nd, pipeline_mode=pl.Buffered(1))


def _rmsnorm(x, g):
    return x * lax.rsqrt(jnp.mean(x * x, axis=-1, keepdims=True) + NORM_EPS) * g


def _split2(x):
    hi = x.astype(BF16)
    lo = (x - hi.astype(F32)).astype(BF16)
    return hi, lo


def _dot(a, b):
    return jnp.dot(a, b, preferred_element_type=F32)


def _dot_split_lhs(x, w_bf16, parts):
    acc = None
    rem = x
    for _ in range(parts):
        p = rem.astype(BF16)
        rem = rem - p.astype(F32)
        t = _dot(p, w_bf16)
        acc = t if acc is None else acc + t
    return acc


def _dot_split_rhs(w_bf16, x, parts):
    acc = None
    rem = x
    for _ in range(parts):
        p = rem.astype(BF16)
        rem = rem - p.astype(F32)
        t = _dot(w_bf16, p)
        acc = t if acc is None else acc + t
    return acc


def _dot3(a, b):
    ah, al = _split2(a)
    bh, bl = _split2(b)
    return _dot(ah, bh) + (_dot(ah, bl) + _dot(al, bh))


def _ada_kernel(c_ref, w_ref, b_ref, o_ref):
    c = c_ref[...]
    s = c * jax.nn.sigmoid(c)
    o_ref[...] = _dot3(s, w_ref[...]) + b_ref[...]


def _ada_mod(cs, w_ada, b_ada):
    d = cs.shape[1]
    nm = w_ada.shape[1] // d
    return pl.pallas_call(
        _ada_kernel,
        grid=(nm,),
        in_specs=[pl.BlockSpec((8, d), lambda j: (0, 0)),
                  pl.BlockSpec((d, d), lambda j: (0, j)),
                  pl.BlockSpec((1, d), lambda j: (0, j))],
        out_specs=pl.BlockSpec((8, d), lambda j: (0, j)),
        out_shape=jax.ShapeDtypeStruct((8, nm * d), F32),
        compiler_params=_cparams(1),
    )(cs, w_ada, b_ada.reshape(1, -1))


def _ffn_kernel(x_ref, mod_ref, g_ref, wg_ref, wu_ref, wd_ref, gf_ref, o_ref, *, mod0, f_chunk, final):
    x = x_ref[...]
    shift = mod_ref[0, mod0:mod0 + 1, :]
    scale = mod_ref[0, mod0 + 1:mod0 + 2, :]
    gate = mod_ref[0, mod0 + 2:mod0 + 3, :]
    h = (_rmsnorm(x, g_ref[...]) * (1.0 + scale) + shift).astype(BF16)
    d_ff = wg_ref.shape[1]
    acc = None
    for f0 in range(0, d_ff, f_chunk):
        gg = _dot(h, wg_ref[:, f0:f0 + f_chunk])
        uu = _dot(h, wu_ref[:, f0:f0 + f_chunk])
        a = (gg * jax.nn.sigmoid(gg) * uu).astype(BF16)
        t = _dot(a, wd_ref[f0:f0 + f_chunk, :])
        acc = t if acc is None else acc + t
    y = x + 0.5 * gate * acc
    if final:
        y = _rmsnorm(y, gf_ref[...])
    o_ref[...] = y


def _ffn(xs, mod, g, wg, wu, wd, gf, *, mod0, final, tiles_per_seq, n_seq):
    n, d = xs.shape
    d_ff = wg.shape[1]
    f_chunk = d_ff // 2 if (d_ff // 2) % LANES == 0 else d_ff
    seq_of = lambda i: jnp.minimum(i // tiles_per_seq, n_seq)
    return pl.pallas_call(
        functools.partial(_ffn_kernel, mod0=mod0, f_chunk=f_chunk, final=final),
        grid=(n // TM,),
        in_specs=[pl.BlockSpec((TM, d), lambda i: (i, 0)),
                  pl.BlockSpec((1, N_MOD, d), lambda i: (seq_of(i), 0, 0)),
                  _resident((1, d)), _resident((d, d_ff)), _resident((d, d_ff)), _resident((d_ff, d)),
                  _resident((1, d))],
        out_specs=pl.BlockSpec((TM, d), lambda i: (i, 0)),
        out_shape=jax.ShapeDtypeStruct((n, d), F32),
        compiler_params=_cparams(1),
    )(xs, mod, g, wg, wu, wd, gf)


def _inproj_kernel(x_ref, mod_ref, g_ref, w_ref, qkv_ref, rw_ref, gate_ref, *, c_rw, c_gate):
    x = x_ref[...]
    h = (_rmsnorm(x, g_ref[...]) * (1.0 + mod_ref[0, 4:5, :]) + mod_ref[0, 3:4, :]).astype(BF16)
    qkv_ref[...] = _dot(h, w_ref[:, :c_rw]).astype(BF16)
    rw_ref[...] = _dot(h, w_ref[:, c_rw:c_gate])
    gate_ref[...] = _dot(h, w_ref[:, c_gate:])


def _inproj(xs, mod, g, w_in, *, c_rw, c_gate, tiles_per_seq, n_seq):
    n, d = xs.shape
    cols = w_in.shape[1]
    seq_of = lambda i: jnp.minimum(i // tiles_per_seq, n_seq)
    return pl.pallas_call(
        functools.partial(_inproj_kernel, c_rw=c_rw, c_gate=c_gate),
        grid=(n // TM_IN,),
        in_specs=[pl.BlockSpec((TM_IN, d), lambda i: (i, 0)),
                  pl.BlockSpec((1, N_MOD, d), lambda i: (seq_of(i), 0, 0)),
                  _resident((1, d)), _resident((d, cols))],
        out_specs=[pl.BlockSpec((TM_IN, c_rw), lambda i: (i, 0)),
                   pl.BlockSpec((TM_IN, c_gate - c_rw), lambda i: (i, 0)),
                   pl.BlockSpec((TM_IN, cols - c_gate), lambda i: (i, 0))],
        out_shape=[jax.ShapeDtypeStruct((n, c_rw), BF16),
                   jax.ShapeDtypeStruct((n, c_gate - c_rw), F32),
                   jax.ShapeDtypeStruct((n, cols - c_gate), F32)],
        compiler_params=_cparams(1),
    )(xs, mod, g, w_in)


NA_KEY_ROWS = 2 * NA_ROWS
NA_PAIRS = NA_KEY_ROWS // 2


def _na_pair_bias(rpb):
    w = GRID_W
    qc = np.arange(w)[:, None]
    kc = np.arange(w)[None, :]
    s_c = np.clip(qc - NA_WIN_W // 2, 0, w - NA_WIN_W)
    col_ok = (kc >= s_c) & (kc < s_c + NA_WIN_W)
    dc = np.clip(kc - qc + NA_WIN_W - 1, 0, 2 * NA_WIN_W - 2)
    b = jnp.where(col_ok[None, None], rpb[:, :, dc], -jnp.inf).astype(F32)
    ninf = jnp.full((rpb.shape[0], 1, w, w), -jnp.inf, F32)
    b = jnp.concatenate([ninf, b, ninf], axis=1)
    return jnp.concatenate([b[:, :-1], b[:, 1:]], axis=-1)


def _na_kernel(q_ref, k0_ref, k1_ref, k2_ref, k3_ref, kc_ref, v0_ref, v1_ref, v2_ref, v3_ref, vc_ref,
               bias_ref, o_ref, kcat, vcat, *, n_blocks):
    j = pl.program_id(2)
    blk = k0_ref.shape[0]
    for s, (kr, vr) in enumerate(((k0_ref, v0_ref), (k1_ref, v1_ref), (k2_ref, v2_ref), (k3_ref, v3_ref),
                                  (kc_ref, vc_ref))):
        kcat[s * blk:s * blk + kr.shape[0], :] = kr[...]
        vcat[s * blk:s * blk + vr.shape[0], :] = vr[...]
    n_loc = 4 * blk
    w = GRID_W
    half = NA_WIN_H // 2
    lane = lax.broadcasted_iota(jnp.int32, (1, LANES), 1)
    first = lane < NA_HEAD_DIM
    q = q_ref[...] * (NA_HEAD_DIM ** -0.5)
    k_all = kcat[...]
    v_all = vcat[...]
    outs = []
    for h in range(2):
        sel = first if h == 0 else jnp.logical_not(first)
        qh = jnp.where(sel, q, jnp.zeros_like(q))
        s_loc = lax.dot_general(qh, k_all[:n_loc], (((1,), (1,)), ((), ())), preferred_element_type=F32)
        s_ctx = lax.dot_general(qh, k_all[n_loc:], (((1,), (1,)), ((), ())), preferred_element_type=F32)
        p_rows = []
        l_rows = []
        for qi in range(NA_ROWS):
            lo = jnp.where(j == 0, max(qi, half), jnp.where(j == n_blocks - 1, min(qi, half), qi))
            m_lo = min(qi, half) // 2
            m_hi = (max(qi, half) + NA_WIN_H - 1) // 2
            n_m = m_hi - m_lo + 1
            key_row = 2 * m_lo + lax.broadcasted_iota(jnp.int32, (1, n_m * LANES), 1) // w
            seen = (key_row >= lo) & (key_row < lo + NA_WIN_H)
            bias = jnp.concatenate(
                [bias_ref[h, min(max(2 * m - qi + half, 0), 2 * NA_WIN_H - 1)] for m in range(m_lo, m_hi + 1)],
                axis=1)
            rows = slice(qi * w, (qi + 1) * w)
            s_q = jnp.where(seen, s_loc[rows, m_lo * LANES:(m_hi + 1) * LANES] + bias, -jnp.inf)
            s_c = s_ctx[rows]
            mx = jnp.maximum(jnp.max(s_q, axis=-1, keepdims=True), jnp.max(s_c, axis=-1, keepdims=True))
            p_q = jnp.exp(s_q - mx)
            p_c = jnp.exp(s_c - mx)
            l_rows.append(jnp.sum(p_q, axis=-1, keepdims=True) + jnp.sum(p_c, axis=-1, keepdims=True))
            pieces = []
            if m_lo > 0:
                pieces.append(jnp.zeros((w, m_lo * LANES), BF16))
            pieces.append(p_q.astype(BF16))
            if m_hi < NA_PAIRS - 1:
                pieces.append(jnp.zeros((w, (NA_PAIRS - 1 - m_hi) * LANES), BF16))
            pieces.append(p_c.astype(BF16))
            p_rows.append(jnp.concatenate(pieces, axis=1))
        p = jnp.concatenate(p_rows, axis=0)
        l = jnp.concatenate(l_rows, axis=0)
        outs.append(_dot(p, v_all) / l)
    o_ref[...] = jnp.where(first, outs[0], outs[1]).astype(o_ref.dtype)


def _na_attention(qkv, bias, *, batch, seq, ctx_len):
    rows = seq // GRID_W
    nj = rows // NA_ROWS
    tq = NA_ROWS * GRID_W
    tk = 4 * GRID_W
    width = NA_HEADS * NA_HEAD_DIM
    n_pairs = width // LANES
    kblocks = seq // tk
    ctx_blk0 = (batch * seq) // ctx_len

    def kv_spec(slot, col0):
        def imap(hp, b, j):
            return (b * kblocks + jnp.clip(2 * j - 1 + slot, 0, kblocks - 1), col0 + hp)
        return pl.BlockSpec((tk, LANES), imap)

    def ctx_spec(col0):
        return pl.BlockSpec((ctx_len, LANES), lambda hp, b, j: (ctx_blk0 + b, col0 + hp))

    in_specs = ([pl.BlockSpec((tq, LANES), lambda hp, b, j: (b * nj + j, hp))]
                + [kv_spec(s, n_pairs) for s in range(4)] + [ctx_spec(n_pairs)]
                + [kv_spec(s, 2 * n_pairs) for s in range(4)] + [ctx_spec(2 * n_pairs)]
                + [pl.BlockSpec((2,) + bias.shape[1:], lambda hp, b, j: (hp, 0, 0, 0))])
    return pl.pallas_call(
        functools.partial(_na_kernel, n_blocks=nj),
        grid=(n_pairs, batch, nj),
        in_specs=in_specs,
        out_specs=pl.BlockSpec((tq, LANES), lambda hp, b, j: (b * nj + j, hp)),
        out_shape=jax.ShapeDtypeStruct((batch * seq, width), BF16),
        scratch_shapes=[pltpu.VMEM((4 * tk + ctx_len, LANES), BF16),
                        pltpu.VMEM((4 * tk + ctx_len, LANES), BF16)],
        compiler_params=_cparams(3),
    )(*([qkv] * 11), bias)


def _head_segsum(x, parts):
    r = lax.broadcasted_iota(jnp.int32, (LANES, LANES), 0) // RW_HEAD_DIM
    c = lax.broadcasted_iota(jnp.int32, (LANES, LANES), 1) // RW_HEAD_DIM
    e = jnp.where(r == c, 1.0, 0.0).astype(BF16)
    cols = [_dot_split_lhs(x[:, j:j + LANES], e, parts) for j in range(0, x.shape[1], LANES)]
    return jnp.concatenate(cols, axis=1)


def _rwkv_kernel(p_ref, prev_ref, next_ref, mu_ref, par_ref, wup_ref, aup_ref, gup_ref, tri_ref,
                 strict_ref, incl_ref, y_ref, bon_ref, g_ref, z_s, *, reverse, n_ctx_tiles, n_x_tiles):
    n = pl.program_id(1)
    ta = p_ref.shape[0]
    width = y_ref.shape[2]
    n_pairs = width // LANES
    c_len = RW_CHUNK

    @pl.when(n == 0)
    def _():
        z_s[...] = jnp.zeros_like(z_s)

    is_ctx = n < n_ctx_tiles
    t_ctx = (n_ctx_tiles - 1 - n) if reverse else n
    t_x = (n_x_tiles - 1 - (n - n_ctx_tiles)) if reverse else (n - n_ctx_tiles)
    tile = jnp.where(is_ctx, t_ctx, t_x)
    last_tile = jnp.where(is_ctx, n_ctx_tiles - 1, n_x_tiles - 1)

    p = p_ref[...]
    row = lax.broadcasted_iota(jnp.int32, (ta, 1), 0)
    prev_row = jnp.where(tile == 0, 0.0, prev_ref[7:8, :])
    next_row = jnp.where(tile == last_tile, 0.0, next_ref[0:1, :])
    p_dn = jnp.where(row == 0, prev_row, pltpu.roll(p, 1, 0))
    p_up = jnp.where(row == ta - 1, next_row, pltpu.roll(p, ta - 1, 0))
    f = p + mu_ref[0:1, :] * (p_dn - p) + mu_ref[1:2, :] * (p_up - p)

    r = f[:, 0:width]
    k = f[:, width:2 * width]
    v = f[:, 2 * width:3 * width]
    o = 3 * width
    wl = f[:, o:o + LANES]
    al = f[:, o + LANES:o + 2 * LANES]
    gl = f[:, o + 2 * LANES:o + 3 * LANES]
    w0 = par_ref[0:1, :]
    a0 = par_ref[1:2, :]
    k_k = par_ref[2:3, :]
    k_a = par_ref[3:4, :]
    r_k = par_ref[4:5, :]

    z = w0 + _dot3(jnp.tanh(wl), wup_ref[...])
    lw = -math.exp(-0.5) * jax.nn.sigmoid(z)
    a = jax.nn.sigmoid(a0 + _dot(al.astype(BF16), aup_ref[...]))
    g_ref[...] = _dot(jax.nn.sigmoid(gl).astype(BF16), gup_ref[...])
    kk = k * k_k
    kk = kk * lax.rsqrt(jnp.maximum(_head_segsum(kk * kk, 2), 1e-12))
    kd = k * (1.0 + (a - 1.0) * k_a)
    bon_ref[0] = _head_segsum(r * kd * r_k, 2) * v
    bb = kk * a
    ci = _dot_split_rhs(tri_ref[...], lw, 3)

    n_chunks = ta // c_len
    n_b = n_chunks * n_pairs
    t_last = 0 if reverse else c_len - 1

    def chunks(t):
        return t.reshape(n_chunks, c_len, width)

    def to_batch(t):
        parts = [t[:, :, j * LANES:(j + 1) * LANES] for j in range(n_pairs)]
        return jnp.stack(parts, axis=1).reshape(n_b, t.shape[1], LANES)

    ci3 = chunks(ci)
    ce3 = ci3 - chunks(lw)
    cm = 0.5 * ci3[:, t_last:t_last + 1, :]
    em = jnp.exp(cm)
    e_k = jnp.exp(cm - ci3)
    kk_t = chunks(kk) * jnp.exp(ce3 - cm)
    r_t = chunks(r) * jnp.exp(ci3 - cm)
    b_t = chunks(bb) * e_k
    k_t = chunks(kd) * e_k
    kk_abs = to_batch(kk_t * em)
    r_abs = to_batch(r_t * em)
    k_bar = to_batch(k_t * em)
    b_bar = to_batch(b_t * em)
    gam = to_batch(em * em)
    kk_t = to_batch(kk_t)
    r_t = to_batch(r_t)
    b_t = to_batch(b_t)
    k_t = to_batch(k_t)
    v_b = to_batch(chunks(v))

    lane = lax.broadcasted_iota(jnp.int32, (1, 1, LANES), 2)
    first = lane < RW_HEAD_DIM
    ri = lax.broadcasted_iota(jnp.int32, (LANES, LANES), 0)
    cj = lax.broadcasted_iota(jnp.int32, (LANES, LANES), 1)
    eye = ri == cj
    blockdiag = (ri // RW_HEAD_DIM) == (cj // RW_HEAD_DIM)
    strict = strict_ref[...] > 0.0
    incl = incl_ref[...] > 0.0

    def stack_heads(t):
        zero = jnp.zeros_like(t)
        return jnp.concatenate([jnp.where(first, t, zero), jnp.where(first, zero, t)], axis=1)

    def bmm(x, y):
        return lax.dot_general(x.astype(BF16), y.astype(BF16), (((2,), (1,)), ((0,), (0,))),
                               preferred_element_type=F32)

    def bmm_nt(x, y):
        return lax.dot_general(x.astype(BF16), y.astype(BF16), (((2,), (2,)), ((0,), (0,))),
                               preferred_element_type=F32)

    def bmm_tn(x, y):
        return lax.dot_general(x.astype(BF16), y.astype(BF16), (((1,), (1,)), ((0,), (0,))),
                               preferred_element_type=F32)

    gm = bmm_nt(jnp.concatenate([kk_t, r_t], axis=1),
                jnp.concatenate([stack_heads(b_t), stack_heads(k_t)], axis=1))
    ab_w = jnp.where(strict, gm[:, :c_len, :LANES], 0.0)
    ak_w = jnp.where(strict, gm[:, :c_len, LANES:], 0.0)
    db_w = jnp.where(incl, gm[:, c_len:, :LANES], 0.0)
    dk_w = jnp.where(incl, gm[:, c_len:, LANES:], 0.0)

    l_bd = stack_heads(ab_w)
    t_bd = jnp.where(eye, 1.0, 0.0) - l_bd
    pw = bmm(l_bd, l_bd)
    span = 2
    while span < c_len:
        t_bd = t_bd + bmm(t_bd, pw)
        span *= 2
        if span < c_len:
            pw = bmm(pw, pw)
    t_w = t_bd[:, :c_len] + t_bd[:, c_len:]

    v_st = stack_heads(v_b)
    w1 = bmm(ak_w, v_st)
    pq = bmm(t_w, jnp.concatenate([stack_heads(kk_abs), stack_heads(w1)], axis=2))
    dpq = bmm(db_w, jnp.concatenate([stack_heads(pq[:, :, :LANES]), stack_heads(pq[:, :, LANES:])], axis=2))
    r_hat = r_abs - dpq[:, :, :LANES]
    y_loc = bmm(dk_w, v_st) - dpq[:, :, LANES:]
    bpq = bmm_tn(b_bar, pq)
    m_t = jnp.where(eye, jnp.broadcast_to(gam, (n_b, LANES, LANES)), 0.0) - bpq[:, :, :LANES]
    g_t = bmm_tn(k_bar, v_b) - bpq[:, :, LANES:]

    zs = z_s[...]
    order = range(n_chunks - 1, -1, -1) if reverse else range(n_chunks)
    for c in order:
        sl = slice(c * n_pairs, (c + 1) * n_pairs)
        y_c = bmm(r_hat[sl], zs) + y_loc[sl]
        y_ref[0, c * c_len:(c + 1) * c_len, :] = jnp.concatenate([y_c[j] for j in range(n_pairs)], axis=1)
        zh = zs.astype(BF16)
        zl = (zs - zh.astype(F32)).astype(BF16)
        m_c = m_t[sl]
        mh = m_c.astype(BF16)
        ml = (m_c - mh.astype(F32)).astype(BF16)
        zs = jnp.where(blockdiag, bmm(mh, zh) + (bmm(mh, zl) + bmm(ml, zh)) + g_t[sl], 0.0)
    z_s[...] = zs


def _rwkv_stream(p_rw, mu, par, wup, aup, gup, *, reverse, batch, seq, ctx_len):
    n_tok, cols = p_rw.shape
    width = RW_HEADS * RW_HEAD_DIM
    ta = RW_TILE
    nct = ctx_len // ta
    nxt = seq // ta
    n_x = batch * seq
    c_len = RW_CHUNK

    def blk(b, n):
        is_ctx = n < nct
        t_ctx = (nct - 1 - n) if reverse else n
        t_x = (nxt - 1 - (n - nct)) if reverse else (n - nct)
        return jnp.where(is_ctx, batch * nxt + b * nct + t_ctx, b * nxt + t_x)

    def xblk(b, n):
        m = jnp.maximum(n, nct) - nct
        return b * nxt + ((nxt - 1 - m) if reverse else m)

    sub = ta // 8
    last8 = n_tok // 8 - 1
    idx = np.arange(ta)
    same_chunk = (idx[:, None] // c_len) == (idx[None, :] // c_len)
    causal = (idx[None, :] >= idx[:, None]) if reverse else (idx[None, :] <= idx[:, None])
    tri = jnp.asarray(same_chunk & causal, BF16)
    t_i = np.arange(c_len)[:, None]
    s_i = np.arange(LANES)[None, :] % RW_HEAD_DIM
    strict = jnp.asarray((s_i > t_i) if reverse else (s_i < t_i), F32)
    incl = jnp.asarray((s_i >= t_i) if reverse else (s_i <= t_i), F32)

    out_block = pl.BlockSpec((1, ta, width), lambda b, n: (0, xblk(b, n), 0))
    return pl.pallas_call(
        functools.partial(_rwkv_kernel, reverse=reverse, n_ctx_tiles=nct, n_x_tiles=nxt),
        grid=(batch, nct + nxt),
        in_specs=[pl.BlockSpec((ta, cols), lambda b, n: (blk(b, n), 0)),
                  pl.BlockSpec((8, cols), lambda b, n: (jnp.maximum(blk(b, n) * sub - 1, 0), 0)),
                  pl.BlockSpec((8, cols), lambda b, n: (jnp.minimum((blk(b, n) + 1) * sub, last8), 0)),
                  _resident(mu.shape), _resident(par.shape), _resident(wup.shape), _resident(aup.shape),
                  _resident(gup.shape), _resident(tri.shape), _resident(strict.shape), _resident(incl.shape)],
        out_specs=[out_block, out_block, pl.BlockSpec((ta, width), lambda b, n: (xblk(b, n), 0))],
        out_shape=[jax.ShapeDtypeStruct((1, n_x, width), F32), jax.ShapeDtypeStruct((1, n_x, width), F32),
                   jax.ShapeDtypeStruct((n_x, width), F32)],
        scratch_shapes=[pltpu.VMEM((width // LANES, LANES, LANES), F32)],
        compiler_params=_cparams(2),
    )(p_rw, p_rw, p_rw, mu, par, wup, aup, gup, tri, strict, incl)


def _merge_kernel(x_ref, mod_ref, ona_ref, yf_ref, yb_ref, bf_ref, bb_ref, g_ref, pg_ref, bg_ref, ln_ref,
                  pna_ref, prw_ref, wout_ref, o_ref):
    d = x_ref.shape[1]
    y = yf_ref[0] + yb_ref[0]
    inv_n = 1.0 / RW_HEAD_DIM
    mean = _head_segsum(y, 2) * inv_n
    yc = y - mean
    var = _head_segsum(yc * yc, 2) * inv_n
    yn = yc * lax.rsqrt(var + RW_GN_EPS) * ln_ref[0:1, :] + ln_ref[1:2, :]
    o_rw = ((yn + bf_ref[0] + bb_ref[0]) * g_ref[...]).astype(BF16)
    gates = jax.nn.sigmoid(pg_ref[...] + bg_ref[...])
    m = gates[:, :d] * _dot(ona_ref[...], pna_ref[...]) + gates[:, d:] * _dot(o_rw, prw_ref[...])
    o_ref[...] = x_ref[...] + mod_ref[0, 5:6, :] * _dot(m.astype(BF16), wout_ref[...])


def _merge(x1, mod, o_na, y_f, y_b, bon_f, bon_b, g, p_gate, b_gate, ln, p_na, p_rw, w_out, *, tiles_per_seq):
    n_x, d = o_na.shape[0], x1.shape[1]
    width = o_na.shape[1]
    tok = lambda i: (i, 0)
    tok3 = lambda i: (0, i, 0)
    return pl.pallas_call(
        _merge_kernel,
        grid=(n_x // TM,),
        in_specs=[pl.BlockSpec((TM, d), tok),
                  pl.BlockSpec((1, N_MOD, d), lambda i: (i // tiles_per_seq, 0, 0)),
                  pl.BlockSpec((TM, width), tok),
                  pl.BlockSpec((1, TM, width), tok3), pl.BlockSpec((1, TM, width), tok3),
                  pl.BlockSpec((1, TM, width), tok3), pl.BlockSpec((1, TM, width), tok3),
                  pl.BlockSpec((TM, width), tok),
                  pl.BlockSpec((TM, 2 * d), tok),
                  _resident((1, 2 * d)), _resident((2, width)),
                  _resident(p_na.shape), _resident(p_rw.shape), _resident(w_out.shape)],
        out_specs=pl.BlockSpec((TM, d), tok),
        out_shape=jax.ShapeDtypeStruct((n_x, d), F32),
        compiler_params=_cparams(1),
    )(x1, mod, o_na, y_f, y_b, bon_f, bon_b, g, p_gate, b_gate, ln, p_na, p_rw, w_out)


def _pad_lora(w_up, direction):
    z = jnp.zeros_like(w_up[0])
    return jnp.concatenate([w_up[0] if direction == 0 else z, w_up[1] if direction == 1 else z], axis=0)


def kernel(x, c, ctx, c_ctx, w_ada, b_ada, norm_ffn1, norm_mix, norm_ffn2, norm_final, ffn1_wg, ffn1_wu, ffn1_wd, ffn2_wg, ffn2_wu, ffn2_wd, w_in, b_gate, na_rpb, rw_mu, rw_w0, rw_w_up, rw_a0, rw_a_up, rw_g_up, rw_k_k, rw_k_a, rw_r_k, rw_ln_w, rw_ln_b, p_na, p_rw, w_out):
    batch, seq, d = x.shape
    ctx_len = ctx.shape[1]
    n_x = batch * seq
    na_width = NA_HEADS * NA_HEAD_DIM
    rw_width = RW_HEADS * RW_HEAD_DIM
    c_rw = 3 * na_width
    c_gate = c_rw + 3 * rw_width + 4 * RW_LORA + RW_GATE_LORA
    assert w_ada.shape[0] == 1, "single layer"
    assert seq % TM == 0 and (batch * ctx_len) % TM == 0 and ctx_len % RW_TILE == 0 and batch + 1 <= 8
    assert seq % (NA_ROWS * GRID_W) == 0 and seq // (NA_ROWS * GRID_W) >= 2 and ctx_len == 4 * GRID_W
    row = lambda t: t.reshape(1, -1)

    cs = jnp.concatenate([c, c_ctx[None], jnp.zeros((8 - batch - 1, d), F32)], axis=0)
    mod = _ada_mod(cs, w_ada[0], b_ada[0])[:batch + 1].reshape(batch + 1, N_MOD, d)

    xs = jnp.concatenate([x.reshape(n_x, d), ctx.reshape(batch * ctx_len, d)], axis=0)
    bf = lambda t: t.astype(BF16)

    x1 = _ffn(xs, mod, row(norm_ffn1[0]), bf(ffn1_wg[0]), bf(ffn1_wu[0]), bf(ffn1_wd[0]), row(norm_final),
              mod0=0, final=False, tiles_per_seq=seq // TM, n_seq=batch)
    qkv, p_rwkv, p_gate = _inproj(x1, mod, row(norm_mix[0]), bf(w_in[0]), c_rw=c_rw, c_gate=c_gate,
                                  tiles_per_seq=seq // TM_IN, n_seq=batch)

    bias = _na_pair_bias(na_rpb[0])
    o_na = _na_attention(qkv, bias, batch=batch, seq=seq, ctx_len=ctx_len)

    streams = []
    for direction in (0, 1):
        par = jnp.concatenate([row(rw_w0[0, direction]), row(rw_a0[0, direction]), row(rw_k_k[0]),
                               row(rw_k_a[0]), row(rw_r_k[0]), jnp.zeros((3, rw_width), F32)], axis=0)
        streams.append(_rwkv_stream(p_rwkv, rw_mu[0], par, _pad_lora(rw_w_up[0], direction),
                                    bf(_pad_lora(rw_a_up[0], direction)), bf(rw_g_up[0]),
                                    reverse=direction == 1, batch=batch, seq=seq, ctx_len=ctx_len))
    (y_f, bon_f, g), (y_b, bon_b, _) = streams

    ln = jnp.stack([rw_ln_w[0], rw_ln_b[0]], axis=0)
    x2 = _merge(x1, mod, o_na, y_f, y_b, bon_f, bon_b, g, p_gate, row(b_gate[0]), ln,
                bf(p_na[0]), bf(p_rw[0]), bf(w_out[0]), tiles_per_seq=seq // TM)
    out = _ffn(x2, mod, row(norm_ffn2[0]), bf(ffn2_wg[0]), bf(ffn2_wu[0]), bf(ffn2_wd[0]), row(norm_final),
               mod0=6, final=True, tiles_per_seq=seq // TM, n_seq=batch)
    return out.reshape(batch, seq, d)
```

```python
import functools
import math

import numpy as np
import jax
import jax.numpy as jnp
from jax import lax
from jax.experimental import pallas as pl
from jax.experimental.pallas import tpu as pltpu

F32 = jnp.float32
BF16 = jnp.bfloat16

NORM_EPS = 1e-6
RW_GN_EPS = 64e-5
N_MOD = 9
GRID_W = 64
NA_HEADS = 8
NA_HEAD_DIM = 64
NA_WIN_H = 8
NA_WIN_W = 16
RW_HEADS = 8
RW_HEAD_DIM = 64
RW_LORA = 64
RW_GATE_LORA = 128

LANES = 128
VMEM_LIMIT = 56 * 1024 * 1024

TM = 512
TM_IN = 256
NA_ROWS = 8
RW_TILE = 256
RW_CHUNK = 64


def _cparams(n_axes):
    return pltpu.CompilerParams(dimension_semantics=("arbitrary",) * n_axes,
                                vmem_limit_bytes=VMEM_LIMIT)


def _resident(shape):
    nd = len(shape)
    return pl.BlockSpec(shape, lambda *_: (0,) * nd, pipeline_mode=pl.Buffered(1))


def _rmsnorm(x, g):
    return x * lax.rsqrt(jnp.mean(x * x, axis=-1, keepdims=True) + NORM_EPS) * g


def _split2(x):
    hi = x.astype(BF16)
    lo = (x - hi.astype(F32)).astype(BF16)
    return hi, lo


def _dot(a, b):
    return jnp.dot(a, b, preferred_element_type=F32)


def _dot_split_lhs(x, w_bf16, parts):
    acc = None
    rem = x
    for _ in range(parts):
        p = rem.astype(BF16)
        rem = rem - p.astype(F32)
        t = _dot(p, w_bf16)
        acc = t if acc is None else acc + t
    return acc


def _dot_split_rhs(w_bf16, x, parts):
    acc = None
    rem = x
    for _ in range(parts):
        p = rem.astype(BF16)
        rem = rem - p.astype(F32)
        t = _dot(w_bf16, p)
        acc = t if acc is None else acc + t
    return acc


def _dot3(a, b):
    ah, al = _split2(a)
    bh, bl = _split2(b)
    return _dot(ah, bh) + (_dot(ah, bl) + _dot(al, bh))


def _ada_kernel(c_ref, w_ref, b_ref, o_ref):
    c = c_ref[...]
    s = c * jax.nn.sigmoid(c)
    o_ref[...] = _dot3(s, w_ref[...]) + b_ref[...]


def _ada_mod(cs, w_ada, b_ada):
    d = cs.shape[1]
    nm = w_ada.shape[1] // d
    return pl.pallas_call(
        _ada_kernel,
        grid=(nm,),
        in_specs=[pl.BlockSpec((8, d), lambda j: (0, 0)),
                  pl.BlockSpec((d, d), lambda j: (0, j)),
                  pl.BlockSpec((1, d), lambda j: (0, j))],
        out_specs=pl.BlockSpec((8, d), lambda j: (0, j)),
        out_shape=jax.ShapeDtypeStruct((8, nm * d), F32),
        compiler_params=_cparams(1),
    )(cs, w_ada, b_ada.reshape(1, -1))


def _ffn_kernel(*refs, mod0, f_chunk, final, n_head_tiles):
    if n_head_tiles is None:
        x_ref, mod_ref, g_ref, wg_ref, wu_ref, wd_ref, gf_ref, o_ref = refs
        x = x_ref[...]
    else:
        x_ref, tail_ref, mod_ref, g_ref, wg_ref, wu_ref, wd_ref, gf_ref, o_ref = refs
        x = jnp.where(pl.program_id(0) < n_head_tiles, x_ref[...], tail_ref[...])
    shift = mod_ref[0, mod0:mod0 + 1, :]
    scale = mod_ref[0, mod0 + 1:mod0 + 2, :]
    gate = mod_ref[0, mod0 + 2:mod0 + 3, :]
    h = (_rmsnorm(x, g_ref[...]) * (1.0 + scale) + shift).astype(BF16)
    d_ff = wg_ref.shape[1]
    acc = None
    for f0 in range(0, d_ff, f_chunk):
        gg = _dot(h, wg_ref[:, f0:f0 + f_chunk])
        uu = _dot(h, wu_ref[:, f0:f0 + f_chunk])
        a = (gg * jax.nn.sigmoid(gg) * uu).astype(BF16)
        t = _dot(a, wd_ref[f0:f0 + f_chunk, :])
        acc = t if acc is None else acc + t
    y = x + 0.5 * gate * acc
    if final:
        y = _rmsnorm(y, gf_ref[...])
    o_ref[...] = y


def _ffn(xs, tail, mod, g, wg, wu, wd, gf, *, mod0, final, tiles_per_seq, n_seq):
    n_head, d = xs.shape
    n = n_head + (0 if tail is None else tail.shape[0])
    d_ff = wg.shape[1]
    f_chunk = d_ff // 2 if (d_ff // 2) % LANES == 0 else d_ff
    seq_of = lambda i: jnp.minimum(i // tiles_per_seq, n_seq)
    head_tiles = n_head // TM
    tokens = [pl.BlockSpec((TM, d), lambda i: (jnp.minimum(i, head_tiles - 1), 0))]
    if tail is not None:
        tokens.append(pl.BlockSpec((TM, d), lambda i: (jnp.maximum(i - head_tiles, 0), 0)))
    return pl.pallas_call(
        functools.partial(_ffn_kernel, mod0=mod0, f_chunk=f_chunk, final=final,
                          n_head_tiles=None if tail is None else head_tiles),
        grid=(n // TM,),
        in_specs=tokens + [pl.BlockSpec((1, N_MOD, d), lambda i: (seq_of(i), 0, 0)),
                           _resident((1, d)), _resident((d, d_ff)), _resident((d, d_ff)), _resident((d_ff, d)),
                           _resident((1, d))],
        out_specs=pl.BlockSpec((TM, d), lambda i: (i, 0)),
        out_shape=jax.ShapeDtypeStruct((n, d), F32),
        compiler_params=_cparams(1),
    )(*([xs] if tail is None else [xs, tail]), mod, g, wg, wu, wd, gf)


def _inproj_kernel(x_ref, mod_ref, g_ref, w_ref, qkv_ref, rw_ref, gate_ref, *, c_rw, c_gate):
    x = x_ref[...]
    h = (_rmsnorm(x, g_ref[...]) * (1.0 + mod_ref[0, 4:5, :]) + mod_ref[0, 3:4, :]).astype(BF16)
    qkv_ref[...] = _dot(h, w_ref[:, :c_rw]).astype(BF16)
    rw_ref[...] = _dot(h, w_ref[:, c_rw:c_gate])
    gate_ref[...] = _dot(h, w_ref[:, c_gate:])


def _inproj(xs, mod, g, w_in, *, c_rw, c_gate, tiles_per_seq, n_seq):
    n, d = xs.shape
    cols = w_in.shape[1]
    seq_of = lambda i: jnp.minimum(i // tiles_per_seq, n_seq)
    return pl.pallas_call(
        functools.partial(_inproj_kernel, c_rw=c_rw, c_gate=c_gate),
        grid=(n // TM_IN,),
        in_specs=[pl.BlockSpec((TM_IN, d), lambda i: (i, 0)),
                  pl.BlockSpec((1, N_MOD, d), lambda i: (seq_of(i), 0, 0)),
                  _resident((1, d)), _resident((d, cols))],
        out_specs=[pl.BlockSpec((TM_IN, c_rw), lambda i: (i, 0)),
                   pl.BlockSpec((TM_IN, c_gate - c_rw), lambda i: (i, 0)),
                   pl.BlockSpec((TM_IN, cols - c_gate), lambda i: (i, 0))],
        out_shape=[jax.ShapeDtypeStruct((n, c_rw), BF16),
                   jax.ShapeDtypeStruct((n, c_gate - c_rw), F32),
                   jax.ShapeDtypeStruct((n, cols - c_gate), F32)],
        compiler_params=_cparams(1),
    )(xs, mod, g, w_in)


NA_KEY_ROWS = 2 * NA_ROWS
NA_PAIRS = NA_KEY_ROWS // 2


def _na_pair_bias(rpb):
    w = GRID_W
    qc = np.arange(w)[:, None]
    kc = np.arange(w)[None, :]
    s_c = np.clip(qc - NA_WIN_W // 2, 0, w - NA_WIN_W)
    col_ok = (kc >= s_c) & (kc < s_c + NA_WIN_W)
    dc = np.clip(kc - qc + NA_WIN_W - 1, 0, 2 * NA_WIN_W - 2)
    b = jnp.where(col_ok[None, None], rpb[:, :, dc], -jnp.inf).astype(F32)
    ninf = jnp.full((rpb.shape[0], 1, w, w), -jnp.inf, F32)
    b = jnp.concatenate([ninf, b, ninf], axis=1)
    return jnp.concatenate([b[:, :-1], b[:, 1:]], axis=-1)


def _na_kernel(q_ref, k0_ref, k1_ref, k2_ref, k3_ref, kc_ref, v0_ref, v1_ref, v2_ref, v3_ref, vc_ref,
               bias_ref, o_ref, kcat, vcat, *, n_blocks):
    j = pl.program_id(2)
    blk = k0_ref.shape[0]
    for s, (kr, vr) in enumerate(((k0_ref, v0_ref), (k1_ref, v1_ref), (k2_ref, v2_ref), (k3_ref, v3_ref),
                                  (kc_ref, vc_ref))):
        kcat[s * blk:s * blk + kr.shape[0], :] = kr[...]
        vcat[s * blk:s * blk + vr.shape[0], :] = vr[...]
    n_loc = 4 * blk
    w = GRID_W
    half = NA_WIN_H // 2
    lane = lax.broadcasted_iota(jnp.int32, (1, LANES), 1)
    first = lane < NA_HEAD_DIM
    q = q_ref[...] * (NA_HEAD_DIM ** -0.5)
    k_all = kcat[...]
    v_all = vcat[...]
    outs = []
    for h in range(2):
        sel = first if h == 0 else jnp.logical_not(first)
        qh = jnp.where(sel, q, jnp.zeros_like(q))
        s_loc = lax.dot_general(qh, k_all[:n_loc], (((1,), (1,)), ((), ())), preferred_element_type=F32)
        s_ctx = lax.dot_general(qh, k_all[n_loc:], (((1,), (1,)), ((), ())), preferred_element_type=F32)
        p_rows = []
        l_rows = []
        for qi in range(NA_ROWS):
            lo = jnp.where(j == 0, max(qi, half), jnp.where(j == n_blocks - 1, min(qi, half), qi))
            m_lo = min(qi, half) // 2
            m_hi = (max(qi, half) + NA_WIN_H - 1) // 2
            n_m = m_hi - m_lo + 1
            key_row = 2 * m_lo + lax.broadcasted_iota(jnp.int32, (1, n_m * LANES), 1) // w
            seen = (key_row >= lo) & (key_row < lo + NA_WIN_H)
            bias = jnp.concatenate(
                [bias_ref[h, min(max(2 * m - qi + half, 0), 2 * NA_WIN_H - 1)] for m in range(m_lo, m_hi + 1)],
                axis=1)
            rows = slice(qi * w, (qi + 1) * w)
            s_q = jnp.where(seen, s_loc[rows, m_lo * LANES:(m_hi + 1) * LANES] + bias, -jnp.inf)
            s_c = s_ctx[rows]
            mx = jnp.maximum(jnp.max(s_q, axis=-1, keepdims=True), jnp.max(s_c, axis=-1, keepdims=True))
            p_q = jnp.exp(s_q - mx)
            p_c = jnp.exp(s_c - mx)
            l_rows.append(jnp.sum(p_q, axis=-1, keepdims=True) + jnp.sum(p_c, axis=-1, keepdims=True))
            pieces = []
            if m_lo > 0:
                pieces.append(jnp.zeros((w, m_lo * LANES), BF16))
            pieces.append(p_q.astype(BF16))
            if m_hi < NA_PAIRS - 1:
                pieces.append(jnp.zeros((w, (NA_PAIRS - 1 - m_hi) * LANES), BF16))
            pieces.append(p_c.astype(BF16))
            p_rows.append(jnp.concatenate(pieces, axis=1))
        p = jnp.concatenate(p_rows, axis=0)
        l = jnp.concatenate(l_rows, axis=0)
        outs.append(_dot(p, v_all) / l)
    o_ref[...] = jnp.where(first, outs[0], outs[1]).astype(o_ref.dtype)


def _na_attention(qkv, bias, *, batch, seq, ctx_len):
    rows = seq // GRID_W
    nj = rows // NA_ROWS
    tq = NA_ROWS * GRID_W
    tk = 4 * GRID_W
    width = NA_HEADS * NA_HEAD_DIM
    n_pairs = width // LANES
    kblocks = seq // tk
    ctx_blk0 = (batch * seq) // ctx_len

    def kv_spec(slot, col0):
        def imap(hp, b, j):
            return (b * kblocks + jnp.clip(2 * j - 1 + slot, 0, kblocks - 1), col0 + hp)
        return pl.BlockSpec((tk, LANES), imap)

    def ctx_spec(col0):
        return pl.BlockSpec((ctx_len, LANES), lambda hp, b, j: (ctx_blk0 + b, col0 + hp))

    in_specs = ([pl.BlockSpec((tq, LANES), lambda hp, b, j: (b * nj + j, hp))]
                + [kv_spec(s, n_pairs) for s in range(4)] + [ctx_spec(n_pairs)]
                + [kv_spec(s, 2 * n_pairs) for s in range(4)] + [ctx_spec(2 * n_pairs)]
                + [pl.BlockSpec((2,) + bias.shape[1:], lambda hp, b, j: (hp, 0, 0, 0))])
    return pl.pallas_call(
        functools.partial(_na_kernel, n_blocks=nj),
        grid=(n_pairs, batch, nj),
        in_specs=in_specs,
        out_specs=pl.BlockSpec((tq, LANES), lambda hp, b, j: (b * nj + j, hp)),
        out_shape=jax.ShapeDtypeStruct((batch * seq, width), BF16),
        scratch_shapes=[pltpu.VMEM((4 * tk + ctx_len, LANES), BF16),
                        pltpu.VMEM((4 * tk + ctx_len, LANES), BF16)],
        compiler_params=_cparams(3),
    )(*([qkv] * 11), bias)


def _head_segsum(x, parts):
    r = lax.broadcasted_iota(jnp.int32, (LANES, LANES), 0) // RW_HEAD_DIM
    c = lax.broadcasted_iota(jnp.int32, (LANES, LANES), 1) // RW_HEAD_DIM
    e = jnp.where(r == c, 1.0, 0.0).astype(BF16)
    cols = [_dot_split_lhs(x[:, j:j + LANES], e, parts) for j in range(0, x.shape[1], LANES)]
    return jnp.concatenate(cols, axis=1)


def _rwkv_prep_kernel(p_ref, prev_ref, next_ref, mu_ref, par_ref, wup_ref, aup_ref, gup_ref, tri_ref,
                      kk_ref, r_ref, b_ref, k_ref, em_ref, v_ref, g_ref, bon_ref, *, n_x_tiles, tiles_x, tiles_ctx):
    i = pl.program_id(0)
    ta = p_ref.shape[0]
    width = v_ref.shape[1]
    c_len = RW_CHUNK
    n_chunks = ta // c_len
    in_x = i < n_x_tiles
    tile = jnp.where(in_x, i % tiles_x, (i - n_x_tiles) % tiles_ctx)
    last_tile = jnp.where(in_x, tiles_x - 1, tiles_ctx - 1)

    p = p_ref[...]
    row = lax.broadcasted_iota(jnp.int32, (ta, 1), 0)
    prev_row = jnp.where(tile == 0, 0.0, prev_ref[7:8, :])
    next_row = jnp.where(tile == last_tile, 0.0, next_ref[0:1, :])
    p_dn = jnp.where(row == 0, prev_row, pltpu.roll(p, 1, 0))
    p_up = jnp.where(row == ta - 1, next_row, pltpu.roll(p, ta - 1, 0))
    f = p + mu_ref[0:1, :] * (p_dn - p) + mu_ref[1:2, :] * (p_up - p)

    r = f[:, 0:width]
    k = f[:, width:2 * width]
    v = f[:, 2 * width:3 * width]
    o = 3 * width
    th = jnp.tanh(f[:, o:o + LANES])
    al = f[:, o + LANES:o + 2 * LANES].astype(BF16)
    gl = f[:, o + 2 * LANES:o + 3 * LANES]
    k_k = par_ref[4:5, :]
    k_a = par_ref[5:6, :]
    r_k = par_ref[6:7, :]

    v_ref[...] = v.astype(BF16)
    g_ref[...] = _dot(jax.nn.sigmoid(gl).astype(BF16), gup_ref[...])
    kk = k * k_k
    kk = kk * lax.rsqrt(jnp.maximum(_head_segsum(kk * kk, 2), 1e-12))

    def chunks(t):
        return t.reshape(n_chunks, c_len, width)

    kd_sum = None
    for d in range(2):
        z = par_ref[d:d + 1, :] + _dot3(th, wup_ref[d])
        lw = -math.exp(-0.5) * jax.nn.sigmoid(z)
        a = jax.nn.sigmoid(par_ref[2 + d:3 + d, :] + _dot(al, aup_ref[d]))
        kd = k * (1.0 + (a - 1.0) * k_a)
        kd_sum = kd if kd_sum is None else kd_sum + kd
        bb = kk * a
        ci3 = chunks(_dot_split_rhs(tri_ref[d], lw, 3))
        ce3 = ci3 - chunks(lw)
        t_last = 0 if d == 1 else c_len - 1
        cm = 0.5 * ci3[:, t_last:t_last + 1, :]
        e_k = jnp.exp(cm - ci3)
        kk_ref[d] = (chunks(kk) * jnp.exp(ce3 - cm)).reshape(ta, width).astype(BF16)
        r_ref[d] = (chunks(r) * jnp.exp(ci3 - cm)).reshape(ta, width).astype(BF16)
        b_ref[d] = (chunks(bb) * e_k).reshape(ta, width).astype(BF16)
        k_ref[d] = (chunks(kd) * e_k).reshape(ta, width).astype(BF16)
        em_ref[d, 0] = jnp.exp(cm).reshape(n_chunks, width)
    bon_ref[...] = _head_segsum(r * kd_sum * r_k, 2) * v


def _rwkv_prep(p_rw, mu, par, wup, aup, gup, *, batch, seq, ctx_len):
    n_tok, cols = p_rw.shape
    width = RW_HEADS * RW_HEAD_DIM
    ta = RW_TILE
    c_len = RW_CHUNK
    n_tiles = n_tok // ta
    sub = ta // 8
    last8 = n_tok // 8 - 1
    idx = np.arange(ta)
    same_chunk = (idx[:, None] // c_len) == (idx[None, :] // c_len)
    tri = jnp.asarray(np.stack([same_chunk & (idx[None, :] <= idx[:, None]),
                                same_chunk & (idx[None, :] >= idx[:, None])]), BF16)
    tok2 = pl.BlockSpec((2, ta, width), lambda i: (0, i, 0))
    tok = pl.BlockSpec((ta, width), lambda i: (i, 0))
    sds = jax.ShapeDtypeStruct
    return pl.pallas_call(
        functools.partial(_rwkv_prep_kernel, n_x_tiles=batch * seq // ta, tiles_x=seq // ta,
                          tiles_ctx=ctx_len // ta),
        grid=(n_tiles,),
        in_specs=[pl.BlockSpec((ta, cols), lambda i: (i, 0)),
                  pl.BlockSpec((8, cols), lambda i: (jnp.maximum(i * sub - 1, 0), 0)),
                  pl.BlockSpec((8, cols), lambda i: (jnp.minimum((i + 1) * sub, last8), 0)),
                  _resident(mu.shape), _resident(par.shape), _resident(wup.shape), _resident(aup.shape),
                  _resident(gup.shape), _resident(tri.shape)],
        out_specs=[tok2, tok2, tok2, tok2,
                   pl.BlockSpec((2, 1, ta // c_len, width), lambda i: (0, i, 0, 0)),
                   tok, tok, tok],
        out_shape=[sds((2, n_tok, width), BF16)] * 4
                  + [sds((2, n_tiles, ta // c_len, width), F32), sds((n_tok, width), BF16),
                     sds((n_tok, width), F32), sds((n_tok, width), F32)],
        compiler_params=_cparams(1),
    )(p_rw, p_rw, p_rw, mu, par, wup, aup, gup, tri)


def _rwkv_kernel(kk_ref, r_ref, b_ref, k_ref, em_ref, v_ref, strict_ref, incl_ref, y_ref, z_s, *, reverse):
    n = pl.program_id(1)
    ta = v_ref.shape[0]
    width = v_ref.shape[1]
    n_pairs = width // LANES
    c_len = RW_CHUNK
    n_chunks = ta // c_len
    n_b = n_chunks * n_pairs

    @pl.when(n == 0)
    def _():
        z_s[...] = jnp.zeros_like(z_s)

    def chunks(t):
        return t.reshape(n_chunks, c_len, width)

    def to_batch(t):
        parts = [t[:, :, j * LANES:(j + 1) * LANES] for j in range(n_pairs)]
        return jnp.stack(parts, axis=1).reshape(n_b, t.shape[1], LANES)

    em = em_ref[0, 0].reshape(n_chunks, 1, width)
    kk_t = chunks(kk_ref[0].astype(F32))
    r_t = chunks(r_ref[0].astype(F32))
    b_t = chunks(b_ref[0].astype(F32))
    k_t = chunks(k_ref[0].astype(F32))
    kk_abs = to_batch(kk_t * em)
    r_abs = to_batch(r_t * em)
    k_bar = to_batch(k_t * em)
    b_bar = to_batch(b_t * em)
    gam = to_batch(em * em)
    kk_t = to_batch(kk_t)
    r_t = to_batch(r_t)
    b_t = to_batch(b_t)
    k_t = to_batch(k_t)
    v_b = to_batch(chunks(v_ref[...].astype(F32)))

    lane = lax.broadcasted_iota(jnp.int32, (1, 1, LANES), 2)
    first = lane < RW_HEAD_DIM
    ri = lax.broadcasted_iota(jnp.int32, (LANES, LANES), 0)
    cj = lax.broadcasted_iota(jnp.int32, (LANES, LANES), 1)
    eye = ri == cj
    blockdiag = (ri // RW_HEAD_DIM) == (cj // RW_HEAD_DIM)
    strict = strict_ref[...] > 0.0
    incl = incl_ref[...] > 0.0

    def stack_heads(t):
        zero = jnp.zeros_like(t)
        return jnp.concatenate([jnp.where(first, t, zero), jnp.where(first, zero, t)], axis=1)

    def bmm(x, y):
        return lax.dot_general(x.astype(BF16), y.astype(BF16), (((2,), (1,)), ((0,), (0,))),
                               preferred_element_type=F32)

    def bmm_nt(x, y):
        return lax.dot_general(x.astype(BF16), y.astype(BF16), (((2,), (2,)), ((0,), (0,))),
                               preferred_element_type=F32)

    def bmm_tn(x, y):
        return lax.dot_general(x.astype(BF16), y.astype(BF16), (((1,), (1,)), ((0,), (0,))),
                               preferred_element_type=F32)

    gm = bmm_nt(jnp.concatenate([kk_t, r_t], axis=1),
                jnp.concatenate([stack_heads(b_t), stack_heads(k_t)], axis=1))
    ab_w = jnp.where(strict, gm[:, :c_len, :LANES], 0.0)
    ak_w = jnp.where(strict, gm[:, :c_len, LANES:], 0.0)
    db_w = jnp.where(incl, gm[:, c_len:, :LANES], 0.0)
    dk_w = jnp.where(incl, gm[:, c_len:, LANES:], 0.0)

    l_bd = stack_heads(ab_w)
    t_bd = jnp.where(eye, 1.0, 0.0) - l_bd
    pw = bmm(l_bd, l_bd)
    span = 2
    while span < c_len:
        t_bd = t_bd + bmm(t_bd, pw)
        span *= 2
        if span < c_len:
            pw = bmm(pw, pw)
    t_w = t_bd[:, :c_len] + t_bd[:, c_len:]

    v_st = stack_heads(v_b)
    w1 = bmm(ak_w, v_st)
    pq = bmm(t_w, jnp.concatenate([stack_heads(kk_abs), stack_heads(w1)], axis=2))
    dpq = bmm(db_w, jnp.concatenate([stack_heads(pq[:, :, :LANES]), stack_heads(pq[:, :, LANES:])], axis=2))
    r_hat = r_abs - dpq[:, :, :LANES]
    y_loc = bmm(dk_w, v_st) - dpq[:, :, LANES:]
    bpq = bmm_tn(b_bar, pq)
    m_t = jnp.where(eye, jnp.broadcast_to(gam, (n_b, LANES, LANES)), 0.0) - bpq[:, :, :LANES]
    g_t = bmm_tn(k_bar, v_b) - bpq[:, :, LANES:]

    zs = z_s[...]
    order = range(n_chunks - 1, -1, -1) if reverse else range(n_chunks)
    for c in order:
        sl = slice(c * n_pairs, (c + 1) * n_pairs)
        y_c = bmm(r_hat[sl], zs) + y_loc[sl]
        y_ref[0, c * c_len:(c + 1) * c_len, :] = jnp.concatenate([y_c[j] for j in range(n_pairs)], axis=1)
        zh = zs.astype(BF16)
        zl = (zs - zh.astype(F32)).astype(BF16)
        m_c = m_t[sl]
        mh = m_c.astype(BF16)
        ml = (m_c - mh.astype(F32)).astype(BF16)
        zs = jnp.where(blockdiag, bmm(mh, zh) + (bmm(mh, zl) + bmm(ml, zh)) + g_t[sl], 0.0)
    z_s[...] = zs


def _rwkv_stream(kk_t, r_t, b_t, k_t, em, v, *, reverse, batch, seq, ctx_len):
    width = RW_HEADS * RW_HEAD_DIM
    ta = RW_TILE
    nct = ctx_len // ta
    nxt = seq // ta
    n_x = batch * seq
    c_len = RW_CHUNK
    d = 1 if reverse else 0

    def blk(b, n):
        is_ctx = n < nct
        t_ctx = (nct - 1 - n) if reverse else n
        t_x = (nxt - 1 - (n - nct)) if reverse else (n - nct)
        return jnp.where(is_ctx, batch * nxt + b * nct + t_ctx, b * nxt + t_x)

    def xblk(b, n):
        m = jnp.maximum(n, nct) - nct
        return b * nxt + ((nxt - 1 - m) if reverse else m)

    t_i = np.arange(c_len)[:, None]
    s_i = np.arange(LANES)[None, :] % RW_HEAD_DIM
    strict = jnp.asarray((s_i > t_i) if reverse else (s_i < t_i), F32)
    incl = jnp.asarray((s_i >= t_i) if reverse else (s_i <= t_i), F32)

    operand = pl.BlockSpec((1, ta, width), lambda b, n: (d, blk(b, n), 0))
    return pl.pallas_call(
        functools.partial(_rwkv_kernel, reverse=reverse),
        grid=(batch, nct + nxt),
        in_specs=[operand, operand, operand, operand,
                  pl.BlockSpec((1, 1, ta // c_len, width), lambda b, n: (d, blk(b, n), 0, 0)),
                  pl.BlockSpec((ta, width), lambda b, n: (blk(b, n), 0)),
                  _resident(strict.shape), _resident(incl.shape)],
        out_specs=pl.BlockSpec((1, ta, width), lambda b, n: (0, xblk(b, n), 0)),
        out_shape=jax.ShapeDtypeStruct((1, n_x, width), F32),
        scratch_shapes=[pltpu.VMEM((width // LANES, LANES, LANES), F32)],
        compiler_params=_cparams(2),
    )(kk_t, r_t, b_t, k_t, em, v, strict, incl)


def _merge_kernel(x_ref, mod_ref, ona_ref, yf_ref, yb_ref, bon_ref, g_ref, pg_ref, bg_ref, ln_ref,
                  pna_ref, prw_ref, wout_ref, o_ref):
    d = x_ref.shape[1]
    y = yf_ref[0] + yb_ref[0]
    inv_n = 1.0 / RW_HEAD_DIM
    mean = _head_segsum(y, 2) * inv_n
    yc = y - mean
    var = _head_segsum(yc * yc, 2) * inv_n
    yn = yc * lax.rsqrt(var + RW_GN_EPS) * ln_ref[0:1, :] + ln_ref[1:2, :]
    o_rw = ((yn + bon_ref[...]) * g_ref[...]).astype(BF16)
    gates = jax.nn.sigmoid(pg_ref[...] + bg_ref[...])
    m = gates[:, :d] * _dot(ona_ref[...], pna_ref[...]) + gates[:, d:] * _dot(o_rw, prw_ref[...])
    o_ref[...] = x_ref[...] + mod_ref[0, 5:6, :] * _dot(m.astype(BF16), wout_ref[...])


def _merge(x1, mod, o_na, y_f, y_b, bon, g, p_gate, b_gate, ln, p_na, p_rw, w_out, *, tiles_per_seq):
    n_x, d = o_na.shape[0], x1.shape[1]
    width = o_na.shape[1]
    tok = lambda i: (i, 0)
    tok3 = lambda i: (0, i, 0)
    return pl.pallas_call(
        _merge_kernel,
        grid=(n_x // TM,),
        in_specs=[pl.BlockSpec((TM, d), tok),
                  pl.BlockSpec((1, N_MOD, d), lambda i: (i // tiles_per_seq, 0, 0)),
                  pl.BlockSpec((TM, width), tok),
                  pl.BlockSpec((1, TM, width), tok3), pl.BlockSpec((1, TM, width), tok3),
                  pl.BlockSpec((TM, width), tok), pl.BlockSpec((TM, width), tok),
                  pl.BlockSpec((TM, 2 * d), tok),
                  _resident((1, 2 * d)), _resident((2, width)),
                  _resident(p_na.shape), _resident(p_rw.shape), _resident(w_out.shape)],
        out_specs=pl.BlockSpec((TM, d), tok),
        out_shape=jax.ShapeDtypeStruct((n_x, d), F32),
        compiler_params=_cparams(1),
    )(x1, mod, o_na, y_f, y_b, bon, g, p_gate, b_gate, ln, p_na, p_rw, w_out)


def _pad_lora(w_up, direction):
    z = jnp.zeros_like(w_up[0])
    return jnp.concatenate([w_up[0] if direction == 0 else z, w_up[1] if direction == 1 else z], axis=0)


def kernel(x, c, ctx, c_ctx, w_ada, b_ada, norm_ffn1, norm_mix, norm_ffn2, norm_final, ffn1_wg, ffn1_wu, ffn1_wd, ffn2_wg, ffn2_wu, ffn2_wd, w_in, b_gate, na_rpb, rw_mu, rw_w0, rw_w_up, rw_a0, rw_a_up, rw_g_up, rw_k_k, rw_k_a, rw_r_k, rw_ln_w, rw_ln_b, p_na, p_rw, w_out):
    batch, seq, d = x.shape
    ctx_len = ctx.shape[1]
    n_x = batch * seq
    na_width = NA_HEADS * NA_HEAD_DIM
    rw_width = RW_HEADS * RW_HEAD_DIM
    c_rw = 3 * na_width
    c_gate = c_rw + 3 * rw_width + 4 * RW_LORA + RW_GATE_LORA
    assert w_ada.shape[0] == 1, "single layer"
    assert seq % TM == 0 and (batch * ctx_len) % TM == 0 and ctx_len % RW_TILE == 0 and batch + 1 <= 8
    assert seq % (NA_ROWS * GRID_W) == 0 and seq // (NA_ROWS * GRID_W) >= 2 and ctx_len == 4 * GRID_W
    row = lambda t: t.reshape(1, -1)

    cs = jnp.concatenate([c, c_ctx[None], jnp.zeros((8 - batch - 1, d), F32)], axis=0)
    mod = _ada_mod(cs, w_ada[0], b_ada[0])[:batch + 1].reshape(batch + 1, N_MOD, d)

    bf = lambda t: t.astype(BF16)

    x1 = _ffn(x.reshape(n_x, d), ctx.reshape(batch * ctx_len, d), mod, row(norm_ffn1[0]),
              bf(ffn1_wg[0]), bf(ffn1_wu[0]), bf(ffn1_wd[0]), row(norm_final),
              mod0=0, final=False, tiles_per_seq=seq // TM, n_seq=batch)
    qkv, p_rwkv, p_gate = _inproj(x1, mod, row(norm_mix[0]), bf(w_in[0]), c_rw=c_rw, c_gate=c_gate,
                                  tiles_per_seq=seq // TM_IN, n_seq=batch)

    bias = _na_pair_bias(na_rpb[0])
    o_na = _na_attention(qkv, bias, batch=batch, seq=seq, ctx_len=ctx_len)

    par = jnp.concatenate([rw_w0[0], rw_a0[0], row(rw_k_k[0]), row(rw_k_a[0]), row(rw_r_k[0]),
                           jnp.zeros((1, rw_width), F32)], axis=0)
    wup = jnp.stack([_pad_lora(rw_w_up[0], 0), _pad_lora(rw_w_up[0], 1)])
    aup = bf(jnp.stack([_pad_lora(rw_a_up[0], 0), _pad_lora(rw_a_up[0], 1)]))
    kk_t, r_t, b_t, k_t, em, v_rw, g, bon = _rwkv_prep(p_rwkv, rw_mu[0], par, wup, aup, bf(rw_g_up[0]),
                                                       batch=batch, seq=seq, ctx_len=ctx_len)
    y_f, y_b = [_rwkv_stream(kk_t, r_t, b_t, k_t, em, v_rw, reverse=rev, batch=batch, seq=seq, ctx_len=ctx_len)
                for rev in (False, True)]

    ln = jnp.stack([rw_ln_w[0], rw_ln_b[0]], axis=0)
    x2 = _merge(x1, mod, o_na, y_f, y_b, bon, g, p_gate, row(b_gate[0]), ln,
                bf(p_na[0]), bf(p_rw[0]), bf(w_out[0]), tiles_per_seq=seq // TM)
    out = _ffn(x2, None, mod, row(norm_ffn2[0]), bf(ffn2_wg[0]), bf(ffn2_wu[0]), bf(ffn2_wd[0]), row(norm_final),
               mod0=6, final=True, tiles_per_seq=seq // TM, n_seq=batch)
    return out.reshape(batch, seq, d)
```

```python
import functools
import math

import numpy as np
import jax
import jax.numpy as jnp
from jax import lax
from jax.experimental import pallas as pl
from jax.experimental.pallas import tpu as pltpu

F32 = jnp.float32
BF16 = jnp.bfloat16

NORM_EPS = 1e-6
RW_GN_EPS = 64e-5
N_MOD = 9
GRID_W = 64
NA_HEADS = 8
NA_HEAD_DIM = 64
NA_WIN_H = 8
NA_WIN_W = 16
RW_HEADS = 8
RW_HEAD_DIM = 64
RW_LORA = 64
RW_GATE_LORA = 128

LANES = 128
VMEM_LIMIT = 56 * 1024 * 1024

TM = 512
TM_IN = 256
NA_ROWS = 8
RW_TILE = 256
RW_CHUNK = 64


def _cparams(n_axes):
    return pltpu.CompilerParams(dimension_semantics=("arbitrary",) * n_axes,
                                vmem_limit_bytes=VMEM_LIMIT)


def _resident(shape):
    nd = len(shape)
    return pl.BlockSpec(shape, lambda *_: (0,) * nd, pipeline_mode=pl.Buffered(1))


def _rmsnorm(x, g):
    return x * lax.rsqrt(jnp.mean(x * x, axis=-1, keepdims=True) + NORM_EPS) * g


def _split2(x):
    hi = x.astype(BF16)
    lo = (x - hi.astype(F32)).astype(BF16)
    return hi, lo


def _dot(a, b):
    return jnp.dot(a, b, preferred_element_type=F32)


def _dot_split_lhs(x, w_bf16, parts):
    acc = None
    rem = x
    for _ in range(parts):
        p = rem.astype(BF16)
        rem = rem - p.astype(F32)
        t = _dot(p, w_bf16)
        acc = t if acc is None else acc + t
    return acc


def _dot_split_rhs(w_bf16, x, parts):
    acc = None
    rem = x
    for _ in range(parts):
        p = rem.astype(BF16)
        rem = rem - p.astype(F32)
        t = _dot(w_bf16, p)
        acc = t if acc is None else acc + t
    return acc


def _dot3(a, b):
    ah, al = _split2(a)
    bh, bl = _split2(b)
    return _dot(ah, bh) + (_dot(ah, bl) + _dot(al, bh))


def _ada_kernel(c_ref, w_ref, b_ref, o_ref):
    c = c_ref[...]
    s = c * jax.nn.sigmoid(c)
    o_ref[...] = _dot3(s, w_ref[...]) + b_ref[...]


def _ada_mod(cs, w_ada, b_ada):
    d = cs.shape[1]
    nm = w_ada.shape[1] // d
    return pl.pallas_call(
        _ada_kernel,
        grid=(nm,),
        in_specs=[pl.BlockSpec((8, d), lambda j: (0, 0)),
                  pl.BlockSpec((d, d), lambda j: (0, j)),
                  pl.BlockSpec((1, d), lambda j: (0, j))],
        out_specs=pl.BlockSpec((8, d), lambda j: (0, j)),
        out_shape=jax.ShapeDtypeStruct((8, nm * d), F32),
        compiler_params=_cparams(1),
    )(cs, w_ada, b_ada.reshape(1, -1))


def _ffn_kernel(*refs, mod0, f_chunk, final, n_head_tiles):
    if n_head_tiles is None:
        x_ref, mod_ref, g_ref, wg_ref, wu_ref, wd_ref, gf_ref, o_ref = refs
        x = x_ref[...]
    else:
        x_ref, tail_ref, mod_ref, g_ref, wg_ref, wu_ref, wd_ref, gf_ref, o_ref = refs
        x = jnp.where(pl.program_id(0) < n_head_tiles, x_ref[...], tail_ref[...])
    shift = mod_ref[0, mod0:mod0 + 1, :]
    scale = mod_ref[0, mod0 + 1:mod0 + 2, :]
    gate = mod_ref[0, mod0 + 2:mod0 + 3, :]
    h = (_rmsnorm(x, g_ref[...]) * (1.0 + scale) + shift).astype(BF16)
    d_ff = wg_ref.shape[1]
    acc = None
    for f0 in range(0, d_ff, f_chunk):
        gg = _dot(h, wg_ref[:, f0:f0 + f_chunk])
        uu = _dot(h, wu_ref[:, f0:f0 + f_chunk])
        a = (gg * jax.nn.sigmoid(gg) * uu).astype(BF16)
        t = _dot(a, wd_ref[f0:f0 + f_chunk, :])
        acc = t if acc is None else acc + t
    y = x + 0.5 * gate * acc
    if final:
        y = _rmsnorm(y, gf_ref[...])
    o_ref[...] = y


def _ffn(xs, tail, mod, g, wg, wu, wd, gf, *, mod0, final, tiles_per_seq, n_seq):
    n_head, d = xs.shape
    n = n_head + (0 if tail is None else tail.shape[0])
    d_ff = wg.shape[1]
    f_chunk = d_ff // 2 if (d_ff // 2) % LANES == 0 else d_ff
    seq_of = lambda i: jnp.minimum(i // tiles_per_seq, n_seq)
    head_tiles = n_head // TM
    tokens = [pl.BlockSpec((TM, d), lambda i: (jnp.minimum(i, head_tiles - 1), 0))]
    if tail is not None:
        tokens.append(pl.BlockSpec((TM, d), lambda i: (jnp.maximum(i - head_tiles, 0), 0)))
    return pl.pallas_call(
        functools.partial(_ffn_kernel, mod0=mod0, f_chunk=f_chunk, final=final,
                          n_head_tiles=None if tail is None else head_tiles),
        grid=(n // TM,),
        in_specs=tokens + [pl.BlockSpec((1, N_MOD, d), lambda i: (seq_of(i), 0, 0)),
                           _resident((1, d)), _resident((d, d_ff)), _resident((d, d_ff)), _resident((d_ff, d)),
                           _resident((1, d))],
        out_specs=pl.BlockSpec((TM, d), lambda i: (i, 0)),
        out_shape=jax.ShapeDtypeStruct((n, d), F32),
        compiler_params=_cparams(1),
    )(*([xs] if tail is None else [xs, tail]), mod, g, wg, wu, wd, gf)


def _head_segsum(x, parts):
    r = lax.broadcasted_iota(jnp.int32, (LANES, LANES), 0) // RW_HEAD_DIM
    c = lax.broadcasted_iota(jnp.int32, (LANES, LANES), 1) // RW_HEAD_DIM
    e = jnp.where(r == c, 1.0, 0.0).astype(BF16)
    cols = [_dot_split_lhs(x[:, j:j + LANES], e, parts) for j in range(0, x.shape[1], LANES)]
    return jnp.concatenate(cols, axis=1)


def _inproj_kernel(x_ref, xprev_ref, xnext_ref, mod_ref, g_ref, w_ref, mu_ref, par_ref, wup_ref, aup_ref,
                   gup_ref, tri_ref, qkv_ref, gate_ref, kk_ref, r_ref, b_ref, k_ref, em_ref, v_ref, g_out_ref,
                   bon_ref, *, c_rw, c_gate, n_x_tiles, tiles_x, tiles_ctx):
    i = pl.program_id(0)
    ta = x_ref.shape[0]
    width = v_ref.shape[1]
    c_len = RW_CHUNK
    n_chunks = ta // c_len
    in_x = i < n_x_tiles
    tile = jnp.where(in_x, i % tiles_x, (i - n_x_tiles) % tiles_ctx)
    last_tile = jnp.where(in_x, tiles_x - 1, tiles_ctx - 1)

    x_all = jnp.concatenate([xprev_ref[...], x_ref[...], xnext_ref[...]], axis=0)
    h_all = (_rmsnorm(x_all, g_ref[...]) * (1.0 + mod_ref[0, 4:5, :]) + mod_ref[0, 3:4, :]).astype(BF16)
    h = h_all[8:ta + 8]
    zero = jnp.zeros((8, h.shape[1]), BF16)
    h_all = jnp.concatenate([jnp.where(tile == 0, zero, h_all[:8]), h,
                             jnp.where(tile == last_tile, zero, h_all[ta + 8:])], axis=0)
    p_all = _dot(h_all, w_ref[:, c_rw:c_gate])

    piece = 4 * LANES
    pieces = ([(qkv_ref, c0, c0) for c0 in range(0, c_rw, piece)]
              + [(gate_ref, c0, c_gate + c0) for c0 in range(0, w_ref.shape[1] - c_gate, piece)])

    def emit_dense(count):
        for _ in range(min(count, len(pieces))):
            ref, dst, src = pieces.pop(0)
            ref[:, dst:dst + piece] = _dot(h, w_ref[:, src:src + piece]).astype(ref.dtype)

    p = p_all[8:ta + 8]
    f = p + mu_ref[0:1, :] * (p_all[7:ta + 7] - p) + mu_ref[1:2, :] * (p_all[9:ta + 9] - p)
    emit_dense(2)

    r = f[:, 0:width]
    k = f[:, width:2 * width]
    v = f[:, 2 * width:3 * width]
    o = 3 * width
    th = jnp.tanh(f[:, o:o + LANES])
    al = f[:, o + LANES:o + 2 * LANES].astype(BF16)
    gl = f[:, o + 2 * LANES:o + 3 * LANES]
    k_k = par_ref[4:5, :]
    k_a = par_ref[5:6, :]
    r_k = par_ref[6:7, :]

    v_ref[...] = v.astype(BF16)
    g_out_ref[...] = _dot(jax.nn.sigmoid(gl).astype(BF16), gup_ref[...])
    kk = k * k_k
    kk = kk * lax.rsqrt(jnp.maximum(_head_segsum(kk * kk, 2), 1e-12))
    emit_dense(1)

    def chunks(t):
        return t.reshape(n_chunks, c_len, width)

    kd_sum = None
    for d in range(2):
        z = par_ref[d:d + 1, :] + _dot3(th, wup_ref[d])
        lw = -math.exp(-0.5) * jax.nn.sigmoid(z)
        a = jax.nn.sigmoid(par_ref[2 + d:3 + d, :] + _dot(al, aup_ref[d]))
        kd = k * (1.0 + (a - 1.0) * k_a)
        kd_sum = kd if kd_sum is None else kd_sum + kd
        bb = kk * a
        emit_dense(1)
        ci3 = chunks(_dot_split_rhs(tri_ref[d], lw, 2))
        ce3 = ci3 - chunks(lw)
        t_last = 0 if d == 1 else c_len - 1
        cm = 0.5 * ci3[:, t_last:t_last + 1, :]
        e_k = jnp.exp(cm - ci3)
        emit_dense(1)
        kk_ref[d] = (chunks(kk) * jnp.exp(ce3 - cm)).reshape(ta, width).astype(BF16)
        r_ref[d] = (chunks(r) * jnp.exp(ci3 - cm)).reshape(ta, width).astype(BF16)
        b_ref[d] = (chunks(bb) * e_k).reshape(ta, width).astype(BF16)
        k_ref[d] = (chunks(kd) * e_k).reshape(ta, width).astype(BF16)
        em_ref[d, 0] = jnp.exp(cm).reshape(n_chunks, width)
    emit_dense(len(pieces))
    bon_ref[...] = _head_segsum(r * kd_sum * r_k, 1) * v


def _inproj(xs, mod, g, w_in, mu, par, wup, aup, gup, *, c_rw, c_gate, batch, seq, ctx_len):
    n, d = xs.shape
    cols = w_in.shape[1]
    width = RW_HEADS * RW_HEAD_DIM
    ta = RW_TILE
    c_len = RW_CHUNK
    n_tiles = n // ta
    tiles_x = seq // ta
    sub = ta // 8
    last8 = n // 8 - 1
    seq_of = lambda i: jnp.minimum(i // tiles_x, batch)
    idx = np.arange(ta)
    same_chunk = (idx[:, None] // c_len) == (idx[None, :] // c_len)
    tri = jnp.asarray(np.stack([same_chunk & (idx[None, :] <= idx[:, None]),
                                same_chunk & (idx[None, :] >= idx[:, None])]), BF16)
    tok2 = pl.BlockSpec((2, ta, width), lambda i: (0, i, 0))
    tok = lambda c: pl.BlockSpec((ta, c), lambda i: (i, 0))
    sds = jax.ShapeDtypeStruct
    return pl.pallas_call(
        functools.partial(_inproj_kernel, c_rw=c_rw, c_gate=c_gate, n_x_tiles=batch * tiles_x, tiles_x=tiles_x,
                          tiles_ctx=ctx_len // ta),
        grid=(n_tiles,),
        in_specs=[tok(d),
                  pl.BlockSpec((8, d), lambda i: (jnp.maximum(i * sub - 1, 0), 0)),
                  pl.BlockSpec((8, d), lambda i: (jnp.minimum((i + 1) * sub, last8), 0)),
                  pl.BlockSpec((1, N_MOD, d), lambda i: (seq_of(i), 0, 0)),
                  _resident((1, d)), _resident((d, cols)),
                  _resident(mu.shape), _resident(par.shape), _resident(wup.shape), _resident(aup.shape),
                  _resident(gup.shape), _resident(tri.shape)],
        out_specs=[tok(c_rw), tok(cols - c_gate), tok2, tok2, tok2, tok2,
                   pl.BlockSpec((2, 1, ta // c_len, width), lambda i: (0, i, 0, 0)),
                   tok(width), tok(width), tok(width)],
        out_shape=[sds((n, c_rw), BF16), sds((n, cols - c_gate), F32)]
                  + [sds((2, n, width), BF16)] * 4
                  + [sds((2, n_tiles, ta // c_len, width), F32), sds((n, width), BF16),
                     sds((n, width), F32), sds((n, width), F32)],
        compiler_params=_cparams(1),
    )(xs, xs, xs, mod, g, w_in, mu, par, wup, aup, gup, tri)


NA_KEY_ROWS = 2 * NA_ROWS
NA_PAIRS = NA_KEY_ROWS // 2


def _na_pair_bias(rpb):
    w = GRID_W
    qc = np.arange(w)[:, None]
    kc = np.arange(w)[None, :]
    s_c = np.clip(qc - NA_WIN_W // 2, 0, w - NA_WIN_W)
    col_ok = (kc >= s_c) & (kc < s_c + NA_WIN_W)
    dc = np.clip(kc - qc + NA_WIN_W - 1, 0, 2 * NA_WIN_W - 2)
    b = jnp.where(col_ok[None, None], rpb[:, :, dc], -jnp.inf).astype(F32)
    ninf = jnp.full((rpb.shape[0], 1, w, w), -jnp.inf, F32)
    b = jnp.concatenate([ninf, b, ninf], axis=1)
    return jnp.concatenate([b[:, :-1], b[:, 1:]], axis=-1)


def _na_kernel(q_ref, k0_ref, k1_ref, k2_ref, k3_ref, kc_ref, v0_ref, v1_ref, v2_ref, v3_ref, vc_ref,
               bias_ref, o_ref, kcat, vcat, *, n_blocks):
    j = pl.program_id(2)
    blk = k0_ref.shape[0]
    for s, (kr, vr) in enumerate(((k0_ref, v0_ref), (k1_ref, v1_ref), (k2_ref, v2_ref), (k3_ref, v3_ref),
                                  (kc_ref, vc_ref))):
        kcat[s * blk:s * blk + kr.shape[0], :] = kr[...]
        vcat[s * blk:s * blk + vr.shape[0], :] = vr[...]
    n_loc = 4 * blk
    w = GRID_W
    half = NA_WIN_H // 2
    lane = lax.broadcasted_iota(jnp.int32, (1, LANES), 1)
    first = lane < NA_HEAD_DIM
    q = q_ref[...] * (NA_HEAD_DIM ** -0.5)
    k_all = kcat[...]
    v_all = vcat[...]
    outs = []
    for h in range(2):
        sel = first if h == 0 else jnp.logical_not(first)
        qh = jnp.where(sel, q, jnp.zeros_like(q))
        s_loc = lax.dot_general(qh, k_all[:n_loc], (((1,), (1,)), ((), ())), preferred_element_type=F32)
        s_ctx = lax.dot_general(qh, k_all[n_loc:], (((1,), (1,)), ((), ())), preferred_element_type=F32)
        p_rows = []
        l_rows = []
        for qi in range(NA_ROWS):
            lo = jnp.where(j == 0, max(qi, half), jnp.where(j == n_blocks - 1, min(qi, half), qi))
            m_lo = min(qi, half) // 2
            m_hi = (max(qi, half) + NA_WIN_H - 1) // 2
            n_m = m_hi - m_lo + 1
            key_row = 2 * m_lo + lax.broadcasted_iota(jnp.int32, (1, n_m * LANES), 1) // w
            seen = (key_row >= lo) & (key_row < lo + NA_WIN_H)
            bias = jnp.concatenate(
                [bias_ref[h, min(max(2 * m - qi + half, 0), 2 * NA_WIN_H - 1)] for m in range(m_lo, m_hi + 1)],
                axis=1)
            rows = slice(qi * w, (qi + 1) * w)
            s_q = jnp.where(seen, s_loc[rows, m_lo * LANES:(m_hi + 1) * LANES] + bias, -jnp.inf)
            s_c = s_ctx[rows]
            mx = jnp.maximum(jnp.max(s_q, axis=-1, keepdims=True), jnp.max(s_c, axis=-1, keepdims=True))
            p_q = jnp.exp(s_q - mx)
            p_c = jnp.exp(s_c - mx)
            l_rows.append(jnp.sum(p_q, axis=-1, keepdims=True) + jnp.sum(p_c, axis=-1, keepdims=True))
            pieces = []
            if m_lo > 0:
                pieces.append(jnp.zeros((w, m_lo * LANES), BF16))
            pieces.append(p_q.astype(BF16))
            if m_hi < NA_PAIRS - 1:
                pieces.append(jnp.zeros((w, (NA_PAIRS - 1 - m_hi) * LANES), BF16))
            pieces.append(p_c.astype(BF16))
            p_rows.append(jnp.concatenate(pieces, axis=1))
        p = jnp.concatenate(p_rows, axis=0)
        l = jnp.concatenate(l_rows, axis=0)
        outs.append(_dot(p, v_all) / l)
    o_ref[...] = jnp.where(first, outs[0], outs[1]).astype(o_ref.dtype)


def _na_attention(qkv, bias, *, batch, seq, ctx_len):
    rows = seq // GRID_W
    nj = rows // NA_ROWS
    tq = NA_ROWS * GRID_W
    tk = 4 * GRID_W
    width = NA_HEADS * NA_HEAD_DIM
    n_pairs = width // LANES
    kblocks = seq // tk
    ctx_blk0 = (batch * seq) // ctx_len

    def kv_spec(slot, col0):
        def imap(hp, b, j):
            return (b * kblocks + jnp.clip(2 * j - 1 + slot, 0, kblocks - 1), col0 + hp)
        return pl.BlockSpec((tk, LANES), imap)

    def ctx_spec(col0):
        return pl.BlockSpec((ctx_len, LANES), lambda hp, b, j: (ctx_blk0 + b, col0 + hp))

    in_specs = ([pl.BlockSpec((tq, LANES), lambda hp, b, j: (b * nj + j, hp))]
                + [kv_spec(s, n_pairs) for s in range(4)] + [ctx_spec(n_pairs)]
                + [kv_spec(s, 2 * n_pairs) for s in range(4)] + [ctx_spec(2 * n_pairs)]
                + [pl.BlockSpec((2,) + bias.shape[1:], lambda hp, b, j: (hp, 0, 0, 0))])
    return pl.pallas_call(
        functools.partial(_na_kernel, n_blocks=nj),
        grid=(n_pairs, batch, nj),
        in_specs=in_specs,
        out_specs=pl.BlockSpec((tq, LANES), lambda hp, b, j: (b * nj + j, hp)),
        out_shape=jax.ShapeDtypeStruct((batch * seq, width), BF16),
        scratch_shapes=[pltpu.VMEM((4 * tk + ctx_len, LANES), BF16),
                        pltpu.VMEM((4 * tk + ctx_len, LANES), BF16)],
        compiler_params=_cparams(3),
    )(*([qkv] * 11), bias)


def _rwkv_kernel(kk_ref, r_ref, b_ref, k_ref, em_ref, v_ref, strict_ref, incl_ref, y_ref, z_s, *, reverse):
    n = pl.program_id(1)
    ta = v_ref.shape[0]
    width = v_ref.shape[1]
    n_pairs = width // LANES
    c_len = RW_CHUNK
    n_chunks = ta // c_len
    n_b = n_chunks * n_pairs

    @pl.when(n == 0)
    def _():
        z_s[...] = jnp.zeros_like(z_s)

    def chunks(t):
        return t.reshape(n_chunks, c_len, width)

    def to_batch(t):
        parts = [t[:, :, j * LANES:(j + 1) * LANES] for j in range(n_pairs)]
        return jnp.stack(parts, axis=1).reshape(n_b, t.shape[1], LANES)

    em = em_ref[0, 0].reshape(n_chunks, 1, width)
    kk_t = chunks(kk_ref[0].astype(F32))
    r_t = chunks(r_ref[0].astype(F32))
    b_t = chunks(b_ref[0].astype(F32))
    k_t = chunks(k_ref[0].astype(F32))
    kk_abs = to_batch(kk_t * em)
    r_abs = to_batch(r_t * em)
    k_bar = to_batch(k_t * em)
    b_bar = to_batch(b_t * em)
    gam = to_batch(em * em)
    kk_t = to_batch(kk_t)
    r_t = to_batch(r_t)
    b_t = to_batch(b_t)
    k_t = to_batch(k_t)
    v_b = to_batch(chunks(v_ref[...].astype(F32)))

    lane = lax.broadcasted_iota(jnp.int32, (1, 1, LANES), 2)
    first = lane < RW_HEAD_DIM
    ri = lax.broadcasted_iota(jnp.int32, (LANES, LANES), 0)
    cj = lax.broadcasted_iota(jnp.int32, (LANES, LANES), 1)
    eye = ri == cj
    blockdiag = (ri // RW_HEAD_DIM) == (cj // RW_HEAD_DIM)
    strict = strict_ref[...] > 0.0
    incl = incl_ref[...] > 0.0

    def stack_heads(t):
        zero = jnp.zeros_like(t)
        return jnp.concatenate([jnp.where(first, t, zero), jnp.where(first, zero, t)], axis=1)

    def bmm(x, y):
        return lax.dot_general(x.astype(BF16), y.astype(BF16), (((2,), (1,)), ((0,), (0,))),
                               preferred_element_type=F32)

    def bmm_nt(x, y):
        return lax.dot_general(x.astype(BF16), y.astype(BF16), (((2,), (2,)), ((0,), (0,))),
                               preferred_element_type=F32)

    def bmm_tn(x, y):
        return lax.dot_general(x.astype(BF16), y.astype(BF16), (((1,), (1,)), ((0,), (0,))),
                               preferred_element_type=F32)

    gm = bmm_nt(jnp.concatenate([kk_t, r_t], axis=1),
                jnp.concatenate([stack_heads(b_t), stack_heads(k_t)], axis=1))
    ab_w = jnp.where(strict, gm[:, :c_len, :LANES], 0.0)
    ak_w = jnp.where(strict, gm[:, :c_len, LANES:], 0.0)
    db_w = jnp.where(incl, gm[:, c_len:, :LANES], 0.0)
    dk_w = jnp.where(incl, gm[:, c_len:, LANES:], 0.0)

    l_bd = stack_heads(ab_w)
    t_bd = jnp.where(eye, 1.0, 0.0) - l_bd
    pw = bmm(l_bd, l_bd)
    span = 2
    while span < c_len:
        t_bd = t_bd + bmm(t_bd, pw)
        span *= 2
        if span < c_len:
            pw = bmm(pw, pw)
    t_w = t_bd[:, :c_len] + t_bd[:, c_len:]

    v_st = stack_heads(v_b)
    w1 = bmm(ak_w, v_st)
    pq = bmm(t_w, jnp.concatenate([stack_heads(kk_abs), stack_heads(w1)], axis=2))
    dpq = bmm(db_w, jnp.concatenate([stack_heads(pq[:, :, :LANES]), stack_heads(pq[:, :, LANES:])], axis=2))
    r_hat = r_abs - dpq[:, :, :LANES]
    y_loc = bmm(dk_w, v_st) - dpq[:, :, LANES:]
    bpq = bmm_tn(b_bar, pq)
    m_t = jnp.where(eye, jnp.broadcast_to(gam, (n_b, LANES, LANES)), 0.0) - bpq[:, :, :LANES]
    g_t = bmm_tn(k_bar, v_b) - bpq[:, :, LANES:]

    zs = z_s[...]
    order = range(n_chunks - 1, -1, -1) if reverse else range(n_chunks)
    for c in order:
        sl = slice(c * n_pairs, (c + 1) * n_pairs)
        y_c = bmm(r_hat[sl], zs) + y_loc[sl]
        y_ref[0, c * c_len:(c + 1) * c_len, :] = jnp.concatenate([y_c[j] for j in range(n_pairs)], axis=1)
        zh = zs.astype(BF16)
        zl = (zs - zh.astype(F32)).astype(BF16)
        m_c = m_t[sl]
        mh = m_c.astype(BF16)
        ml = (m_c - mh.astype(F32)).astype(BF16)
        zs = jnp.where(blockdiag, bmm(mh, zh) + (bmm(mh, zl) + bmm(ml, zh)) + g_t[sl], 0.0)
    z_s[...] = zs


def _rwkv_stream(kk_t, r_t, b_t, k_t, em, v, *, reverse, batch, seq, ctx_len):
    width = RW_HEADS * RW_HEAD_DIM
    ta = RW_TILE
    nct = ctx_len // ta
    nxt = seq // ta
    n_x = batch * seq
    c_len = RW_CHUNK
    d = 1 if reverse else 0

    def blk(b, n):
        is_ctx = n < nct
        t_ctx = (nct - 1 - n) if reverse else n
        t_x = (nxt - 1 - (n - nct)) if reverse else (n - nct)
        return jnp.where(is_ctx, batch * nxt + b * nct + t_ctx, b * nxt + t_x)

    def xblk(b, n):
        m = jnp.maximum(n, nct) - nct
        return b * nxt + ((nxt - 1 - m) if reverse else m)

    t_i = np.arange(c_len)[:, None]
    s_i = np.arange(LANES)[None, :] % RW_HEAD_DIM
    strict = jnp.asarray((s_i > t_i) if reverse else (s_i < t_i), F32)
    incl = jnp.asarray((s_i >= t_i) if reverse else (s_i <= t_i), F32)

    operand = pl.BlockSpec((1, ta, width), lambda b, n: (d, blk(b, n), 0))
    return pl.pallas_call(
        functools.partial(_rwkv_kernel, reverse=reverse),
        grid=(batch, nct + nxt),
        in_specs=[operand, operand, operand, operand,
                  pl.BlockSpec((1, 1, ta // c_len, width), lambda b, n: (d, blk(b, n), 0, 0)),
                  pl.BlockSpec((ta, width), lambda b, n: (blk(b, n), 0)),
                  _resident(strict.shape), _resident(incl.shape)],
        out_specs=pl.BlockSpec((1, ta, width), lambda b, n: (0, xblk(b, n), 0)),
        out_shape=jax.ShapeDtypeStruct((1, n_x, width), F32),
        scratch_shapes=[pltpu.VMEM((width // LANES, LANES, LANES), F32)],
        compiler_params=_cparams(2),
    )(kk_t, r_t, b_t, k_t, em, v, strict, incl)


def _merge_kernel(x_ref, mod_ref, ona_ref, yf_ref, yb_ref, bon_ref, g_ref, pg_ref, bg_ref, ln_ref,
                  pna_ref, prw_ref, wout_ref, o_ref):
    d = x_ref.shape[1]
    y = yf_ref[0] + yb_ref[0]
    inv_n = 1.0 / RW_HEAD_DIM
    mean = _head_segsum(y, 2) * inv_n
    yc = y - mean
    var = _head_segsum(yc * yc, 2) * inv_n
    yn = yc * lax.rsqrt(var + RW_GN_EPS) * ln_ref[0:1, :] + ln_ref[1:2, :]
    o_rw = ((yn + bon_ref[...]) * g_ref[...]).astype(BF16)
    gates = jax.nn.sigmoid(pg_ref[...] + bg_ref[...])
    m = gates[:, :d] * _dot(ona_ref[...], pna_ref[...]) + gates[:, d:] * _dot(o_rw, prw_ref[...])
    o_ref[...] = x_ref[...] + mod_ref[0, 5:6, :] * _dot(m.astype(BF16), wout_ref[...])


def _merge(x1, mod, o_na, y_f, y_b, bon, g, p_gate, b_gate, ln, p_na, p_rw, w_out, *, tiles_per_seq):
    n_x, d = o_na.shape[0], x1.shape[1]
    width = o_na.shape[1]
    tok = lambda i: (i, 0)
    tok3 = lambda i: (0, i, 0)
    return pl.pallas_call(
        _merge_kernel,
        grid=(n_x // TM,),
        in_specs=[pl.BlockSpec((TM, d), tok),
                  pl.BlockSpec((1, N_MOD, d), lambda i: (i // tiles_per_seq, 0, 0)),
                  pl.BlockSpec((TM, width), tok),
                  pl.BlockSpec((1, TM, width), tok3), pl.BlockSpec((1, TM, width), tok3),
                  pl.BlockSpec((TM, width), tok), pl.BlockSpec((TM, width), tok),
                  pl.BlockSpec((TM, 2 * d), tok),
                  _resident((1, 2 * d)), _resident((2, width)),
                  _resident(p_na.shape), _resident(p_rw.shape), _resident(w_out.shape)],
        out_specs=pl.BlockSpec((TM, d), tok),
        out_shape=jax.ShapeDtypeStruct((n_x, d), F32),
        compiler_params=_cparams(1),
    )(x1, mod, o_na, y_f, y_b, bon, g, p_gate, b_gate, ln, p_na, p_rw, w_out)


def _pad_lora(w_up, direction):
    z = jnp.zeros_like(w_up[0])
    return jnp.concatenate([w_up[0] if direction == 0 else z, w_up[1] if direction == 1 else z], axis=0)


def kernel(x, c, ctx, c_ctx, w_ada, b_ada, norm_ffn1, norm_mix, norm_ffn2, norm_final, ffn1_wg, ffn1_wu, ffn1_wd, ffn2_wg, ffn2_wu, ffn2_wd, w_in, b_gate, na_rpb, rw_mu, rw_w0, rw_w_up, rw_a0, rw_a_up, rw_g_up, rw_k_k, rw_k_a, rw_r_k, rw_ln_w, rw_ln_b, p_na, p_rw, w_out):
    batch, seq, d = x.shape
    ctx_len = ctx.shape[1]
    n_x = batch * seq
    na_width = NA_HEADS * NA_HEAD_DIM
    rw_width = RW_HEADS * RW_HEAD_DIM
    c_rw = 3 * na_width
    c_gate = c_rw + 3 * rw_width + 4 * RW_LORA + RW_GATE_LORA
    assert w_ada.shape[0] == 1, "single layer"
    assert seq % TM == 0 and (batch * ctx_len) % TM == 0 and ctx_len % RW_TILE == 0 and batch + 1 <= 8
    assert seq % (NA_ROWS * GRID_W) == 0 and seq // (NA_ROWS * GRID_W) >= 2 and ctx_len == 4 * GRID_W
    row = lambda t: t.reshape(1, -1)

    cs = jnp.concatenate([c, c_ctx[None], jnp.zeros((8 - batch - 1, d), F32)], axis=0)
    mod = _ada_mod(cs, w_ada[0], b_ada[0])[:batch + 1].reshape(batch + 1, N_MOD, d)

    bf = lambda t: t.astype(BF16)

    x1 = _ffn(x.reshape(n_x, d), ctx.reshape(batch * ctx_len, d), mod, row(norm_ffn1[0]),
              bf(ffn1_wg[0]), bf(ffn1_wu[0]), bf(ffn1_wd[0]), row(norm_final),
              mod0=0, final=False, tiles_per_seq=seq // TM, n_seq=batch)
    par = jnp.concatenate([rw_w0[0], rw_a0[0], row(rw_k_k[0]), row(rw_k_a[0]), row(rw_r_k[0]),
                           jnp.zeros((1, rw_width), F32)], axis=0)
    wup = jnp.stack([_pad_lora(rw_w_up[0], 0), _pad_lora(rw_w_up[0], 1)])
    aup = bf(jnp.stack([_pad_lora(rw_a_up[0], 0), _pad_lora(rw_a_up[0], 1)]))
    qkv, p_gate, kk_t, r_t, b_t, k_t, em, v_rw, g, bon = _inproj(
        x1, mod, row(norm_mix[0]), bf(w_in[0]), rw_mu[0], par, wup, aup, bf(rw_g_up[0]),
        c_rw=c_rw, c_gate=c_gate, batch=batch, seq=seq, ctx_len=ctx_len)

    bias = _na_pair_bias(na_rpb[0])
    o_na = _na_attention(qkv, bias, batch=batch, seq=seq, ctx_len=ctx_len)

    y_f, y_b = [_rwkv_stream(kk_t, r_t, b_t, k_t, em, v_rw, reverse=rev, batch=batch, seq=seq, ctx_len=ctx_len)
                for rev in (False, True)]

    ln = jnp.stack([rw_ln_w[0], rw_ln_b[0]], axis=0)
    x2 = _merge(x1, mod, o_na, y_f, y_b, bon, g, p_gate, row(b_gate[0]), ln,
                bf(p_na[0]), bf(p_rw[0]), bf(w_out[0]), tiles_per_seq=seq // TM)
    out = _ffn(x2, None, mod, row(norm_ffn2[0]), bf(ffn2_wg[0]), bf(ffn2_wu[0]), bf(ffn2_wd[0]), row(norm_final),
               mod0=6, final=True, tiles_per_seq=seq // TM, n_seq=batch)
    return out.reshape(batch, seq, d)
```

```python
import functools
import math

import numpy as np
import jax
import jax.numpy as jnp
from jax import lax
from jax.experimental import pallas as pl
from jax.experimental.pallas import tpu as pltpu

F32 = jnp.float32
BF16 = jnp.bfloat16

NORM_EPS = 1e-6
RW_GN_EPS = 64e-5
N_MOD = 9
GRID_W = 64
NA_HEADS = 8
NA_HEAD_DIM = 64
NA_WIN_H = 8
NA_WIN_W = 16
RW_HEADS = 8
RW_HEAD_DIM = 64
RW_LORA = 64
RW_GATE_LORA = 128

LANES = 128
VMEM_LIMIT = 56 * 1024 * 1024

TM = 512
TM_IN = 256
NA_ROWS = 8
RW_TILE = 256
RW_CHUNK = 64
RW_INV_BASE = 8


def _cparams(n_axes):
    return pltpu.CompilerParams(dimension_semantics=("arbitrary",) * n_axes,
                                vmem_limit_bytes=VMEM_LIMIT)


def _resident(shape):
    nd = len(shape)
    return pl.BlockSpec(shape, lambda *_: (0,) * nd, pipeline_mode=pl.Buffered(1))


def _rmsnorm(x, g):
    return x * lax.rsqrt(jnp.mean(x * x, axis=-1, keepdims=True) + NORM_EPS) * g


def _split2(x):
    hi = x.astype(BF16)
    lo = (x - hi.astype(F32)).astype(BF16)
    return hi, lo


def _dot(a, b):
    return jnp.dot(a, b, preferred_element_type=F32)


def _dot_split_lhs(x, w_bf16, parts):
    acc = None
    rem = x
    for _ in range(parts):
        p = rem.astype(BF16)
        rem = rem - p.astype(F32)
        t = _dot(p, w_bf16)
        acc = t if acc is None else acc + t
    return acc


def _dot_split_rhs(w_bf16, x, parts):
    acc = None
    rem = x
    for _ in range(parts):
        p = rem.astype(BF16)
        rem = rem - p.astype(F32)
        t = _dot(w_bf16, p)
        acc = t if acc is None else acc + t
    return acc


def _dot3(a, b):
    ah, al = _split2(a)
    bh, bl = _split2(b)
    return _dot(ah, bh) + (_dot(ah, bl) + _dot(al, bh))


def _ada_kernel(c_ref, w_ref, b_ref, o_ref):
    c = c_ref[...]
    s = c * jax.nn.sigmoid(c)
    o_ref[...] = _dot3(s, w_ref[...]) + b_ref[...]


def _ada_mod(cs, w_ada, b_ada):
    d = cs.shape[1]
    nm = w_ada.shape[1] // d
    return pl.pallas_call(
        _ada_kernel,
        grid=(nm,),
        in_specs=[pl.BlockSpec((8, d), lambda j: (0, 0)),
                  pl.BlockSpec((d, d), lambda j: (0, j)),
                  pl.BlockSpec((1, d), lambda j: (0, j))],
        out_specs=pl.BlockSpec((8, d), lambda j: (0, j)),
        out_shape=jax.ShapeDtypeStruct((8, nm * d), F32),
        compiler_params=_cparams(1),
    )(cs, w_ada, b_ada.reshape(1, -1))


def _ffn_kernel(*refs, mod0, f_chunk, final, n_head_tiles):
    if n_head_tiles is None:
        x_ref, mod_ref, g_ref, wg_ref, wu_ref, wd_ref, gf_ref, o_ref = refs
        x = x_ref[...]
    else:
        x_ref, tail_ref, mod_ref, g_ref, wg_ref, wu_ref, wd_ref, gf_ref, o_ref = refs
        x = jnp.where(pl.program_id(0) < n_head_tiles, x_ref[...], tail_ref[...])
    shift = mod_ref[0, mod0:mod0 + 1, :]
    scale = mod_ref[0, mod0 + 1:mod0 + 2, :]
    gate = mod_ref[0, mod0 + 2:mod0 + 3, :]
    h = (_rmsnorm(x, g_ref[...]) * (1.0 + scale) + shift).astype(BF16)
    d_ff = wg_ref.shape[1]
    acc = None
    for f0 in range(0, d_ff, f_chunk):
        gg = _dot(h, wg_ref[:, f0:f0 + f_chunk])
        uu = _dot(h, wu_ref[:, f0:f0 + f_chunk])
        a = (gg * jax.nn.sigmoid(gg) * uu).astype(BF16)
        t = _dot(a, wd_ref[f0:f0 + f_chunk, :])
        acc = t if acc is None else acc + t
    y = x + 0.5 * gate * acc
    if final:
        y = _rmsnorm(y, gf_ref[...])
    o_ref[...] = y


def _ffn(xs, tail, mod, g, wg, wu, wd, gf, *, mod0, final, tiles_per_seq, n_seq):
    n_head, d = xs.shape
    n = n_head + (0 if tail is None else tail.shape[0])
    d_ff = wg.shape[1]
    f_chunk = d_ff // 2 if (d_ff // 2) % LANES == 0 else d_ff
    seq_of = lambda i: jnp.minimum(i // tiles_per_seq, n_seq)
    head_tiles = n_head // TM
    tokens = [pl.BlockSpec((TM, d), lambda i: (jnp.minimum(i, head_tiles - 1), 0))]
    if tail is not None:
        tokens.append(pl.BlockSpec((TM, d), lambda i: (jnp.maximum(i - head_tiles, 0), 0)))
    return pl.pallas_call(
        functools.partial(_ffn_kernel, mod0=mod0, f_chunk=f_chunk, final=final,
                          n_head_tiles=None if tail is None else head_tiles),
        grid=(n // TM,),
        in_specs=tokens + [pl.BlockSpec((1, N_MOD, d), lambda i: (seq_of(i), 0, 0)),
                           _resident((1, d)), _resident((d, d_ff)), _resident((d, d_ff)), _resident((d_ff, d)),
                           _resident((1, d))],
        out_specs=pl.BlockSpec((TM, d), lambda i: (i, 0)),
        out_shape=jax.ShapeDtypeStruct((n, d), F32),
        compiler_params=_cparams(1),
    )(*([xs] if tail is None else [xs, tail]), mod, g, wg, wu, wd, gf)


def _head_segsum(x, parts):
    r = lax.broadcasted_iota(jnp.int32, (LANES, LANES), 0) // RW_HEAD_DIM
    c = lax.broadcasted_iota(jnp.int32, (LANES, LANES), 1) // RW_HEAD_DIM
    e = jnp.where(r == c, 1.0, 0.0).astype(BF16)
    cols = [_dot_split_lhs(x[:, j:j + LANES], e, parts) for j in range(0, x.shape[1], LANES)]
    return jnp.concatenate(cols, axis=1)


def _inproj_kernel(x_ref, xprev_ref, xnext_ref, mod_ref, g_ref, w_ref, mu_ref, par_ref, wup_ref, aup_ref,
                   gup_ref, tri_ref, qkv_ref, gate_ref, kk_ref, r_ref, b_ref, k_ref, em_ref, v_ref, g_out_ref,
                   bon_ref, *, c_rw, c_gate, n_x_tiles, tiles_x, tiles_ctx):
    i = pl.program_id(0)
    ta = x_ref.shape[0]
    width = v_ref.shape[1]
    c_len = RW_CHUNK
    n_chunks = ta // c_len
    in_x = i < n_x_tiles
    tile = jnp.where(in_x, i % tiles_x, (i - n_x_tiles) % tiles_ctx)
    last_tile = jnp.where(in_x, tiles_x - 1, tiles_ctx - 1)

    x_all = jnp.concatenate([xprev_ref[...], x_ref[...], xnext_ref[...]], axis=0)
    h_all = (_rmsnorm(x_all, g_ref[...]) * (1.0 + mod_ref[0, 4:5, :]) + mod_ref[0, 3:4, :]).astype(BF16)
    h = h_all[8:ta + 8]
    zero = jnp.zeros((8, h.shape[1]), BF16)
    h_all = jnp.concatenate([jnp.where(tile == 0, zero, h_all[:8]), h,
                             jnp.where(tile == last_tile, zero, h_all[ta + 8:])], axis=0)
    p_all = _dot(h_all, w_ref[:, c_rw:c_gate])

    piece = 4 * LANES
    pieces = ([(qkv_ref, c0, c0) for c0 in range(0, c_rw, piece)]
              + [(gate_ref, c0, c_gate + c0) for c0 in range(0, w_ref.shape[1] - c_gate, piece)])

    def emit_dense(count):
        for _ in range(min(count, len(pieces))):
            ref, dst, src = pieces.pop(0)
            ref[:, dst:dst + piece] = _dot(h, w_ref[:, src:src + piece]).astype(ref.dtype)

    p = p_all[8:ta + 8]
    f = p + mu_ref[0:1, :] * (p_all[7:ta + 7] - p) + mu_ref[1:2, :] * (p_all[9:ta + 9] - p)
    emit_dense(2)

    r = f[:, 0:width]
    k = f[:, width:2 * width]
    v = f[:, 2 * width:3 * width]
    o = 3 * width
    th = jnp.tanh(f[:, o:o + LANES])
    al = f[:, o + LANES:o + 2 * LANES].astype(BF16)
    gl = f[:, o + 2 * LANES:o + 3 * LANES]
    k_k = par_ref[4:5, :]
    k_a = par_ref[5:6, :]
    r_k = par_ref[6:7, :]

    v_ref[...] = v.astype(BF16)
    g_out_ref[...] = _dot(jax.nn.sigmoid(gl).astype(BF16), gup_ref[...])
    kk = k * k_k
    kk = kk * lax.rsqrt(jnp.maximum(_head_segsum(kk * kk, 2), 1e-12))
    emit_dense(1)

    def chunks(t):
        return t.reshape(n_chunks, c_len, width)

    kd_sum = None
    for d in range(2):
        z = par_ref[d:d + 1, :] + _dot3(th, wup_ref[d])
        lw = -math.exp(-0.5) * jax.nn.sigmoid(z)
        a = jax.nn.sigmoid(par_ref[2 + d:3 + d, :] + _dot(al, aup_ref[d]))
        kd = k * (1.0 + (a - 1.0) * k_a)
        kd_sum = kd if kd_sum is None else kd_sum + kd
        bb = kk * a
        emit_dense(1)
        ci3 = chunks(_dot_split_rhs(tri_ref[d], lw, 2))
        ce3 = ci3 - chunks(lw)
        t_last = 0 if d == 1 else c_len - 1
        cm = 0.5 * ci3[:, t_last:t_last + 1, :]
        e_k = jnp.exp(cm - ci3)
        emit_dense(1)
        kk_ref[d] = (chunks(kk) * jnp.exp(ce3 - cm)).reshape(ta, width).astype(BF16)
        r_ref[d] = (chunks(r) * jnp.exp(ci3 - cm)).reshape(ta, width).astype(BF16)
        b_ref[d] = (chunks(bb) * e_k).reshape(ta, width).astype(BF16)
        k_ref[d] = (chunks(kd) * e_k).reshape(ta, width).astype(BF16)
        em_ref[d, 0] = jnp.exp(cm).reshape(n_chunks, width)
    emit_dense(len(pieces))
    bon_ref[...] = _head_segsum(r * kd_sum * r_k, 1) * v


def _inproj(xs, mod, g, w_in, mu, par, wup, aup, gup, *, c_rw, c_gate, batch, seq, ctx_len):
    n, d = xs.shape
    cols = w_in.shape[1]
    width = RW_HEADS * RW_HEAD_DIM
    ta = RW_TILE
    c_len = RW_CHUNK
    n_tiles = n // ta
    tiles_x = seq // ta
    sub = ta // 8
    last8 = n // 8 - 1
    seq_of = lambda i: jnp.minimum(i // tiles_x, batch)
    idx = np.arange(ta)
    same_chunk = (idx[:, None] // c_len) == (idx[None, :] // c_len)
    tri = jnp.asarray(np.stack([same_chunk & (idx[None, :] <= idx[:, None]),
                                same_chunk & (idx[None, :] >= idx[:, None])]), BF16)
    tok2 = pl.BlockSpec((2, ta, width), lambda i: (0, i, 0))
    tok = lambda c: pl.BlockSpec((ta, c), lambda i: (i, 0))
    sds = jax.ShapeDtypeStruct
    return pl.pallas_call(
        functools.partial(_inproj_kernel, c_rw=c_rw, c_gate=c_gate, n_x_tiles=batch * tiles_x, tiles_x=tiles_x,
                          tiles_ctx=ctx_len // ta),
        grid=(n_tiles,),
        in_specs=[tok(d),
                  pl.BlockSpec((8, d), lambda i: (jnp.maximum(i * sub - 1, 0), 0)),
                  pl.BlockSpec((8, d), lambda i: (jnp.minimum((i + 1) * sub, last8), 0)),
                  pl.BlockSpec((1, N_MOD, d), lambda i: (seq_of(i), 0, 0)),
                  _resident((1, d)), _resident((d, cols)),
                  _resident(mu.shape), _resident(par.shape), _resident(wup.shape), _resident(aup.shape),
                  _resident(gup.shape), _resident(tri.shape)],
        out_specs=[tok(c_rw), tok(cols - c_gate), tok2, tok2, tok2, tok2,
                   pl.BlockSpec((2, 1, ta // c_len, width), lambda i: (0, i, 0, 0)),
                   tok(width), tok(width), tok(width)],
        out_shape=[sds((n, c_rw), BF16), sds((n, cols - c_gate), F32)]
                  + [sds((2, n, width), BF16)] * 4
                  + [sds((2, n_tiles, ta // c_len, width), F32), sds((n, width), BF16),
                     sds((n, width), F32), sds((n, width), F32)],
        compiler_params=_cparams(1),
    )(xs, xs, xs, mod, g, w_in, mu, par, wup, aup, gup, tri)


NA_KEY_ROWS = 2 * NA_ROWS
NA_PAIRS = NA_KEY_ROWS // 2


def _na_pair_bias(rpb):
    w = GRID_W
    qc = np.arange(w)[:, None]
    kc = np.arange(w)[None, :]
    s_c = np.clip(qc - NA_WIN_W // 2, 0, w - NA_WIN_W)
    col_ok = (kc >= s_c) & (kc < s_c + NA_WIN_W)
    dc = np.clip(kc - qc + NA_WIN_W - 1, 0, 2 * NA_WIN_W - 2)
    b = jnp.where(col_ok[None, None], rpb[:, :, dc], -jnp.inf).astype(F32)
    ninf = jnp.full((rpb.shape[0], 1, w, w), -jnp.inf, F32)
    b = jnp.concatenate([ninf, b, ninf], axis=1)
    return jnp.concatenate([b[:, :-1], b[:, 1:]], axis=-1)


def _na_kernel(q_ref, k0_ref, k1_ref, k2_ref, k3_ref, kc_ref, v0_ref, v1_ref, v2_ref, v3_ref, vc_ref,
               bias_ref, o_ref, kcat, vcat, *, n_blocks):
    j = pl.program_id(2)
    blk = k0_ref.shape[0]
    for s, (kr, vr) in enumerate(((k0_ref, v0_ref), (k1_ref, v1_ref), (k2_ref, v2_ref), (k3_ref, v3_ref),
                                  (kc_ref, vc_ref))):
        kcat[s * blk:s * blk + kr.shape[0], :] = kr[...]
        vcat[s * blk:s * blk + vr.shape[0], :] = vr[...]
    n_loc = 4 * blk
    w = GRID_W
    half = NA_WIN_H // 2
    lane = lax.broadcasted_iota(jnp.int32, (1, LANES), 1)
    first = lane < NA_HEAD_DIM
    q = q_ref[...] * (NA_HEAD_DIM ** -0.5)
    k_all = kcat[...]
    v_all = vcat[...]
    outs = []
    for h in range(2):
        sel = first if h == 0 else jnp.logical_not(first)
        qh = jnp.where(sel, q, jnp.zeros_like(q))
        s_loc = lax.dot_general(qh, k_all[:n_loc], (((1,), (1,)), ((), ())), preferred_element_type=F32)
        s_ctx = lax.dot_general(qh, k_all[n_loc:], (((1,), (1,)), ((), ())), preferred_element_type=F32)
        p_rows = []
        l_rows = []
        for qi in range(NA_ROWS):
            lo = jnp.where(j == 0, max(qi, half), jnp.where(j == n_blocks - 1, min(qi, half), qi))
            m_lo = min(qi, half) // 2
            m_hi = (max(qi, half) + NA_WIN_H - 1) // 2
            n_m = m_hi - m_lo + 1
            key_row = 2 * m_lo + lax.broadcasted_iota(jnp.int32, (1, n_m * LANES), 1) // w
            seen = (key_row >= lo) & (key_row < lo + NA_WIN_H)
            bias = jnp.concatenate(
                [bias_ref[h, min(max(2 * m - qi + half, 0), 2 * NA_WIN_H - 1)] for m in range(m_lo, m_hi + 1)],
                axis=1)
            rows = slice(qi * w, (qi + 1) * w)
            s_q = jnp.where(seen, s_loc[rows, m_lo * LANES:(m_hi + 1) * LANES] + bias, -jnp.inf)
            s_c = s_ctx[rows]
            mx = jnp.maximum(jnp.max(s_q, axis=-1, keepdims=True), jnp.max(s_c, axis=-1, keepdims=True))
            p_q = jnp.exp(s_q - mx)
            p_c = jnp.exp(s_c - mx)
            l_rows.append(jnp.sum(p_q, axis=-1, keepdims=True) + jnp.sum(p_c, axis=-1, keepdims=True))
            pieces = []
            if m_lo > 0:
                pieces.append(jnp.zeros((w, m_lo * LANES), BF16))
            pieces.append(p_q.astype(BF16))
            if m_hi < NA_PAIRS - 1:
                pieces.append(jnp.zeros((w, (NA_PAIRS - 1 - m_hi) * LANES), BF16))
            pieces.append(p_c.astype(BF16))
            p_rows.append(jnp.concatenate(pieces, axis=1))
        p = jnp.concatenate(p_rows, axis=0)
        l = jnp.concatenate(l_rows, axis=0)
        outs.append(_dot(p, v_all) / l)
    o_ref[...] = jnp.where(first, outs[0], outs[1]).astype(o_ref.dtype)


def _na_attention(qkv, bias, *, batch, seq, ctx_len):
    rows = seq // GRID_W
    nj = rows // NA_ROWS
    tq = NA_ROWS * GRID_W
    tk = 4 * GRID_W
    width = NA_HEADS * NA_HEAD_DIM
    n_pairs = width // LANES
    kblocks = seq // tk
    ctx_blk0 = (batch * seq) // ctx_len

    def kv_spec(slot, col0):
        def imap(hp, b, j):
            return (b * kblocks + jnp.clip(2 * j - 1 + slot, 0, kblocks - 1), col0 + hp)
        return pl.BlockSpec((tk, LANES), imap)

    def ctx_spec(col0):
        return pl.BlockSpec((ctx_len, LANES), lambda hp, b, j: (ctx_blk0 + b, col0 + hp))

    in_specs = ([pl.BlockSpec((tq, LANES), lambda hp, b, j: (b * nj + j, hp))]
                + [kv_spec(s, n_pairs) for s in range(4)] + [ctx_spec(n_pairs)]
                + [kv_spec(s, 2 * n_pairs) for s in range(4)] + [ctx_spec(2 * n_pairs)]
                + [pl.BlockSpec((2,) + bias.shape[1:], lambda hp, b, j: (hp, 0, 0, 0))])
    return pl.pallas_call(
        functools.partial(_na_kernel, n_blocks=nj),
        grid=(n_pairs, batch, nj),
        in_specs=in_specs,
        out_specs=pl.BlockSpec((tq, LANES), lambda hp, b, j: (b * nj + j, hp)),
        out_shape=jax.ShapeDtypeStruct((batch * seq, width), BF16),
        scratch_shapes=[pltpu.VMEM((4 * tk + ctx_len, LANES), BF16),
                        pltpu.VMEM((4 * tk + ctx_len, LANES), BF16)],
        compiler_params=_cparams(3),
    )(*([qkv] * 11), bias)


def _rwkv_kernel(kk_ref, r_ref, b_ref, k_ref, em_ref, v_ref, strict_ref, incl_ref, y_ref, z_s, *, reverse):
    n = pl.program_id(1)
    ta = v_ref.shape[0]
    width = v_ref.shape[1]
    n_pairs = width // LANES
    c_len = RW_CHUNK
    n_chunks = ta // c_len
    n_b = n_chunks * n_pairs

    @pl.when(n == 0)
    def _():
        z_s[...] = jnp.zeros_like(z_s)

    def chunks(t):
        return t.reshape(n_chunks, c_len, width)

    def to_batch(t):
        parts = [t[:, :, j * LANES:(j + 1) * LANES] for j in range(n_pairs)]
        return jnp.stack(parts, axis=1).reshape(n_b, t.shape[1], LANES)

    em = em_ref[0, 0].reshape(n_chunks, 1, width)
    kk_t = chunks(kk_ref[0].astype(F32))
    r_t = chunks(r_ref[0].astype(F32))
    b_t = chunks(b_ref[0].astype(F32))
    k_t = chunks(k_ref[0].astype(F32))
    kk_abs = to_batch(kk_t * em)
    r_abs = to_batch(r_t * em)
    k_bar = to_batch(k_t * em)
    b_bar = to_batch(b_t * em)
    gam = to_batch(em * em)
    kk_t = to_batch(kk_t)
    r_t = to_batch(r_t)
    b_t = to_batch(b_t)
    k_t = to_batch(k_t)
    v_b = to_batch(chunks(v_ref[...].astype(F32)))

    lane = lax.broadcasted_iota(jnp.int32, (1, 1, LANES), 2)
    first = lane < RW_HEAD_DIM
    ri = lax.broadcasted_iota(jnp.int32, (LANES, LANES), 0)
    cj = lax.broadcasted_iota(jnp.int32, (LANES, LANES), 1)
    eye = ri == cj
    blockdiag = (ri // RW_HEAD_DIM) == (cj // RW_HEAD_DIM)
    strict = strict_ref[...] > 0.0
    incl = incl_ref[...] > 0.0

    def stack_heads(t):
        zero = jnp.zeros_like(t)
        return jnp.concatenate([jnp.where(first, t, zero), jnp.where(first, zero, t)], axis=1)

    def bmm(x, y):
        return lax.dot_general(x.astype(BF16), y.astype(BF16), (((2,), (1,)), ((0,), (0,))),
                               preferred_element_type=F32)

    def bmm_nt(x, y):
        return lax.dot_general(x.astype(BF16), y.astype(BF16), (((2,), (2,)), ((0,), (0,))),
                               preferred_element_type=F32)

    def bmm_tn(x, y):
        return lax.dot_general(x.astype(BF16), y.astype(BF16), (((1,), (1,)), ((0,), (0,))),
                               preferred_element_type=F32)

    gm = bmm_nt(jnp.concatenate([kk_t, r_t], axis=1),
                jnp.concatenate([stack_heads(b_t), stack_heads(k_t)], axis=1))
    ab_w = jnp.where(strict, gm[:, :c_len, :LANES], 0.0)
    ak_w = jnp.where(strict, gm[:, :c_len, LANES:], 0.0)
    db_w = jnp.where(incl, gm[:, c_len:, :LANES], 0.0)
    dk_w = jnp.where(incl, gm[:, c_len:, LANES:], 0.0)

    l_bd = stack_heads(ab_w)
    base = RW_INV_BASE
    same = lambda s: (ri // s) == (cj // s)
    n_d = jnp.where(same(base), l_bd, 0.0)
    t_bd = jnp.where(eye, 1.0, 0.0) - n_d
    pw = bmm(n_d, n_d)
    span = 2
    while span < base:
        t_bd = t_bd + bmm(t_bd, pw)
        span *= 2
        if span < base:
            pw = bmm(pw, pw)
    size = base
    while size < c_len:
        off = jnp.where(jnp.logical_and(same(2 * size), jnp.logical_not(same(size))), l_bd, 0.0)
        t_bd = t_bd - bmm(bmm(t_bd, off), t_bd)
        size *= 2
    t_w = t_bd[:, :c_len] + t_bd[:, c_len:]

    v_st = stack_heads(v_b)
    w1 = bmm(ak_w, v_st)
    pq = bmm(t_w, jnp.concatenate([stack_heads(kk_abs), stack_heads(w1)], axis=2))
    dpq = bmm(db_w, jnp.concatenate([stack_heads(pq[:, :, :LANES]), stack_heads(pq[:, :, LANES:])], axis=2))
    r_hat = r_abs - dpq[:, :, :LANES]
    y_loc = bmm(dk_w, v_st) - dpq[:, :, LANES:]
    bpq = bmm_tn(b_bar, pq)
    m_t = jnp.where(eye, jnp.broadcast_to(gam, (n_b, LANES, LANES)), 0.0) - bpq[:, :, :LANES]
    g_t = bmm_tn(k_bar, v_b) - bpq[:, :, LANES:]

    zs = z_s[...]
    order = range(n_chunks - 1, -1, -1) if reverse else range(n_chunks)
    for c in order:
        sl = slice(c * n_pairs, (c + 1) * n_pairs)
        y_c = bmm(r_hat[sl], zs) + y_loc[sl]
        y_ref[0, c * c_len:(c + 1) * c_len, :] = jnp.concatenate([y_c[j] for j in range(n_pairs)], axis=1)
        zh = zs.astype(BF16)
        zl = (zs - zh.astype(F32)).astype(BF16)
        m_c = m_t[sl]
        mh = m_c.astype(BF16)
        ml = (m_c - mh.astype(F32)).astype(BF16)
        zs = jnp.where(blockdiag, bmm(mh, zh) + (bmm(mh, zl) + bmm(ml, zh)) + g_t[sl], 0.0)
    z_s[...] = zs


def _rwkv_stream(kk_t, r_t, b_t, k_t, em, v, *, reverse, batch, seq, ctx_len):
    width = RW_HEADS * RW_HEAD_DIM
    ta = RW_TILE
    nct = ctx_len // ta
    nxt = seq // ta
    n_x = batch * seq
    c_len = RW_CHUNK
    d = 1 if reverse else 0

    def blk(b, n):
        is_ctx = n < nct
        t_ctx = (nct - 1 - n) if reverse else n
        t_x = (nxt - 1 - (n - nct)) if reverse else (n - nct)
        return jnp.where(is_ctx, batch * nxt + b * nct + t_ctx, b * nxt + t_x)

    def xblk(b, n):
        m = jnp.maximum(n, nct) - nct
        return b * nxt + ((nxt - 1 - m) if reverse else m)

    t_i = np.arange(c_len)[:, None]
    s_i = np.arange(LANES)[None, :] % RW_HEAD_DIM
    strict = jnp.asarray((s_i > t_i) if reverse else (s_i < t_i), F32)
    incl = jnp.asarray((s_i >= t_i) if reverse else (s_i <= t_i), F32)

    operand = pl.BlockSpec((1, ta, width), lambda b, n: (d, blk(b, n), 0))
    return pl.pallas_call(
        functools.partial(_rwkv_kernel, reverse=reverse),
        grid=(batch, nct + nxt),
        in_specs=[operand, operand, operand, operand,
                  pl.BlockSpec((1, 1, ta // c_len, width), lambda b, n: (d, blk(b, n), 0, 0)),
                  pl.BlockSpec((ta, width), lambda b, n: (blk(b, n), 0)),
                  _resident(strict.shape), _resident(incl.shape)],
        out_specs=pl.BlockSpec((1, ta, width), lambda b, n: (0, xblk(b, n), 0)),
        out_shape=jax.ShapeDtypeStruct((1, n_x, width), F32),
        scratch_shapes=[pltpu.VMEM((width // LANES, LANES, LANES), F32)],
        compiler_params=_cparams(2),
    )(kk_t, r_t, b_t, k_t, em, v, strict, incl)


def _merge_kernel(x_ref, mod_ref, ona_ref, yf_ref, yb_ref, bon_ref, g_ref, pg_ref, bg_ref, ln_ref,
                  pna_ref, prw_ref, wout_ref, o_ref):
    d = x_ref.shape[1]
    y = yf_ref[0] + yb_ref[0]
    inv_n = 1.0 / RW_HEAD_DIM
    mean = _head_segsum(y, 2) * inv_n
    yc = y - mean
    var = _head_segsum(yc * yc, 2) * inv_n
    yn = yc * lax.rsqrt(var + RW_GN_EPS) * ln_ref[0:1, :] + ln_ref[1:2, :]
    o_rw = ((yn + bon_ref[...]) * g_ref[...]).astype(BF16)
    gates = jax.nn.sigmoid(pg_ref[...] + bg_ref[...])
    m = gates[:, :d] * _dot(ona_ref[...], pna_ref[...]) + gates[:, d:] * _dot(o_rw, prw_ref[...])
    o_ref[...] = x_ref[...] + mod_ref[0, 5:6, :] * _dot(m.astype(BF16), wout_ref[...])


def _merge(x1, mod, o_na, y_f, y_b, bon, g, p_gate, b_gate, ln, p_na, p_rw, w_out, *, tiles_per_seq):
    n_x, d = o_na.shape[0], x1.shape[1]
    width = o_na.shape[1]
    tok = lambda i: (i, 0)
    tok3 = lambda i: (0, i, 0)
    return pl.pallas_call(
        _merge_kernel,
        grid=(n_x // TM,),
        in_specs=[pl.BlockSpec((TM, d), tok),
                  pl.BlockSpec((1, N_MOD, d), lambda i: (i // tiles_per_seq, 0, 0)),
                  pl.BlockSpec((TM, width), tok),
                  pl.BlockSpec((1, TM, width), tok3), pl.BlockSpec((1, TM, width), tok3),
                  pl.BlockSpec((TM, width), tok), pl.BlockSpec((TM, width), tok),
                  pl.BlockSpec((TM, 2 * d), tok),
                  _resident((1, 2 * d)), _resident((2, width)),
                  _resident(p_na.shape), _resident(p_rw.shape), _resident(w_out.shape)],
        out_specs=pl.BlockSpec((TM, d), tok),
        out_shape=jax.ShapeDtypeStruct((n_x, d), F32),
        compiler_params=_cparams(1),
    )(x1, mod, o_na, y_f, y_b, bon, g, p_gate, b_gate, ln, p_na, p_rw, w_out)


def _pad_lora(w_up, direction):
    z = jnp.zeros_like(w_up[0])
    return jnp.concatenate([w_up[0] if direction == 0 else z, w_up[1] if direction == 1 else z], axis=0)


def kernel(x, c, ctx, c_ctx, w_ada, b_ada, norm_ffn1, norm_mix, norm_ffn2, norm_final, ffn1_wg, ffn1_wu, ffn1_wd, ffn2_wg, ffn2_wu, ffn2_wd, w_in, b_gate, na_rpb, rw_mu, rw_w0, rw_w_up, rw_a0, rw_a_up, rw_g_up, rw_k_k, rw_k_a, rw_r_k, rw_ln_w, rw_ln_b, p_na, p_rw, w_out):
    batch, seq, d = x.shape
    ctx_len = ctx.shape[1]
    n_x = batch * seq
    na_width = NA_HEADS * NA_HEAD_DIM
    rw_width = RW_HEADS * RW_HEAD_DIM
    c_rw = 3 * na_width
    c_gate = c_rw + 3 * rw_width + 4 * RW_LORA + RW_GATE_LORA
    assert w_ada.shape[0] == 1, "single layer"
    assert seq % TM == 0 and (batch * ctx_len) % TM == 0 and ctx_len % RW_TILE == 0 and batch + 1 <= 8
    assert seq % (NA_ROWS * GRID_W) == 0 and seq // (NA_ROWS * GRID_W) >= 2 and ctx_len == 4 * GRID_W
    row = lambda t: t.reshape(1, -1)

    cs = jnp.concatenate([c, c_ctx[None], jnp.zeros((8 - batch - 1, d), F32)], axis=0)
    mod = _ada_mod(cs, w_ada[0], b_ada[0])[:batch + 1].reshape(batch + 1, N_MOD, d)

    bf = lambda t: t.astype(BF16)

    x1 = _ffn(x.reshape(n_x, d), ctx.reshape(batch * ctx_len, d), mod, row(norm_ffn1[0]),
              bf(ffn1_wg[0]), bf(ffn1_wu[0]), bf(ffn1_wd[0]), row(norm_final),
              mod0=0, final=False, tiles_per_seq=seq // TM, n_seq=batch)
    par = jnp.concatenate([rw_w0[0], rw_a0[0], row(rw_k_k[0]), row(rw_k_a[0]), row(rw_r_k[0]),
                           jnp.zeros((1, rw_width), F32)], axis=0)
    wup = jnp.stack([_pad_lora(rw_w_up[0], 0), _pad_lora(rw_w_up[0], 1)])
    aup = bf(jnp.stack([_pad_lora(rw_a_up[0], 0), _pad_lora(rw_a_up[0], 1)]))
    qkv, p_gate, kk_t, r_t, b_t, k_t, em, v_rw, g, bon = _inproj(
        x1, mod, row(norm_mix[0]), bf(w_in[0]), rw_mu[0], par, wup, aup, bf(rw_g_up[0]),
        c_rw=c_rw, c_gate=c_gate, batch=batch, seq=seq, ctx_len=ctx_len)

    bias = _na_pair_bias(na_rpb[0])
    o_na = _na_attention(qkv, bias, batch=batch, seq=seq, ctx_len=ctx_len)

    y_f, y_b = [_rwkv_stream(kk_t, r_t, b_t, k_t, em, v_rw, reverse=rev, batch=batch, seq=seq, ctx_len=ctx_len)
                for rev in (False, True)]

    ln = jnp.stack([rw_ln_w[0], rw_ln_b[0]], axis=0)
    x2 = _merge(x1, mod, o_na, y_f, y_b, bon, g, p_gate, row(b_gate[0]), ln,
                bf(p_na[0]), bf(p_rw[0]), bf(w_out[0]), tiles_per_seq=seq // TM)
    out = _ffn(x2, None, mod, row(norm_ffn2[0]), bf(ffn2_wg[0]), bf(ffn2_wu[0]), bf(ffn2_wd[0]), row(norm_final),
               mod0=6, final=True, tiles_per_seq=seq // TM, n_seq=batch)
    return out.reshape(batch, seq, d)
```

```python
import functools
import math

import numpy as np
import jax
import jax.numpy as jnp
from jax import lax
from jax.experimental import pallas as pl
from jax.experimental.pallas import tpu as pltpu

F32 = jnp.float32
BF16 = jnp.bfloat16

NORM_EPS = 1e-6
RW_GN_EPS = 64e-5
N_MOD = 9
GRID_W = 64
NA_HEADS = 8
NA_HEAD_DIM = 64
NA_WIN_H = 8
NA_WIN_W = 16
RW_HEADS = 8
RW_HEAD_DIM = 64
RW_LORA = 64
RW_GATE_LORA = 128

LANES = 128
VMEM_LIMIT = 56 * 1024 * 1024

TM = 512
TM_IN = 256
NA_ROWS = 8
RW_TILE = 256
RW_CHUNK = 64
RW_INV_BASE = 8


def _cparams(n_axes):
    return pltpu.CompilerParams(dimension_semantics=("arbitrary",) * n_axes,
                                vmem_limit_bytes=VMEM_LIMIT)


def _resident(shape):
    nd = len(shape)
    return pl.BlockSpec(shape, lambda *_: (0,) * nd, pipeline_mode=pl.Buffered(1))


def _rmsnorm(x, g):
    return x * lax.rsqrt(jnp.mean(x * x, axis=-1, keepdims=True) + NORM_EPS) * g


def _split2(x):
    hi = x.astype(BF16)
    lo = (x - hi.astype(F32)).astype(BF16)
    return hi, lo


def _dot(a, b):
    return jnp.dot(a, b, preferred_element_type=F32)


def _dot_split_lhs(x, w_bf16, parts):
    acc = None
    rem = x
    for _ in range(parts):
        p = rem.astype(BF16)
        rem = rem - p.astype(F32)
        t = _dot(p, w_bf16)
        acc = t if acc is None else acc + t
    return acc


def _dot_split_rhs(w_bf16, x, parts):
    acc = None
    rem = x
    for _ in range(parts):
        p = rem.astype(BF16)
        rem = rem - p.astype(F32)
        t = _dot(w_bf16, p)
        acc = t if acc is None else acc + t
    return acc


def _dot3(a, b):
    ah, al = _split2(a)
    bh, bl = _split2(b)
    return _dot(ah, bh) + (_dot(ah, bl) + _dot(al, bh))


def _ada_kernel(c_ref, w_ref, b_ref, o_ref):
    c = c_ref[...]
    s = c * jax.nn.sigmoid(c)
    o_ref[...] = _dot3(s, w_ref[...]) + b_ref[...]


def _ada_mod(cs, w_ada, b_ada):
    d = cs.shape[1]
    nm = w_ada.shape[1] // d
    return pl.pallas_call(
        _ada_kernel,
        grid=(nm,),
        in_specs=[pl.BlockSpec((8, d), lambda j: (0, 0)),
                  pl.BlockSpec((d, d), lambda j: (0, j)),
                  pl.BlockSpec((1, d), lambda j: (0, j))],
        out_specs=pl.BlockSpec((8, d), lambda j: (0, j)),
        out_shape=jax.ShapeDtypeStruct((8, nm * d), F32),
        compiler_params=_cparams(1),
    )(cs, w_ada, b_ada.reshape(1, -1))


def _ffn_kernel(*refs, mod0, f_chunk, final, n_head_tiles):
    if n_head_tiles is None:
        x_ref, mod_ref, g_ref, wg_ref, wu_ref, wd_ref, gf_ref, o_ref = refs
        x = x_ref[...]
    else:
        x_ref, tail_ref, mod_ref, g_ref, wg_ref, wu_ref, wd_ref, gf_ref, o_ref = refs
        x = jnp.where(pl.program_id(0) < n_head_tiles, x_ref[...], tail_ref[...])
    shift = mod_ref[0, mod0:mod0 + 1, :]
    scale = mod_ref[0, mod0 + 1:mod0 + 2, :]
    gate = mod_ref[0, mod0 + 2:mod0 + 3, :]
    h = (_rmsnorm(x, g_ref[...]) * (1.0 + scale) + shift).astype(BF16)
    d_ff = wg_ref.shape[1]
    acc = None
    for f0 in range(0, d_ff, f_chunk):
        gg = _dot(h, wg_ref[:, f0:f0 + f_chunk])
        uu = _dot(h, wu_ref[:, f0:f0 + f_chunk])
        a = (gg * jax.nn.sigmoid(gg) * uu).astype(BF16)
        t = _dot(a, wd_ref[f0:f0 + f_chunk, :])
        acc = t if acc is None else acc + t
    y = x + 0.5 * gate * acc
    if final:
        y = _rmsnorm(y, gf_ref[...])
    o_ref[...] = y


def _ffn(xs, tail, mod, g, wg, wu, wd, gf, *, mod0, final, tiles_per_seq, n_seq):
    n_head, d = xs.shape
    n = n_head + (0 if tail is None else tail.shape[0])
    d_ff = wg.shape[1]
    f_chunk = d_ff // 2 if (d_ff // 2) % LANES == 0 else d_ff
    seq_of = lambda i: jnp.minimum(i // tiles_per_seq, n_seq)
    head_tiles = n_head // TM
    tokens = [pl.BlockSpec((TM, d), lambda i: (jnp.minimum(i, head_tiles - 1), 0))]
    if tail is not None:
        tokens.append(pl.BlockSpec((TM, d), lambda i: (jnp.maximum(i - head_tiles, 0), 0)))
    return pl.pallas_call(
        functools.partial(_ffn_kernel, mod0=mod0, f_chunk=f_chunk, final=final,
                          n_head_tiles=None if tail is None else head_tiles),
        grid=(n // TM,),
        in_specs=tokens + [pl.BlockSpec((1, N_MOD, d), lambda i: (seq_of(i), 0, 0)),
                           _resident((1, d)), _resident((d, d_ff)), _resident((d, d_ff)), _resident((d_ff, d)),
                           _resident((1, d))],
        out_specs=pl.BlockSpec((TM, d), lambda i: (i, 0)),
        out_shape=jax.ShapeDtypeStruct((n, d), F32),
        compiler_params=_cparams(1),
    )(*([xs] if tail is None else [xs, tail]), mod, g, wg, wu, wd, gf)


def _head_segsum(x):
    r = lax.broadcasted_iota(jnp.int32, (LANES, LANES), 0) // RW_HEAD_DIM
    c = lax.broadcasted_iota(jnp.int32, (LANES, LANES), 1) // RW_HEAD_DIM
    e = jnp.where(r == c, 1.0, 0.0).astype(BF16)
    xb = x.astype(BF16)
    return jnp.concatenate([_dot(xb[:, j:j + LANES], e) for j in range(0, x.shape[1], LANES)], axis=1)


def _inproj_kernel(x_ref, xprev_ref, xnext_ref, mod_ref, g_ref, w_ref, mu_ref, par_ref, wup_ref, aup_ref,
                   gup_ref, tri_ref, qkv_ref, gate_ref, kk_ref, r_ref, b_ref, k_ref, em_ref, v_ref, g_out_ref,
                   bon_ref, *, c_rw, c_gate, n_x_tiles, tiles_x, tiles_ctx):
    i = pl.program_id(0)
    ta = x_ref.shape[0]
    width = v_ref.shape[1]
    c_len = RW_CHUNK
    n_chunks = ta // c_len
    in_x = i < n_x_tiles
    tile = jnp.where(in_x, i % tiles_x, (i - n_x_tiles) % tiles_ctx)
    last_tile = jnp.where(in_x, tiles_x - 1, tiles_ctx - 1)

    x_all = jnp.concatenate([xprev_ref[...], x_ref[...], xnext_ref[...]], axis=0)
    h_all = (_rmsnorm(x_all, g_ref[...]) * (1.0 + mod_ref[0, 4:5, :]) + mod_ref[0, 3:4, :]).astype(BF16)
    h = h_all[8:ta + 8]
    zero = jnp.zeros((8, h.shape[1]), BF16)
    h_all = jnp.concatenate([jnp.where(tile == 0, zero, h_all[:8]), h,
                             jnp.where(tile == last_tile, zero, h_all[ta + 8:])], axis=0)
    p_all = _dot(h_all, w_ref[:, c_rw:c_gate])

    piece = 4 * LANES
    pieces = ([(qkv_ref, c0, c0) for c0 in range(0, c_rw, piece)]
              + [(gate_ref, c0, c_gate + c0) for c0 in range(0, w_ref.shape[1] - c_gate, piece)])

    def emit_dense(count):
        for _ in range(min(count, len(pieces))):
            ref, dst, src = pieces.pop(0)
            ref[:, dst:dst + piece] = _dot(h, w_ref[:, src:src + piece]).astype(ref.dtype)

    p = p_all[8:ta + 8]
    f = p + mu_ref[0:1, :] * (p_all[7:ta + 7] - p) + mu_ref[1:2, :] * (p_all[9:ta + 9] - p)
    emit_dense(2)

    r = f[:, 0:width]
    k = f[:, width:2 * width]
    v = f[:, 2 * width:3 * width]
    o = 3 * width
    th = jnp.tanh(f[:, o:o + LANES]).astype(BF16)
    al = f[:, o + LANES:o + 2 * LANES].astype(BF16)
    gl = f[:, o + 2 * LANES:o + 3 * LANES]
    k_k = par_ref[4:5, :]
    k_a = par_ref[5:6, :]
    r_k = par_ref[6:7, :]

    v_ref[...] = v.astype(BF16)
    g_out_ref[...] = _dot(jax.nn.sigmoid(gl).astype(BF16), gup_ref[...])
    kk = k * k_k
    kk = kk * lax.rsqrt(jnp.maximum(_head_segsum(kk * kk), 1e-12))
    emit_dense(1)

    def chunks(t):
        return t.reshape(n_chunks, c_len, width)

    kd_sum = None
    for d in range(2):
        z = par_ref[d:d + 1, :] + _dot(th, wup_ref[d])
        lw = -math.exp(-0.5) * jax.nn.sigmoid(z)
        a = jax.nn.sigmoid(par_ref[2 + d:3 + d, :] + _dot(al, aup_ref[d]))
        kd = k * (1.0 + (a - 1.0) * k_a)
        kd_sum = kd if kd_sum is None else kd_sum + kd
        bb = kk * a
        emit_dense(1)
        ci3 = chunks(_dot(tri_ref[d], lw.astype(BF16)))
        ce3 = ci3 - chunks(lw)
        t_last = 0 if d == 1 else c_len - 1
        cm = 0.5 * ci3[:, t_last:t_last + 1, :]
        e_k = jnp.exp(cm - ci3)
        emit_dense(1)
        kk_ref[d] = (chunks(kk) * jnp.exp(ce3 - cm)).reshape(ta, width).astype(BF16)
        r_ref[d] = (chunks(r) * jnp.exp(ci3 - cm)).reshape(ta, width).astype(BF16)
        b_ref[d] = (chunks(bb) * e_k).reshape(ta, width).astype(BF16)
        k_ref[d] = (chunks(kd) * e_k).reshape(ta, width).astype(BF16)
        em_ref[d, 0] = jnp.exp(cm).reshape(n_chunks, width)
    emit_dense(len(pieces))
    bon_ref[...] = _head_segsum(r * kd_sum * r_k) * v


def _inproj(xs, mod, g, w_in, mu, par, wup, aup, gup, *, c_rw, c_gate, batch, seq, ctx_len):
    n, d = xs.shape
    cols = w_in.shape[1]
    width = RW_HEADS * RW_HEAD_DIM
    ta = RW_TILE
    c_len = RW_CHUNK
    n_tiles = n // ta
    tiles_x = seq // ta
    sub = ta // 8
    last8 = n // 8 - 1
    seq_of = lambda i: jnp.minimum(i // tiles_x, batch)
    idx = np.arange(ta)
    same_chunk = (idx[:, None] // c_len) == (idx[None, :] // c_len)
    tri = jnp.asarray(np.stack([same_chunk & (idx[None, :] <= idx[:, None]),
                                same_chunk & (idx[None, :] >= idx[:, None])]), BF16)
    tok2 = pl.BlockSpec((2, ta, width), lambda i: (0, i, 0))
    tok = lambda c: pl.BlockSpec((ta, c), lambda i: (i, 0))
    sds = jax.ShapeDtypeStruct
    return pl.pallas_call(
        functools.partial(_inproj_kernel, c_rw=c_rw, c_gate=c_gate, n_x_tiles=batch * tiles_x, tiles_x=tiles_x,
                          tiles_ctx=ctx_len // ta),
        grid=(n_tiles,),
        in_specs=[tok(d),
                  pl.BlockSpec((8, d), lambda i: (jnp.maximum(i * sub - 1, 0), 0)),
                  pl.BlockSpec((8, d), lambda i: (jnp.minimum((i + 1) * sub, last8), 0)),
                  pl.BlockSpec((1, N_MOD, d), lambda i: (seq_of(i), 0, 0)),
                  _resident((1, d)), _resident((d, cols)),
                  _resident(mu.shape), _resident(par.shape), _resident(wup.shape), _resident(aup.shape),
                  _resident(gup.shape), _resident(tri.shape)],
        out_specs=[tok(c_rw), tok(cols - c_gate), tok2, tok2, tok2, tok2,
                   pl.BlockSpec((2, 1, ta // c_len, width), lambda i: (0, i, 0, 0)),
                   tok(width), tok(width), tok(width)],
        out_shape=[sds((n, c_rw), BF16), sds((n, cols - c_gate), F32)]
                  + [sds((2, n, width), BF16)] * 4
                  + [sds((2, n_tiles, ta // c_len, width), F32), sds((n, width), BF16),
                     sds((n, width), F32), sds((n, width), F32)],
        compiler_params=_cparams(1),
    )(xs, xs, xs, mod, g, w_in, mu, par, wup, aup, gup, tri)


NA_KEY_ROWS = 2 * NA_ROWS
NA_PAIRS = NA_KEY_ROWS // 2


def _na_pair_bias(rpb):
    w = GRID_W
    qc = np.arange(w)[:, None]
    kc = np.arange(w)[None, :]
    s_c = np.clip(qc - NA_WIN_W // 2, 0, w - NA_WIN_W)
    col_ok = (kc >= s_c) & (kc < s_c + NA_WIN_W)
    dc = np.clip(kc - qc + NA_WIN_W - 1, 0, 2 * NA_WIN_W - 2)
    b = jnp.where(col_ok[None, None], rpb[:, :, dc], -jnp.inf).astype(F32)
    ninf = jnp.full((rpb.shape[0], 1, w, w), -jnp.inf, F32)
    b = jnp.concatenate([ninf, b, ninf], axis=1)
    return jnp.concatenate([b[:, :-1], b[:, 1:]], axis=-1)


def _na_kernel(q_ref, k0_ref, k1_ref, k2_ref, k3_ref, kc_ref, v0_ref, v1_ref, v2_ref, v3_ref, vc_ref,
               bias_ref, o_ref, kcat, vcat, *, n_blocks):
    j = pl.program_id(2)
    blk = k0_ref.shape[0]
    for s, (kr, vr) in enumerate(((k0_ref, v0_ref), (k1_ref, v1_ref), (k2_ref, v2_ref), (k3_ref, v3_ref),
                                  (kc_ref, vc_ref))):
        kcat[s * blk:s * blk + kr.shape[0], :] = kr[...]
        vcat[s * blk:s * blk + vr.shape[0], :] = vr[...]
    n_loc = 4 * blk
    w = GRID_W
    half = NA_WIN_H // 2
    lane = lax.broadcasted_iota(jnp.int32, (1, LANES), 1)
    first = lane < NA_HEAD_DIM
    q = q_ref[...] * (NA_HEAD_DIM ** -0.5)
    k_all = kcat[...]
    v_all = vcat[...]
    outs = []
    for h in range(2):
        sel = first if h == 0 else jnp.logical_not(first)
        qh = jnp.where(sel, q, jnp.zeros_like(q))
        s_loc = lax.dot_general(qh, k_all[:n_loc], (((1,), (1,)), ((), ())), preferred_element_type=F32)
        s_ctx = lax.dot_general(qh, k_all[n_loc:], (((1,), (1,)), ((), ())), preferred_element_type=F32)
        p_rows = []
        l_rows = []
        for qi in range(NA_ROWS):
            lo = jnp.where(j == 0, max(qi, half), jnp.where(j == n_blocks - 1, min(qi, half), qi))
            m_lo = min(qi, half) // 2
            m_hi = (max(qi, half) + NA_WIN_H - 1) // 2
            n_m = m_hi - m_lo + 1
            key_row = 2 * m_lo + lax.broadcasted_iota(jnp.int32, (1, n_m * LANES), 1) // w
            seen = (key_row >= lo) & (key_row < lo + NA_WIN_H)
            bias = jnp.concatenate(
                [bias_ref[h, min(max(2 * m - qi + half, 0), 2 * NA_WIN_H - 1)] for m in range(m_lo, m_hi + 1)],
                axis=1)
            rows = slice(qi * w, (qi + 1) * w)
            s_q = jnp.where(seen, s_loc[rows, m_lo * LANES:(m_hi + 1) * LANES] + bias, -jnp.inf)
            s_c = s_ctx[rows]
            mx = jnp.maximum(jnp.max(s_q, axis=-1, keepdims=True), jnp.max(s_c, axis=-1, keepdims=True))
            p_q = jnp.exp(s_q - mx)
            p_c = jnp.exp(s_c - mx)
            l_rows.append(jnp.sum(p_q, axis=-1, keepdims=True) + jnp.sum(p_c, axis=-1, keepdims=True))
            pieces = []
            if m_lo > 0:
                pieces.append(jnp.zeros((w, m_lo * LANES), BF16))
            pieces.append(p_q.astype(BF16))
            if m_hi < NA_PAIRS - 1:
                pieces.append(jnp.zeros((w, (NA_PAIRS - 1 - m_hi) * LANES), BF16))
            pieces.append(p_c.astype(BF16))
            p_rows.append(jnp.concatenate(pieces, axis=1))
        p = jnp.concatenate(p_rows, axis=0)
        l = jnp.concatenate(l_rows, axis=0)
        outs.append(_dot(p, v_all) / l)
    o_ref[...] = jnp.where(first, outs[0], outs[1]).astype(o_ref.dtype)


def _na_attention(qkv, bias, *, batch, seq, ctx_len):
    rows = seq // GRID_W
    nj = rows // NA_ROWS
    tq = NA_ROWS * GRID_W
    tk = 4 * GRID_W
    width = NA_HEADS * NA_HEAD_DIM
    n_pairs = width // LANES
    kblocks = seq // tk
    ctx_blk0 = (batch * seq) // ctx_len

    def kv_spec(slot, col0):
        def imap(hp, b, j):
            return (b * kblocks + jnp.clip(2 * j - 1 + slot, 0, kblocks - 1), col0 + hp)
        return pl.BlockSpec((tk, LANES), imap)

    def ctx_spec(col0):
        return pl.BlockSpec((ctx_len, LANES), lambda hp, b, j: (ctx_blk0 + b, col0 + hp))

    in_specs = ([pl.BlockSpec((tq, LANES), lambda hp, b, j: (b * nj + j, hp))]
                + [kv_spec(s, n_pairs) for s in range(4)] + [ctx_spec(n_pairs)]
                + [kv_spec(s, 2 * n_pairs) for s in range(4)] + [ctx_spec(2 * n_pairs)]
                + [pl.BlockSpec((2,) + bias.shape[1:], lambda hp, b, j: (hp, 0, 0, 0))])
    return pl.pallas_call(
        functools.partial(_na_kernel, n_blocks=nj),
        grid=(n_pairs, batch, nj),
        in_specs=in_specs,
        out_specs=pl.BlockSpec((tq, LANES), lambda hp, b, j: (b * nj + j, hp)),
        out_shape=jax.ShapeDtypeStruct((batch * seq, width), BF16),
        scratch_shapes=[pltpu.VMEM((4 * tk + ctx_len, LANES), BF16),
                        pltpu.VMEM((4 * tk + ctx_len, LANES), BF16)],
        compiler_params=_cparams(3),
    )(*([qkv] * 11), bias)


def _rwkv_kernel(kk_ref, r_ref, b_ref, k_ref, em_ref, v_ref, strict_ref, incl_ref, y_ref, z_s, *, reverse):
    n = pl.program_id(1)
    ta = v_ref.shape[0]
    width = v_ref.shape[1]
    n_pairs = width // LANES
    c_len = RW_CHUNK
    n_chunks = ta // c_len
    n_b = n_chunks * n_pairs

    @pl.when(n == 0)
    def _():
        z_s[...] = jnp.zeros_like(z_s)

    def chunks(t):
        return t.reshape(n_chunks, c_len, width)

    def to_batch(t):
        parts = [t[:, :, j * LANES:(j + 1) * LANES] for j in range(n_pairs)]
        return jnp.stack(parts, axis=1).reshape(n_b, t.shape[1], LANES)

    em = em_ref[0, 0].reshape(n_chunks, 1, width)
    kk_t = chunks(kk_ref[0].astype(F32))
    r_t = chunks(r_ref[0].astype(F32))
    b_t = chunks(b_ref[0].astype(F32))
    k_t = chunks(k_ref[0].astype(F32))
    kk_abs = to_batch(kk_t * em)
    r_abs = to_batch(r_t * em)
    k_bar = to_batch(k_t * em)
    b_bar = to_batch(b_t * em)
    gam = to_batch(em * em)
    kk_t = to_batch(kk_t)
    r_t = to_batch(r_t)
    b_t = to_batch(b_t)
    k_t = to_batch(k_t)
    v_b = to_batch(chunks(v_ref[...].astype(F32)))

    lane = lax.broadcasted_iota(jnp.int32, (1, 1, LANES), 2)
    first = lane < RW_HEAD_DIM
    ri = lax.broadcasted_iota(jnp.int32, (LANES, LANES), 0)
    cj = lax.broadcasted_iota(jnp.int32, (LANES, LANES), 1)
    eye = ri == cj
    blockdiag = (ri // RW_HEAD_DIM) == (cj // RW_HEAD_DIM)
    strict = strict_ref[...] > 0.0
    incl = incl_ref[...] > 0.0

    def stack_heads(t):
        zero = jnp.zeros_like(t)
        return jnp.concatenate([jnp.where(first, t, zero), jnp.where(first, zero, t)], axis=1)

    def bmm(x, y):
        return lax.dot_general(x.astype(BF16), y.astype(BF16), (((2,), (1,)), ((0,), (0,))),
                               preferred_element_type=F32)

    def bmm_nt(x, y):
        return lax.dot_general(x.astype(BF16), y.astype(BF16), (((2,), (2,)), ((0,), (0,))),
                               preferred_element_type=F32)

    def bmm_tn(x, y):
        return lax.dot_general(x.astype(BF16), y.astype(BF16), (((1,), (1,)), ((0,), (0,))),
                               preferred_element_type=F32)

    gm = bmm_nt(jnp.concatenate([kk_t, r_t], axis=1),
                jnp.concatenate([stack_heads(b_t), stack_heads(k_t)], axis=1))
    ab_w = jnp.where(strict, gm[:, :c_len, :LANES], 0.0)
    ak_w = jnp.where(strict, gm[:, :c_len, LANES:], 0.0)
    db_w = jnp.where(incl, gm[:, c_len:, :LANES], 0.0)
    dk_w = jnp.where(incl, gm[:, c_len:, LANES:], 0.0)

    l_bd = stack_heads(ab_w)
    base = RW_INV_BASE
    same = lambda s: (ri // s) == (cj // s)
    n_d = jnp.where(same(base), l_bd, 0.0)
    t_bd = jnp.where(eye, 1.0, 0.0) - n_d
    pw = bmm(n_d, n_d)
    span = 2
    while span < base:
        t_bd = t_bd + bmm(t_bd, pw)
        span *= 2
        if span < base:
            pw = bmm(pw, pw)
    size = base
    while size < c_len:
        off = jnp.where(jnp.logical_and(same(2 * size), jnp.logical_not(same(size))), l_bd, 0.0)
        t_bd = t_bd - bmm(bmm(t_bd, off), t_bd)
        size *= 2
    t_w = t_bd[:, :c_len] + t_bd[:, c_len:]

    akv = bmm(jnp.concatenate([ak_w, dk_w], axis=1), stack_heads(v_b))
    pq = bmm(t_w, jnp.concatenate([stack_heads(kk_abs), stack_heads(akv[:, :c_len])], axis=2))
    dpq = bmm(db_w, jnp.concatenate([stack_heads(pq[:, :, :LANES]), stack_heads(pq[:, :, LANES:])], axis=2))
    r_hat = r_abs - dpq[:, :, :LANES]
    y_loc = akv[:, c_len:] - dpq[:, :, LANES:]
    bpq = bmm_tn(b_bar, pq)
    m_t = jnp.where(eye, jnp.broadcast_to(gam, (n_b, LANES, LANES)), 0.0) - bpq[:, :, :LANES]
    g_t = bmm_tn(k_bar, v_b) - bpq[:, :, LANES:]

    zs = z_s[...]
    order = range(n_chunks - 1, -1, -1) if reverse else range(n_chunks)
    for c in order:
        sl = slice(c * n_pairs, (c + 1) * n_pairs)
        both = bmm(jnp.concatenate([r_hat[sl], m_t[sl]], axis=1), zs)
        y_c = both[:, :c_len] + y_loc[sl]
        y_ref[0, c * c_len:(c + 1) * c_len, :] = jnp.concatenate([y_c[j] for j in range(n_pairs)], axis=1)
        zs = jnp.where(blockdiag, both[:, c_len:] + g_t[sl], 0.0)
    z_s[...] = zs


def _rwkv_stream(kk_t, r_t, b_t, k_t, em, v, *, reverse, batch, seq, ctx_len):
    width = RW_HEADS * RW_HEAD_DIM
    ta = RW_TILE
    nct = ctx_len // ta
    nxt = seq // ta
    n_x = batch * seq
    c_len = RW_CHUNK
    d = 1 if reverse else 0

    def blk(b, n):
        is_ctx = n < nct
        t_ctx = (nct - 1 - n) if reverse else n
        t_x = (nxt - 1 - (n - nct)) if reverse else (n - nct)
        return jnp.where(is_ctx, batch * nxt + b * nct + t_ctx, b * nxt + t_x)

    def xblk(b, n):
        m = jnp.maximum(n, nct) - nct
        return b * nxt + ((nxt - 1 - m) if reverse else m)

    t_i = np.arange(c_len)[:, None]
    s_i = np.arange(LANES)[None, :] % RW_HEAD_DIM
    strict = jnp.asarray((s_i > t_i) if reverse else (s_i < t_i), F32)
    incl = jnp.asarray((s_i >= t_i) if reverse else (s_i <= t_i), F32)

    operand = pl.BlockSpec((1, ta, width), lambda b, n: (d, blk(b, n), 0))
    return pl.pallas_call(
        functools.partial(_rwkv_kernel, reverse=reverse),
        grid=(batch, nct + nxt),
        in_specs=[operand, operand, operand, operand,
                  pl.BlockSpec((1, 1, ta // c_len, width), lambda b, n: (d, blk(b, n), 0, 0)),
                  pl.BlockSpec((ta, width), lambda b, n: (blk(b, n), 0)),
                  _resident(strict.shape), _resident(incl.shape)],
        out_specs=pl.BlockSpec((1, ta, width), lambda b, n: (0, xblk(b, n), 0)),
        out_shape=jax.ShapeDtypeStruct((1, n_x, width), F32),
        scratch_shapes=[pltpu.VMEM((width // LANES, LANES, LANES), F32)],
        compiler_params=_cparams(2),
    )(kk_t, r_t, b_t, k_t, em, v, strict, incl)


def _merge_kernel(x_ref, mod_ref, ona_ref, yf_ref, yb_ref, bon_ref, g_ref, pg_ref, bg_ref, ln_ref,
                  pna_ref, prw_ref, wout_ref, o_ref):
    d = x_ref.shape[1]
    y = yf_ref[0] + yb_ref[0]
    inv_n = 1.0 / RW_HEAD_DIM
    mean = _head_segsum(y) * inv_n
    yc = y - mean
    var = _head_segsum(yc * yc) * inv_n
    yn = yc * lax.rsqrt(var + RW_GN_EPS) * ln_ref[0:1, :] + ln_ref[1:2, :]
    o_rw = ((yn + bon_ref[...]) * g_ref[...]).astype(BF16)
    gates = jax.nn.sigmoid(pg_ref[...] + bg_ref[...])
    m = gates[:, :d] * _dot(ona_ref[...], pna_ref[...]) + gates[:, d:] * _dot(o_rw, prw_ref[...])
    o_ref[...] = x_ref[...] + mod_ref[0, 5:6, :] * _dot(m.astype(BF16), wout_ref[...])


def _merge(x1, mod, o_na, y_f, y_b, bon, g, p_gate, b_gate, ln, p_na, p_rw, w_out, *, tiles_per_seq):
    n_x, d = o_na.shape[0], x1.shape[1]
    width = o_na.shape[1]
    tok = lambda i: (i, 0)
    tok3 = lambda i: (0, i, 0)
    return pl.pallas_call(
        _merge_kernel,
        grid=(n_x // TM,),
        in_specs=[pl.BlockSpec((TM, d), tok),
                  pl.BlockSpec((1, N_MOD, d), lambda i: (i // tiles_per_seq, 0, 0)),
                  pl.BlockSpec((TM, width), tok),
                  pl.BlockSpec((1, TM, width), tok3), pl.BlockSpec((1, TM, width), tok3),
                  pl.BlockSpec((TM, width), tok), pl.BlockSpec((TM, width), tok),
                  pl.BlockSpec((TM, 2 * d), tok),
                  _resident((1, 2 * d)), _resident((2, width)),
                  _resident(p_na.shape), _resident(p_rw.shape), _resident(w_out.shape)],
        out_specs=pl.BlockSpec((TM, d), tok),
        out_shape=jax.ShapeDtypeStruct((n_x, d), F32),
        compiler_params=_cparams(1),
    )(x1, mod, o_na, y_f, y_b, bon, g, p_gate, b_gate, ln, p_na, p_rw, w_out)


def _pad_lora(w_up, direction):
    z = jnp.zeros_like(w_up[0])
    return jnp.concatenate([w_up[0] if direction == 0 else z, w_up[1] if direction == 1 else z], axis=0)


def kernel(x, c, ctx, c_ctx, w_ada, b_ada, norm_ffn1, norm_mix, norm_ffn2, norm_final, ffn1_wg, ffn1_wu, ffn1_wd, ffn2_wg, ffn2_wu, ffn2_wd, w_in, b_gate, na_rpb, rw_mu, rw_w0, rw_w_up, rw_a0, rw_a_up, rw_g_up, rw_k_k, rw_k_a, rw_r_k, rw_ln_w, rw_ln_b, p_na, p_rw, w_out):
    batch, seq, d = x.shape
    ctx_len = ctx.shape[1]
    n_x = batch * seq
    na_width = NA_HEADS * NA_HEAD_DIM
    rw_width = RW_HEADS * RW_HEAD_DIM
    c_rw = 3 * na_width
    c_gate = c_rw + 3 * rw_width + 4 * RW_LORA + RW_GATE_LORA
    assert w_ada.shape[0] == 1, "single layer"
    assert seq % TM == 0 and (batch * ctx_len) % TM == 0 and ctx_len % RW_TILE == 0 and batch + 1 <= 8
    assert seq % (NA_ROWS * GRID_W) == 0 and seq // (NA_ROWS * GRID_W) >= 2 and ctx_len == 4 * GRID_W
    row = lambda t: t.reshape(1, -1)

    cs = jnp.concatenate([c, c_ctx[None], jnp.zeros((8 - batch - 1, d), F32)], axis=0)
    mod = _ada_mod(cs, w_ada[0], b_ada[0])[:batch + 1].reshape(batch + 1, N_MOD, d)

    bf = lambda t: t.astype(BF16)

    x1 = _ffn(x.reshape(n_x, d), ctx.reshape(batch * ctx_len, d), mod, row(norm_ffn1[0]),
              bf(ffn1_wg[0]), bf(ffn1_wu[0]), bf(ffn1_wd[0]), row(norm_final),
              mod0=0, final=False, tiles_per_seq=seq // TM, n_seq=batch)
    par = jnp.concatenate([rw_w0[0], rw_a0[0], row(rw_k_k[0]), row(rw_k_a[0]), row(rw_r_k[0]),
                           jnp.zeros((1, rw_width), F32)], axis=0)
    wup = bf(jnp.stack([_pad_lora(rw_w_up[0], 0), _pad_lora(rw_w_up[0], 1)]))
    aup = bf(jnp.stack([_pad_lora(rw_a_up[0], 0), _pad_lora(rw_a_up[0], 1)]))
    qkv, p_gate, kk_t, r_t, b_t, k_t, em, v_rw, g, bon = _inproj(
        x1, mod, row(norm_mix[0]), bf(w_in[0]), rw_mu[0], par, wup, aup, bf(rw_g_up[0]),
        c_rw=c_rw, c_gate=c_gate, batch=batch, seq=seq, ctx_len=ctx_len)

    bias = _na_pair_bias(na_rpb[0])
    o_na = _na_attention(qkv, bias, batch=batch, seq=seq, ctx_len=ctx_len)

    y_f, y_b = [_rwkv_stream(kk_t, r_t, b_t, k_t, em, v_rw, reverse=rev, batch=batch, seq=seq, ctx_len=ctx_len)
                for rev in (False, True)]

    ln = jnp.stack([rw_ln_w[0], rw_ln_b[0]], axis=0)
    x2 = _merge(x1, mod, o_na, y_f, y_b, bon, g, p_gate, row(b_gate[0]), ln,
                bf(p_na[0]), bf(p_rw[0]), bf(w_out[0]), tiles_per_seq=seq // TM)
    out = _ffn(x2, None, mod, row(norm_ffn2[0]), bf(ffn2_wg[0]), bf(ffn2_wu[0]), bf(ffn2_wd[0]), row(norm_final),
               mod0=6, final=True, tiles_per_seq=seq // TM, n_seq=batch)
    return out.reshape(batch, seq, d)
```

```python
import functools
import math

import numpy as np
import jax
import jax.numpy as jnp
from jax import lax
from jax.experimental import pallas as pl
from jax.experimental.pallas import tpu as pltpu

F32 = jnp.float32
BF16 = jnp.bfloat16

NORM_EPS = 1e-6
RW_GN_EPS = 64e-5
N_MOD = 9
GRID_W = 64
NA_HEADS = 8
NA_HEAD_DIM = 64
NA_WIN_H = 8
NA_WIN_W = 16
RW_HEADS = 8
RW_HEAD_DIM = 64
RW_LORA = 64
RW_GATE_LORA = 128

LANES = 128
VMEM_LIMIT = 56 * 1024 * 1024

TM = 512
TM_IN = 256
NA_ROWS = 8
RW_TILE = 256
RW_CHUNK = 64
RW_INV_BASE = 8


def _cparams(n_axes):
    return pltpu.CompilerParams(dimension_semantics=("arbitrary",) * n_axes,
                                vmem_limit_bytes=VMEM_LIMIT)


def _resident(shape):
    nd = len(shape)
    return pl.BlockSpec(shape, lambda *_: (0,) * nd, pipeline_mode=pl.Buffered(1))


def _rmsnorm(x, g):
    return x * lax.rsqrt(jnp.mean(x * x, axis=-1, keepdims=True) + NORM_EPS) * g


def _split2(x):
    hi = x.astype(BF16)
    lo = (x - hi.astype(F32)).astype(BF16)
    return hi, lo


def _dot(a, b):
    return jnp.dot(a, b, preferred_element_type=F32)


def _dot_split_lhs(x, w_bf16, parts):
    acc = None
    rem = x
    for _ in range(parts):
        p = rem.astype(BF16)
        rem = rem - p.astype(F32)
        t = _dot(p, w_bf16)
        acc = t if acc is None else acc + t
    return acc


def _dot_split_rhs(w_bf16, x, parts):
    acc = None
    rem = x
    for _ in range(parts):
        p = rem.astype(BF16)
        rem = rem - p.astype(F32)
        t = _dot(w_bf16, p)
        acc = t if acc is None else acc + t
    return acc


def _dot3(a, b):
    ah, al = _split2(a)
    bh, bl = _split2(b)
    return _dot(ah, bh) + (_dot(ah, bl) + _dot(al, bh))


def _ada_kernel(c_ref, w_ref, b_ref, o_ref):
    c = c_ref[...]
    s = c * jax.nn.sigmoid(c)
    o_ref[...] = _dot3(s, w_ref[...]) + b_ref[...]


def _ada_mod(cs, w_ada, b_ada):
    d = cs.shape[1]
    nm = w_ada.shape[1] // d
    return pl.pallas_call(
        _ada_kernel,
        grid=(nm,),
        in_specs=[pl.BlockSpec((8, d), lambda j: (0, 0)),
                  pl.BlockSpec((d, d), lambda j: (0, j)),
                  pl.BlockSpec((1, d), lambda j: (0, j))],
        out_specs=pl.BlockSpec((8, d), lambda j: (0, j)),
        out_shape=jax.ShapeDtypeStruct((8, nm * d), F32),
        compiler_params=_cparams(1),
    )(cs, w_ada, b_ada.reshape(1, -1))


def _ffn_kernel(*refs, mod0, f_chunk, final, n_head_tiles):
    if n_head_tiles is None:
        x_ref, mod_ref, g_ref, wg_ref, wu_ref, wd_ref, gf_ref, o_ref = refs
        x = x_ref[...]
    else:
        x_ref, tail_ref, mod_ref, g_ref, wg_ref, wu_ref, wd_ref, gf_ref, o_ref = refs
        x = jnp.where(pl.program_id(0) < n_head_tiles, x_ref[...], tail_ref[...])
    shift = mod_ref[0, mod0:mod0 + 1, :]
    scale = mod_ref[0, mod0 + 1:mod0 + 2, :]
    gate = mod_ref[0, mod0 + 2:mod0 + 3, :]
    h = (_rmsnorm(x, g_ref[...]) * (1.0 + scale) + shift).astype(BF16)
    d_ff = wg_ref.shape[1]
    acc = None
    for f0 in range(0, d_ff, f_chunk):
        gg = _dot(h, wg_ref[:, f0:f0 + f_chunk])
        uu = _dot(h, wu_ref[:, f0:f0 + f_chunk])
        a = (gg * jax.nn.sigmoid(gg) * uu).astype(BF16)
        t = _dot(a, wd_ref[f0:f0 + f_chunk, :])
        acc = t if acc is None else acc + t
    y = x + 0.5 * gate * acc
    if final:
        y = _rmsnorm(y, gf_ref[...])
    o_ref[...] = y


def _ffn(xs, tail, mod, g, wg, wu, wd, gf, *, mod0, final, tiles_per_seq, n_seq):
    n_head, d = xs.shape
    n = n_head + (0 if tail is None else tail.shape[0])
    d_ff = wg.shape[1]
    f_chunk = d_ff // 2 if (d_ff // 2) % LANES == 0 else d_ff
    seq_of = lambda i: jnp.minimum(i // tiles_per_seq, n_seq)
    head_tiles = n_head // TM
    tokens = [pl.BlockSpec((TM, d), lambda i: (jnp.minimum(i, head_tiles - 1), 0))]
    if tail is not None:
        tokens.append(pl.BlockSpec((TM, d), lambda i: (jnp.maximum(i - head_tiles, 0), 0)))
    return pl.pallas_call(
        functools.partial(_ffn_kernel, mod0=mod0, f_chunk=f_chunk, final=final,
                          n_head_tiles=None if tail is None else head_tiles),
        grid=(n // TM,),
        in_specs=tokens + [pl.BlockSpec((1, N_MOD, d), lambda i: (seq_of(i), 0, 0)),
                           _resident((1, d)), _resident((d, d_ff)), _resident((d, d_ff)), _resident((d_ff, d)),
                           _resident((1, d))],
        out_specs=pl.BlockSpec((TM, d), lambda i: (i, 0)),
        out_shape=jax.ShapeDtypeStruct((n, d), F32),
        compiler_params=_cparams(1),
    )(*([xs] if tail is None else [xs, tail]), mod, g, wg, wu, wd, gf)


def _head_segsum(x):
    r = lax.broadcasted_iota(jnp.int32, (LANES, LANES), 0) // RW_HEAD_DIM
    c = lax.broadcasted_iota(jnp.int32, (LANES, LANES), 1) // RW_HEAD_DIM
    e = jnp.where(r == c, 1.0, 0.0).astype(BF16)
    xb = x.astype(BF16)
    return jnp.concatenate([_dot(xb[:, j:j + LANES], e) for j in range(0, x.shape[1], LANES)], axis=1)


def _inproj_kernel(x_ref, xprev_ref, xnext_ref, mod_ref, g_ref, w_ref, mu_ref, par_ref, wup_ref, aup_ref,
                   gup_ref, tri_ref, qkv_ref, gate_ref, kk_ref, r_ref, b_ref, k_ref, em_ref, v_ref, g_out_ref,
                   bon_ref, *, c_rw, c_gate, n_x_tiles, tiles_x, tiles_ctx):
    i = pl.program_id(0)
    ta = x_ref.shape[0]
    width = v_ref.shape[1]
    c_len = RW_CHUNK
    n_chunks = ta // c_len
    in_x = i < n_x_tiles
    tile = jnp.where(in_x, i % tiles_x, (i - n_x_tiles) % tiles_ctx)
    last_tile = jnp.where(in_x, tiles_x - 1, tiles_ctx - 1)

    x_all = jnp.concatenate([xprev_ref[...], x_ref[...], xnext_ref[...]], axis=0)
    h_all = (_rmsnorm(x_all, g_ref[...]) * (1.0 + mod_ref[0, 4:5, :]) + mod_ref[0, 3:4, :]).astype(BF16)
    h = h_all[8:ta + 8]
    zero = jnp.zeros((8, h.shape[1]), BF16)
    h_all = jnp.concatenate([jnp.where(tile == 0, zero, h_all[:8]), h,
                             jnp.where(tile == last_tile, zero, h_all[ta + 8:])], axis=0)
    p_all = _dot(h_all, w_ref[:, c_rw:c_gate])

    piece = 4 * LANES
    pieces = ([(qkv_ref, c0, c0) for c0 in range(0, c_rw, piece)]
              + [(gate_ref, c0, c_gate + c0) for c0 in range(0, w_ref.shape[1] - c_gate, piece)])

    def emit_dense(count):
        for _ in range(min(count, len(pieces))):
            ref, dst, src = pieces.pop(0)
            ref[:, dst:dst + piece] = _dot(h, w_ref[:, src:src + piece]).astype(ref.dtype)

    p = p_all[8:ta + 8]
    f = p + mu_ref[0:1, :] * (p_all[7:ta + 7] - p) + mu_ref[1:2, :] * (p_all[9:ta + 9] - p)
    emit_dense(2)

    r = f[:, 0:width]
    k = f[:, width:2 * width]
    v = f[:, 2 * width:3 * width]
    o = 3 * width
    th = jnp.tanh(f[:, o:o + LANES]).astype(BF16)
    al = f[:, o + LANES:o + 2 * LANES].astype(BF16)
    gl = f[:, o + 2 * LANES:o + 3 * LANES]
    k_k = par_ref[4:5, :]
    k_a = par_ref[5:6, :]
    r_k = par_ref[6:7, :]

    v_ref[...] = v.astype(BF16)
    g_out_ref[...] = _dot(jax.nn.sigmoid(gl).astype(BF16), gup_ref[...])
    kk = k * k_k
    kk = kk * lax.rsqrt(jnp.maximum(_head_segsum(kk * kk), 1e-12))
    emit_dense(1)

    def chunks(t):
        return t.reshape(n_chunks, c_len, width)

    kd_sum = None
    for d in range(2):
        z = par_ref[d:d + 1, :] + _dot(th, wup_ref[d])
        lw = -math.exp(-0.5) * jax.nn.sigmoid(z)
        a = jax.nn.sigmoid(par_ref[2 + d:3 + d, :] + _dot(al, aup_ref[d]))
        kd = k * (1.0 + (a - 1.0) * k_a)
        kd_sum = kd if kd_sum is None else kd_sum + kd
        bb = kk * a
        emit_dense(1)
        ci3 = chunks(_dot(tri_ref[d], lw.astype(BF16)))
        ce3 = ci3 - chunks(lw)
        t_last = 0 if d == 1 else c_len - 1
        cm = 0.5 * ci3[:, t_last:t_last + 1, :]
        e_k = jnp.exp(cm - ci3)
        emit_dense(1)
        kk_ref[d] = (chunks(kk) * jnp.exp(ce3 - cm)).reshape(ta, width).astype(BF16)
        r_ref[d] = (chunks(r) * jnp.exp(ci3 - cm)).reshape(ta, width).astype(BF16)
        b_ref[d] = (chunks(bb) * e_k).reshape(ta, width).astype(BF16)
        k_ref[d] = (chunks(kd) * e_k).reshape(ta, width).astype(BF16)
        em_ref[d, 0] = jnp.exp(cm).reshape(n_chunks, width)
    emit_dense(len(pieces))
    bon_ref[...] = _head_segsum(r * kd_sum * r_k) * v


def _inproj(xs, mod, g, w_in, mu, par, wup, aup, gup, *, c_rw, c_gate, batch, seq, ctx_len):
    n, d = xs.shape
    cols = w_in.shape[1]
    width = RW_HEADS * RW_HEAD_DIM
    ta = RW_TILE
    c_len = RW_CHUNK
    n_tiles = n // ta
    tiles_x = seq // ta
    sub = ta // 8
    last8 = n // 8 - 1
    seq_of = lambda i: jnp.minimum(i // tiles_x, batch)
    idx = np.arange(ta)
    same_chunk = (idx[:, None] // c_len) == (idx[None, :] // c_len)
    tri = jnp.asarray(np.stack([same_chunk & (idx[None, :] <= idx[:, None]),
                                same_chunk & (idx[None, :] >= idx[:, None])]), BF16)
    tok2 = pl.BlockSpec((2, ta, width), lambda i: (0, i, 0))
    tok = lambda c: pl.BlockSpec((ta, c), lambda i: (i, 0))
    sds = jax.ShapeDtypeStruct
    return pl.pallas_call(
        functools.partial(_inproj_kernel, c_rw=c_rw, c_gate=c_gate, n_x_tiles=batch * tiles_x, tiles_x=tiles_x,
                          tiles_ctx=ctx_len // ta),
        grid=(n_tiles,),
        in_specs=[tok(d),
                  pl.BlockSpec((8, d), lambda i: (jnp.maximum(i * sub - 1, 0), 0)),
                  pl.BlockSpec((8, d), lambda i: (jnp.minimum((i + 1) * sub, last8), 0)),
                  pl.BlockSpec((1, N_MOD, d), lambda i: (seq_of(i), 0, 0)),
                  _resident((1, d)), _resident((d, cols)),
                  _resident(mu.shape), _resident(par.shape), _resident(wup.shape), _resident(aup.shape),
                  _resident(gup.shape), _resident(tri.shape)],
        out_specs=[tok(c_rw), tok(cols - c_gate), tok2, tok2, tok2, tok2,
                   pl.BlockSpec((2, 1, ta // c_len, width), lambda i: (0, i, 0, 0)),
                   tok(width), tok(width), tok(width)],
        out_shape=[sds((n, c_rw), BF16), sds((n, cols - c_gate), F32)]
                  + [sds((2, n, width), BF16)] * 4
                  + [sds((2, n_tiles, ta // c_len, width), F32), sds((n, width), BF16),
                     sds((n, width), F32), sds((n, width), F32)],
        compiler_params=_cparams(1),
    )(xs, xs, xs, mod, g, w_in, mu, par, wup, aup, gup, tri)


NA_KEY_ROWS = 2 * NA_ROWS
NA_PAIRS = NA_KEY_ROWS // 2


def _na_pair_bias(rpb):
    w = GRID_W
    qc = np.arange(w)[:, None]
    kc = np.arange(w)[None, :]
    s_c = np.clip(qc - NA_WIN_W // 2, 0, w - NA_WIN_W)
    col_ok = (kc >= s_c) & (kc < s_c + NA_WIN_W)
    dc = np.clip(kc - qc + NA_WIN_W - 1, 0, 2 * NA_WIN_W - 2)
    onehot = jnp.asarray(dc[None] == np.arange(2 * NA_WIN_W - 1)[:, None, None], F32)
    b = jnp.einsum('hrd,dqk->hrqk', rpb.astype(F32), onehot, precision=lax.Precision.HIGHEST)
    b = jnp.where(col_ok[None, None], b, -jnp.inf)
    ninf = jnp.full((rpb.shape[0], 1, w, w), -jnp.inf, F32)
    b = jnp.concatenate([ninf, b, ninf], axis=1)
    return jnp.concatenate([b[:, :-1], b[:, 1:]], axis=-1)


def _na_kernel(q_ref, k0_ref, k1_ref, k2_ref, k3_ref, kc_ref, v0_ref, v1_ref, v2_ref, v3_ref, vc_ref,
               bias_ref, o_ref, kcat, vcat, *, n_blocks):
    j = pl.program_id(2)
    blk = k0_ref.shape[0]
    for s, (kr, vr) in enumerate(((k0_ref, v0_ref), (k1_ref, v1_ref), (k2_ref, v2_ref), (k3_ref, v3_ref),
                                  (kc_ref, vc_ref))):
        kcat[s * blk:s * blk + kr.shape[0], :] = kr[...]
        vcat[s * blk:s * blk + vr.shape[0], :] = vr[...]
    n_loc = 4 * blk
    w = GRID_W
    half = NA_WIN_H // 2
    lane = lax.broadcasted_iota(jnp.int32, (1, LANES), 1)
    first = lane < NA_HEAD_DIM
    q = q_ref[...] * (NA_HEAD_DIM ** -0.5)
    k_all = kcat[...]
    v_all = vcat[...]
    outs = []
    for h in range(2):
        sel = first if h == 0 else jnp.logical_not(first)
        qh = jnp.where(sel, q, jnp.zeros_like(q))
        s_loc = lax.dot_general(qh, k_all[:n_loc], (((1,), (1,)), ((), ())), preferred_element_type=F32)
        s_ctx = lax.dot_general(qh, k_all[n_loc:], (((1,), (1,)), ((), ())), preferred_element_type=F32)
        p_rows = []
        l_rows = []
        for qi in range(NA_ROWS):
            lo = jnp.where(j == 0, max(qi, half), jnp.where(j == n_blocks - 1, min(qi, half), qi))
            m_lo = min(qi, half) // 2
            m_hi = (max(qi, half) + NA_WIN_H - 1) // 2
            n_m = m_hi - m_lo + 1
            key_row = 2 * m_lo + lax.broadcasted_iota(jnp.int32, (1, n_m * LANES), 1) // w
            seen = (key_row >= lo) & (key_row < lo + NA_WIN_H)
            bias = jnp.concatenate(
                [bias_ref[h, min(max(2 * m - qi + half, 0), 2 * NA_WIN_H - 1)] for m in range(m_lo, m_hi + 1)],
                axis=1)
            rows = slice(qi * w, (qi + 1) * w)
            s_q = jnp.where(seen, s_loc[rows, m_lo * LANES:(m_hi + 1) * LANES] + bias, -jnp.inf)
            s_c = s_ctx[rows]
            mx = jnp.maximum(jnp.max(s_q, axis=-1, keepdims=True), jnp.max(s_c, axis=-1, keepdims=True))
            p_q = jnp.exp(s_q - mx)
            p_c = jnp.exp(s_c - mx)
            l_rows.append(jnp.sum(p_q, axis=-1, keepdims=True) + jnp.sum(p_c, axis=-1, keepdims=True))
            pieces = []
            if m_lo > 0:
                pieces.append(jnp.zeros((w, m_lo * LANES), BF16))
            pieces.append(p_q.astype(BF16))
            if m_hi < NA_PAIRS - 1:
                pieces.append(jnp.zeros((w, (NA_PAIRS - 1 - m_hi) * LANES), BF16))
            pieces.append(p_c.astype(BF16))
            p_rows.append(jnp.concatenate(pieces, axis=1))
        p = jnp.concatenate(p_rows, axis=0)
        l = jnp.concatenate(l_rows, axis=0)
        outs.append(_dot(p, v_all) / l)
    o_ref[...] = jnp.where(first, outs[0], outs[1]).astype(o_ref.dtype)


def _na_attention(qkv, bias, *, batch, seq, ctx_len):
    rows = seq // GRID_W
    nj = rows // NA_ROWS
    tq = NA_ROWS * GRID_W
    tk = 4 * GRID_W
    width = NA_HEADS * NA_HEAD_DIM
    n_pairs = width // LANES
    kblocks = seq // tk
    ctx_blk0 = (batch * seq) // ctx_len

    def kv_spec(slot, col0):
        def imap(hp, b, j):
            return (b * kblocks + jnp.clip(2 * j - 1 + slot, 0, kblocks - 1), col0 + hp)
        return pl.BlockSpec((tk, LANES), imap)

    def ctx_spec(col0):
        return pl.BlockSpec((ctx_len, LANES), lambda hp, b, j: (ctx_blk0 + b, col0 + hp))

    in_specs = ([pl.BlockSpec((tq, LANES), lambda hp, b, j: (b * nj + j, hp))]
                + [kv_spec(s, n_pairs) for s in range(4)] + [ctx_spec(n_pairs)]
                + [kv_spec(s, 2 * n_pairs) for s in range(4)] + [ctx_spec(2 * n_pairs)]
                + [pl.BlockSpec((2,) + bias.shape[1:], lambda hp, b, j: (hp, 0, 0, 0))])
    return pl.pallas_call(
        functools.partial(_na_kernel, n_blocks=nj),
        grid=(n_pairs, batch, nj),
        in_specs=in_specs,
        out_specs=pl.BlockSpec((tq, LANES), lambda hp, b, j: (b * nj + j, hp)),
        out_shape=jax.ShapeDtypeStruct((batch * seq, width), BF16),
        scratch_shapes=[pltpu.VMEM((4 * tk + ctx_len, LANES), BF16),
                        pltpu.VMEM((4 * tk + ctx_len, LANES), BF16)],
        compiler_params=_cparams(3),
    )(*([qkv] * 11), bias)


def _rwkv_kernel(kk_ref, r_ref, b_ref, k_ref, em_ref, v_ref, strict_ref, incl_ref, y_ref, z_s, *, reverse):
    n = pl.program_id(1)
    ta = v_ref.shape[0]
    width = v_ref.shape[1]
    n_pairs = width // LANES
    c_len = RW_CHUNK
    n_chunks = ta // c_len
    n_b = n_chunks * n_pairs

    @pl.when(n == 0)
    def _():
        z_s[...] = jnp.zeros_like(z_s)

    def chunks(t):
        return t.reshape(n_chunks, c_len, width)

    def to_batch(t):
        parts = [t[:, :, j * LANES:(j + 1) * LANES] for j in range(n_pairs)]
        return jnp.stack(parts, axis=1).reshape(n_b, t.shape[1], LANES)

    em = em_ref[0, 0].reshape(n_chunks, 1, width)
    kk_t = chunks(kk_ref[0].astype(F32))
    r_t = chunks(r_ref[0].astype(F32))
    b_t = chunks(b_ref[0].astype(F32))
    k_t = chunks(k_ref[0].astype(F32))
    kk_abs = to_batch(kk_t * em)
    r_abs = to_batch(r_t * em)
    k_bar = to_batch(k_t * em)
    b_bar = to_batch(b_t * em)
    gam = to_batch(em * em)
    kk_t = to_batch(kk_t)
    r_t = to_batch(r_t)
    b_t = to_batch(b_t)
    k_t = to_batch(k_t)
    v_b = to_batch(chunks(v_ref[...].astype(F32)))

    lane = lax.broadcasted_iota(jnp.int32, (1, 1, LANES), 2)
    first = lane < RW_HEAD_DIM
    ri = lax.broadcasted_iota(jnp.int32, (LANES, LANES), 0)
    cj = lax.broadcasted_iota(jnp.int32, (LANES, LANES), 1)
    eye = ri == cj
    blockdiag = (ri // RW_HEAD_DIM) == (cj // RW_HEAD_DIM)
    strict = strict_ref[...] > 0.0
    incl = incl_ref[...] > 0.0

    def stack_heads(t):
        zero = jnp.zeros_like(t)
        return jnp.concatenate([jnp.where(first, t, zero), jnp.where(first, zero, t)], axis=1)

    def bmm(x, y):
        return lax.dot_general(x.astype(BF16), y.astype(BF16), (((2,), (1,)), ((0,), (0,))),
                               preferred_element_type=F32)

    def bmm_nt(x, y):
        return lax.dot_general(x.astype(BF16), y.astype(BF16), (((2,), (2,)), ((0,), (0,))),
                               preferred_element_type=F32)

    def bmm_tn(x, y):
        return lax.dot_general(x.astype(BF16), y.astype(BF16), (((1,), (1,)), ((0,), (0,))),
                               preferred_element_type=F32)

    gm = bmm_nt(jnp.concatenate([kk_t, r_t], axis=1),
                jnp.concatenate([stack_heads(b_t), stack_heads(k_t)], axis=1))
    ab_w = jnp.where(strict, gm[:, :c_len, :LANES], 0.0)
    ak_w = jnp.where(strict, gm[:, :c_len, LANES:], 0.0)
    db_w = jnp.where(incl, gm[:, c_len:, :LANES], 0.0)
    dk_w = jnp.where(incl, gm[:, c_len:, LANES:], 0.0)

    l_bd = stack_heads(ab_w)
    base = RW_INV_BASE
    same = lambda s: (ri // s) == (cj // s)
    n_d = jnp.where(same(base), l_bd, 0.0)
    t_bd = jnp.where(eye, 1.0, 0.0) - n_d
    pw = bmm(n_d, n_d)
    span = 2
    while span < base:
        t_bd = t_bd + bmm(t_bd, pw)
        span *= 2
        if span < base:
            pw = bmm(pw, pw)
    size = base
    while size < c_len:
        off = jnp.where(jnp.logical_and(same(2 * size), jnp.logical_not(same(size))), l_bd, 0.0)
        t_bd = t_bd - bmm(bmm(t_bd, off), t_bd)
        size *= 2
    t_w = t_bd[:, :c_len] + t_bd[:, c_len:]

    akv = bmm(jnp.concatenate([ak_w, dk_w], axis=1), stack_heads(v_b))
    pq = bmm(t_w, jnp.concatenate([stack_heads(kk_abs), stack_heads(akv[:, :c_len])], axis=2))
    dpq = bmm(db_w, jnp.concatenate([stack_heads(pq[:, :, :LANES]), stack_heads(pq[:, :, LANES:])], axis=2))
    r_hat = r_abs - dpq[:, :, :LANES]
    y_loc = akv[:, c_len:] - dpq[:, :, LANES:]
    bpq = bmm_tn(b_bar, pq)
    m_t = jnp.where(eye, jnp.broadcast_to(gam, (n_b, LANES, LANES)), 0.0) - bpq[:, :, :LANES]
    g_t = bmm_tn(k_bar, v_b) - bpq[:, :, LANES:]

    zs = z_s[...]
    order = range(n_chunks - 1, -1, -1) if reverse else range(n_chunks)
    for c in order:
        sl = slice(c * n_pairs, (c + 1) * n_pairs)
        both = bmm(jnp.concatenate([r_hat[sl], m_t[sl]], axis=1), zs)
        y_c = both[:, :c_len] + y_loc[sl]
        y_ref[0, c * c_len:(c + 1) * c_len, :] = jnp.concatenate([y_c[j] for j in range(n_pairs)], axis=1)
        zs = jnp.where(blockdiag, both[:, c_len:] + g_t[sl], 0.0)
    z_s[...] = zs


def _rwkv_stream(kk_t, r_t, b_t, k_t, em, v, *, reverse, batch, seq, ctx_len):
    width = RW_HEADS * RW_HEAD_DIM
    ta = RW_TILE
    nct = ctx_len // ta
    nxt = seq // ta
    n_x = batch * seq
    c_len = RW_CHUNK
    d = 1 if reverse else 0

    def blk(b, n):
        is_ctx = n < nct
        t_ctx = (nct - 1 - n) if reverse else n
        t_x = (nxt - 1 - (n - nct)) if reverse else (n - nct)
        return jnp.where(is_ctx, batch * nxt + b * nct + t_ctx, b * nxt + t_x)

    def xblk(b, n):
        m = jnp.maximum(n, nct) - nct
        return b * nxt + ((nxt - 1 - m) if reverse else m)

    t_i = np.arange(c_len)[:, None]
    s_i = np.arange(LANES)[None, :] % RW_HEAD_DIM
    strict = jnp.asarray((s_i > t_i) if reverse else (s_i < t_i), F32)
    incl = jnp.asarray((s_i >= t_i) if reverse else (s_i <= t_i), F32)

    operand = pl.BlockSpec((1, ta, width), lambda b, n: (d, blk(b, n), 0))
    return pl.pallas_call(
        functools.partial(_rwkv_kernel, reverse=reverse),
        grid=(batch, nct + nxt),
        in_specs=[operand, operand, operand, operand,
                  pl.BlockSpec((1, 1, ta // c_len, width), lambda b, n: (d, blk(b, n), 0, 0)),
                  pl.BlockSpec((ta, width), lambda b, n: (blk(b, n), 0)),
                  _resident(strict.shape), _resident(incl.shape)],
        out_specs=pl.BlockSpec((1, ta, width), lambda b, n: (0, xblk(b, n), 0)),
        out_shape=jax.ShapeDtypeStruct((1, n_x, width), F32),
        scratch_shapes=[pltpu.VMEM((width // LANES, LANES, LANES), F32)],
        compiler_params=_cparams(2),
    )(kk_t, r_t, b_t, k_t, em, v, strict, incl)


def _merge_kernel(x_ref, mod_ref, ona_ref, yf_ref, yb_ref, bon_ref, g_ref, pg_ref, bg_ref, ln_ref,
                  pna_ref, prw_ref, wout_ref, o_ref):
    d = x_ref.shape[1]
    y = yf_ref[0] + yb_ref[0]
    inv_n = 1.0 / RW_HEAD_DIM
    mean = _head_segsum(y) * inv_n
    yc = y - mean
    var = _head_segsum(yc * yc) * inv_n
    yn = yc * lax.rsqrt(var + RW_GN_EPS) * ln_ref[0:1, :] + ln_ref[1:2, :]
    o_rw = ((yn + bon_ref[...]) * g_ref[...]).astype(BF16)
    gates = jax.nn.sigmoid(pg_ref[...] + bg_ref[...])
    m = gates[:, :d] * _dot(ona_ref[...], pna_ref[...]) + gates[:, d:] * _dot(o_rw, prw_ref[...])
    o_ref[...] = x_ref[...] + mod_ref[0, 5:6, :] * _dot(m.astype(BF16), wout_ref[...])


def _merge(x1, mod, o_na, y_f, y_b, bon, g, p_gate, b_gate, ln, p_na, p_rw, w_out, *, tiles_per_seq):
    n_x, d = o_na.shape[0], x1.shape[1]
    width = o_na.shape[1]
    tok = lambda i: (i, 0)
    tok3 = lambda i: (0, i, 0)
    return pl.pallas_call(
        _merge_kernel,
        grid=(n_x // TM,),
        in_specs=[pl.BlockSpec((TM, d), tok),
                  pl.BlockSpec((1, N_MOD, d), lambda i: (i // tiles_per_seq, 0, 0)),
                  pl.BlockSpec((TM, width), tok),
                  pl.BlockSpec((1, TM, width), tok3), pl.BlockSpec((1, TM, width), tok3),
                  pl.BlockSpec((TM, width), tok), pl.BlockSpec((TM, width), tok),
                  pl.BlockSpec((TM, 2 * d), tok),
                  _resident((1, 2 * d)), _resident((2, width)),
                  _resident(p_na.shape), _resident(p_rw.shape), _resident(w_out.shape)],
        out_specs=pl.BlockSpec((TM, d), tok),
        out_shape=jax.ShapeDtypeStruct((n_x, d), F32),
        compiler_params=_cparams(1),
    )(x1, mod, o_na, y_f, y_b, bon, g, p_gate, b_gate, ln, p_na, p_rw, w_out)


def _pad_lora(w_up, direction):
    z = jnp.zeros_like(w_up[0])
    return jnp.concatenate([w_up[0] if direction == 0 else z, w_up[1] if direction == 1 else z], axis=0)


def kernel(x, c, ctx, c_ctx, w_ada, b_ada, norm_ffn1, norm_mix, norm_ffn2, norm_final, ffn1_wg, ffn1_wu, ffn1_wd, ffn2_wg, ffn2_wu, ffn2_wd, w_in, b_gate, na_rpb, rw_mu, rw_w0, rw_w_up, rw_a0, rw_a_up, rw_g_up, rw_k_k, rw_k_a, rw_r_k, rw_ln_w, rw_ln_b, p_na, p_rw, w_out):
    batch, seq, d = x.shape
    ctx_len = ctx.shape[1]
    n_x = batch * seq
    na_width = NA_HEADS * NA_HEAD_DIM
    rw_width = RW_HEADS * RW_HEAD_DIM
    c_rw = 3 * na_width
    c_gate = c_rw + 3 * rw_width + 4 * RW_LORA + RW_GATE_LORA
    assert w_ada.shape[0] == 1, "single layer"
    assert seq % TM == 0 and (batch * ctx_len) % TM == 0 and ctx_len % RW_TILE == 0 and batch + 1 <= 8
    assert seq % (NA_ROWS * GRID_W) == 0 and seq // (NA_ROWS * GRID_W) >= 2 and ctx_len == 4 * GRID_W
    row = lambda t: t.reshape(1, -1)

    cs = jnp.concatenate([c, c_ctx[None], jnp.zeros((8 - batch - 1, d), F32)], axis=0)
    mod = _ada_mod(cs, w_ada[0], b_ada[0])[:batch + 1].reshape(batch + 1, N_MOD, d)

    bf = lambda t: t.astype(BF16)

    x1 = _ffn(x.reshape(n_x, d), ctx.reshape(batch * ctx_len, d), mod, row(norm_ffn1[0]),
              bf(ffn1_wg[0]), bf(ffn1_wu[0]), bf(ffn1_wd[0]), row(norm_final),
              mod0=0, final=False, tiles_per_seq=seq // TM, n_seq=batch)
    par = jnp.concatenate([rw_w0[0], rw_a0[0], row(rw_k_k[0]), row(rw_k_a[0]), row(rw_r_k[0]),
                           jnp.zeros((1, rw_width), F32)], axis=0)
    wup = bf(jnp.stack([_pad_lora(rw_w_up[0], 0), _pad_lora(rw_w_up[0], 1)]))
    aup = bf(jnp.stack([_pad_lora(rw_a_up[0], 0), _pad_lora(rw_a_up[0], 1)]))
    qkv, p_gate, kk_t, r_t, b_t, k_t, em, v_rw, g, bon = _inproj(
        x1, mod, row(norm_mix[0]), bf(w_in[0]), rw_mu[0], par, wup, aup, bf(rw_g_up[0]),
        c_rw=c_rw, c_gate=c_gate, batch=batch, seq=seq, ctx_len=ctx_len)

    bias = _na_pair_bias(na_rpb[0])
    o_na = _na_attention(qkv, bias, batch=batch, seq=seq, ctx_len=ctx_len)

    y_f, y_b = [_rwkv_stream(kk_t, r_t, b_t, k_t, em, v_rw, reverse=rev, batch=batch, seq=seq, ctx_len=ctx_len)
                for rev in (False, True)]

    ln = jnp.stack([rw_ln_w[0], rw_ln_b[0]], axis=0)
    x2 = _merge(x1, mod, o_na, y_f, y_b, bon, g, p_gate, row(b_gate[0]), ln,
                bf(p_na[0]), bf(p_rw[0]), bf(w_out[0]), tiles_per_seq=seq // TM)
    out = _ffn(x2, None, mod, row(norm_ffn2[0]), bf(ffn2_wg[0]), bf(ffn2_wu[0]), bf(ffn2_wd[0]), row(norm_final),
               mod0=6, final=True, tiles_per_seq=seq // TM, n_seq=batch)
    return out.reshape(batch, seq, d)
```

```python
import functools
import math

import numpy as np
import jax
import jax.numpy as jnp
from jax import lax
from jax.experimental import pallas as pl
from jax.experimental.pallas import tpu as pltpu

F32 = jnp.float32
BF16 = jnp.bfloat16

NORM_EPS = 1e-6
RW_GN_EPS = 64e-5
N_MOD = 9
GRID_W = 64
NA_HEADS = 8
NA_HEAD_DIM = 64
NA_WIN_H = 8
NA_WIN_W = 16
RW_HEADS = 8
RW_HEAD_DIM = 64
RW_LORA = 64
RW_GATE_LORA = 128

LANES = 128
VMEM_LIMIT = 56 * 1024 * 1024

TM = 512
TM_IN = 256
NA_ROWS = 8
RW_TILE = 256
RW_CHUNK = 64
RW_INV_BASE = 8


def _cparams(n_axes):
    return pltpu.CompilerParams(dimension_semantics=("arbitrary",) * n_axes,
                                vmem_limit_bytes=VMEM_LIMIT)


def _resident(shape):
    nd = len(shape)
    return pl.BlockSpec(shape, lambda *_: (0,) * nd, pipeline_mode=pl.Buffered(1))


def _rmsnorm(x, g):
    return x * lax.rsqrt(jnp.mean(x * x, axis=-1, keepdims=True) + NORM_EPS) * g


def _split2(x):
    hi = x.astype(BF16)
    lo = (x - hi.astype(F32)).astype(BF16)
    return hi, lo


def _dot(a, b):
    return jnp.dot(a, b, preferred_element_type=F32)


def _dot_split_lhs(x, w_bf16, parts):
    acc = None
    rem = x
    for _ in range(parts):
        p = rem.astype(BF16)
        rem = rem - p.astype(F32)
        t = _dot(p, w_bf16)
        acc = t if acc is None else acc + t
    return acc


def _dot_split_rhs(w_bf16, x, parts):
    acc = None
    rem = x
    for _ in range(parts):
        p = rem.astype(BF16)
        rem = rem - p.astype(F32)
        t = _dot(w_bf16, p)
        acc = t if acc is None else acc + t
    return acc


def _dot3(a, b):
    ah, al = _split2(a)
    bh, bl = _split2(b)
    return _dot(ah, bh) + (_dot(ah, bl) + _dot(al, bh))


def _ada_kernel(c_ref, w_ref, b_ref, o_ref):
    c = c_ref[...]
    s = c * jax.nn.sigmoid(c)
    o_ref[...] = _dot3(s, w_ref[...]) + b_ref[...]


def _ada_mod(cs, w_ada, b_ada):
    d = cs.shape[1]
    nm = w_ada.shape[1] // d
    return pl.pallas_call(
        _ada_kernel,
        grid=(nm,),
        in_specs=[pl.BlockSpec((8, d), lambda j: (0, 0)),
                  pl.BlockSpec((d, d), lambda j: (0, j)),
                  pl.BlockSpec((1, d), lambda j: (0, j))],
        out_specs=pl.BlockSpec((8, d), lambda j: (0, j)),
        out_shape=jax.ShapeDtypeStruct((8, nm * d), F32),
        compiler_params=_cparams(1),
    )(cs, w_ada, b_ada.reshape(1, -1))


def _ffn_kernel(*refs, mod0, f_chunk, final, n_head_tiles):
    if n_head_tiles is None:
        x_ref, mod_ref, g_ref, wg_ref, wu_ref, wd_ref, gf_ref, o_ref = refs
        x = x_ref[...]
    else:
        x_ref, tail_ref, mod_ref, g_ref, wg_ref, wu_ref, wd_ref, gf_ref, o_ref = refs
        x = jnp.where(pl.program_id(0) < n_head_tiles, x_ref[...], tail_ref[...])
    shift = mod_ref[0, mod0:mod0 + 1, :]
    scale = mod_ref[0, mod0 + 1:mod0 + 2, :]
    gate = mod_ref[0, mod0 + 2:mod0 + 3, :]
    h = (_rmsnorm(x, g_ref[...]) * (1.0 + scale) + shift).astype(BF16)
    d_ff = wg_ref.shape[1]
    acc = None
    for f0 in range(0, d_ff, f_chunk):
        gg = _dot(h, wg_ref[:, f0:f0 + f_chunk])
        uu = _dot(h, wu_ref[:, f0:f0 + f_chunk])
        a = (gg * jax.nn.sigmoid(gg) * uu).astype(BF16)
        t = _dot(a, wd_ref[f0:f0 + f_chunk, :])
        acc = t if acc is None else acc + t
    y = x + 0.5 * gate * acc
    if final:
        y = _rmsnorm(y, gf_ref[...])
    o_ref[...] = y


def _ffn(xs, tail, mod, g, wg, wu, wd, gf, *, mod0, final, tiles_per_seq, n_seq):
    n_head, d = xs.shape
    n = n_head + (0 if tail is None else tail.shape[0])
    d_ff = wg.shape[1]
    f_chunk = d_ff // 2 if (d_ff // 2) % LANES == 0 else d_ff
    seq_of = lambda i: jnp.minimum(i // tiles_per_seq, n_seq)
    head_tiles = n_head // TM
    tokens = [pl.BlockSpec((TM, d), lambda i: (jnp.minimum(i, head_tiles - 1), 0))]
    if tail is not None:
        tokens.append(pl.BlockSpec((TM, d), lambda i: (jnp.maximum(i - head_tiles, 0), 0)))
    return pl.pallas_call(
        functools.partial(_ffn_kernel, mod0=mod0, f_chunk=f_chunk, final=final,
                          n_head_tiles=None if tail is None else head_tiles),
        grid=(n // TM,),
        in_specs=tokens + [pl.BlockSpec((1, N_MOD, d), lambda i: (seq_of(i), 0, 0)),
                           _resident((1, d)), _resident((d, d_ff)), _resident((d, d_ff)), _resident((d_ff, d)),
                           _resident((1, d))],
        out_specs=pl.BlockSpec((TM, d), lambda i: (i, 0)),
        out_shape=jax.ShapeDtypeStruct((n, d), F32),
        compiler_params=_cparams(1),
    )(*([xs] if tail is None else [xs, tail]), mod, g, wg, wu, wd, gf)


def _head_segsum(x):
    r = lax.broadcasted_iota(jnp.int32, (LANES, LANES), 0) // RW_HEAD_DIM
    c = lax.broadcasted_iota(jnp.int32, (LANES, LANES), 1) // RW_HEAD_DIM
    e = jnp.where(r == c, 1.0, 0.0).astype(BF16)
    xb = x.astype(BF16)
    return jnp.concatenate([_dot(xb[:, j:j + LANES], e) for j in range(0, x.shape[1], LANES)], axis=1)


def _inproj_kernel(x_ref, xprev_ref, xnext_ref, mod_ref, g_ref, w_ref, mu_ref, par_ref, wup_ref, aup_ref,
                   gup_ref, tri_ref, qkv_ref, gate_ref, kk_ref, r_ref, b_ref, k_ref, em_ref, v_ref, g_out_ref,
                   bon_ref, *, c_rw, c_gate, n_x_tiles, tiles_x, tiles_ctx):
    i = pl.program_id(0)
    ta = x_ref.shape[0]
    width = v_ref.shape[1]
    c_len = RW_CHUNK
    n_chunks = ta // c_len
    in_x = i < n_x_tiles
    tile = jnp.where(in_x, i % tiles_x, (i - n_x_tiles) % tiles_ctx)
    last_tile = jnp.where(in_x, tiles_x - 1, tiles_ctx - 1)

    x_all = jnp.concatenate([xprev_ref[...], x_ref[...], xnext_ref[...]], axis=0)
    h_all = (_rmsnorm(x_all, g_ref[...]) * (1.0 + mod_ref[0, 4:5, :]) + mod_ref[0, 3:4, :]).astype(BF16)
    h = h_all[8:ta + 8]
    zero = jnp.zeros((8, h.shape[1]), BF16)
    h_all = jnp.concatenate([jnp.where(tile == 0, zero, h_all[:8]), h,
                             jnp.where(tile == last_tile, zero, h_all[ta + 8:])], axis=0)
    p_all = _dot(h_all, w_ref[:, c_rw:c_gate])

    piece = 4 * LANES
    pieces = ([(qkv_ref, c0, c0) for c0 in range(0, c_rw, piece)]
              + [(gate_ref, c0, c_gate + c0) for c0 in range(0, w_ref.shape[1] - c_gate, piece)])

    def emit_dense(count):
        for _ in range(min(count, len(pieces))):
            ref, dst, src = pieces.pop(0)
            ref[:, dst:dst + piece] = _dot(h, w_ref[:, src:src + piece]).astype(ref.dtype)

    p = p_all[8:ta + 8]
    f = p + mu_ref[0:1, :] * (p_all[7:ta + 7] - p) + mu_ref[1:2, :] * (p_all[9:ta + 9] - p)
    emit_dense(2)

    r = f[:, 0:width]
    k = f[:, width:2 * width]
    v = f[:, 2 * width:3 * width]
    o = 3 * width
    th = jnp.tanh(f[:, o:o + LANES]).astype(BF16)
    al = f[:, o + LANES:o + 2 * LANES].astype(BF16)
    gl = f[:, o + 2 * LANES:o + 3 * LANES]
    k_k = par_ref[4:5, :]
    k_a = par_ref[5:6, :]
    r_k = par_ref[6:7, :]

    v_ref[...] = v.astype(BF16)
    g_out_ref[...] = _dot(jax.nn.sigmoid(gl).astype(BF16), gup_ref[...])
    kk = k * k_k
    kk = kk * lax.rsqrt(jnp.maximum(_head_segsum(kk * kk), 1e-12))
    emit_dense(1)

    def chunks(t):
        return t.reshape(n_chunks, c_len, width)

    kd_sum = None
    for d in range(2):
        z = par_ref[d:d + 1, :] + _dot(th, wup_ref[d])
        lw = -math.exp(-0.5) * jax.nn.sigmoid(z)
        a = jax.nn.sigmoid(par_ref[2 + d:3 + d, :] + _dot(al, aup_ref[d]))
        kd = k * (1.0 + (a - 1.0) * k_a)
        kd_sum = kd if kd_sum is None else kd_sum + kd
        bb = kk * a
        emit_dense(1)
        ci3 = chunks(_dot(tri_ref[d], lw.astype(BF16)))
        ce3 = ci3 - chunks(lw)
        t_last = 0 if d == 1 else c_len - 1
        cm = 0.5 * ci3[:, t_last:t_last + 1, :]
        e_k = jnp.exp(cm - ci3)
        emit_dense(1)
        kk_ref[d] = (chunks(kk) * jnp.exp(ce3 - cm)).reshape(ta, width).astype(BF16)
        r_ref[d] = (chunks(r) * jnp.exp(ci3 - cm)).reshape(ta, width).astype(BF16)
        b_ref[d] = (chunks(bb) * e_k).reshape(ta, width).astype(BF16)
        k_ref[d] = (chunks(kd) * e_k).reshape(ta, width).astype(BF16)
        em_ref[d, 0] = jnp.exp(cm).reshape(n_chunks, width)
    emit_dense(len(pieces))
    bon_ref[...] = _head_segsum(r * kd_sum * r_k) * v


def _inproj(xs, mod, g, w_in, mu, par, wup, aup, gup, *, c_rw, c_gate, batch, seq, ctx_len):
    n, d = xs.shape
    cols = w_in.shape[1]
    width = RW_HEADS * RW_HEAD_DIM
    ta = RW_TILE
    c_len = RW_CHUNK
    n_tiles = n // ta
    tiles_x = seq // ta
    sub = ta // 8
    last8 = n // 8 - 1
    seq_of = lambda i: jnp.minimum(i // tiles_x, batch)
    idx = np.arange(ta)
    same_chunk = (idx[:, None] // c_len) == (idx[None, :] // c_len)
    tri = jnp.asarray(np.stack([same_chunk & (idx[None, :] <= idx[:, None]),
                                same_chunk & (idx[None, :] >= idx[:, None])]), BF16)
    tok2 = pl.BlockSpec((2, ta, width), lambda i: (0, i, 0))
    tok = lambda c: pl.BlockSpec((ta, c), lambda i: (i, 0))
    sds = jax.ShapeDtypeStruct
    return pl.pallas_call(
        functools.partial(_inproj_kernel, c_rw=c_rw, c_gate=c_gate, n_x_tiles=batch * tiles_x, tiles_x=tiles_x,
                          tiles_ctx=ctx_len // ta),
        grid=(n_tiles,),
        in_specs=[tok(d),
                  pl.BlockSpec((8, d), lambda i: (jnp.maximum(i * sub - 1, 0), 0)),
                  pl.BlockSpec((8, d), lambda i: (jnp.minimum((i + 1) * sub, last8), 0)),
                  pl.BlockSpec((1, N_MOD, d), lambda i: (seq_of(i), 0, 0)),
                  _resident((1, d)), _resident((d, cols)),
                  _resident(mu.shape), _resident(par.shape), _resident(wup.shape), _resident(aup.shape),
                  _resident(gup.shape), _resident(tri.shape)],
        out_specs=[tok(c_rw), tok(cols - c_gate), tok2, tok2, tok2, tok2,
                   pl.BlockSpec((2, 1, ta // c_len, width), lambda i: (0, i, 0, 0)),
                   tok(width), tok(width), tok(width)],
        out_shape=[sds((n, c_rw), BF16), sds((n, cols - c_gate), F32)]
                  + [sds((2, n, width), BF16)] * 4
                  + [sds((2, n_tiles, ta // c_len, width), F32), sds((n, width), BF16),
                     sds((n, width), F32), sds((n, width), F32)],
        compiler_params=_cparams(1),
    )(xs, xs, xs, mod, g, w_in, mu, par, wup, aup, gup, tri)


NA_KEY_ROWS = 2 * NA_ROWS
NA_PAIRS = NA_KEY_ROWS // 2


def _na_pair_bias(rpb):
    w = GRID_W
    qc = np.arange(w)[:, None]
    kc = np.arange(w)[None, :]
    s_c = np.clip(qc - NA_WIN_W // 2, 0, w - NA_WIN_W)
    col_ok = (kc >= s_c) & (kc < s_c + NA_WIN_W)
    dc = np.clip(kc - qc + NA_WIN_W - 1, 0, 2 * NA_WIN_W - 2)
    onehot = jnp.asarray(dc[None] == np.arange(2 * NA_WIN_W - 1)[:, None, None], F32)
    b = jnp.einsum('hrd,dqk->hrqk', rpb.astype(F32), onehot, precision=lax.Precision.HIGHEST)
    b = jnp.where(col_ok[None, None], b, -jnp.inf)
    ninf = jnp.full((rpb.shape[0], 1, w, w), -jnp.inf, F32)
    b = jnp.concatenate([ninf, b, ninf], axis=1)
    return jnp.concatenate([b[:, :-1], b[:, 1:]], axis=-1)


def _na_kernel(q_ref, k0_ref, k1_ref, k2_ref, k3_ref, kc_ref, v0_ref, v1_ref, v2_ref, v3_ref, vc_ref,
               bias_ref, o_ref, kcat, vcat, *, n_blocks):
    j = pl.program_id(2)
    blk = k0_ref.shape[0]
    for s, (kr, vr) in enumerate(((k0_ref, v0_ref), (k1_ref, v1_ref), (k2_ref, v2_ref), (k3_ref, v3_ref),
                                  (kc_ref, vc_ref))):
        kcat[s * blk:s * blk + kr.shape[0], :] = kr[...]
        vcat[s * blk:s * blk + vr.shape[0], :] = vr[...]
    n_loc = 4 * blk
    w = GRID_W
    half = NA_WIN_H // 2
    lane = lax.broadcasted_iota(jnp.int32, (1, LANES), 1)
    first = lane < NA_HEAD_DIM
    q = q_ref[...] * (NA_HEAD_DIM ** -0.5)
    k_all = kcat[...]
    v_all = vcat[...]
    outs = []
    for h in range(2):
        sel = first if h == 0 else jnp.logical_not(first)
        qh = jnp.where(sel, q, jnp.zeros_like(q))
        s_loc = lax.dot_general(qh, k_all[:n_loc], (((1,), (1,)), ((), ())), preferred_element_type=F32)
        s_ctx = lax.dot_general(qh, k_all[n_loc:], (((1,), (1,)), ((), ())), preferred_element_type=F32)
        p_rows = []
        l_rows = []
        for qi in range(NA_ROWS):
            lo = jnp.where(j == 0, max(qi, half), jnp.where(j == n_blocks - 1, min(qi, half), qi))
            m_lo = min(qi, half) // 2
            m_hi = (max(qi, half) + NA_WIN_H - 1) // 2
            n_m = m_hi - m_lo + 1
            key_row = 2 * m_lo + lax.broadcasted_iota(jnp.int32, (1, n_m * LANES), 1) // w
            seen = (key_row >= lo) & (key_row < lo + NA_WIN_H)
            bias = jnp.concatenate(
                [bias_ref[h, min(max(2 * m - qi + half, 0), 2 * NA_WIN_H - 1)] for m in range(m_lo, m_hi + 1)],
                axis=1)
            rows = slice(qi * w, (qi + 1) * w)
            s_q = jnp.where(seen, s_loc[rows, m_lo * LANES:(m_hi + 1) * LANES] + bias, -jnp.inf)
            s_c = s_ctx[rows]
            mx = jnp.maximum(jnp.max(s_q, axis=-1, keepdims=True), jnp.max(s_c, axis=-1, keepdims=True))
            p_q = jnp.exp(s_q - mx)
            p_c = jnp.exp(s_c - mx)
            l_rows.append(jnp.sum(p_q, axis=-1, keepdims=True) + jnp.sum(p_c, axis=-1, keepdims=True))
            pieces = []
            if m_lo > 0:
                pieces.append(jnp.zeros((w, m_lo * LANES), BF16))
            pieces.append(p_q.astype(BF16))
            if m_hi < NA_PAIRS - 1:
                pieces.append(jnp.zeros((w, (NA_PAIRS - 1 - m_hi) * LANES), BF16))
            pieces.append(p_c.astype(BF16))
            p_rows.append(jnp.concatenate(pieces, axis=1))
        p = jnp.concatenate(p_rows, axis=0)
        l = jnp.concatenate(l_rows, axis=0)
        outs.append(_dot(p, v_all) / l)
    o_ref[...] = jnp.where(first, outs[0], outs[1]).astype(o_ref.dtype)


def _na_attention(qkv, bias, *, batch, seq, ctx_len):
    rows = seq // GRID_W
    nj = rows // NA_ROWS
    tq = NA_ROWS * GRID_W
    tk = 4 * GRID_W
    width = NA_HEADS * NA_HEAD_DIM
    n_pairs = width // LANES
    kblocks = seq // tk
    ctx_blk0 = (batch * seq) // ctx_len

    def kv_spec(slot, col0):
        def imap(hp, b, j):
            return (b * kblocks + jnp.clip(2 * j - 1 + slot, 0, kblocks - 1), col0 + hp)
        return pl.BlockSpec((tk, LANES), imap)

    def ctx_spec(col0):
        return pl.BlockSpec((ctx_len, LANES), lambda hp, b, j: (ctx_blk0 + b, col0 + hp))

    in_specs = ([pl.BlockSpec((tq, LANES), lambda hp, b, j: (b * nj + j, hp))]
                + [kv_spec(s, n_pairs) for s in range(4)] + [ctx_spec(n_pairs)]
                + [kv_spec(s, 2 * n_pairs) for s in range(4)] + [ctx_spec(2 * n_pairs)]
                + [pl.BlockSpec((2,) + bias.shape[1:], lambda hp, b, j: (hp, 0, 0, 0))])
    return pl.pallas_call(
        functools.partial(_na_kernel, n_blocks=nj),
        grid=(n_pairs, batch, nj),
        in_specs=in_specs,
        out_specs=pl.BlockSpec((tq, LANES), lambda hp, b, j: (b * nj + j, hp)),
        out_shape=jax.ShapeDtypeStruct((batch * seq, width), BF16),
        scratch_shapes=[pltpu.VMEM((4 * tk + ctx_len, LANES), BF16),
                        pltpu.VMEM((4 * tk + ctx_len, LANES), BF16)],
        compiler_params=_cparams(3),
    )(*([qkv] * 11), bias)


def _rwkv_kernel(kkf_ref, rf_ref, bf_ref, kf_ref, emf_ref, vf_ref, kkb_ref, rb_ref, bb_ref, kb_ref, emb_ref, vb_ref,
                 strict_ref, incl_ref, yf_ref, yb_ref, z_s):
    n = pl.program_id(1)
    ta = vf_ref.shape[0]
    width = vf_ref.shape[1]
    n_pairs = width // LANES
    c_len = RW_CHUNK
    n_chunks = ta // c_len
    n_d = n_chunks * n_pairs
    n_b = 2 * n_d

    @pl.when(n == 0)
    def _():
        z_s[...] = jnp.zeros_like(z_s)

    def chunks(t):
        return t.reshape(n_chunks, c_len, width)

    def to_batch(t):
        parts = [t[:, :, j * LANES:(j + 1) * LANES] for j in range(n_pairs)]
        return jnp.stack(parts, axis=1).reshape(n_d, t.shape[1], LANES)

    def load_direction(kk_ref, r_ref, b_ref, k_ref, em_ref, v_ref):
        em = em_ref[0, 0].reshape(n_chunks, 1, width)
        kk_t = chunks(kk_ref[0].astype(F32))
        r_t = chunks(r_ref[0].astype(F32))
        b_t = chunks(b_ref[0].astype(F32))
        k_t = chunks(k_ref[0].astype(F32))
        return (to_batch(kk_t * em),
                to_batch(r_t * em),
                to_batch(k_t * em),
                to_batch(b_t * em),
                to_batch(em * em),
                to_batch(kk_t), to_batch(r_t), to_batch(b_t), to_batch(k_t),
                to_batch(chunks(v_ref[...].astype(F32))))

    fwd = load_direction(kkf_ref, rf_ref, bf_ref, kf_ref, emf_ref, vf_ref)
    bwd = load_direction(kkb_ref, rb_ref, bb_ref, kb_ref, emb_ref, vb_ref)
    kk_abs, r_abs, k_bar, b_bar, gam, kk_t, r_t, b_t, k_t, v_b = [
        jnp.concatenate([f, b], axis=0) for f, b in zip(fwd, bwd)]

    lane = lax.broadcasted_iota(jnp.int32, (1, 1, LANES), 2)
    first = lane < RW_HEAD_DIM
    ri = lax.broadcasted_iota(jnp.int32, (LANES, LANES), 0)
    cj = lax.broadcasted_iota(jnp.int32, (LANES, LANES), 1)
    eye = ri == cj
    blockdiag = (ri // RW_HEAD_DIM) == (cj // RW_HEAD_DIM)

    def causal(mask_ref, t):
        return jnp.concatenate([jnp.where(mask_ref[0] > 0.0, t[:n_d], 0.0),
                                jnp.where(mask_ref[1] > 0.0, t[n_d:], 0.0)], axis=0)

    def stack_heads(t):
        zero = jnp.zeros_like(t)
        return jnp.concatenate([jnp.where(first, t, zero), jnp.where(first, zero, t)], axis=1)

    def bmm(x, y):
        return lax.dot_general(x.astype(BF16), y.astype(BF16), (((2,), (1,)), ((0,), (0,))),
                               preferred_element_type=F32)

    def bmm_nt(x, y):
        return lax.dot_general(x.astype(BF16), y.astype(BF16), (((2,), (2,)), ((0,), (0,))),
                               preferred_element_type=F32)

    def bmm_tn(x, y):
        return lax.dot_general(x.astype(BF16), y.astype(BF16), (((1,), (1,)), ((0,), (0,))),
                               preferred_element_type=F32)

    gm = bmm_nt(jnp.concatenate([kk_t, r_t], axis=1),
                jnp.concatenate([stack_heads(b_t), stack_heads(k_t)], axis=1))
    ab_w = causal(strict_ref, gm[:, :c_len, :LANES])
    ak_w = causal(strict_ref, gm[:, :c_len, LANES:])
    db_w = causal(incl_ref, gm[:, c_len:, :LANES])
    dk_w = causal(incl_ref, gm[:, c_len:, LANES:])

    l_bd = stack_heads(ab_w)
    base = RW_INV_BASE
    same = lambda s: (ri // s) == (cj // s)
    l_base = jnp.where(same(base), l_bd, 0.0)
    t_bd = jnp.where(eye, 1.0, 0.0) - l_base
    pw = bmm(l_base, l_base)
    span = 2
    while span < base:
        t_bd = t_bd + bmm(t_bd, pw)
        span *= 2
        if span < base:
            pw = bmm(pw, pw)
    size = base
    while size < c_len:
        off = jnp.where(jnp.logical_and(same(2 * size), jnp.logical_not(same(size))), l_bd, 0.0)
        t_bd = t_bd - bmm(bmm(t_bd, off), t_bd)
        size *= 2
    t_w = t_bd[:, :c_len] + t_bd[:, c_len:]

    akv = bmm(jnp.concatenate([ak_w, dk_w], axis=1), stack_heads(v_b))
    pq = bmm(t_w, jnp.concatenate([stack_heads(kk_abs), stack_heads(akv[:, :c_len])], axis=2))
    dpq = bmm(db_w, jnp.concatenate([stack_heads(pq[:, :, :LANES]), stack_heads(pq[:, :, LANES:])], axis=2))
    r_hat = r_abs - dpq[:, :, :LANES]
    y_loc = akv[:, c_len:] - dpq[:, :, LANES:]
    bpq = bmm_tn(b_bar, pq)
    m_t = jnp.where(eye, jnp.broadcast_to(gam, (n_b, LANES, LANES)), 0.0) - bpq[:, :, :LANES]
    g_t = bmm_tn(k_bar, v_b) - bpq[:, :, LANES:]

    zs = z_s[...]
    for step in range(n_chunks):
        c_f, c_b = step, n_chunks - 1 - step
        pick = lambda t: jnp.concatenate([t[c_f * n_pairs:(c_f + 1) * n_pairs],
                                          t[n_d + c_b * n_pairs:n_d + (c_b + 1) * n_pairs]], axis=0)
        both = bmm(jnp.concatenate([pick(r_hat), pick(m_t)], axis=1), zs)
        y_c = both[:, :c_len] + pick(y_loc)
        yf_ref[0, c_f * c_len:(c_f + 1) * c_len, :] = jnp.concatenate([y_c[j] for j in range(n_pairs)], axis=1)
        yb_ref[0, c_b * c_len:(c_b + 1) * c_len, :] = jnp.concatenate(
            [y_c[n_pairs + j] for j in range(n_pairs)], axis=1)
        zs = jnp.where(blockdiag, both[:, c_len:] + pick(g_t), 0.0)
    z_s[...] = zs


def _rwkv_scan(kk_t, r_t, b_t, k_t, em, v, *, batch, seq, ctx_len):
    width = RW_HEADS * RW_HEAD_DIM
    ta = RW_TILE
    nct = ctx_len // ta
    nxt = seq // ta
    n_x = batch * seq
    c_len = RW_CHUNK

    def blk(reverse):
        def index(b, n):
            t_ctx = (nct - 1 - n) if reverse else n
            t_x = (nxt - 1 - (n - nct)) if reverse else (n - nct)
            return jnp.where(n < nct, batch * nxt + b * nct + t_ctx, b * nxt + t_x)
        return index

    def xblk(reverse):
        def index(b, n):
            m = jnp.maximum(n, nct) - nct
            return b * nxt + ((nxt - 1 - m) if reverse else m)
        return index

    t_i = np.arange(c_len)[:, None]
    s_i = np.arange(LANES)[None, :] % RW_HEAD_DIM
    strict = jnp.asarray(np.stack([s_i < t_i, s_i > t_i]), F32)
    incl = jnp.asarray(np.stack([s_i <= t_i, s_i >= t_i]), F32)

    def operands(d):
        index = blk(d == 1)
        tile = pl.BlockSpec((1, ta, width), lambda b, n: (d, index(b, n), 0))
        return [tile, tile, tile, tile,
                pl.BlockSpec((1, 1, ta // c_len, width), lambda b, n: (d, index(b, n), 0, 0)),
                pl.BlockSpec((ta, width), lambda b, n: (index(b, n), 0))]

    def out_spec(reverse):
        index = xblk(reverse)
        return pl.BlockSpec((1, ta, width), lambda b, n: (0, index(b, n), 0))

    y_shape = jax.ShapeDtypeStruct((1, n_x, width), F32)
    return pl.pallas_call(
        _rwkv_kernel,
        grid=(batch, nct + nxt),
        in_specs=operands(0) + operands(1) + [_resident(strict.shape), _resident(incl.shape)],
        out_specs=[out_spec(False), out_spec(True)],
        out_shape=[y_shape, y_shape],
        scratch_shapes=[pltpu.VMEM((2 * (width // LANES), LANES, LANES), F32)],
        compiler_params=_cparams(2),
    )(kk_t, r_t, b_t, k_t, em, v, kk_t, r_t, b_t, k_t, em, v, strict, incl)


def _merge_kernel(x_ref, mod_ref, ona_ref, yf_ref, yb_ref, bon_ref, g_ref, pg_ref, bg_ref, ln_ref,
                  pna_ref, prw_ref, wout_ref, o_ref):
    d = x_ref.shape[1]
    y = yf_ref[0] + yb_ref[0]
    inv_n = 1.0 / RW_HEAD_DIM
    mean = _head_segsum(y) * inv_n
    yc = y - mean
    var = _head_segsum(yc * yc) * inv_n
    yn = yc * lax.rsqrt(var + RW_GN_EPS) * ln_ref[0:1, :] + ln_ref[1:2, :]
    o_rw = ((yn + bon_ref[...]) * g_ref[...]).astype(BF16)
    gates = jax.nn.sigmoid(pg_ref[...] + bg_ref[...])
    m = gates[:, :d] * _dot(ona_ref[...], pna_ref[...]) + gates[:, d:] * _dot(o_rw, prw_ref[...])
    o_ref[...] = x_ref[...] + mod_ref[0, 5:6, :] * _dot(m.astype(BF16), wout_ref[...])


def _merge(x1, mod, o_na, y_f, y_b, bon, g, p_gate, b_gate, ln, p_na, p_rw, w_out, *, tiles_per_seq):
    n_x, d = o_na.shape[0], x1.shape[1]
    width = o_na.shape[1]
    tok = lambda i: (i, 0)
    tok3 = lambda i: (0, i, 0)
    return pl.pallas_call(
        _merge_kernel,
        grid=(n_x // TM,),
        in_specs=[pl.BlockSpec((TM, d), tok),
                  pl.BlockSpec((1, N_MOD, d), lambda i: (i // tiles_per_seq, 0, 0)),
                  pl.BlockSpec((TM, width), tok),
                  pl.BlockSpec((1, TM, width), tok3), pl.BlockSpec((1, TM, width), tok3),
                  pl.BlockSpec((TM, width), tok), pl.BlockSpec((TM, width), tok),
                  pl.BlockSpec((TM, 2 * d), tok),
                  _resident((1, 2 * d)), _resident((2, width)),
                  _resident(p_na.shape), _resident(p_rw.shape), _resident(w_out.shape)],
        out_specs=pl.BlockSpec((TM, d), tok),
        out_shape=jax.ShapeDtypeStruct((n_x, d), F32),
        compiler_params=_cparams(1),
    )(x1, mod, o_na, y_f, y_b, bon, g, p_gate, b_gate, ln, p_na, p_rw, w_out)


def _pad_lora(w_up, direction):
    z = jnp.zeros_like(w_up[0])
    return jnp.concatenate([w_up[0] if direction == 0 else z, w_up[1] if direction == 1 else z], axis=0)


def kernel(x, c, ctx, c_ctx, w_ada, b_ada, norm_ffn1, norm_mix, norm_ffn2, norm_final, ffn1_wg, ffn1_wu, ffn1_wd, ffn2_wg, ffn2_wu, ffn2_wd, w_in, b_gate, na_rpb, rw_mu, rw_w0, rw_w_up, rw_a0, rw_a_up, rw_g_up, rw_k_k, rw_k_a, rw_r_k, rw_ln_w, rw_ln_b, p_na, p_rw, w_out):
    batch, seq, d = x.shape
    ctx_len = ctx.shape[1]
    n_x = batch * seq
    na_width = NA_HEADS * NA_HEAD_DIM
    rw_width = RW_HEADS * RW_HEAD_DIM
    c_rw = 3 * na_width
    c_gate = c_rw + 3 * rw_width + 4 * RW_LORA + RW_GATE_LORA
    assert w_ada.shape[0] == 1, "single layer"
    assert seq % TM == 0 and (batch * ctx_len) % TM == 0 and ctx_len % RW_TILE == 0 and batch + 1 <= 8
    assert seq % (NA_ROWS * GRID_W) == 0 and seq // (NA_ROWS * GRID_W) >= 2 and ctx_len == 4 * GRID_W
    row = lambda t: t.reshape(1, -1)

    cs = jnp.concatenate([c, c_ctx[None], jnp.zeros((8 - batch - 1, d), F32)], axis=0)
    mod = _ada_mod(cs, w_ada[0], b_ada[0])[:batch + 1].reshape(batch + 1, N_MOD, d)

    bf = lambda t: t.astype(BF16)

    x1 = _ffn(x.reshape(n_x, d), ctx.reshape(batch * ctx_len, d), mod, row(norm_ffn1[0]),
              bf(ffn1_wg[0]), bf(ffn1_wu[0]), bf(ffn1_wd[0]), row(norm_final),
              mod0=0, final=False, tiles_per_seq=seq // TM, n_seq=batch)
    par = jnp.concatenate([rw_w0[0], rw_a0[0], row(rw_k_k[0]), row(rw_k_a[0]), row(rw_r_k[0]),
                           jnp.zeros((1, rw_width), F32)], axis=0)
    wup = bf(jnp.stack([_pad_lora(rw_w_up[0], 0), _pad_lora(rw_w_up[0], 1)]))
    aup = bf(jnp.stack([_pad_lora(rw_a_up[0], 0), _pad_lora(rw_a_up[0], 1)]))
    qkv, p_gate, kk_t, r_t, b_t, k_t, em, v_rw, g, bon = _inproj(
        x1, mod, row(norm_mix[0]), bf(w_in[0]), rw_mu[0], par, wup, aup, bf(rw_g_up[0]),
        c_rw=c_rw, c_gate=c_gate, batch=batch, seq=seq, ctx_len=ctx_len)

    bias = _na_pair_bias(na_rpb[0])
    o_na = _na_attention(qkv, bias, batch=batch, seq=seq, ctx_len=ctx_len)

    y_f, y_b = _rwkv_scan(kk_t, r_t, b_t, k_t, em, v_rw, batch=batch, seq=seq, ctx_len=ctx_len)

    ln = jnp.stack([rw_ln_w[0], rw_ln_b[0]], axis=0)
    x2 = _merge(x1, mod, o_na, y_f, y_b, bon, g, p_gate, row(b_gate[0]), ln,
                bf(p_na[0]), bf(p_rw[0]), bf(w_out[0]), tiles_per_seq=seq // TM)
    out = _ffn(x2, None, mod, row(norm_ffn2[0]), bf(ffn2_wg[0]), bf(ffn2_wu[0]), bf(ffn2_wd[0]), row(norm_final),
               mod0=6, final=True, tiles_per_seq=seq // TM, n_seq=batch)
    return out.reshape(batch, seq, d)
```

```python
import functools
import math

import numpy as np
import jax
import jax.numpy as jnp
from jax import lax
from jax.experimental import pallas as pl
from jax.experimental.pallas import tpu as pltpu

F32 = jnp.float32
BF16 = jnp.bfloat16

NORM_EPS = 1e-6
RW_GN_EPS = 64e-5
N_MOD = 9
GRID_W = 64
NA_HEADS = 8
NA_HEAD_DIM = 64
NA_WIN_H = 8
NA_WIN_W = 16
RW_HEADS = 8
RW_HEAD_DIM = 64
RW_LORA = 64
RW_GATE_LORA = 128

LANES = 128
VMEM_LIMIT = 56 * 1024 * 1024

TM = 512
TM_IN = 256
NA_ROWS = 8
RW_TILE = 256
RW_CHUNK = 64
RW_INV_BASE = 8


def _cparams(n_axes):
    return pltpu.CompilerParams(dimension_semantics=("arbitrary",) * n_axes,
                                vmem_limit_bytes=VMEM_LIMIT)


def _resident(shape):
    nd = len(shape)
    return pl.BlockSpec(shape, lambda *_: (0,) * nd, pipeline_mode=pl.Buffered(1))


def _rmsnorm(x, g):
    return x * lax.rsqrt(jnp.mean(x * x, axis=-1, keepdims=True) + NORM_EPS) * g


def _split2(x):
    hi = x.astype(BF16)
    lo = (x - hi.astype(F32)).astype(BF16)
    return hi, lo


def _dot(a, b):
    return jnp.dot(a, b, preferred_element_type=F32)


def _dot_split_lhs(x, w_bf16, parts):
    acc = None
    rem = x
    for _ in range(parts):
        p = rem.astype(BF16)
        rem = rem - p.astype(F32)
        t = _dot(p, w_bf16)
        acc = t if acc is None else acc + t
    return acc


def _dot_split_rhs(w_bf16, x, parts):
    acc = None
    rem = x
    for _ in range(parts):
        p = rem.astype(BF16)
        rem = rem - p.astype(F32)
        t = _dot(w_bf16, p)
        acc = t if acc is None else acc + t
    return acc


def _dot3(a, b):
    ah, al = _split2(a)
    bh, bl = _split2(b)
    return _dot(ah, bh) + (_dot(ah, bl) + _dot(al, bh))


def _ada_kernel(c_ref, w_ref, b_ref, o_ref):
    c = c_ref[...]
    s = c * jax.nn.sigmoid(c)
    o_ref[...] = _dot3(s, w_ref[...]) + b_ref[...]


def _ada_mod(cs, w_ada, b_ada):
    d = cs.shape[1]
    nm = w_ada.shape[1] // d
    return pl.pallas_call(
        _ada_kernel,
        grid=(nm,),
        in_specs=[pl.BlockSpec((8, d), lambda j: (0, 0)),
                  pl.BlockSpec((d, d), lambda j: (0, j)),
                  pl.BlockSpec((1, d), lambda j: (0, j))],
        out_specs=pl.BlockSpec((8, d), lambda j: (0, j)),
        out_shape=jax.ShapeDtypeStruct((8, nm * d), F32),
        compiler_params=_cparams(1),
    )(cs, w_ada, b_ada.reshape(1, -1))


def _ffn_kernel(*refs, mod0, f_chunk, final, n_head_tiles):
    if n_head_tiles is None:
        x_ref, mod_ref, g_ref, wg_ref, wu_ref, wd_ref, gf_ref, o_ref = refs
        x = x_ref[...]
    else:
        x_ref, tail_ref, mod_ref, g_ref, wg_ref, wu_ref, wd_ref, gf_ref, o_ref = refs
        x = jnp.where(pl.program_id(0) < n_head_tiles, x_ref[...], tail_ref[...])
    shift = mod_ref[0, mod0:mod0 + 1, :]
    scale = mod_ref[0, mod0 + 1:mod0 + 2, :]
    gate = mod_ref[0, mod0 + 2:mod0 + 3, :]
    h = (_rmsnorm(x, g_ref[...]) * (1.0 + scale) + shift).astype(BF16)
    d_ff = wg_ref.shape[1]
    acc = None
    for f0 in range(0, d_ff, f_chunk):
        gg = _dot(h, wg_ref[:, f0:f0 + f_chunk])
        uu = _dot(h, wu_ref[:, f0:f0 + f_chunk])
        a = (gg * jax.nn.sigmoid(gg) * uu).astype(BF16)
        t = _dot(a, wd_ref[f0:f0 + f_chunk, :])
        acc = t if acc is None else acc + t
    y = x + 0.5 * gate * acc
    if final:
        y = _rmsnorm(y, gf_ref[...])
    o_ref[...] = y


def _ffn(xs, tail, mod, g, wg, wu, wd, gf, *, mod0, final, tiles_per_seq, n_seq):
    n_head, d = xs.shape
    n = n_head + (0 if tail is None else tail.shape[0])
    d_ff = wg.shape[1]
    f_chunk = d_ff // 2 if (d_ff // 2) % LANES == 0 else d_ff
    seq_of = lambda i: jnp.minimum(i // tiles_per_seq, n_seq)
    head_tiles = n_head // TM
    tokens = [pl.BlockSpec((TM, d), lambda i: (jnp.minimum(i, head_tiles - 1), 0))]
    if tail is not None:
        tokens.append(pl.BlockSpec((TM, d), lambda i: (jnp.maximum(i - head_tiles, 0), 0)))
    return pl.pallas_call(
        functools.partial(_ffn_kernel, mod0=mod0, f_chunk=f_chunk, final=final,
                          n_head_tiles=None if tail is None else head_tiles),
        grid=(n // TM,),
        in_specs=tokens + [pl.BlockSpec((1, N_MOD, d), lambda i: (seq_of(i), 0, 0)),
                           _resident((1, d)), _resident((d, d_ff)), _resident((d, d_ff)), _resident((d_ff, d)),
                           _resident((1, d))],
        out_specs=pl.BlockSpec((TM, d), lambda i: (i, 0)),
        out_shape=jax.ShapeDtypeStruct((n, d), F32),
        compiler_params=_cparams(1),
    )(*([xs] if tail is None else [xs, tail]), mod, g, wg, wu, wd, gf)


def _head_segsum(x):
    r = lax.broadcasted_iota(jnp.int32, (LANES, LANES), 0) // RW_HEAD_DIM
    c = lax.broadcasted_iota(jnp.int32, (LANES, LANES), 1) // RW_HEAD_DIM
    e = jnp.where(r == c, 1.0, 0.0).astype(BF16)
    xb = x.astype(BF16)
    return jnp.concatenate([_dot(xb[:, j:j + LANES], e) for j in range(0, x.shape[1], LANES)], axis=1)


def _inproj_kernel(x_ref, xprev_ref, xnext_ref, mod_ref, g_ref, w_ref, mu_ref, par_ref, wup_ref, aup_ref,
                   gup_ref, tri_ref, qkv_ref, gate_ref, kk_ref, r_ref, b_ref, k_ref, em_ref, v_ref, g_out_ref,
                   bon_ref, p_s, *, c_rw, c_gate, n_x_tiles, tiles_x, tiles_ctx):
    i = pl.program_id(0)
    ta = x_ref.shape[0]
    width = v_ref.shape[1]
    c_len = RW_CHUNK
    n_chunks = ta // c_len
    in_x = i < n_x_tiles
    tile = jnp.where(in_x, i % tiles_x, (i - n_x_tiles) % tiles_ctx)
    last_tile = jnp.where(in_x, tiles_x - 1, tiles_ctx - 1)

    x_all = jnp.concatenate([xprev_ref[...], x_ref[...], xnext_ref[...]], axis=0)
    h_all = (_rmsnorm(x_all, g_ref[...]) * (1.0 + mod_ref[0, 4:5, :]) + mod_ref[0, 3:4, :]).astype(BF16)
    h = h_all[8:ta + 8]
    zero = jnp.zeros((8, h.shape[1]), BF16)
    h_all = jnp.concatenate([jnp.where(tile == 0, zero, h_all[:8]), h,
                             jnp.where(tile == last_tile, zero, h_all[ta + 8:])], axis=0)
    p_s[...] = _dot(h_all, w_ref[:, c_rw:c_gate])

    piece = 4 * LANES
    pieces = ([(qkv_ref, c0, c0) for c0 in range(0, c_rw, piece)]
              + [(gate_ref, c0, c_gate + c0) for c0 in range(0, w_ref.shape[1] - c_gate, piece)])

    def emit_dense(count):
        for _ in range(min(count, len(pieces))):
            ref, dst, src = pieces.pop(0)
            ref[:, dst:dst + piece] = _dot(h, w_ref[:, src:src + piece]).astype(ref.dtype)

    p = p_s[8:ta + 8, :]
    f = p + mu_ref[0:1, :] * (p_s[7:ta + 7, :] - p) + mu_ref[1:2, :] * (p_s[9:ta + 9, :] - p)
    emit_dense(2)

    r = f[:, 0:width]
    k = f[:, width:2 * width]
    v = f[:, 2 * width:3 * width]
    o = 3 * width
    th = jnp.tanh(f[:, o:o + LANES]).astype(BF16)
    al = f[:, o + LANES:o + 2 * LANES].astype(BF16)
    gl = f[:, o + 2 * LANES:o + 3 * LANES]
    k_k = par_ref[4:5, :]
    k_a = par_ref[5:6, :]
    r_k = par_ref[6:7, :]

    v_ref[...] = v.astype(BF16)
    g_out_ref[...] = _dot(jax.nn.sigmoid(gl).astype(BF16), gup_ref[...])
    kk = k * k_k
    kk = kk * lax.rsqrt(jnp.maximum(_head_segsum(kk * kk), 1e-12))
    emit_dense(1)

    def chunks(t):
        return t.reshape(n_chunks, c_len, width)

    kd_sum = None
    for d in range(2):
        z = par_ref[d:d + 1, :] + _dot(th, wup_ref[d])
        lw = -math.exp(-0.5) * jax.nn.sigmoid(z)
        a = jax.nn.sigmoid(par_ref[2 + d:3 + d, :] + _dot(al, aup_ref[d]))
        kd = k * (1.0 + (a - 1.0) * k_a)
        kd_sum = kd if kd_sum is None else kd_sum + kd
        bb = kk * a
        emit_dense(1)
        ci3 = chunks(_dot(tri_ref[d], lw.astype(BF16)))
        ce3 = ci3 - chunks(lw)
        t_last = 0 if d == 1 else c_len - 1
        cm = 0.5 * ci3[:, t_last:t_last + 1, :]
        e_k = jnp.exp(cm - ci3)
        emit_dense(1)
        kk_ref[d] = (chunks(kk) * jnp.exp(ce3 - cm)).reshape(ta, width).astype(BF16)
        r_ref[d] = (chunks(r) * jnp.exp(ci3 - cm)).reshape(ta, width).astype(BF16)
        b_ref[d] = (chunks(bb) * e_k).reshape(ta, width).astype(BF16)
        k_ref[d] = (chunks(kd) * e_k).reshape(ta, width).astype(BF16)
        em_ref[d, 0] = jnp.exp(cm).reshape(n_chunks, width)
    emit_dense(len(pieces))
    bon_ref[...] = _head_segsum(r * kd_sum * r_k) * v


def _inproj(xs, mod, g, w_in, mu, par, wup, aup, gup, *, c_rw, c_gate, batch, seq, ctx_len):
    n, d = xs.shape
    cols = w_in.shape[1]
    width = RW_HEADS * RW_HEAD_DIM
    ta = RW_TILE
    c_len = RW_CHUNK
    n_tiles = n // ta
    tiles_x = seq // ta
    sub = ta // 8
    last8 = n // 8 - 1
    seq_of = lambda i: jnp.minimum(i // tiles_x, batch)
    idx = np.arange(ta)
    same_chunk = (idx[:, None] // c_len) == (idx[None, :] // c_len)
    tri = jnp.asarray(np.stack([same_chunk & (idx[None, :] <= idx[:, None]),
                                same_chunk & (idx[None, :] >= idx[:, None])]), BF16)
    tok2 = pl.BlockSpec((2, ta, width), lambda i: (0, i, 0))
    tok = lambda c: pl.BlockSpec((ta, c), lambda i: (i, 0))
    sds = jax.ShapeDtypeStruct
    return pl.pallas_call(
        functools.partial(_inproj_kernel, c_rw=c_rw, c_gate=c_gate, n_x_tiles=batch * tiles_x, tiles_x=tiles_x,
                          tiles_ctx=ctx_len // ta),
        grid=(n_tiles,),
        in_specs=[tok(d),
                  pl.BlockSpec((8, d), lambda i: (jnp.maximum(i * sub - 1, 0), 0)),
                  pl.BlockSpec((8, d), lambda i: (jnp.minimum((i + 1) * sub, last8), 0)),
                  pl.BlockSpec((1, N_MOD, d), lambda i: (seq_of(i), 0, 0)),
                  _resident((1, d)), _resident((d, cols)),
                  _resident(mu.shape), _resident(par.shape), _resident(wup.shape), _resident(aup.shape),
                  _resident(gup.shape), _resident(tri.shape)],
        out_specs=[tok(c_rw), tok(cols - c_gate), tok2, tok2, tok2, tok2,
                   pl.BlockSpec((2, 1, ta // c_len, width), lambda i: (0, i, 0, 0)),
                   tok(width), tok(width), tok(width)],
        out_shape=[sds((n, c_rw), BF16), sds((n, cols - c_gate), F32)]
                  + [sds((2, n, width), BF16)] * 4
                  + [sds((2, n_tiles, ta // c_len, width), F32), sds((n, width), BF16),
                     sds((n, width), F32), sds((n, width), F32)],
        scratch_shapes=[pltpu.VMEM((ta + 16, c_gate - c_rw), F32)],
        compiler_params=_cparams(1),
    )(xs, xs, xs, mod, g, w_in, mu, par, wup, aup, gup, tri)


NA_KEY_ROWS = 2 * NA_ROWS
NA_PAIRS = NA_KEY_ROWS // 2


def _na_pair_bias(rpb):
    w = GRID_W
    qc = np.arange(w)[:, None]
    kc = np.arange(w)[None, :]
    s_c = np.clip(qc - NA_WIN_W // 2, 0, w - NA_WIN_W)
    col_ok = (kc >= s_c) & (kc < s_c + NA_WIN_W)
    dc = np.clip(kc - qc + NA_WIN_W - 1, 0, 2 * NA_WIN_W - 2)
    onehot = jnp.asarray(dc[None] == np.arange(2 * NA_WIN_W - 1)[:, None, None], F32)
    b = jnp.einsum('hrd,dqk->hrqk', rpb.astype(F32), onehot, precision=lax.Precision.HIGHEST)
    b = jnp.where(col_ok[None, None], b, -jnp.inf)
    ninf = jnp.full((rpb.shape[0], 1, w, w), -jnp.inf, F32)
    b = jnp.concatenate([ninf, b, ninf], axis=1)
    return jnp.concatenate([b[:, :-1], b[:, 1:]], axis=-1)


def _na_kernel(q_ref, k0_ref, k1_ref, k2_ref, k3_ref, kc_ref, v0_ref, v1_ref, v2_ref, v3_ref, vc_ref,
               bias_ref, o_ref, kcat, vcat, *, n_blocks):
    j = pl.program_id(2)
    blk = k0_ref.shape[0]
    for s, (kr, vr) in enumerate(((k0_ref, v0_ref), (k1_ref, v1_ref), (k2_ref, v2_ref), (k3_ref, v3_ref),
                                  (kc_ref, vc_ref))):
        kcat[s * blk:s * blk + kr.shape[0], :] = kr[...]
        vcat[s * blk:s * blk + vr.shape[0], :] = vr[...]
    n_loc = 4 * blk
    w = GRID_W
    half = NA_WIN_H // 2
    sub_rows = NA_ROWS // 2
    sub_pairs = (sub_rows + NA_WIN_H) // 2
    tq = sub_rows * w
    lane = lax.broadcasted_iota(jnp.int32, (1, LANES), 1)
    first = lane < NA_HEAD_DIM
    nt = (((1,), (1,)), ((), ()))

    items = [(pp, h, sb) for pp in range(q_ref.shape[1] // LANES) for sb in range(2) for h in range(2)]

    def scores(item):
        pp, h, sb = item
        cols = slice(pp * LANES, (pp + 1) * LANES)
        q = q_ref[sb * tq:(sb + 1) * tq, cols] * (NA_HEAD_DIM ** -0.5)
        qh = jnp.where(first if h == 0 else jnp.logical_not(first), q, jnp.zeros_like(q))
        k0 = 2 * sb * LANES
        s_loc = lax.dot_general(qh, kcat[k0:k0 + sub_pairs * LANES, cols], nt, preferred_element_type=F32)
        s_ctx = lax.dot_general(qh, kcat[n_loc:, cols], nt, preferred_element_type=F32)
        return s_loc, s_ctx

    def attend(item, s_loc, s_ctx):
        pp, h, sb = item
        cols = slice(pp * LANES, (pp + 1) * LANES)
        p_rows = []
        l_rows = []
        for ql in range(sub_rows):
            qi = sb * sub_rows + ql
            lo = jnp.where(j == 0, max(qi, half), jnp.where(j == n_blocks - 1, min(qi, half), qi))
            m_lo = min(qi, half) // 2
            m_hi = (max(qi, half) + NA_WIN_H - 1) // 2
            n_m = m_hi - m_lo + 1
            key_row = 2 * m_lo + lax.broadcasted_iota(jnp.int32, (1, n_m * LANES), 1) // w
            seen = (key_row >= lo) & (key_row < lo + NA_WIN_H)
            bias = jnp.concatenate(
                [bias_ref[2 * pp + h, min(max(2 * m - qi + half, 0), 2 * NA_WIN_H - 1)]
                 for m in range(m_lo, m_hi + 1)], axis=1)
            rows = slice(ql * w, (ql + 1) * w)
            t_lo, t_hi = m_lo - 2 * sb, m_hi - 2 * sb
            s_q = jnp.where(seen, s_loc[rows, t_lo * LANES:(t_hi + 1) * LANES] + bias, -jnp.inf)
            s_c = s_ctx[rows]
            mx = jnp.maximum(jnp.max(s_q, axis=-1, keepdims=True), jnp.max(s_c, axis=-1, keepdims=True))
            p_q = jnp.exp(s_q - mx)
            p_c = jnp.exp(s_c - mx)
            l_rows.append(jnp.sum(p_q, axis=-1, keepdims=True) + jnp.sum(p_c, axis=-1, keepdims=True))
            pieces = []
            if t_lo > 0:
                pieces.append(jnp.zeros((w, t_lo * LANES), BF16))
            pieces.append(p_q.astype(BF16))
            if t_hi < sub_pairs - 1:
                pieces.append(jnp.zeros((w, (sub_pairs - 1 - t_hi) * LANES), BF16))
            pieces.append(p_c.astype(BF16))
            p_rows.append(jnp.concatenate(pieces, axis=1))
        p = jnp.concatenate(p_rows, axis=0)
        l = jnp.concatenate(l_rows, axis=0)
        k0 = 2 * sb * LANES
        o = (_dot(p[:, :sub_pairs * LANES], vcat[k0:k0 + sub_pairs * LANES, cols])
             + _dot(p[:, sub_pairs * LANES:], vcat[n_loc:, cols]))
        return o / l

    pending = scores(items[0])
    outs = {}
    for n, item in enumerate(items):
        current = pending
        if n + 1 < len(items):
            pending = scores(items[n + 1])
        outs[item] = attend(item, *current)
        pp, h, sb = item
        if h == 1:
            o_ref[sb * tq:(sb + 1) * tq, pp * LANES:(pp + 1) * LANES] = jnp.where(
                first, outs[(pp, 0, sb)], outs[(pp, 1, sb)]).astype(o_ref.dtype)


def _na_attention(qkv, bias, *, batch, seq, ctx_len):
    rows = seq // GRID_W
    nj = rows // NA_ROWS
    tq = NA_ROWS * GRID_W
    tk = 4 * GRID_W
    width = NA_HEADS * NA_HEAD_DIM
    n_pairs = width // LANES
    kblocks = seq // tk
    ctx_blk0 = (batch * seq) // ctx_len

    lanes = 2 * LANES
    n_groups = width // lanes

    def kv_spec(slot, col0):
        def imap(hg, b, j):
            return (b * kblocks + jnp.clip(2 * j - 1 + slot, 0, kblocks - 1), col0 + hg)
        return pl.BlockSpec((tk, lanes), imap)

    def ctx_spec(col0):
        return pl.BlockSpec((ctx_len, lanes), lambda hg, b, j: (ctx_blk0 + b, col0 + hg))

    in_specs = ([pl.BlockSpec((tq, lanes), lambda hg, b, j: (b * nj + j, hg))]
                + [kv_spec(s, n_groups) for s in range(4)] + [ctx_spec(n_groups)]
                + [kv_spec(s, 2 * n_groups) for s in range(4)] + [ctx_spec(2 * n_groups)]
                + [pl.BlockSpec((4,) + bias.shape[1:], lambda hg, b, j: (hg, 0, 0, 0))])
    return pl.pallas_call(
        functools.partial(_na_kernel, n_blocks=nj),
        grid=(n_groups, batch, nj),
        in_specs=in_specs,
        out_specs=pl.BlockSpec((tq, lanes), lambda hg, b, j: (b * nj + j, hg)),
        out_shape=jax.ShapeDtypeStruct((batch * seq, width), BF16),
        scratch_shapes=[pltpu.VMEM((4 * tk + ctx_len, lanes), BF16),
                        pltpu.VMEM((4 * tk + ctx_len, lanes), BF16)],
        compiler_params=_cparams(3),
    )(*([qkv] * 11), bias)


def _rwkv_kernel(kkf_ref, rf_ref, bf_ref, kf_ref, emf_ref, vf_ref, kkb_ref, rb_ref, bb_ref, kb_ref, emb_ref, vb_ref,
                 strict_ref, incl_ref, yf_ref, yb_ref, z_s):
    n = pl.program_id(1)
    ta = vf_ref.shape[0]
    width = vf_ref.shape[1]
    n_pairs = width // LANES
    c_len = RW_CHUNK
    n_chunks = ta // c_len
    n_d = n_chunks * n_pairs
    n_b = 2 * n_d

    @pl.when(n == 0)
    def _():
        z_s[...] = jnp.zeros_like(z_s)

    def chunks(t):
        return t.reshape(n_chunks, c_len, width)

    def to_batch(t):
        parts = [t[:, :, j * LANES:(j + 1) * LANES] for j in range(n_pairs)]
        return jnp.stack(parts, axis=1).reshape(n_d, t.shape[1], LANES)

    def load_direction(kk_ref, r_ref, b_ref, k_ref, em_ref, v_ref):
        em = em_ref[0, 0].reshape(n_chunks, 1, width)
        kk_t = chunks(kk_ref[0].astype(F32))
        r_t = chunks(r_ref[0].astype(F32))
        b_t = chunks(b_ref[0].astype(F32))
        k_t = chunks(k_ref[0].astype(F32))
        return (to_batch(kk_t * em),
                to_batch(r_t * em),
                to_batch(k_t * em),
                to_batch(b_t * em),
                to_batch(em * em),
                to_batch(kk_t), to_batch(r_t), to_batch(b_t), to_batch(k_t),
                to_batch(chunks(v_ref[...].astype(F32))))

    fwd = load_direction(kkf_ref, rf_ref, bf_ref, kf_ref, emf_ref, vf_ref)
    bwd = load_direction(kkb_ref, rb_ref, bb_ref, kb_ref, emb_ref, vb_ref)
    kk_abs, r_abs, k_bar, b_bar, gam, kk_t, r_t, b_t, k_t, v_b = [
        jnp.concatenate([f, b], axis=0) for f, b in zip(fwd, bwd)]

    lane = lax.broadcasted_iota(jnp.int32, (1, 1, LANES), 2)
    first = lane < RW_HEAD_DIM
    ri = lax.broadcasted_iota(jnp.int32, (LANES, LANES), 0)
    cj = lax.broadcasted_iota(jnp.int32, (LANES, LANES), 1)
    eye = ri == cj
    blockdiag = (ri // RW_HEAD_DIM) == (cj // RW_HEAD_DIM)

    def causal(mask_ref, t):
        return jnp.concatenate([jnp.where(mask_ref[0] > 0.0, t[:n_d], 0.0),
                                jnp.where(mask_ref[1] > 0.0, t[n_d:], 0.0)], axis=0)

    def stack_heads(t):
        zero = jnp.zeros_like(t)
        return jnp.concatenate([jnp.where(first, t, zero), jnp.where(first, zero, t)], axis=1)

    def bmm(x, y):
        return lax.dot_general(x.astype(BF16), y.astype(BF16), (((2,), (1,)), ((0,), (0,))),
                               preferred_element_type=F32)

    def bmm_nt(x, y):
        return lax.dot_general(x.astype(BF16), y.astype(BF16), (((2,), (2,)), ((0,), (0,))),
                               preferred_element_type=F32)

    def bmm_tn(x, y):
        return lax.dot_general(x.astype(BF16), y.astype(BF16), (((1,), (1,)), ((0,), (0,))),
                               preferred_element_type=F32)

    gm = bmm_nt(jnp.concatenate([kk_t, r_t], axis=1),
                jnp.concatenate([stack_heads(b_t), stack_heads(k_t)], axis=1))
    ab_w = causal(strict_ref, gm[:, :c_len, :LANES])
    ak_w = causal(strict_ref, gm[:, :c_len, LANES:])
    db_w = causal(incl_ref, gm[:, c_len:, :LANES])
    dk_w = causal(incl_ref, gm[:, c_len:, LANES:])

    l_bd = stack_heads(ab_w)
    base = RW_INV_BASE
    same = lambda s: (ri // s) == (cj // s)
    l_base = jnp.where(same(base), l_bd, 0.0)
    t_bd = jnp.where(eye, 1.0, 0.0) - l_base
    pw = bmm(l_base, l_base)
    span = 2
    while span < base:
        t_bd = t_bd + bmm(t_bd, pw)
        span *= 2
        if span < base:
            pw = bmm(pw, pw)
    size = base
    while size < c_len:
        off = jnp.where(jnp.logical_and(same(2 * size), jnp.logical_not(same(size))), l_bd, 0.0)
        t_bd = t_bd - bmm(bmm(t_bd, off), t_bd)
        size *= 2
    t_w = t_bd[:, :c_len] + t_bd[:, c_len:]

    akv = bmm(jnp.concatenate([ak_w, dk_w], axis=1), stack_heads(v_b))
    pq = bmm(t_w, jnp.concatenate([stack_heads(kk_abs), stack_heads(akv[:, :c_len])], axis=2))
    dpq = bmm(db_w, jnp.concatenate([stack_heads(pq[:, :, :LANES]), stack_heads(pq[:, :, LANES:])], axis=2))
    r_hat = r_abs - dpq[:, :, :LANES]
    y_loc = akv[:, c_len:] - dpq[:, :, LANES:]
    bpq = bmm_tn(b_bar, pq)
    m_t = jnp.where(eye, jnp.broadcast_to(gam, (n_b, LANES, LANES)), 0.0) - bpq[:, :, :LANES]
    g_t = bmm_tn(k_bar, v_b) - bpq[:, :, LANES:]

    zs = z_s[...]
    for step in range(n_chunks):
        c_f, c_b = step, n_chunks - 1 - step
        pick = lambda t: jnp.concatenate([t[c_f * n_pairs:(c_f + 1) * n_pairs],
                                          t[n_d + c_b * n_pairs:n_d + (c_b + 1) * n_pairs]], axis=0)
        both = bmm(jnp.concatenate([pick(r_hat), pick(m_t)], axis=1), zs)
        y_c = both[:, :c_len] + pick(y_loc)
        yf_ref[0, c_f * c_len:(c_f + 1) * c_len, :] = jnp.concatenate([y_c[j] for j in range(n_pairs)], axis=1)
        yb_ref[0, c_b * c_len:(c_b + 1) * c_len, :] = jnp.concatenate(
            [y_c[n_pairs + j] for j in range(n_pairs)], axis=1)
        zs = jnp.where(blockdiag, both[:, c_len:] + pick(g_t), 0.0)
    z_s[...] = zs


def _rwkv_scan(kk_t, r_t, b_t, k_t, em, v, *, batch, seq, ctx_len):
    width = RW_HEADS * RW_HEAD_DIM
    ta = RW_TILE
    nct = ctx_len // ta
    nxt = seq // ta
    n_x = batch * seq
    c_len = RW_CHUNK

    def blk(reverse):
        def index(b, n):
            t_ctx = (nct - 1 - n) if reverse else n
            t_x = (nxt - 1 - (n - nct)) if reverse else (n - nct)
            return jnp.where(n < nct, batch * nxt + b * nct + t_ctx, b * nxt + t_x)
        return index

    def xblk(reverse):
        def index(b, n):
            m = jnp.maximum(n, nct) - nct
            return b * nxt + ((nxt - 1 - m) if reverse else m)
        return index

    t_i = np.arange(c_len)[:, None]
    s_i = np.arange(LANES)[None, :] % RW_HEAD_DIM
    strict = jnp.asarray(np.stack([s_i < t_i, s_i > t_i]), F32)
    incl = jnp.asarray(np.stack([s_i <= t_i, s_i >= t_i]), F32)

    def operands(d):
        index = blk(d == 1)
        tile = pl.BlockSpec((1, ta, width), lambda b, n: (d, index(b, n), 0))
        return [tile, tile, tile, tile,
                pl.BlockSpec((1, 1, ta // c_len, width), lambda b, n: (d, index(b, n), 0, 0)),
                pl.BlockSpec((ta, width), lambda b, n: (index(b, n), 0))]

    def out_spec(reverse):
        index = xblk(reverse)
        return pl.BlockSpec((1, ta, width), lambda b, n: (0, index(b, n), 0))

    y_shape = jax.ShapeDtypeStruct((1, n_x, width), F32)
    return pl.pallas_call(
        _rwkv_kernel,
        grid=(batch, nct + nxt),
        in_specs=operands(0) + operands(1) + [_resident(strict.shape), _resident(incl.shape)],
        out_specs=[out_spec(False), out_spec(True)],
        out_shape=[y_shape, y_shape],
        scratch_shapes=[pltpu.VMEM((2 * (width // LANES), LANES, LANES), F32)],
        compiler_params=_cparams(2),
    )(kk_t, r_t, b_t, k_t, em, v, kk_t, r_t, b_t, k_t, em, v, strict, incl)


def _merge_kernel(x_ref, mod_ref, ona_ref, yf_ref, yb_ref, bon_ref, g_ref, pg_ref, bg_ref, ln_ref,
                  pna_ref, prw_ref, wout_ref, o_ref):
    d = x_ref.shape[1]
    y = yf_ref[0] + yb_ref[0]
    inv_n = 1.0 / RW_HEAD_DIM
    mean = _head_segsum(y) * inv_n
    yc = y - mean
    var = _head_segsum(yc * yc) * inv_n
    yn = yc * lax.rsqrt(var + RW_GN_EPS) * ln_ref[0:1, :] + ln_ref[1:2, :]
    o_rw = ((yn + bon_ref[...]) * g_ref[...]).astype(BF16)
    gates = jax.nn.sigmoid(pg_ref[...] + bg_ref[...])
    m = gates[:, :d] * _dot(ona_ref[...], pna_ref[...]) + gates[:, d:] * _dot(o_rw, prw_ref[...])
    o_ref[...] = x_ref[...] + mod_ref[0, 5:6, :] * _dot(m.astype(BF16), wout_ref[...])


def _merge(x1, mod, o_na, y_f, y_b, bon, g, p_gate, b_gate, ln, p_na, p_rw, w_out, *, tiles_per_seq):
    n_x, d = o_na.shape[0], x1.shape[1]
    width = o_na.shape[1]
    tok = lambda i: (i, 0)
    tok3 = lambda i: (0, i, 0)
    return pl.pallas_call(
        _merge_kernel,
        grid=(n_x // TM,),
        in_specs=[pl.BlockSpec((TM, d), tok),
                  pl.BlockSpec((1, N_MOD, d), lambda i: (i // tiles_per_seq, 0, 0)),
                  pl.BlockSpec((TM, width), tok),
                  pl.BlockSpec((1, TM, width), tok3), pl.BlockSpec((1, TM, width), tok3),
                  pl.BlockSpec((TM, width), tok), pl.BlockSpec((TM, width), tok),
                  pl.BlockSpec((TM, 2 * d), tok),
                  _resident((1, 2 * d)), _resident((2, width)),
                  _resident(p_na.shape), _resident(p_rw.shape), _resident(w_out.shape)],
        out_specs=pl.BlockSpec((TM, d), tok),
        out_shape=jax.ShapeDtypeStruct((n_x, d), F32),
        compiler_params=_cparams(1),
    )(x1, mod, o_na, y_f, y_b, bon, g, p_gate, b_gate, ln, p_na, p_rw, w_out)


def _pad_lora(w_up, direction):
    z = jnp.zeros_like(w_up[0])
    return jnp.concatenate([w_up[0] if direction == 0 else z, w_up[1] if direction == 1 else z], axis=0)


def kernel(x, c, ctx, c_ctx, w_ada, b_ada, norm_ffn1, norm_mix, norm_ffn2, norm_final, ffn1_wg, ffn1_wu, ffn1_wd, ffn2_wg, ffn2_wu, ffn2_wd, w_in, b_gate, na_rpb, rw_mu, rw_w0, rw_w_up, rw_a0, rw_a_up, rw_g_up, rw_k_k, rw_k_a, rw_r_k, rw_ln_w, rw_ln_b, p_na, p_rw, w_out):
    batch, seq, d = x.shape
    ctx_len = ctx.shape[1]
    n_x = batch * seq
    na_width = NA_HEADS * NA_HEAD_DIM
    rw_width = RW_HEADS * RW_HEAD_DIM
    c_rw = 3 * na_width
    c_gate = c_rw + 3 * rw_width + 4 * RW_LORA + RW_GATE_LORA
    assert w_ada.shape[0] == 1, "single layer"
    assert seq % TM == 0 and (batch * ctx_len) % TM == 0 and ctx_len % RW_TILE == 0 and batch + 1 <= 8
    assert seq % (NA_ROWS * GRID_W) == 0 and seq // (NA_ROWS * GRID_W) >= 2 and ctx_len == 4 * GRID_W
    row = lambda t: t.reshape(1, -1)

    cs = jnp.concatenate([c, c_ctx[None], jnp.zeros((8 - batch - 1, d), F32)], axis=0)
    mod = _ada_mod(cs, w_ada[0], b_ada[0])[:batch + 1].reshape(batch + 1, N_MOD, d)

    bf = lambda t: t.astype(BF16)

    x1 = _ffn(x.reshape(n_x, d), ctx.reshape(batch * ctx_len, d), mod, row(norm_ffn1[0]),
              bf(ffn1_wg[0]), bf(ffn1_wu[0]), bf(ffn1_wd[0]), row(norm_final),
              mod0=0, final=False, tiles_per_seq=seq // TM, n_seq=batch)
    par = jnp.concatenate([rw_w0[0], rw_a0[0], row(rw_k_k[0]), row(rw_k_a[0]), row(rw_r_k[0]),
                           jnp.zeros((1, rw_width), F32)], axis=0)
    wup = bf(jnp.stack([_pad_lora(rw_w_up[0], 0), _pad_lora(rw_w_up[0], 1)]))
    aup = bf(jnp.stack([_pad_lora(rw_a_up[0], 0), _pad_lora(rw_a_up[0], 1)]))
    qkv, p_gate, kk_t, r_t, b_t, k_t, em, v_rw, g, bon = _inproj(
        x1, mod, row(norm_mix[0]), bf(w_in[0]), rw_mu[0], par, wup, aup, bf(rw_g_up[0]),
        c_rw=c_rw, c_gate=c_gate, batch=batch, seq=seq, ctx_len=ctx_len)

    bias = _na_pair_bias(na_rpb[0])
    o_na = _na_attention(qkv, bias, batch=batch, seq=seq, ctx_len=ctx_len)

    y_f, y_b = _rwkv_scan(kk_t, r_t, b_t, k_t, em, v_rw, batch=batch, seq=seq, ctx_len=ctx_len)

    ln = jnp.stack([rw_ln_w[0], rw_ln_b[0]], axis=0)
    x2 = _merge(x1, mod, o_na, y_f, y_b, bon, g, p_gate, row(b_gate[0]), ln,
                bf(p_na[0]), bf(p_rw[0]), bf(w_out[0]), tiles_per_seq=seq // TM)
    out = _ffn(x2, None, mod, row(norm_ffn2[0]), bf(ffn2_wg[0]), bf(ffn2_wu[0]), bf(ffn2_wd[0]), row(norm_final),
               mod0=6, final=True, tiles_per_seq=seq // TM, n_seq=batch)
    return out.reshape(batch, seq, d)
```

```python
import functools
import math

import numpy as np
import jax
import jax.numpy as jnp
from jax import lax
from jax.experimental import pallas as pl
from jax.experimental.pallas import tpu as pltpu

F32 = jnp.float32
BF16 = jnp.bfloat16

NORM_EPS = 1e-6
RW_GN_EPS = 64e-5
N_MOD = 9
GRID_W = 64
NA_HEADS = 8
NA_HEAD_DIM = 64
NA_WIN_H = 8
NA_WIN_W = 16
RW_HEADS = 8
RW_HEAD_DIM = 64
RW_LORA = 64
RW_GATE_LORA = 128

LANES = 128
VMEM_LIMIT = 56 * 1024 * 1024

TM = 512
FFN_CHUNK = 256
TM_IN = 256
NA_ROWS = 8
RW_TILE = 256
RW_CHUNK = 64
RW_INV_BASE = 8
assert RW_CHUNK == RW_HEAD_DIM


def _cparams(n_axes):
    return pltpu.CompilerParams(dimension_semantics=("arbitrary",) * n_axes,
                                vmem_limit_bytes=VMEM_LIMIT)


def _resident(shape):
    nd = len(shape)
    return pl.BlockSpec(shape, lambda *_: (0,) * nd, pipeline_mode=pl.Buffered(1))


def _rmsnorm(x, g):
    return x * lax.rsqrt(jnp.mean(x * x, axis=-1, keepdims=True) + NORM_EPS) * g


def _split2(x):
    hi = x.astype(BF16)
    lo = (x - hi.astype(F32)).astype(BF16)
    return hi, lo


def _dot(a, b):
    return jnp.dot(a, b, preferred_element_type=F32)


def _dot_split_lhs(x, w_bf16, parts):
    acc = None
    rem = x
    for _ in range(parts):
        p = rem.astype(BF16)
        rem = rem - p.astype(F32)
        t = _dot(p, w_bf16)
        acc = t if acc is None else acc + t
    return acc


def _dot_split_rhs(w_bf16, x, parts):
    acc = None
    rem = x
    for _ in range(parts):
        p = rem.astype(BF16)
        rem = rem - p.astype(F32)
        t = _dot(w_bf16, p)
        acc = t if acc is None else acc + t
    return acc


def _dot3(a, b):
    ah, al = _split2(a)
    bh, bl = _split2(b)
    return _dot(ah, bh) + (_dot(ah, bl) + _dot(al, bh))


def _ada_kernel(c_ref, w_ref, b_ref, o_ref):
    c = c_ref[...]
    s = c * jax.nn.sigmoid(c)
    o_ref[...] = _dot3(s, w_ref[...]) + b_ref[...]


def _ada_mod(cs, w_ada, b_ada):
    d = cs.shape[1]
    nm = w_ada.shape[1] // d
    return pl.pallas_call(
        _ada_kernel,
        grid=(nm,),
        in_specs=[pl.BlockSpec((8, d), lambda j: (0, 0)),
                  pl.BlockSpec((d, d), lambda j: (0, j)),
                  pl.BlockSpec((1, d), lambda j: (0, j))],
        out_specs=pl.BlockSpec((8, d), lambda j: (0, j)),
        out_shape=jax.ShapeDtypeStruct((8, nm * d), F32),
        compiler_params=_cparams(1),
    )(cs, w_ada, b_ada.reshape(1, -1))


def _ffn_kernel(*refs, mod0, f_chunk, final, n_head_tiles):
    if n_head_tiles is None:
        x_ref, mod_ref, g_ref, wg_ref, wu_ref, wd_ref, gf_ref, o_ref = refs
        x = x_ref[...]
    else:
        x_ref, tail_ref, mod_ref, g_ref, wg_ref, wu_ref, wd_ref, gf_ref, o_ref = refs
        x = jnp.where(pl.program_id(0) < n_head_tiles, x_ref[...], tail_ref[...])
    shift = mod_ref[0, mod0:mod0 + 1, :]
    scale = mod_ref[0, mod0 + 1:mod0 + 2, :]
    gate = mod_ref[0, mod0 + 2:mod0 + 3, :]
    h = (_rmsnorm(x, g_ref[...]) * (1.0 + scale) + shift).astype(BF16)
    d_ff = wg_ref.shape[1]
    bounds = [(f0, min(f0 + f_chunk, d_ff)) for f0 in range(0, d_ff, f_chunk)]

    def gate_up(lo, hi):
        return _dot(h, wg_ref[:, lo:hi]), _dot(h, wu_ref[:, lo:hi])

    acc = None
    pending = gate_up(*bounds[0])
    for c, (lo, hi) in enumerate(bounds):
        gg, uu = pending
        if c + 1 < len(bounds):
            pending = gate_up(*bounds[c + 1])
        a = (gg * jax.nn.sigmoid(gg) * uu).astype(BF16)
        t = _dot(a, wd_ref[lo:hi, :])
        acc = t if acc is None else acc + t
    y = x + 0.5 * gate * acc
    if final:
        y = _rmsnorm(y, gf_ref[...])
    o_ref[...] = y


def _ffn(xs, tail, mod, g, wg, wu, wd, gf, *, mod0, final, tiles_per_seq, n_seq):
    n_head, d = xs.shape
    n = n_head + (0 if tail is None else tail.shape[0])
    d_ff = wg.shape[1]
    f_chunk = FFN_CHUNK
    seq_of = lambda i: jnp.minimum(i // tiles_per_seq, n_seq)
    head_tiles = n_head // TM
    tokens = [pl.BlockSpec((TM, d), lambda i: (jnp.minimum(i, head_tiles - 1), 0))]
    if tail is not None:
        tokens.append(pl.BlockSpec((TM, d), lambda i: (jnp.maximum(i - head_tiles, 0), 0)))
    return pl.pallas_call(
        functools.partial(_ffn_kernel, mod0=mod0, f_chunk=f_chunk, final=final,
                          n_head_tiles=None if tail is None else head_tiles),
        grid=(n // TM,),
        in_specs=tokens + [pl.BlockSpec((1, N_MOD, d), lambda i: (seq_of(i), 0, 0)),
                           _resident((1, d)), _resident((d, d_ff)), _resident((d, d_ff)), _resident((d_ff, d)),
                           _resident((1, d))],
        out_specs=pl.BlockSpec((TM, d), lambda i: (i, 0)),
        out_shape=jax.ShapeDtypeStruct((n, d), F32),
        compiler_params=_cparams(1),
    )(*([xs] if tail is None else [xs, tail]), mod, g, wg, wu, wd, gf)


def _head_segsum(x):
    r = lax.broadcasted_iota(jnp.int32, (LANES, LANES), 0) // RW_HEAD_DIM
    c = lax.broadcasted_iota(jnp.int32, (LANES, LANES), 1) // RW_HEAD_DIM
    e = jnp.where(r == c, 1.0, 0.0).astype(BF16)
    xb = x.astype(BF16)
    return jnp.concatenate([_dot(xb[:, j:j + LANES], e) for j in range(0, x.shape[1], LANES)], axis=1)


def _inproj_kernel(x_ref, xprev_ref, xnext_ref, mod_ref, g_ref, w_ref, mu_ref, par_ref, wup_ref, aup_ref,
                   gup_ref, tri_ref, qkv_ref, gate_ref, kk_ref, r_ref, b_ref, k_ref, em_ref, v_ref, g_out_ref,
                   bon_ref, p_s, *, c_rw, c_gate, n_x_tiles, tiles_x, tiles_ctx):
    i = pl.program_id(0)
    ta = x_ref.shape[0]
    width = v_ref.shape[1]
    c_len = RW_CHUNK
    n_chunks = ta // c_len
    in_x = i < n_x_tiles
    tile = jnp.where(in_x, i % tiles_x, (i - n_x_tiles) % tiles_ctx)
    last_tile = jnp.where(in_x, tiles_x - 1, tiles_ctx - 1)

    x_all = jnp.concatenate([xprev_ref[...], x_ref[...], xnext_ref[...]], axis=0)
    h_all = (_rmsnorm(x_all, g_ref[...]) * (1.0 + mod_ref[0, 4:5, :]) + mod_ref[0, 3:4, :]).astype(BF16)
    h = h_all[8:ta + 8]
    zero = jnp.zeros((8, h.shape[1]), BF16)
    h_all = jnp.concatenate([jnp.where(tile == 0, zero, h_all[:8]), h,
                             jnp.where(tile == last_tile, zero, h_all[ta + 8:])], axis=0)
    p_s[...] = _dot(h_all, w_ref[:, c_rw:c_gate])

    piece = 4 * LANES
    pieces = ([(qkv_ref, c0, c0) for c0 in range(0, c_rw, piece)]
              + [(gate_ref, c0, c_gate + c0) for c0 in range(0, w_ref.shape[1] - c_gate, piece)])

    def emit_dense(count):
        for _ in range(min(count, len(pieces))):
            ref, dst, src = pieces.pop(0)
            ref[:, dst:dst + piece] = _dot(h, w_ref[:, src:src + piece]).astype(ref.dtype)

    p = p_s[8:ta + 8, :]
    f = p + mu_ref[0:1, :] * (p_s[7:ta + 7, :] - p) + mu_ref[1:2, :] * (p_s[9:ta + 9, :] - p)
    emit_dense(2)

    r = f[:, 0:width]
    k = f[:, width:2 * width]
    v = f[:, 2 * width:3 * width]
    o = 3 * width
    th = jnp.tanh(f[:, o:o + LANES]).astype(BF16)
    al = f[:, o + LANES:o + 2 * LANES].astype(BF16)
    gl = f[:, o + 2 * LANES:o + 3 * LANES]
    k_k = par_ref[4:5, :]
    k_a = par_ref[5:6, :]
    r_k = par_ref[6:7, :]

    v_ref[...] = v.astype(BF16)
    g_out_ref[...] = _dot(jax.nn.sigmoid(gl).astype(BF16), gup_ref[...])
    kk = k * k_k
    kk = kk * lax.rsqrt(jnp.maximum(_head_segsum(kk * kk), 1e-12))
    emit_dense(1)

    def chunks(t):
        return t.reshape(n_chunks, c_len, width)

    kd_sum = None
    for d in range(2):
        z = par_ref[d:d + 1, :] + _dot(th, wup_ref[d])
        lw = -math.exp(-0.5) * jax.nn.sigmoid(z)
        a = jax.nn.sigmoid(par_ref[2 + d:3 + d, :] + _dot(al, aup_ref[d]))
        kd = k * (1.0 + (a - 1.0) * k_a)
        kd_sum = kd if kd_sum is None else kd_sum + kd
        bb = kk * a
        emit_dense(1)
        ci3 = chunks(_dot(tri_ref[d], lw.astype(BF16)))
        ce3 = ci3 - chunks(lw)
        t_last = 0 if d == 1 else c_len - 1
        cm = 0.5 * ci3[:, t_last:t_last + 1, :]
        e_k = jnp.exp(cm - ci3)
        emit_dense(1)
        kk_ref[d] = (chunks(kk) * jnp.exp(ce3 - cm)).reshape(ta, width).astype(BF16)
        r_ref[d] = (chunks(r) * jnp.exp(ci3 - cm)).reshape(ta, width).astype(BF16)
        b_ref[d] = (chunks(bb) * e_k).reshape(ta, width).astype(BF16)
        k_ref[d] = (chunks(kd) * e_k).reshape(ta, width).astype(BF16)
        em_ref[d, 0] = jnp.exp(cm).reshape(n_chunks, width)
    emit_dense(len(pieces))
    bon_ref[...] = _head_segsum(r * kd_sum * r_k) * v


def _inproj(xs, mod, g, w_in, mu, par, wup, aup, gup, *, c_rw, c_gate, batch, seq, ctx_len):
    n, d = xs.shape
    cols = w_in.shape[1]
    width = RW_HEADS * RW_HEAD_DIM
    ta = RW_TILE
    c_len = RW_CHUNK
    n_tiles = n // ta
    tiles_x = seq // ta
    sub = ta // 8
    last8 = n // 8 - 1
    seq_of = lambda i: jnp.minimum(i // tiles_x, batch)
    idx = np.arange(ta)
    same_chunk = (idx[:, None] // c_len) == (idx[None, :] // c_len)
    tri = jnp.asarray(np.stack([same_chunk & (idx[None, :] <= idx[:, None]),
                                same_chunk & (idx[None, :] >= idx[:, None])]), BF16)
    tok2 = pl.BlockSpec((2, ta, width), lambda i: (0, i, 0))
    tok = lambda c: pl.BlockSpec((ta, c), lambda i: (i, 0))
    sds = jax.ShapeDtypeStruct
    return pl.pallas_call(
        functools.partial(_inproj_kernel, c_rw=c_rw, c_gate=c_gate, n_x_tiles=batch * tiles_x, tiles_x=tiles_x,
                          tiles_ctx=ctx_len // ta),
        grid=(n_tiles,),
        in_specs=[tok(d),
                  pl.BlockSpec((8, d), lambda i: (jnp.maximum(i * sub - 1, 0), 0)),
                  pl.BlockSpec((8, d), lambda i: (jnp.minimum((i + 1) * sub, last8), 0)),
                  pl.BlockSpec((1, N_MOD, d), lambda i: (seq_of(i), 0, 0)),
                  _resident((1, d)), _resident((d, cols)),
                  _resident(mu.shape), _resident(par.shape), _resident(wup.shape), _resident(aup.shape),
                  _resident(gup.shape), _resident(tri.shape)],
        out_specs=[tok(c_rw), tok(cols - c_gate), tok2, tok2, tok2, tok2,
                   pl.BlockSpec((2, 1, ta // c_len, width), lambda i: (0, i, 0, 0)),
                   tok(width), tok(width), tok(width)],
        out_shape=[sds((n, c_rw), BF16), sds((n, cols - c_gate), F32)]
                  + [sds((2, n, width), BF16)] * 4
                  + [sds((2, n_tiles, ta // c_len, width), F32), sds((n, width), BF16),
                     sds((n, width), F32), sds((n, width), F32)],
        scratch_shapes=[pltpu.VMEM((ta + 16, c_gate - c_rw), F32)],
        compiler_params=_cparams(1),
    )(xs, xs, xs, mod, g, w_in, mu, par, wup, aup, gup, tri)


NA_KEY_ROWS = 2 * NA_ROWS
NA_PAIRS = NA_KEY_ROWS // 2


def _na_pair_bias(rpb):
    w = GRID_W
    qc = np.arange(w)[:, None]
    kc = np.arange(w)[None, :]
    s_c = np.clip(qc - NA_WIN_W // 2, 0, w - NA_WIN_W)
    col_ok = (kc >= s_c) & (kc < s_c + NA_WIN_W)
    dc = np.clip(kc - qc + NA_WIN_W - 1, 0, 2 * NA_WIN_W - 2)
    onehot = jnp.asarray(dc[None] == np.arange(2 * NA_WIN_W - 1)[:, None, None], F32)
    b = jnp.einsum('hrd,dqk->hrqk', rpb.astype(F32), onehot, precision=lax.Precision.HIGHEST)
    b = jnp.where(col_ok[None, None], b, -jnp.inf)
    ninf = jnp.full((rpb.shape[0], 1, w, w), -jnp.inf, F32)
    b = jnp.concatenate([ninf, b, ninf], axis=1)
    return jnp.concatenate([b[:, :-1], b[:, 1:]], axis=-1)


def _na_kernel(q_ref, k0_ref, k1_ref, k2_ref, k3_ref, kc_ref, v0_ref, v1_ref, v2_ref, v3_ref, vc_ref,
               bias_ref, o_ref, kcat, vcat, *, n_blocks):
    j = pl.program_id(2)
    blk = k0_ref.shape[0]
    for s, (kr, vr) in enumerate(((k0_ref, v0_ref), (k1_ref, v1_ref), (k2_ref, v2_ref), (k3_ref, v3_ref),
                                  (kc_ref, vc_ref))):
        kcat[s * blk:s * blk + kr.shape[0], :] = kr[...]
        vcat[s * blk:s * blk + vr.shape[0], :] = vr[...]
    n_loc = 4 * blk
    w = GRID_W
    half = NA_WIN_H // 2
    sub_rows = NA_ROWS // 2
    sub_pairs = (sub_rows + NA_WIN_H) // 2
    tq = sub_rows * w
    lane = lax.broadcasted_iota(jnp.int32, (1, LANES), 1)
    first = lane < NA_HEAD_DIM
    nt = (((1,), (1,)), ((), ()))

    items = [(pp, h, sb) for pp in range(q_ref.shape[1] // LANES) for sb in range(2) for h in range(2)]

    def scores(item):
        pp, h, sb = item
        cols = slice(pp * LANES, (pp + 1) * LANES)
        q = q_ref[sb * tq:(sb + 1) * tq, cols] * (NA_HEAD_DIM ** -0.5)
        qh = jnp.where(first if h == 0 else jnp.logical_not(first), q, jnp.zeros_like(q))
        k0 = 2 * sb * LANES
        s_loc = lax.dot_general(qh, kcat[k0:k0 + sub_pairs * LANES, cols], nt, preferred_element_type=F32)
        s_ctx = lax.dot_general(qh, kcat[n_loc:, cols], nt, preferred_element_type=F32)
        return s_loc, s_ctx

    def attend(item, s_loc, s_ctx):
        pp, h, sb = item
        cols = slice(pp * LANES, (pp + 1) * LANES)
        p_rows = []
        l_rows = []
        for ql in range(sub_rows):
            qi = sb * sub_rows + ql
            lo = jnp.where(j == 0, max(qi, half), jnp.where(j == n_blocks - 1, min(qi, half), qi))
            m_lo = min(qi, half) // 2
            m_hi = (max(qi, half) + NA_WIN_H - 1) // 2
            n_m = m_hi - m_lo + 1
            key_row = 2 * m_lo + lax.broadcasted_iota(jnp.int32, (1, n_m * LANES), 1) // w
            seen = (key_row >= lo) & (key_row < lo + NA_WIN_H)
            bias = jnp.concatenate(
                [bias_ref[2 * pp + h, min(max(2 * m - qi + half, 0), 2 * NA_WIN_H - 1)]
                 for m in range(m_lo, m_hi + 1)], axis=1)
            rows = slice(ql * w, (ql + 1) * w)
            t_lo, t_hi = m_lo - 2 * sb, m_hi - 2 * sb
            s_q = jnp.where(seen, s_loc[rows, t_lo * LANES:(t_hi + 1) * LANES] + bias, -jnp.inf)
            s_c = s_ctx[rows]
            mx = jnp.maximum(jnp.max(s_q, axis=-1, keepdims=True), jnp.max(s_c, axis=-1, keepdims=True))
            p_q = jnp.exp(s_q - mx)
            p_c = jnp.exp(s_c - mx)
            l_rows.append(jnp.sum(p_q, axis=-1, keepdims=True) + jnp.sum(p_c, axis=-1, keepdims=True))
            pieces = []
            if t_lo > 0:
                pieces.append(jnp.zeros((w, t_lo * LANES), BF16))
            pieces.append(p_q.astype(BF16))
            if t_hi < sub_pairs - 1:
                pieces.append(jnp.zeros((w, (sub_pairs - 1 - t_hi) * LANES), BF16))
            pieces.append(p_c.astype(BF16))
            p_rows.append(jnp.concatenate(pieces, axis=1))
        p = jnp.concatenate(p_rows, axis=0)
        l = jnp.concatenate(l_rows, axis=0)
        k0 = 2 * sb * LANES
        o = (_dot(p[:, :sub_pairs * LANES], vcat[k0:k0 + sub_pairs * LANES, cols])
             + _dot(p[:, sub_pairs * LANES:], vcat[n_loc:, cols]))
        return o / l

    pending = scores(items[0])
    outs = {}
    for n, item in enumerate(items):
        current = pending
        if n + 1 < len(items):
            pending = scores(items[n + 1])
        outs[item] = attend(item, *current)
        pp, h, sb = item
        if h == 1:
            o_ref[sb * tq:(sb + 1) * tq, pp * LANES:(pp + 1) * LANES] = jnp.where(
                first, outs[(pp, 0, sb)], outs[(pp, 1, sb)]).astype(o_ref.dtype)


def _na_attention(qkv, bias, *, batch, seq, ctx_len):
    rows = seq // GRID_W
    nj = rows // NA_ROWS
    tq = NA_ROWS * GRID_W
    tk = 4 * GRID_W
    width = NA_HEADS * NA_HEAD_DIM
    n_pairs = width // LANES
    kblocks = seq // tk
    ctx_blk0 = (batch * seq) // ctx_len

    lanes = 2 * LANES
    n_groups = width // lanes

    def kv_spec(slot, col0):
        def imap(hg, b, j):
            return (b * kblocks + jnp.clip(2 * j - 1 + slot, 0, kblocks - 1), col0 + hg)
        return pl.BlockSpec((tk, lanes), imap)

    def ctx_spec(col0):
        return pl.BlockSpec((ctx_len, lanes), lambda hg, b, j: (ctx_blk0 + b, col0 + hg))

    in_specs = ([pl.BlockSpec((tq, lanes), lambda hg, b, j: (b * nj + j, hg))]
                + [kv_spec(s, n_groups) for s in range(4)] + [ctx_spec(n_groups)]
                + [kv_spec(s, 2 * n_groups) for s in range(4)] + [ctx_spec(2 * n_groups)]
                + [pl.BlockSpec((4,) + bias.shape[1:], lambda hg, b, j: (hg, 0, 0, 0))])
    return pl.pallas_call(
        functools.partial(_na_kernel, n_blocks=nj),
        grid=(n_groups, batch, nj),
        in_specs=in_specs,
        out_specs=pl.BlockSpec((tq, lanes), lambda hg, b, j: (b * nj + j, hg)),
        out_shape=jax.ShapeDtypeStruct((batch * seq, width), BF16),
        scratch_shapes=[pltpu.VMEM((4 * tk + ctx_len, lanes), BF16),
                        pltpu.VMEM((4 * tk + ctx_len, lanes), BF16)],
        compiler_params=_cparams(3),
    )(*([qkv] * 11), bias)


def _rwkv_kernel(kkf_ref, rf_ref, bf_ref, kf_ref, emf_ref, vf_ref, kkb_ref, rb_ref, bb_ref, kb_ref, emb_ref, vb_ref,
                 strict_ref, incl_ref, yf_ref, yb_ref, z_s):
    n = pl.program_id(1)
    ta = vf_ref.shape[0]
    width = vf_ref.shape[1]
    n_pairs = width // LANES
    c_len = RW_CHUNK
    n_chunks = ta // c_len
    n_d = n_chunks * n_pairs
    n_b = 2 * n_d

    @pl.when(n == 0)
    def _():
        z_s[...] = jnp.zeros_like(z_s)

    def chunks(t):
        return t.reshape(n_chunks, c_len, width)

    def to_batch(t):
        parts = [t[:, :, j * LANES:(j + 1) * LANES] for j in range(n_pairs)]
        return jnp.stack(parts, axis=1).reshape(n_d, t.shape[1], LANES)

    def load_direction(kk_ref, r_ref, b_ref, k_ref, em_ref, v_ref):
        em = em_ref[0, 0].reshape(n_chunks, 1, width)
        kk_t = chunks(kk_ref[0].astype(F32))
        r_t = chunks(r_ref[0].astype(F32))
        b_t = chunks(b_ref[0].astype(F32))
        k_t = chunks(k_ref[0].astype(F32))
        return (to_batch(kk_t * em),
                to_batch(r_t * em),
                to_batch(k_t * em),
                to_batch(b_t * em),
                to_batch(em * em),
                to_batch(kk_t), to_batch(r_t), to_batch(b_t), to_batch(k_t),
                to_batch(chunks(v_ref[...].astype(F32))))

    fwd = load_direction(kkf_ref, rf_ref, bf_ref, kf_ref, emf_ref, vf_ref)
    bwd = load_direction(kkb_ref, rb_ref, bb_ref, kb_ref, emb_ref, vb_ref)
    kk_abs, r_abs, k_bar, b_bar, gam, kk_t, r_t, b_t, k_t, v_b = [
        jnp.concatenate([f, b], axis=0) for f, b in zip(fwd, bwd)]

    lane = lax.broadcasted_iota(jnp.int32, (1, 1, LANES), 2)
    first = lane < RW_HEAD_DIM
    ri = lax.broadcasted_iota(jnp.int32, (c_len, LANES), 0)
    cj = lax.broadcasted_iota(jnp.int32, (c_len, LANES), 1) % RW_HEAD_DIM
    eye = ri == cj

    def causal(mask_ref, t):
        return jnp.concatenate([jnp.where(mask_ref[0] > 0.0, t[:n_d], 0.0),
                                jnp.where(mask_ref[1] > 0.0, t[n_d:], 0.0)], axis=0)

    def stack_heads(t):
        zero = jnp.zeros_like(t)
        return jnp.concatenate([jnp.where(first, t, zero), jnp.where(first, zero, t)], axis=1)

    def bmm(x, y):
        return lax.dot_general(x.astype(BF16), y.astype(BF16), (((2,), (1,)), ((0,), (0,))),
                               preferred_element_type=F32)

    def bmm_nt(x, y):
        return lax.dot_general(x.astype(BF16), y.astype(BF16), (((2,), (2,)), ((0,), (0,))),
                               preferred_element_type=F32)

    def bmm_tn(x, y):
        return lax.dot_general(x.astype(BF16), y.astype(BF16), (((1,), (1,)), ((0,), (0,))),
                               preferred_element_type=F32)

    gm = bmm_nt(jnp.concatenate([kk_t, r_t], axis=1),
                jnp.concatenate([stack_heads(b_t), stack_heads(k_t)], axis=1))
    ab_w = causal(strict_ref, gm[:, :c_len, :LANES])
    ak_w = causal(strict_ref, gm[:, :c_len, LANES:])
    db_w = causal(incl_ref, gm[:, c_len:, :LANES])
    dk_w = causal(incl_ref, gm[:, c_len:, LANES:])

    base = RW_INV_BASE
    same = lambda s: (ri // s) == (cj // s)
    l_base = jnp.where(same(base), ab_w, 0.0)
    t_w = jnp.where(eye, 1.0, 0.0) - l_base
    pw = bmm(l_base, stack_heads(l_base))
    span = 2
    while span < base:
        t_w = t_w + bmm(t_w, stack_heads(pw))
        span *= 2
        if span < base:
            pw = bmm(pw, stack_heads(pw))
    size = base
    while size < c_len:
        off = jnp.where(jnp.logical_and(same(2 * size), jnp.logical_not(same(size))), ab_w, 0.0)
        t_w = t_w - bmm(bmm(t_w, stack_heads(off)), stack_heads(t_w))
        size *= 2

    def head_diag(t):
        return jnp.where(first, t[:, :RW_HEAD_DIM], t[:, RW_HEAD_DIM:])

    akv = bmm(jnp.concatenate([ak_w, dk_w], axis=1), stack_heads(v_b))
    pq = bmm(t_w, jnp.concatenate([stack_heads(kk_abs), stack_heads(akv[:, :c_len])], axis=2))
    dpq = bmm(db_w, jnp.concatenate([stack_heads(pq[:, :, :LANES]), stack_heads(pq[:, :, LANES:])], axis=2))
    r_hat = r_abs - dpq[:, :, :LANES]
    y_loc = akv[:, c_len:] - dpq[:, :, LANES:]
    bpq = bmm_tn(b_bar, pq)
    m_w = jnp.where(eye, jnp.broadcast_to(gam, (n_b, RW_HEAD_DIM, LANES)), 0.0) - head_diag(bpq[:, :, :LANES])
    g_w = head_diag(bmm_tn(k_bar, v_b) - bpq[:, :, LANES:])

    zs = z_s[...]
    for step in range(n_chunks):
        c_f, c_b = step, n_chunks - 1 - step
        pick = lambda t: jnp.concatenate([t[c_f * n_pairs:(c_f + 1) * n_pairs],
                                          t[n_d + c_b * n_pairs:n_d + (c_b + 1) * n_pairs]], axis=0)
        both = bmm(jnp.concatenate([pick(r_hat), pick(m_w)], axis=1), stack_heads(zs))
        y_c = both[:, :c_len] + pick(y_loc)
        yf_ref[0, c_f * c_len:(c_f + 1) * c_len, :] = jnp.concatenate([y_c[j] for j in range(n_pairs)], axis=1)
        yb_ref[0, c_b * c_len:(c_b + 1) * c_len, :] = jnp.concatenate(
            [y_c[n_pairs + j] for j in range(n_pairs)], axis=1)
        zs = both[:, c_len:] + pick(g_w)
    z_s[...] = zs


def _rwkv_scan(kk_t, r_t, b_t, k_t, em, v, *, batch, seq, ctx_len):
    width = RW_HEADS * RW_HEAD_DIM
    ta = RW_TILE
    nct = ctx_len // ta
    nxt = seq // ta
    n_x = batch * seq
    c_len = RW_CHUNK

    def blk(reverse):
        def index(b, n):
            t_ctx = (nct - 1 - n) if reverse else n
            t_x = (nxt - 1 - (n - nct)) if reverse else (n - nct)
            return jnp.where(n < nct, batch * nxt + b * nct + t_ctx, b * nxt + t_x)
        return index

    def xblk(reverse):
        def index(b, n):
            m = jnp.maximum(n, nct) - nct
            return b * nxt + ((nxt - 1 - m) if reverse else m)
        return index

    t_i = np.arange(c_len)[:, None]
    s_i = np.arange(LANES)[None, :] % RW_HEAD_DIM
    strict = jnp.asarray(np.stack([s_i < t_i, s_i > t_i]), F32)
    incl = jnp.asarray(np.stack([s_i <= t_i, s_i >= t_i]), F32)

    def operands(d):
        index = blk(d == 1)
        tile = pl.BlockSpec((1, ta, width), lambda b, n: (d, index(b, n), 0))
        return [tile, tile, tile, tile,
                pl.BlockSpec((1, 1, ta // c_len, width), lambda b, n: (d, index(b, n), 0, 0)),
                pl.BlockSpec((ta, width), lambda b, n: (index(b, n), 0))]

    def out_spec(reverse):
        index = xblk(reverse)
        return pl.BlockSpec((1, ta, width), lambda b, n: (0, index(b, n), 0))

    y_shape = jax.ShapeDtypeStruct((1, n_x, width), F32)
    return pl.pallas_call(
        _rwkv_kernel,
        grid=(batch, nct + nxt),
        in_specs=operands(0) + operands(1) + [_resident(strict.shape), _resident(incl.shape)],
        out_specs=[out_spec(False), out_spec(True)],
        out_shape=[y_shape, y_shape],
        scratch_shapes=[pltpu.VMEM((2 * (width // LANES), RW_HEAD_DIM, LANES), F32)],
        compiler_params=_cparams(2),
    )(kk_t, r_t, b_t, k_t, em, v, kk_t, r_t, b_t, k_t, em, v, strict, incl)


def _merge_kernel(x_ref, mod_ref, ona_ref, yf_ref, yb_ref, bon_ref, g_ref, pg_ref, bg_ref, ln_ref,
                  pna_ref, prw_ref, wout_ref, o_ref):
    d = x_ref.shape[1]
    y = yf_ref[0] + yb_ref[0]
    inv_n = 1.0 / RW_HEAD_DIM
    mean = _head_segsum(y) * inv_n
    yc = y - mean
    var = _head_segsum(yc * yc) * inv_n
    yn = yc * lax.rsqrt(var + RW_GN_EPS) * ln_ref[0:1, :] + ln_ref[1:2, :]
    o_rw = ((yn + bon_ref[...]) * g_ref[...]).astype(BF16)
    gates = jax.nn.sigmoid(pg_ref[...] + bg_ref[...])
    m = gates[:, :d] * _dot(ona_ref[...], pna_ref[...]) + gates[:, d:] * _dot(o_rw, prw_ref[...])
    o_ref[...] = x_ref[...] + mod_ref[0, 5:6, :] * _dot(m.astype(BF16), wout_ref[...])


def _merge(x1, mod, o_na, y_f, y_b, bon, g, p_gate, b_gate, ln, p_na, p_rw, w_out, *, tiles_per_seq):
    n_x, d = o_na.shape[0], x1.shape[1]
    width = o_na.shape[1]
    tok = lambda i: (i, 0)
    tok3 = lambda i: (0, i, 0)
    return pl.pallas_call(
        _merge_kernel,
        grid=(n_x // TM,),
        in_specs=[pl.BlockSpec((TM, d), tok),
                  pl.BlockSpec((1, N_MOD, d), lambda i: (i // tiles_per_seq, 0, 0)),
                  pl.BlockSpec((TM, width), tok),
                  pl.BlockSpec((1, TM, width), tok3), pl.BlockSpec((1, TM, width), tok3),
                  pl.BlockSpec((TM, width), tok), pl.BlockSpec((TM, width), tok),
                  pl.BlockSpec((TM, 2 * d), tok),
                  _resident((1, 2 * d)), _resident((2, width)),
                  _resident(p_na.shape), _resident(p_rw.shape), _resident(w_out.shape)],
        out_specs=pl.BlockSpec((TM, d), tok),
        out_shape=jax.ShapeDtypeStruct((n_x, d), F32),
        compiler_params=_cparams(1),
    )(x1, mod, o_na, y_f, y_b, bon, g, p_gate, b_gate, ln, p_na, p_rw, w_out)


def _pad_lora(w_up, direction):
    z = jnp.zeros_like(w_up[0])
    return jnp.concatenate([w_up[0] if direction == 0 else z, w_up[1] if direction == 1 else z], axis=0)


def kernel(x, c, ctx, c_ctx, w_ada, b_ada, norm_ffn1, norm_mix, norm_ffn2, norm_final, ffn1_wg, ffn1_wu, ffn1_wd, ffn2_wg, ffn2_wu, ffn2_wd, w_in, b_gate, na_rpb, rw_mu, rw_w0, rw_w_up, rw_a0, rw_a_up, rw_g_up, rw_k_k, rw_k_a, rw_r_k, rw_ln_w, rw_ln_b, p_na, p_rw, w_out):
    batch, seq, d = x.shape
    ctx_len = ctx.shape[1]
    n_x = batch * seq
    na_width = NA_HEADS * NA_HEAD_DIM
    rw_width = RW_HEADS * RW_HEAD_DIM
    c_rw = 3 * na_width
    c_gate = c_rw + 3 * rw_width + 4 * RW_LORA + RW_GATE_LORA
    assert w_ada.shape[0] == 1, "single layer"
    assert seq % TM == 0 and (batch * ctx_len) % TM == 0 and ctx_len % RW_TILE == 0 and batch + 1 <= 8
    assert seq % (NA_ROWS * GRID_W) == 0 and seq // (NA_ROWS * GRID_W) >= 2 and ctx_len == 4 * GRID_W
    row = lambda t: t.reshape(1, -1)

    cs = jnp.concatenate([c, c_ctx[None], jnp.zeros((8 - batch - 1, d), F32)], axis=0)
    mod = _ada_mod(cs, w_ada[0], b_ada[0])[:batch + 1].reshape(batch + 1, N_MOD, d)

    bf = lambda t: t.astype(BF16)

    x1 = _ffn(x.reshape(n_x, d), ctx.reshape(batch * ctx_len, d), mod, row(norm_ffn1[0]),
              bf(ffn1_wg[0]), bf(ffn1_wu[0]), bf(ffn1_wd[0]), row(norm_final),
              mod0=0, final=False, tiles_per_seq=seq // TM, n_seq=batch)
    par = jnp.concatenate([rw_w0[0], rw_a0[0], row(rw_k_k[0]), row(rw_k_a[0]), row(rw_r_k[0]),
                           jnp.zeros((1, rw_width), F32)], axis=0)
    wup = bf(jnp.stack([_pad_lora(rw_w_up[0], 0), _pad_lora(rw_w_up[0], 1)]))
    aup = bf(jnp.stack([_pad_lora(rw_a_up[0], 0), _pad_lora(rw_a_up[0], 1)]))
    qkv, p_gate, kk_t, r_t, b_t, k_t, em, v_rw, g, bon = _inproj(
        x1, mod, row(norm_mix[0]), bf(w_in[0]), rw_mu[0], par, wup, aup, bf(rw_g_up[0]),
        c_rw=c_rw, c_gate=c_gate, batch=batch, seq=seq, ctx_len=ctx_len)

    bias = _na_pair_bias(na_rpb[0])
    o_na = _na_attention(qkv, bias, batch=batch, seq=seq, ctx_len=ctx_len)

    y_f, y_b = _rwkv_scan(kk_t, r_t, b_t, k_t, em, v_rw, batch=batch, seq=seq, ctx_len=ctx_len)

    ln = jnp.stack([rw_ln_w[0], rw_ln_b[0]], axis=0)
    x2 = _merge(x1, mod, o_na, y_f, y_b, bon, g, p_gate, row(b_gate[0]), ln,
                bf(p_na[0]), bf(p_rw[0]), bf(w_out[0]), tiles_per_seq=seq // TM)
    out = _ffn(x2, None, mod, row(norm_ffn2[0]), bf(ffn2_wg[0]), bf(ffn2_wu[0]), bf(ffn2_wd[0]), row(norm_final),
               mod0=6, final=True, tiles_per_seq=seq // TM, n_seq=batch)
    return out.reshape(batch, seq, d)
```

```python
import functools
import math

import numpy as np
import jax
import jax.numpy as jnp
from jax import lax
from jax.experimental import pallas as pl
from jax.experimental.pallas import tpu as pltpu

F32 = jnp.float32
BF16 = jnp.bfloat16

NORM_EPS = 1e-6
RW_GN_EPS = 64e-5
N_MOD = 9
GRID_W = 64
NA_HEADS = 8
NA_HEAD_DIM = 64
NA_WIN_H = 8
NA_WIN_W = 16
RW_HEADS = 8
RW_HEAD_DIM = 64
RW_LORA = 64
RW_GATE_LORA = 128

LANES = 128
VMEM_LIMIT = 56 * 1024 * 1024

TM = 512
FFN_CHUNK = 256
TM_IN = 256
NA_ROWS = 8
RW_TILE = 256
RW_CHUNK = 64
RW_INV_BASE = 8
assert RW_CHUNK == RW_HEAD_DIM


def _cparams(n_axes):
    return pltpu.CompilerParams(dimension_semantics=("arbitrary",) * n_axes,
                                vmem_limit_bytes=VMEM_LIMIT)


def _resident(shape):
    nd = len(shape)
    return pl.BlockSpec(shape, lambda *_: (0,) * nd, pipeline_mode=pl.Buffered(1))


def _rmsnorm(x, g):
    return x * lax.rsqrt(jnp.mean(x * x, axis=-1, keepdims=True) + NORM_EPS) * g


def _split2(x):
    hi = x.astype(BF16)
    lo = (x - hi.astype(F32)).astype(BF16)
    return hi, lo


def _dot(a, b):
    return jnp.dot(a, b, preferred_element_type=F32)


def _dot_split_lhs(x, w_bf16, parts):
    acc = None
    rem = x
    for _ in range(parts):
        p = rem.astype(BF16)
        rem = rem - p.astype(F32)
        t = _dot(p, w_bf16)
        acc = t if acc is None else acc + t
    return acc


def _dot_split_rhs(w_bf16, x, parts):
    acc = None
    rem = x
    for _ in range(parts):
        p = rem.astype(BF16)
        rem = rem - p.astype(F32)
        t = _dot(w_bf16, p)
        acc = t if acc is None else acc + t
    return acc


def _dot3(a, b):
    ah, al = _split2(a)
    bh, bl = _split2(b)
    return _dot(ah, bh) + (_dot(ah, bl) + _dot(al, bh))


def _cast_kernel(w_ref, o_ref):
    o_ref[...] = w_ref[...].astype(o_ref.dtype)


def _to_bf16(w, rows=256):
    r, c = w.shape
    rows = rows if r % rows == 0 else r
    return pl.pallas_call(
        _cast_kernel,
        grid=(r // rows,),
        in_specs=[pl.BlockSpec((rows, c), lambda i: (i, 0))],
        out_specs=pl.BlockSpec((rows, c), lambda i: (i, 0)),
        out_shape=jax.ShapeDtypeStruct((r, c), BF16),
        compiler_params=_cparams(1),
    )(w)


def _ada_kernel(c_ref, w_ref, b_ref, o_ref):
    c = c_ref[...]
    s = c * jax.nn.sigmoid(c)
    o_ref[...] = _dot3(s, w_ref[...]) + b_ref[...]


def _ada_mod(cs, w_ada, b_ada):
    d = cs.shape[1]
    nm = w_ada.shape[1] // d
    return pl.pallas_call(
        _ada_kernel,
        grid=(nm,),
        in_specs=[pl.BlockSpec((8, d), lambda j: (0, 0)),
                  pl.BlockSpec((d, d), lambda j: (0, j)),
                  pl.BlockSpec((1, d), lambda j: (0, j))],
        out_specs=pl.BlockSpec((8, d), lambda j: (0, j)),
        out_shape=jax.ShapeDtypeStruct((8, nm * d), F32),
        compiler_params=_cparams(1),
    )(cs, w_ada, b_ada.reshape(1, -1))


def _ffn_kernel(*refs, mod0, f_chunk, final, n_head_tiles):
    if n_head_tiles is None:
        x_ref, mod_ref, g_ref, wg_ref, wu_ref, wd_ref, gf_ref, o_ref = refs
        x = x_ref[...]
    else:
        x_ref, tail_ref, mod_ref, g_ref, wg_ref, wu_ref, wd_ref, gf_ref, o_ref = refs
        x = jnp.where(pl.program_id(0) < n_head_tiles, x_ref[...], tail_ref[...])
    o_ref[...] = _swiglu_half_step(x, mod_ref, g_ref, wg_ref, wu_ref, wd_ref, gf_ref,
                                   mod0=mod0, f_chunk=f_chunk, final=final)


def _swiglu_half_step(x, mod_ref, g_ref, wg_ref, wu_ref, wd_ref, gf_ref, *, mod0, f_chunk, final):
    shift = mod_ref[0, mod0:mod0 + 1, :]
    scale = mod_ref[0, mod0 + 1:mod0 + 2, :]
    gate = mod_ref[0, mod0 + 2:mod0 + 3, :]
    h = (_rmsnorm(x, g_ref[...]) * (1.0 + scale) + shift).astype(BF16)
    d_ff = wg_ref.shape[1]
    bounds = [(f0, min(f0 + f_chunk, d_ff)) for f0 in range(0, d_ff, f_chunk)]

    def gate_up(lo, hi):
        return _dot(h, wg_ref[:, lo:hi]), _dot(h, wu_ref[:, lo:hi])

    acc = None
    pending = gate_up(*bounds[0])
    for c, (lo, hi) in enumerate(bounds):
        gg, uu = pending
        if c + 1 < len(bounds):
            pending = gate_up(*bounds[c + 1])
        a = (gg * jax.nn.sigmoid(gg) * uu).astype(BF16)
        t = _dot(a, wd_ref[lo:hi, :])
        acc = t if acc is None else acc + t
    y = x + 0.5 * gate * acc
    if final:
        y = _rmsnorm(y, gf_ref[...])
    return y


def _ffn(xs, tail, mod, g, wg, wu, wd, gf, *, mod0, final, tiles_per_seq, n_seq):
    n_head, d = xs.shape
    n = n_head + (0 if tail is None else tail.shape[0])
    d_ff = wg.shape[1]
    f_chunk = FFN_CHUNK
    seq_of = lambda i: jnp.minimum(i // tiles_per_seq, n_seq)
    head_tiles = n_head // TM
    tokens = [pl.BlockSpec((TM, d), lambda i: (jnp.minimum(i, head_tiles - 1), 0))]
    if tail is not None:
        tokens.append(pl.BlockSpec((TM, d), lambda i: (jnp.maximum(i - head_tiles, 0), 0)))
    return pl.pallas_call(
        functools.partial(_ffn_kernel, mod0=mod0, f_chunk=f_chunk, final=final,
                          n_head_tiles=None if tail is None else head_tiles),
        grid=(n // TM,),
        in_specs=tokens + [pl.BlockSpec((1, N_MOD, d), lambda i: (seq_of(i), 0, 0)),
                           _resident((1, d)), _resident((d, d_ff)), _resident((d, d_ff)), _resident((d_ff, d)),
                           _resident((1, d))],
        out_specs=pl.BlockSpec((TM, d), lambda i: (i, 0)),
        out_shape=jax.ShapeDtypeStruct((n, d), F32),
        compiler_params=_cparams(1),
    )(*([xs] if tail is None else [xs, tail]), mod, g, wg, wu, wd, gf)


def _head_segsum(x):
    r = lax.broadcasted_iota(jnp.int32, (LANES, LANES), 0) // RW_HEAD_DIM
    c = lax.broadcasted_iota(jnp.int32, (LANES, LANES), 1) // RW_HEAD_DIM
    e = jnp.where(r == c, 1.0, 0.0).astype(BF16)
    xb = x.astype(BF16)
    return jnp.concatenate([_dot(xb[:, j:j + LANES], e) for j in range(0, x.shape[1], LANES)], axis=1)


def _inproj_kernel(x_ref, xprev_ref, xnext_ref, mod_ref, g_ref, w_ref, mu_ref, par_ref, wup_ref, aup_ref,
                   gup_ref, tri_ref, qkv_ref, gate_ref, kk_ref, r_ref, b_ref, k_ref, em_ref, v_ref, g_out_ref,
                   bon_ref, p_s, *, c_rw, c_gate, n_x_tiles, tiles_x, tiles_ctx):
    i = pl.program_id(0)
    ta = x_ref.shape[0]
    width = v_ref.shape[1]
    c_len = RW_CHUNK
    n_chunks = ta // c_len
    in_x = i < n_x_tiles
    tile = jnp.where(in_x, i % tiles_x, (i - n_x_tiles) % tiles_ctx)
    last_tile = jnp.where(in_x, tiles_x - 1, tiles_ctx - 1)

    x_all = jnp.concatenate([xprev_ref[...], x_ref[...], xnext_ref[...]], axis=0)
    h_all = (_rmsnorm(x_all, g_ref[...]) * (1.0 + mod_ref[0, 4:5, :]) + mod_ref[0, 3:4, :]).astype(BF16)
    h = h_all[8:ta + 8]
    zero = jnp.zeros((8, h.shape[1]), BF16)
    h_all = jnp.concatenate([jnp.where(tile == 0, zero, h_all[:8]), h,
                             jnp.where(tile == last_tile, zero, h_all[ta + 8:])], axis=0)
    p_s[...] = _dot(h_all, w_ref[:, c_rw:c_gate])

    piece = 4 * LANES
    pieces = ([(qkv_ref, c0, c0) for c0 in range(0, c_rw, piece)]
              + [(gate_ref, c0, c_gate + c0) for c0 in range(0, w_ref.shape[1] - c_gate, piece)])

    def emit_dense(count):
        for _ in range(min(count, len(pieces))):
            ref, dst, src = pieces.pop(0)
            ref[:, dst:dst + piece] = _dot(h, w_ref[:, src:src + piece]).astype(ref.dtype)

    p = p_s[8:ta + 8, :]
    f = p + mu_ref[0:1, :] * (p_s[7:ta + 7, :] - p) + mu_ref[1:2, :] * (p_s[9:ta + 9, :] - p)
    emit_dense(2)

    r = f[:, 0:width]
    k = f[:, width:2 * width]
    v = f[:, 2 * width:3 * width]
    o = 3 * width
    th = jnp.tanh(f[:, o:o + LANES]).astype(BF16)
    al = f[:, o + LANES:o + 2 * LANES].astype(BF16)
    gl = f[:, o + 2 * LANES:o + 3 * LANES]
    k_k = par_ref[4:5, :]
    k_a = par_ref[5:6, :]
    r_k = par_ref[6:7, :]

    v_ref[...] = v.astype(BF16)
    g_out_ref[...] = _dot(jax.nn.sigmoid(gl).astype(BF16), gup_ref[...])
    kk = k * k_k
    kk = kk * lax.rsqrt(jnp.maximum(_head_segsum(kk * kk), 1e-12))
    emit_dense(1)

    def chunks(t):
        return t.reshape(n_chunks, c_len, width)

    kd_sum = None
    for d in range(2):
        z = par_ref[d:d + 1, :] + _dot(th, wup_ref[d])
        lw = -math.exp(-0.5) * jax.nn.sigmoid(z)
        a = jax.nn.sigmoid(par_ref[2 + d:3 + d, :] + _dot(al, aup_ref[d]))
        kd = k * (1.0 + (a - 1.0) * k_a)
        kd_sum = kd if kd_sum is None else kd_sum + kd
        bb = kk * a
        emit_dense(1)
        ci3 = chunks(_dot(tri_ref[d], lw.astype(BF16)))
        ce3 = ci3 - chunks(lw)
        t_last = 0 if d == 1 else c_len - 1
        cm = 0.5 * ci3[:, t_last:t_last + 1, :]
        e_k = jnp.exp(cm - ci3)
        emit_dense(1)
        kk_ref[d] = (chunks(kk) * jnp.exp(ce3 - cm)).reshape(ta, width).astype(BF16)
        r_ref[d] = (chunks(r) * jnp.exp(ci3 - cm)).reshape(ta, width).astype(BF16)
        b_ref[d] = (chunks(bb) * e_k).reshape(ta, width).astype(BF16)
        k_ref[d] = (chunks(kd) * e_k).reshape(ta, width).astype(BF16)
        em_ref[d, 0] = jnp.exp(cm).reshape(n_chunks, width)
    emit_dense(len(pieces))
    bon_ref[...] = _head_segsum(r * kd_sum * r_k) * v


def _inproj(xs, mod, g, w_in, mu, par, wup, aup, gup, *, c_rw, c_gate, batch, seq, ctx_len):
    n, d = xs.shape
    cols = w_in.shape[1]
    width = RW_HEADS * RW_HEAD_DIM
    ta = RW_TILE
    c_len = RW_CHUNK
    n_tiles = n // ta
    tiles_x = seq // ta
    sub = ta // 8
    last8 = n // 8 - 1
    seq_of = lambda i: jnp.minimum(i // tiles_x, batch)
    idx = np.arange(ta)
    same_chunk = (idx[:, None] // c_len) == (idx[None, :] // c_len)
    tri = jnp.asarray(np.stack([same_chunk & (idx[None, :] <= idx[:, None]),
                                same_chunk & (idx[None, :] >= idx[:, None])]), BF16)
    tok2 = pl.BlockSpec((2, ta, width), lambda i: (0, i, 0))
    tok = lambda c: pl.BlockSpec((ta, c), lambda i: (i, 0))
    sds = jax.ShapeDtypeStruct
    return pl.pallas_call(
        functools.partial(_inproj_kernel, c_rw=c_rw, c_gate=c_gate, n_x_tiles=batch * tiles_x, tiles_x=tiles_x,
                          tiles_ctx=ctx_len // ta),
        grid=(n_tiles,),
        in_specs=[tok(d),
                  pl.BlockSpec((8, d), lambda i: (jnp.maximum(i * sub - 1, 0), 0)),
                  pl.BlockSpec((8, d), lambda i: (jnp.minimum((i + 1) * sub, last8), 0)),
                  pl.BlockSpec((1, N_MOD, d), lambda i: (seq_of(i), 0, 0)),
                  _resident((1, d)), _resident((d, cols)),
                  _resident(mu.shape), _resident(par.shape), _resident(wup.shape), _resident(aup.shape),
                  _resident(gup.shape), _resident(tri.shape)],
        out_specs=[tok(c_rw), tok(cols - c_gate), tok2, tok2, tok2, tok2,
                   pl.BlockSpec((2, 1, ta // c_len, width), lambda i: (0, i, 0, 0)),
                   tok(width), tok(width), tok(width)],
        out_shape=[sds((n, c_rw), BF16), sds((n, cols - c_gate), BF16)]
                  + [sds((2, n, width), BF16)] * 4
                  + [sds((2, n_tiles, ta // c_len, width), F32), sds((n, width), BF16),
                     sds((n, width), F32), sds((n, width), F32)],
        scratch_shapes=[pltpu.VMEM((ta + 16, c_gate - c_rw), F32)],
        compiler_params=_cparams(1),
    )(xs, xs, xs, mod, g, w_in, mu, par, wup, aup, gup, tri)


NA_KEY_ROWS = 2 * NA_ROWS
NA_PAIRS = NA_KEY_ROWS // 2


def _na_pair_bias(rpb):
    w = GRID_W
    qc = np.arange(w)[:, None]
    kc = np.arange(w)[None, :]
    s_c = np.clip(qc - NA_WIN_W // 2, 0, w - NA_WIN_W)
    col_ok = (kc >= s_c) & (kc < s_c + NA_WIN_W)
    dc = np.clip(kc - qc + NA_WIN_W - 1, 0, 2 * NA_WIN_W - 2)
    onehot = jnp.asarray(dc[None] == np.arange(2 * NA_WIN_W - 1)[:, None, None], F32)
    b = jnp.einsum('hrd,dqk->hrqk', rpb.astype(F32), onehot, precision=lax.Precision.HIGHEST)
    b = jnp.where(col_ok[None, None], b, -jnp.inf)
    ninf = jnp.full((rpb.shape[0], 1, w, w), -jnp.inf, F32)
    b = jnp.concatenate([ninf, b, ninf], axis=1)
    return jnp.concatenate([b[:, :-1], b[:, 1:]], axis=-1)


def _na_kernel(q_ref, k0_ref, k1_ref, k2_ref, k3_ref, kc_ref, v0_ref, v1_ref, v2_ref, v3_ref, vc_ref,
               bias_ref, o_ref, kcat, vcat, *, n_blocks):
    j = pl.program_id(2)
    blk = k0_ref.shape[0]
    for s, (kr, vr) in enumerate(((k0_ref, v0_ref), (k1_ref, v1_ref), (k2_ref, v2_ref), (k3_ref, v3_ref),
                                  (kc_ref, vc_ref))):
        kcat[s * blk:s * blk + kr.shape[0], :] = kr[...]
        vcat[s * blk:s * blk + vr.shape[0], :] = vr[...]
    n_loc = 4 * blk
    w = GRID_W
    half = NA_WIN_H // 2
    sub_rows = NA_ROWS // 2
    sub_pairs = (sub_rows + NA_WIN_H) // 2
    tq = sub_rows * w
    lane = lax.broadcasted_iota(jnp.int32, (1, LANES), 1)
    first = lane < NA_HEAD_DIM
    nt = (((1,), (1,)), ((), ()))

    items = [(pp, h, sb) for pp in range(q_ref.shape[1] // LANES) for sb in range(2) for h in range(2)]

    def scores(item):
        pp, h, sb = item
        cols = slice(pp * LANES, (pp + 1) * LANES)
        q = q_ref[sb * tq:(sb + 1) * tq, cols] * (NA_HEAD_DIM ** -0.5)
        qh = jnp.where(first if h == 0 else jnp.logical_not(first), q, jnp.zeros_like(q))
        k0 = 2 * sb * LANES
        s_loc = lax.dot_general(qh, kcat[k0:k0 + sub_pairs * LANES, cols], nt, preferred_element_type=F32)
        s_ctx = lax.dot_general(qh, kcat[n_loc:, cols], nt, preferred_element_type=F32)
        return s_loc, s_ctx

    def attend(item, s_loc, s_ctx):
        pp, h, sb = item
        cols = slice(pp * LANES, (pp + 1) * LANES)
        p_rows = []
        l_rows = []
        for ql in range(sub_rows):
            qi = sb * sub_rows + ql
            lo = jnp.where(j == 0, max(qi, half), jnp.where(j == n_blocks - 1, min(qi, half), qi))
            m_lo = min(qi, half) // 2
            m_hi = (max(qi, half) + NA_WIN_H - 1) // 2
            n_m = m_hi - m_lo + 1
            key_row = 2 * m_lo + lax.broadcasted_iota(jnp.int32, (1, n_m * LANES), 1) // w
            seen = (key_row >= lo) & (key_row < lo + NA_WIN_H)
            bias = jnp.concatenate(
                [bias_ref[2 * pp + h, min(max(2 * m - qi + half, 0), 2 * NA_WIN_H - 1)]
                 for m in range(m_lo, m_hi + 1)], axis=1)
            rows = slice(ql * w, (ql + 1) * w)
            t_lo, t_hi = m_lo - 2 * sb, m_hi - 2 * sb
            s_q = jnp.where(seen, s_loc[rows, t_lo * LANES:(t_hi + 1) * LANES] + bias, -jnp.inf)
            s_c = s_ctx[rows]
            mx = jnp.maximum(jnp.max(s_q, axis=-1, keepdims=True), jnp.max(s_c, axis=-1, keepdims=True))
            p_q = jnp.exp(s_q - mx)
            p_c = jnp.exp(s_c - mx)
            l_rows.append(jnp.sum(p_q, axis=-1, keepdims=True) + jnp.sum(p_c, axis=-1, keepdims=True))
            pieces = []
            if t_lo > 0:
                pieces.append(jnp.zeros((w, t_lo * LANES), BF16))
            pieces.append(p_q.astype(BF16))
            if t_hi < sub_pairs - 1:
                pieces.append(jnp.zeros((w, (sub_pairs - 1 - t_hi) * LANES), BF16))
            pieces.append(p_c.astype(BF16))
            p_rows.append(jnp.concatenate(pieces, axis=1))
        p = jnp.concatenate(p_rows, axis=0)
        l = jnp.concatenate(l_rows, axis=0)
        k0 = 2 * sb * LANES
        o = (_dot(p[:, :sub_pairs * LANES], vcat[k0:k0 + sub_pairs * LANES, cols])
             + _dot(p[:, sub_pairs * LANES:], vcat[n_loc:, cols]))
        return o / l

    pending = scores(items[0])
    outs = {}
    for n, item in enumerate(items):
        current = pending
        if n + 1 < len(items):
            pending = scores(items[n + 1])
        outs[item] = attend(item, *current)
        pp, h, sb = item
        if h == 1:
            o_ref[sb * tq:(sb + 1) * tq, pp * LANES:(pp + 1) * LANES] = jnp.where(
                first, outs[(pp, 0, sb)], outs[(pp, 1, sb)]).astype(o_ref.dtype)


def _na_attention(qkv, bias, *, batch, seq, ctx_len):
    rows = seq // GRID_W
    nj = rows // NA_ROWS
    tq = NA_ROWS * GRID_W
    tk = 4 * GRID_W
    width = NA_HEADS * NA_HEAD_DIM
    n_pairs = width // LANES
    kblocks = seq // tk
    ctx_blk0 = (batch * seq) // ctx_len

    lanes = 2 * LANES
    n_groups = width // lanes

    def kv_spec(slot, col0):
        def imap(hg, b, j):
            return (b * kblocks + jnp.clip(2 * j - 1 + slot, 0, kblocks - 1), col0 + hg)
        return pl.BlockSpec((tk, lanes), imap)

    def ctx_spec(col0):
        return pl.BlockSpec((ctx_len, lanes), lambda hg, b, j: (ctx_blk0 + b, col0 + hg))

    in_specs = ([pl.BlockSpec((tq, lanes), lambda hg, b, j: (b * nj + j, hg))]
                + [kv_spec(s, n_groups) for s in range(4)] + [ctx_spec(n_groups)]
                + [kv_spec(s, 2 * n_groups) for s in range(4)] + [ctx_spec(2 * n_groups)]
                + [pl.BlockSpec((4,) + bias.shape[1:], lambda hg, b, j: (hg, 0, 0, 0))])
    return pl.pallas_call(
        functools.partial(_na_kernel, n_blocks=nj),
        grid=(n_groups, batch, nj),
        in_specs=in_specs,
        out_specs=pl.BlockSpec((tq, lanes), lambda hg, b, j: (b * nj + j, hg)),
        out_shape=jax.ShapeDtypeStruct((batch * seq, width), BF16),
        scratch_shapes=[pltpu.VMEM((4 * tk + ctx_len, lanes), BF16),
                        pltpu.VMEM((4 * tk + ctx_len, lanes), BF16)],
        compiler_params=_cparams(3),
    )(*([qkv] * 11), bias)


def _rwkv_kernel(kkf_ref, rf_ref, bf_ref, kf_ref, emf_ref, vf_ref, kkb_ref, rb_ref, bb_ref, kb_ref, emb_ref, vb_ref,
                 strict_ref, incl_ref, yf_ref, yb_ref, z_s):
    n = pl.program_id(1)
    ta = vf_ref.shape[0]
    width = vf_ref.shape[1]
    n_pairs = width // LANES
    c_len = RW_CHUNK
    n_chunks = ta // c_len
    n_d = n_chunks * n_pairs
    n_b = 2 * n_d

    @pl.when(n == 0)
    def _():
        z_s[...] = jnp.zeros_like(z_s)

    def chunks(t):
        return t.reshape(n_chunks, c_len, width)

    def to_batch(t):
        parts = [t[:, :, j * LANES:(j + 1) * LANES] for j in range(n_pairs)]
        return jnp.stack(parts, axis=1).reshape(n_d, t.shape[1], LANES)

    def load_direction(kk_ref, r_ref, b_ref, k_ref, em_ref, v_ref):
        em = em_ref[0, 0].reshape(n_chunks, 1, width)
        kk_t = chunks(kk_ref[0].astype(F32))
        r_t = chunks(r_ref[0].astype(F32))
        b_t = chunks(b_ref[0].astype(F32))
        k_t = chunks(k_ref[0].astype(F32))
        return (to_batch(kk_t * em),
                to_batch(r_t * em),
                to_batch(k_t * em),
                to_batch(b_t * em),
                to_batch(em * em),
                to_batch(kk_t), to_batch(r_t), to_batch(b_t), to_batch(k_t),
                to_batch(chunks(v_ref[...].astype(F32))))

    fwd = load_direction(kkf_ref, rf_ref, bf_ref, kf_ref, emf_ref, vf_ref)
    bwd = load_direction(kkb_ref, rb_ref, bb_ref, kb_ref, emb_ref, vb_ref)
    kk_abs, r_abs, k_bar, b_bar, gam, kk_t, r_t, b_t, k_t, v_b = [
        jnp.concatenate([f, b], axis=0) for f, b in zip(fwd, bwd)]

    lane = lax.broadcasted_iota(jnp.int32, (1, 1, LANES), 2)
    first = lane < RW_HEAD_DIM
    ri = lax.broadcasted_iota(jnp.int32, (c_len, LANES), 0)
    cj = lax.broadcasted_iota(jnp.int32, (c_len, LANES), 1) % RW_HEAD_DIM
    eye = ri == cj

    def causal(mask_ref, t):
        return jnp.concatenate([jnp.where(mask_ref[0] > 0.0, t[:n_d], 0.0),
                                jnp.where(mask_ref[1] > 0.0, t[n_d:], 0.0)], axis=0)

    def stack_heads(t):
        zero = jnp.zeros_like(t)
        return jnp.concatenate([jnp.where(first, t, zero), jnp.where(first, zero, t)], axis=1)

    def bmm(x, y):
        return lax.dot_general(x.astype(BF16), y.astype(BF16), (((2,), (1,)), ((0,), (0,))),
                               preferred_element_type=F32)

    def bmm_nt(x, y):
        return lax.dot_general(x.astype(BF16), y.astype(BF16), (((2,), (2,)), ((0,), (0,))),
                               preferred_element_type=F32)

    def bmm_tn(x, y):
        return lax.dot_general(x.astype(BF16), y.astype(BF16), (((1,), (1,)), ((0,), (0,))),
                               preferred_element_type=F32)

    gm = bmm_nt(jnp.concatenate([kk_t, r_t], axis=1),
                jnp.concatenate([stack_heads(b_t), stack_heads(k_t)], axis=1))
    ab_w = causal(strict_ref, gm[:, :c_len, :LANES])
    ak_w = causal(strict_ref, gm[:, :c_len, LANES:])
    db_w = causal(incl_ref, gm[:, c_len:, :LANES])
    dk_w = causal(incl_ref, gm[:, c_len:, LANES:])

    base = RW_INV_BASE
    same = lambda s: (ri // s) == (cj // s)
    l_base = jnp.where(same(base), ab_w, 0.0)
    t_w = jnp.where(eye, 1.0, 0.0) - l_base
    pw = bmm(l_base, stack_heads(l_base))
    span = 2
    while span < base:
        t_w = t_w + bmm(t_w, stack_heads(pw))
        span *= 2
        if span < base:
            pw = bmm(pw, stack_heads(pw))
    size = base
    while size < c_len:
        off = jnp.where(jnp.logical_and(same(2 * size), jnp.logical_not(same(size))), ab_w, 0.0)
        t_w = t_w - bmm(bmm(t_w, stack_heads(off)), stack_heads(t_w))
        size *= 2

    def head_diag(t):
        return jnp.where(first, t[:, :RW_HEAD_DIM], t[:, RW_HEAD_DIM:])

    akv = bmm(jnp.concatenate([ak_w, dk_w], axis=1), stack_heads(v_b))
    pq = bmm(t_w, jnp.concatenate([stack_heads(kk_abs), stack_heads(akv[:, :c_len])], axis=2))
    dpq = bmm(db_w, jnp.concatenate([stack_heads(pq[:, :, :LANES]), stack_heads(pq[:, :, LANES:])], axis=2))
    r_hat = r_abs - dpq[:, :, :LANES]
    y_loc = akv[:, c_len:] - dpq[:, :, LANES:]
    bpq = bmm_tn(b_bar, pq)
    m_w = jnp.where(eye, jnp.broadcast_to(gam, (n_b, RW_HEAD_DIM, LANES)), 0.0) - head_diag(bpq[:, :, :LANES])
    g_w = head_diag(bmm_tn(k_bar, v_b) - bpq[:, :, LANES:])

    zs = z_s[...]
    for step in range(n_chunks):
        c_f, c_b = step, n_chunks - 1 - step
        pick = lambda t: jnp.concatenate([t[c_f * n_pairs:(c_f + 1) * n_pairs],
                                          t[n_d + c_b * n_pairs:n_d + (c_b + 1) * n_pairs]], axis=0)
        both = bmm(jnp.concatenate([pick(r_hat), pick(m_w)], axis=1), stack_heads(zs))
        y_c = both[:, :c_len] + pick(y_loc)
        yf_ref[0, c_f * c_len:(c_f + 1) * c_len, :] = jnp.concatenate([y_c[j] for j in range(n_pairs)], axis=1)
        yb_ref[0, c_b * c_len:(c_b + 1) * c_len, :] = jnp.concatenate(
            [y_c[n_pairs + j] for j in range(n_pairs)], axis=1)
        zs = both[:, c_len:] + pick(g_w)
    z_s[...] = zs


def _rwkv_scan(kk_t, r_t, b_t, k_t, em, v, *, batch, seq, ctx_len):
    width = RW_HEADS * RW_HEAD_DIM
    ta = RW_TILE
    nct = ctx_len // ta
    nxt = seq // ta
    n_x = batch * seq
    c_len = RW_CHUNK

    def blk(reverse):
        def index(b, n):
            t_ctx = (nct - 1 - n) if reverse else n
            t_x = (nxt - 1 - (n - nct)) if reverse else (n - nct)
            return jnp.where(n < nct, batch * nxt + b * nct + t_ctx, b * nxt + t_x)
        return index

    def xblk(reverse):
        def index(b, n):
            m = jnp.maximum(n, nct) - nct
            return b * nxt + ((nxt - 1 - m) if reverse else m)
        return index

    t_i = np.arange(c_len)[:, None]
    s_i = np.arange(LANES)[None, :] % RW_HEAD_DIM
    strict = jnp.asarray(np.stack([s_i < t_i, s_i > t_i]), F32)
    incl = jnp.asarray(np.stack([s_i <= t_i, s_i >= t_i]), F32)

    def operands(d):
        index = blk(d == 1)
        tile = pl.BlockSpec((1, ta, width), lambda b, n: (d, index(b, n), 0))
        return [tile, tile, tile, tile,
                pl.BlockSpec((1, 1, ta // c_len, width), lambda b, n: (d, index(b, n), 0, 0)),
                pl.BlockSpec((ta, width), lambda b, n: (index(b, n), 0))]

    def out_spec(reverse):
        index = xblk(reverse)
        return pl.BlockSpec((1, ta, width), lambda b, n: (0, index(b, n), 0))

    y_shape = jax.ShapeDtypeStruct((1, n_x, width), F32)
    return pl.pallas_call(
        _rwkv_kernel,
        grid=(batch, nct + nxt),
        in_specs=operands(0) + operands(1) + [_resident(strict.shape), _resident(incl.shape)],
        out_specs=[out_spec(False), out_spec(True)],
        out_shape=[y_shape, y_shape],
        scratch_shapes=[pltpu.VMEM((2 * (width // LANES), RW_HEAD_DIM, LANES), F32)],
        compiler_params=_cparams(2),
    )(kk_t, r_t, b_t, k_t, em, v, kk_t, r_t, b_t, k_t, em, v, strict, incl)


def _merge_kernel(x_ref, mod_ref, ona_ref, yf_ref, yb_ref, bon_ref, g_ref, pg_ref, bg_ref, ln_ref,
                  pna_ref, prw_ref, wout_ref, gn_ref, wg_ref, wu_ref, wd_ref, gf_ref, o_ref, *, f_chunk):
    d = x_ref.shape[1]
    y = yf_ref[0] + yb_ref[0]
    inv_n = 1.0 / RW_HEAD_DIM
    mean = _head_segsum(y) * inv_n
    yc = y - mean
    var = _head_segsum(yc * yc) * inv_n
    yn = yc * lax.rsqrt(var + RW_GN_EPS) * ln_ref[0:1, :] + ln_ref[1:2, :]
    o_rw = ((yn + bon_ref[...]) * g_ref[...]).astype(BF16)
    gates = jax.nn.sigmoid(pg_ref[...] + bg_ref[...])
    m = gates[:, :d] * _dot(ona_ref[...], pna_ref[...]) + gates[:, d:] * _dot(o_rw, prw_ref[...])
    x2 = x_ref[...] + mod_ref[0, 5:6, :] * _dot(m.astype(BF16), wout_ref[...])
    o_ref[...] = _swiglu_half_step(x2, mod_ref, gn_ref, wg_ref, wu_ref, wd_ref, gf_ref,
                                   mod0=6, f_chunk=f_chunk, final=True)


def _merge_ffn(x1, mod, o_na, y_f, y_b, bon, g, p_gate, b_gate, ln, p_na, p_rw, w_out, gn, wg, wu, wd, gf, *,
               tiles_per_seq):
    n_x, d = o_na.shape[0], x1.shape[1]
    width = o_na.shape[1]
    tok = lambda i: (i, 0)
    tok3 = lambda i: (0, i, 0)
    return pl.pallas_call(
        functools.partial(_merge_kernel, f_chunk=FFN_CHUNK),
        grid=(n_x // TM,),
        in_specs=[pl.BlockSpec((TM, d), tok),
                  pl.BlockSpec((1, N_MOD, d), lambda i: (i // tiles_per_seq, 0, 0)),
                  pl.BlockSpec((TM, width), tok),
                  pl.BlockSpec((1, TM, width), tok3), pl.BlockSpec((1, TM, width), tok3),
                  pl.BlockSpec((TM, width), tok), pl.BlockSpec((TM, width), tok),
                  pl.BlockSpec((TM, 2 * d), tok),
                  _resident((1, 2 * d)), _resident((2, width)),
                  _resident(p_na.shape), _resident(p_rw.shape), _resident(w_out.shape),
                  _resident(gn.shape), _resident(wg.shape), _resident(wu.shape), _resident(wd.shape),
                  _resident(gf.shape)],
        out_specs=pl.BlockSpec((TM, d), tok),
        out_shape=jax.ShapeDtypeStruct((n_x, d), F32),
        compiler_params=_cparams(1),
    )(x1, mod, o_na, y_f, y_b, bon, g, p_gate, b_gate, ln, p_na, p_rw, w_out, gn, wg, wu, wd, gf)


def _pad_lora(w_up, direction):
    z = jnp.zeros_like(w_up[0])
    return jnp.concatenate([w_up[0] if direction == 0 else z, w_up[1] if direction == 1 else z], axis=0)


def kernel(x, c, ctx, c_ctx, w_ada, b_ada, norm_ffn1, norm_mix, norm_ffn2, norm_final, ffn1_wg, ffn1_wu, ffn1_wd, ffn2_wg, ffn2_wu, ffn2_wd, w_in, b_gate, na_rpb, rw_mu, rw_w0, rw_w_up, rw_a0, rw_a_up, rw_g_up, rw_k_k, rw_k_a, rw_r_k, rw_ln_w, rw_ln_b, p_na, p_rw, w_out):
    batch, seq, d = x.shape
    ctx_len = ctx.shape[1]
    n_x = batch * seq
    na_width = NA_HEADS * NA_HEAD_DIM
    rw_width = RW_HEADS * RW_HEAD_DIM
    c_rw = 3 * na_width
    c_gate = c_rw + 3 * rw_width + 4 * RW_LORA + RW_GATE_LORA
    assert w_ada.shape[0] == 1, "single layer"
    assert seq % TM == 0 and (batch * ctx_len) % TM == 0 and ctx_len % RW_TILE == 0 and batch + 1 <= 8
    assert seq % (NA_ROWS * GRID_W) == 0 and seq // (NA_ROWS * GRID_W) >= 2 and ctx_len == 4 * GRID_W
    row = lambda t: t.reshape(1, -1)

    cs = jnp.concatenate([c, c_ctx[None], jnp.zeros((8 - batch - 1, d), F32)], axis=0)
    mod = _ada_mod(cs, w_ada[0], b_ada[0])[:batch + 1].reshape(batch + 1, N_MOD, d)

    bf = lambda t: t.astype(BF16)

    x1 = _ffn(x.reshape(n_x, d), ctx.reshape(batch * ctx_len, d), mod, row(norm_ffn1[0]),
              _to_bf16(ffn1_wg[0]), _to_bf16(ffn1_wu[0]), _to_bf16(ffn1_wd[0]), row(norm_final),
              mod0=0, final=False, tiles_per_seq=seq // TM, n_seq=batch)
    par = jnp.concatenate([rw_w0[0], rw_a0[0], row(rw_k_k[0]), row(rw_k_a[0]), row(rw_r_k[0]),
                           jnp.zeros((1, rw_width), F32)], axis=0)
    wup = bf(jnp.stack([_pad_lora(rw_w_up[0], 0), _pad_lora(rw_w_up[0], 1)]))
    aup = bf(jnp.stack([_pad_lora(rw_a_up[0], 0), _pad_lora(rw_a_up[0], 1)]))
    qkv, p_gate, kk_t, r_t, b_t, k_t, em, v_rw, g, bon = _inproj(
        x1, mod, row(norm_mix[0]), _to_bf16(w_in[0]), rw_mu[0], par, wup, aup, bf(rw_g_up[0]),
        c_rw=c_rw, c_gate=c_gate, batch=batch, seq=seq, ctx_len=ctx_len)

    bias = _na_pair_bias(na_rpb[0])
    o_na = _na_attention(qkv, bias, batch=batch, seq=seq, ctx_len=ctx_len)

    y_f, y_b = _rwkv_scan(kk_t, r_t, b_t, k_t, em, v_rw, batch=batch, seq=seq, ctx_len=ctx_len)

    ln = jnp.stack([rw_ln_w[0], rw_ln_b[0]], axis=0)
    out = _merge_ffn(x1, mod, o_na, y_f, y_b, bon, g, p_gate, row(b_gate[0]), ln,
                     _to_bf16(p_na[0]), _to_bf16(p_rw[0]), _to_bf16(w_out[0]), row(norm_ffn2[0]),
                     _to_bf16(ffn2_wg[0]), _to_bf16(ffn2_wu[0]), _to_bf16(ffn2_wd[0]), row(norm_final),
                     tiles_per_seq=seq // TM)
    return out.reshape(batch, seq, d)
```

```python
import functools
import math

import numpy as np
import jax
import jax.numpy as jnp
from jax import lax
from jax.experimental import pallas as pl
from jax.experimental.pallas import tpu as pltpu

F32 = jnp.float32
BF16 = jnp.bfloat16

NORM_EPS = 1e-6
RW_GN_EPS = 64e-5
N_MOD = 9
GRID_W = 64
NA_HEADS = 8
NA_HEAD_DIM = 64
NA_WIN_H = 8
NA_WIN_W = 16
RW_HEADS = 8
RW_HEAD_DIM = 64
RW_LORA = 64
RW_GATE_LORA = 128

LANES = 128
VMEM_LIMIT = 56 * 1024 * 1024

TM = 512
FFN_CHUNK = 256
TM_IN = 256
NA_ROWS = 8
RW_TILE = 256
RW_CHUNK = 64
RW_INV_BASE = 8
assert RW_CHUNK == RW_HEAD_DIM


def _cparams(n_axes):
    return pltpu.CompilerParams(dimension_semantics=("arbitrary",) * n_axes,
                                vmem_limit_bytes=VMEM_LIMIT)


def _resident(shape):
    nd = len(shape)
    return pl.BlockSpec(shape, lambda *_: (0,) * nd, pipeline_mode=pl.Buffered(1))


def _rmsnorm(x, g):
    return x * lax.rsqrt(jnp.mean(x * x, axis=-1, keepdims=True) + NORM_EPS) * g


def _split2(x):
    hi = x.astype(BF16)
    lo = (x - hi.astype(F32)).astype(BF16)
    return hi, lo


def _dot(a, b):
    return jnp.dot(a, b, preferred_element_type=F32)


def _dot_split_lhs(x, w_bf16, parts):
    acc = None
    rem = x
    for _ in range(parts):
        p = rem.astype(BF16)
        rem = rem - p.astype(F32)
        t = _dot(p, w_bf16)
        acc = t if acc is None else acc + t
    return acc


def _dot_split_rhs(w_bf16, x, parts):
    acc = None
    rem = x
    for _ in range(parts):
        p = rem.astype(BF16)
        rem = rem - p.astype(F32)
        t = _dot(w_bf16, p)
        acc = t if acc is None else acc + t
    return acc


def _dot3(a, b):
    ah, al = _split2(a)
    bh, bl = _split2(b)
    return _dot(ah, bh) + (_dot(ah, bl) + _dot(al, bh))


WEIGHT_CHUNKS = 16


def _hbm_spec():
    return pl.BlockSpec(memory_space=pl.ANY)


def _stage_shape(w):
    return pltpu.VMEM((2, w.shape[0] // WEIGHT_CHUNKS, w.shape[1]), F32)


def _fetch_as_bf16(src_hbm, dst, stage, sem):
    rows = stage.shape[1]

    def chunk_copy(c):
        return pltpu.make_async_copy(src_hbm.at[c * rows:(c + 1) * rows, :], stage.at[c % 2], sem.at[c % 2])

    chunk_copy(0).start()
    chunk_copy(1).start()
    for c in range(WEIGHT_CHUNKS):
        chunk_copy(c).wait()
        dst[c * rows:(c + 1) * rows, :] = stage[c % 2].astype(dst.dtype)
        if c + 2 < WEIGHT_CHUNKS:
            chunk_copy(c + 2).start()


def _ada_kernel(c_ref, w_ref, b_ref, o_ref):
    c = c_ref[...]
    s = c * jax.nn.sigmoid(c)
    o_ref[...] = _dot3(s, w_ref[...]) + b_ref[...]


def _ada_mod(cs, w_ada, b_ada):
    d = cs.shape[1]
    nm = w_ada.shape[1] // d
    return pl.pallas_call(
        _ada_kernel,
        grid=(nm,),
        in_specs=[pl.BlockSpec((8, d), lambda j: (0, 0)),
                  pl.BlockSpec((d, d), lambda j: (0, j)),
                  pl.BlockSpec((1, d), lambda j: (0, j))],
        out_specs=pl.BlockSpec((8, d), lambda j: (0, j)),
        out_shape=jax.ShapeDtypeStruct((8, nm * d), F32),
        compiler_params=_cparams(1),
    )(cs, w_ada, b_ada.reshape(1, -1))


def _ffn_scratch(wg, wu, wd):
    return [pltpu.VMEM(wg.shape, BF16), pltpu.VMEM(wu.shape, BF16), pltpu.VMEM(wd.shape, BF16),
            _stage_shape(wg), _stage_shape(wd), pltpu.SemaphoreType.DMA((2,))]


def _ffn_load_weights(wg_hbm, wu_hbm, wd_hbm, wg_s, wu_s, wd_s, stage_in, stage_out, sem):
    _fetch_as_bf16(wg_hbm, wg_s, stage_in, sem)
    _fetch_as_bf16(wu_hbm, wu_s, stage_in, sem)
    _fetch_as_bf16(wd_hbm, wd_s, stage_out, sem)


def _ffn_kernel(x_ref, tail_ref, mod_ref, g_ref, wg_hbm, wu_hbm, wd_hbm, gf_ref, o_ref,
                wg_s, wu_s, wd_s, stage_in, stage_out, sem, *, mod0, f_chunk, n_head_tiles):
    @pl.when(pl.program_id(0) == 0)
    def _():
        _ffn_load_weights(wg_hbm, wu_hbm, wd_hbm, wg_s, wu_s, wd_s, stage_in, stage_out, sem)

    x = jnp.where(pl.program_id(0) < n_head_tiles, x_ref[...], tail_ref[...])
    o_ref[...] = _swiglu_half_step(x, mod_ref, g_ref, wg_s, wu_s, wd_s, gf_ref,
                                   mod0=mod0, f_chunk=f_chunk, final=False)


def _swiglu_half_step(x, mod_ref, g_ref, wg_ref, wu_ref, wd_ref, gf_ref, *, mod0, f_chunk, final):
    shift = mod_ref[0, mod0:mod0 + 1, :]
    scale = mod_ref[0, mod0 + 1:mod0 + 2, :]
    gate = mod_ref[0, mod0 + 2:mod0 + 3, :]
    h = (_rmsnorm(x, g_ref[...]) * (1.0 + scale) + shift).astype(BF16)
    d_ff = wg_ref.shape[1]
    bounds = [(f0, min(f0 + f_chunk, d_ff)) for f0 in range(0, d_ff, f_chunk)]

    def gate_up(lo, hi):
        return _dot(h, wg_ref[:, lo:hi]), _dot(h, wu_ref[:, lo:hi])

    acc = None
    pending = gate_up(*bounds[0])
    for c, (lo, hi) in enumerate(bounds):
        gg, uu = pending
        if c + 1 < len(bounds):
            pending = gate_up(*bounds[c + 1])
        a = (gg * jax.nn.sigmoid(gg) * uu).astype(BF16)
        t = _dot(a, wd_ref[lo:hi, :])
        acc = t if acc is None else acc + t
    y = x + 0.5 * gate * acc
    if final:
        y = _rmsnorm(y, gf_ref[...])
    return y


def _ffn(xs, tail, mod, g, wg, wu, wd, gf, *, mod0, tiles_per_seq, n_seq):
    n_head, d = xs.shape
    n = n_head + tail.shape[0]
    seq_of = lambda i: jnp.minimum(i // tiles_per_seq, n_seq)
    head_tiles = n_head // TM
    return pl.pallas_call(
        functools.partial(_ffn_kernel, mod0=mod0, f_chunk=FFN_CHUNK, n_head_tiles=head_tiles),
        grid=(n // TM,),
        in_specs=[pl.BlockSpec((TM, d), lambda i: (jnp.minimum(i, head_tiles - 1), 0)),
                  pl.BlockSpec((TM, d), lambda i: (jnp.maximum(i - head_tiles, 0), 0)),
                  pl.BlockSpec((1, N_MOD, d), lambda i: (seq_of(i), 0, 0)),
                  _resident((1, d)), _hbm_spec(), _hbm_spec(), _hbm_spec(), _resident((1, d))],
        out_specs=pl.BlockSpec((TM, d), lambda i: (i, 0)),
        out_shape=jax.ShapeDtypeStruct((n, d), F32),
        scratch_shapes=_ffn_scratch(wg, wu, wd),
        compiler_params=_cparams(1),
    )(xs, tail, mod, g, wg, wu, wd, gf)


def _head_segsum(x):
    r = lax.broadcasted_iota(jnp.int32, (LANES, LANES), 0) // RW_HEAD_DIM
    c = lax.broadcasted_iota(jnp.int32, (LANES, LANES), 1) // RW_HEAD_DIM
    e = jnp.where(r == c, 1.0, 0.0).astype(BF16)
    xb = x.astype(BF16)
    return jnp.concatenate([_dot(xb[:, j:j + LANES], e) for j in range(0, x.shape[1], LANES)], axis=1)


def _inproj_kernel(x_ref, xprev_ref, xnext_ref, mod_ref, g_ref, w_hbm, mu_ref, par_ref, wup_ref, aup_ref,
                   gup_ref, tri_ref, qkv_ref, gate_ref, kk_ref, r_ref, b_ref, k_ref, em_ref, v_ref, g_out_ref,
                   bon_ref, p_s, w_ref, stage, sem, *, c_rw, c_gate, n_x_tiles, tiles_x, tiles_ctx):
    i = pl.program_id(0)

    @pl.when(i == 0)
    def _():
        _fetch_as_bf16(w_hbm, w_ref, stage, sem)

    ta = x_ref.shape[0]
    width = v_ref.shape[1]
    c_len = RW_CHUNK
    n_chunks = ta // c_len
    in_x = i < n_x_tiles
    tile = jnp.where(in_x, i % tiles_x, (i - n_x_tiles) % tiles_ctx)
    last_tile = jnp.where(in_x, tiles_x - 1, tiles_ctx - 1)

    x_all = jnp.concatenate([xprev_ref[...], x_ref[...], xnext_ref[...]], axis=0)
    h_all = (_rmsnorm(x_all, g_ref[...]) * (1.0 + mod_ref[0, 4:5, :]) + mod_ref[0, 3:4, :]).astype(BF16)
    h = h_all[8:ta + 8]
    zero = jnp.zeros((8, h.shape[1]), BF16)
    h_all = jnp.concatenate([jnp.where(tile == 0, zero, h_all[:8]), h,
                             jnp.where(tile == last_tile, zero, h_all[ta + 8:])], axis=0)
    p_s[...] = _dot(h_all, w_ref[:, c_rw:c_gate])

    piece = 4 * LANES
    pieces = ([(qkv_ref, c0, c0) for c0 in range(0, c_rw, piece)]
              + [(gate_ref, c0, c_gate + c0) for c0 in range(0, w_ref.shape[1] - c_gate, piece)])

    def emit_dense(count):
        for _ in range(min(count, len(pieces))):
            ref, dst, src = pieces.pop(0)
            ref[:, dst:dst + piece] = _dot(h, w_ref[:, src:src + piece]).astype(ref.dtype)

    p = p_s[8:ta + 8, :]
    f = p + mu_ref[0:1, :] * (p_s[7:ta + 7, :] - p) + mu_ref[1:2, :] * (p_s[9:ta + 9, :] - p)
    emit_dense(2)

    r = f[:, 0:width]
    k = f[:, width:2 * width]
    v = f[:, 2 * width:3 * width]
    o = 3 * width
    th = jnp.tanh(f[:, o:o + LANES]).astype(BF16)
    al = f[:, o + LANES:o + 2 * LANES].astype(BF16)
    gl = f[:, o + 2 * LANES:o + 3 * LANES]
    k_k = par_ref[4:5, :]
    k_a = par_ref[5:6, :]
    r_k = par_ref[6:7, :]

    v_ref[...] = v.astype(BF16)
    g_out_ref[...] = _dot(jax.nn.sigmoid(gl).astype(BF16), gup_ref[...])
    kk = k * k_k
    kk = kk * lax.rsqrt(jnp.maximum(_head_segsum(kk * kk), 1e-12))
    emit_dense(1)

    def chunks(t):
        return t.reshape(n_chunks, c_len, width)

    kd_sum = None
    for d in range(2):
        z = par_ref[d:d + 1, :] + _dot(th, wup_ref[d])
        lw = -math.exp(-0.5) * jax.nn.sigmoid(z)
        a = jax.nn.sigmoid(par_ref[2 + d:3 + d, :] + _dot(al, aup_ref[d]))
        kd = k * (1.0 + (a - 1.0) * k_a)
        kd_sum = kd if kd_sum is None else kd_sum + kd
        bb = kk * a
        emit_dense(1)
        ci3 = chunks(_dot(tri_ref[d], lw.astype(BF16)))
        ce3 = ci3 - chunks(lw)
        t_last = 0 if d == 1 else c_len - 1
        cm = 0.5 * ci3[:, t_last:t_last + 1, :]
        e_k = jnp.exp(cm - ci3)
        emit_dense(1)
        kk_ref[d] = (chunks(kk) * jnp.exp(ce3 - cm)).reshape(ta, width).astype(BF16)
        r_ref[d] = (chunks(r) * jnp.exp(ci3 - cm)).reshape(ta, width).astype(BF16)
        b_ref[d] = (chunks(bb) * e_k).reshape(ta, width).astype(BF16)
        k_ref[d] = (chunks(kd) * e_k).reshape(ta, width).astype(BF16)
        em_ref[d, 0] = jnp.exp(cm).reshape(n_chunks, width)
    emit_dense(len(pieces))
    bon_ref[...] = _head_segsum(r * kd_sum * r_k) * v


def _inproj(xs, mod, g, w_in, mu, par, wup, aup, gup, *, c_rw, c_gate, batch, seq, ctx_len):
    n, d = xs.shape
    cols = w_in.shape[1]
    width = RW_HEADS * RW_HEAD_DIM
    ta = RW_TILE
    c_len = RW_CHUNK
    n_tiles = n // ta
    tiles_x = seq // ta
    sub = ta // 8
    last8 = n // 8 - 1
    seq_of = lambda i: jnp.minimum(i // tiles_x, batch)
    idx = np.arange(ta)
    same_chunk = (idx[:, None] // c_len) == (idx[None, :] // c_len)
    tri = jnp.asarray(np.stack([same_chunk & (idx[None, :] <= idx[:, None]),
                                same_chunk & (idx[None, :] >= idx[:, None])]), BF16)
    tok2 = pl.BlockSpec((2, ta, width), lambda i: (0, i, 0))
    tok = lambda c: pl.BlockSpec((ta, c), lambda i: (i, 0))
    sds = jax.ShapeDtypeStruct
    return pl.pallas_call(
        functools.partial(_inproj_kernel, c_rw=c_rw, c_gate=c_gate, n_x_tiles=batch * tiles_x, tiles_x=tiles_x,
                          tiles_ctx=ctx_len // ta),
        grid=(n_tiles,),
        in_specs=[tok(d),
                  pl.BlockSpec((8, d), lambda i: (jnp.maximum(i * sub - 1, 0), 0)),
                  pl.BlockSpec((8, d), lambda i: (jnp.minimum((i + 1) * sub, last8), 0)),
                  pl.BlockSpec((1, N_MOD, d), lambda i: (seq_of(i), 0, 0)),
                  _resident((1, d)), _hbm_spec(),
                  _resident(mu.shape), _resident(par.shape), _resident(wup.shape), _resident(aup.shape),
                  _resident(gup.shape), _resident(tri.shape)],
        out_specs=[tok(c_rw), tok(cols - c_gate), tok2, tok2, tok2, tok2,
                   pl.BlockSpec((2, 1, ta // c_len, width), lambda i: (0, i, 0, 0)),
                   tok(width), tok(width), tok(width)],
        out_shape=[sds((n, c_rw), BF16), sds((n, cols - c_gate), BF16)]
                  + [sds((2, n, width), BF16)] * 4
                  + [sds((2, n_tiles, ta // c_len, width), F32), sds((n, width), BF16),
                     sds((n, width), F32), sds((n, width), F32)],
        scratch_shapes=[pltpu.VMEM((ta + 16, c_gate - c_rw), F32), pltpu.VMEM(w_in.shape, BF16),
                        _stage_shape(w_in), pltpu.SemaphoreType.DMA((2,))],
        compiler_params=_cparams(1),
    )(xs, xs, xs, mod, g, w_in, mu, par, wup, aup, gup, tri)


NA_KEY_ROWS = 2 * NA_ROWS
NA_PAIRS = NA_KEY_ROWS // 2


def _na_pair_bias(rpb):
    w = GRID_W
    qc = np.arange(w)[:, None]
    kc = np.arange(w)[None, :]
    s_c = np.clip(qc - NA_WIN_W // 2, 0, w - NA_WIN_W)
    col_ok = (kc >= s_c) & (kc < s_c + NA_WIN_W)
    dc = np.clip(kc - qc + NA_WIN_W - 1, 0, 2 * NA_WIN_W - 2)
    onehot = jnp.asarray(dc[None] == np.arange(2 * NA_WIN_W - 1)[:, None, None], F32)
    b = jnp.einsum('hrd,dqk->hrqk', rpb.astype(F32), onehot, precision=lax.Precision.HIGHEST)
    b = jnp.where(col_ok[None, None], b, -jnp.inf)
    ninf = jnp.full((rpb.shape[0], 1, w, w), -jnp.inf, F32)
    b = jnp.concatenate([ninf, b, ninf], axis=1)
    return jnp.concatenate([b[:, :-1], b[:, 1:]], axis=-1)


def _na_kernel(q_ref, k0_ref, k1_ref, k2_ref, k3_ref, kc_ref, v0_ref, v1_ref, v2_ref, v3_ref, vc_ref,
               bias_ref, o_ref, kcat, vcat, *, n_blocks):
    j = pl.program_id(2)
    blk = k0_ref.shape[0]
    for s, (kr, vr) in enumerate(((k0_ref, v0_ref), (k1_ref, v1_ref), (k2_ref, v2_ref), (k3_ref, v3_ref),
                                  (kc_ref, vc_ref))):
        kcat[s * blk:s * blk + kr.shape[0], :] = kr[...]
        vcat[s * blk:s * blk + vr.shape[0], :] = vr[...]
    n_loc = 4 * blk
    w = GRID_W
    half = NA_WIN_H // 2
    sub_rows = NA_ROWS // 2
    sub_pairs = (sub_rows + NA_WIN_H) // 2
    tq = sub_rows * w
    lane = lax.broadcasted_iota(jnp.int32, (1, LANES), 1)
    first = lane < NA_HEAD_DIM
    nt = (((1,), (1,)), ((), ()))

    items = [(pp, h, sb) for pp in range(q_ref.shape[1] // LANES) for sb in range(2) for h in range(2)]

    def scores(item):
        pp, h, sb = item
        cols = slice(pp * LANES, (pp + 1) * LANES)
        q = q_ref[sb * tq:(sb + 1) * tq, cols] * (NA_HEAD_DIM ** -0.5)
        qh = jnp.where(first if h == 0 else jnp.logical_not(first), q, jnp.zeros_like(q))
        k0 = 2 * sb * LANES
        s_loc = lax.dot_general(qh, kcat[k0:k0 + sub_pairs * LANES, cols], nt, preferred_element_type=F32)
        s_ctx = lax.dot_general(qh, kcat[n_loc:, cols], nt, preferred_element_type=F32)
        return s_loc, s_ctx

    def attend(item, s_loc, s_ctx):
        pp, h, sb = item
        cols = slice(pp * LANES, (pp + 1) * LANES)
        p_rows = []
        l_rows = []
        for ql in range(sub_rows):
            qi = sb * sub_rows + ql
            lo = jnp.where(j == 0, max(qi, half), jnp.where(j == n_blocks - 1, min(qi, half), qi))
            m_lo = min(qi, half) // 2
            m_hi = (max(qi, half) + NA_WIN_H - 1) // 2
            n_m = m_hi - m_lo + 1
            key_row = 2 * m_lo + lax.broadcasted_iota(jnp.int32, (1, n_m * LANES), 1) // w
            seen = (key_row >= lo) & (key_row < lo + NA_WIN_H)
            bias = jnp.concatenate(
                [bias_ref[2 * pp + h, min(max(2 * m - qi + half, 0), 2 * NA_WIN_H - 1)]
                 for m in range(m_lo, m_hi + 1)], axis=1)
            rows = slice(ql * w, (ql + 1) * w)
            t_lo, t_hi = m_lo - 2 * sb, m_hi - 2 * sb
            s_q = jnp.where(seen, s_loc[rows, t_lo * LANES:(t_hi + 1) * LANES] + bias, -jnp.inf)
            s_c = s_ctx[rows]
            mx = jnp.maximum(jnp.max(s_q, axis=-1, keepdims=True), jnp.max(s_c, axis=-1, keepdims=True))
            p_q = jnp.exp(s_q - mx)
            p_c = jnp.exp(s_c - mx)
            l_rows.append(jnp.sum(p_q, axis=-1, keepdims=True) + jnp.sum(p_c, axis=-1, keepdims=True))
            pieces = []
            if t_lo > 0:
                pieces.append(jnp.zeros((w, t_lo * LANES), BF16))
            pieces.append(p_q.astype(BF16))
            if t_hi < sub_pairs - 1:
                pieces.append(jnp.zeros((w, (sub_pairs - 1 - t_hi) * LANES), BF16))
            pieces.append(p_c.astype(BF16))
            p_rows.append(jnp.concatenate(pieces, axis=1))
        p = jnp.concatenate(p_rows, axis=0)
        l = jnp.concatenate(l_rows, axis=0)
        k0 = 2 * sb * LANES
        o = (_dot(p[:, :sub_pairs * LANES], vcat[k0:k0 + sub_pairs * LANES, cols])
             + _dot(p[:, sub_pairs * LANES:], vcat[n_loc:, cols]))
        return o / l

    pending = scores(items[0])
    outs = {}
    for n, item in enumerate(items):
        current = pending
        if n + 1 < len(items):
            pending = scores(items[n + 1])
        outs[item] = attend(item, *current)
        pp, h, sb = item
        if h == 1:
            o_ref[sb * tq:(sb + 1) * tq, pp * LANES:(pp + 1) * LANES] = jnp.where(
                first, outs[(pp, 0, sb)], outs[(pp, 1, sb)]).astype(o_ref.dtype)


def _na_attention(qkv, bias, *, batch, seq, ctx_len):
    rows = seq // GRID_W
    nj = rows // NA_ROWS
    tq = NA_ROWS * GRID_W
    tk = 4 * GRID_W
    width = NA_HEADS * NA_HEAD_DIM
    n_pairs = width // LANES
    kblocks = seq // tk
    ctx_blk0 = (batch * seq) // ctx_len

    lanes = 2 * LANES
    n_groups = width // lanes

    def kv_spec(slot, col0):
        def imap(hg, b, j):
            return (b * kblocks + jnp.clip(2 * j - 1 + slot, 0, kblocks - 1), col0 + hg)
        return pl.BlockSpec((tk, lanes), imap)

    def ctx_spec(col0):
        return pl.BlockSpec((ctx_len, lanes), lambda hg, b, j: (ctx_blk0 + b, col0 + hg))

    in_specs = ([pl.BlockSpec((tq, lanes), lambda hg, b, j: (b * nj + j, hg))]
                + [kv_spec(s, n_groups) for s in range(4)] + [ctx_spec(n_groups)]
                + [kv_spec(s, 2 * n_groups) for s in range(4)] + [ctx_spec(2 * n_groups)]
                + [pl.BlockSpec((4,) + bias.shape[1:], lambda hg, b, j: (hg, 0, 0, 0))])
    return pl.pallas_call(
        functools.partial(_na_kernel, n_blocks=nj),
        grid=(n_groups, batch, nj),
        in_specs=in_specs,
        out_specs=pl.BlockSpec((tq, lanes), lambda hg, b, j: (b * nj + j, hg)),
        out_shape=jax.ShapeDtypeStruct((batch * seq, width), BF16),
        scratch_shapes=[pltpu.VMEM((4 * tk + ctx_len, lanes), BF16),
                        pltpu.VMEM((4 * tk + ctx_len, lanes), BF16)],
        compiler_params=_cparams(3),
    )(*([qkv] * 11), bias)


def _rwkv_kernel(kkf_ref, rf_ref, bf_ref, kf_ref, emf_ref, vf_ref, kkb_ref, rb_ref, bb_ref, kb_ref, emb_ref, vb_ref,
                 strict_ref, incl_ref, yf_ref, yb_ref, z_s):
    n = pl.program_id(1)
    ta = vf_ref.shape[0]
    width = vf_ref.shape[1]
    n_pairs = width // LANES
    c_len = RW_CHUNK
    n_chunks = ta // c_len
    n_d = n_chunks * n_pairs
    n_b = 2 * n_d

    @pl.when(n == 0)
    def _():
        z_s[...] = jnp.zeros_like(z_s)

    def chunks(t):
        return t.reshape(n_chunks, c_len, width)

    def to_batch(t):
        parts = [t[:, :, j * LANES:(j + 1) * LANES] for j in range(n_pairs)]
        return jnp.stack(parts, axis=1).reshape(n_d, t.shape[1], LANES)

    def load_direction(kk_ref, r_ref, b_ref, k_ref, em_ref, v_ref):
        em = em_ref[0, 0].reshape(n_chunks, 1, width)
        kk_t = chunks(kk_ref[0].astype(F32))
        r_t = chunks(r_ref[0].astype(F32))
        b_t = chunks(b_ref[0].astype(F32))
        k_t = chunks(k_ref[0].astype(F32))
        return (to_batch(kk_t * em),
                to_batch(r_t * em),
                to_batch(k_t * em),
                to_batch(b_t * em),
                to_batch(em * em),
                to_batch(kk_t), to_batch(r_t), to_batch(b_t), to_batch(k_t),
                to_batch(chunks(v_ref[...].astype(F32))))

    fwd = load_direction(kkf_ref, rf_ref, bf_ref, kf_ref, emf_ref, vf_ref)
    bwd = load_direction(kkb_ref, rb_ref, bb_ref, kb_ref, emb_ref, vb_ref)
    kk_abs, r_abs, k_bar, b_bar, gam, kk_t, r_t, b_t, k_t, v_b = [
        jnp.concatenate([f, b], axis=0) for f, b in zip(fwd, bwd)]

    lane = lax.broadcasted_iota(jnp.int32, (1, 1, LANES), 2)
    first = lane < RW_HEAD_DIM
    ri = lax.broadcasted_iota(jnp.int32, (c_len, LANES), 0)
    cj = lax.broadcasted_iota(jnp.int32, (c_len, LANES), 1) % RW_HEAD_DIM
    eye = ri == cj

    def causal(mask_ref, t):
        return jnp.concatenate([jnp.where(mask_ref[0] > 0.0, t[:n_d], 0.0),
                                jnp.where(mask_ref[1] > 0.0, t[n_d:], 0.0)], axis=0)

    def stack_heads(t):
        zero = jnp.zeros_like(t)
        return jnp.concatenate([jnp.where(first, t, zero), jnp.where(first, zero, t)], axis=1)

    def bmm(x, y):
        return lax.dot_general(x.astype(BF16), y.astype(BF16), (((2,), (1,)), ((0,), (0,))),
                               preferred_element_type=F32)

    def bmm_nt(x, y):
        return lax.dot_general(x.astype(BF16), y.astype(BF16), (((2,), (2,)), ((0,), (0,))),
                               preferred_element_type=F32)

    def bmm_tn(x, y):
        return lax.dot_general(x.astype(BF16), y.astype(BF16), (((1,), (1,)), ((0,), (0,))),
                               preferred_element_type=F32)

    gm = bmm_nt(jnp.concatenate([kk_t, r_t], axis=1),
                jnp.concatenate([stack_heads(b_t), stack_heads(k_t)], axis=1))
    ab_w = causal(strict_ref, gm[:, :c_len, :LANES])
    ak_w = causal(strict_ref, gm[:, :c_len, LANES:])
    db_w = causal(incl_ref, gm[:, c_len:, :LANES])
    dk_w = causal(incl_ref, gm[:, c_len:, LANES:])

    base = RW_INV_BASE
    same = lambda s: (ri // s) == (cj // s)
    l_base = jnp.where(same(base), ab_w, 0.0)
    t_w = jnp.where(eye, 1.0, 0.0) - l_base
    pw = bmm(l_base, stack_heads(l_base))
    span = 2
    while span < base:
        t_w = t_w + bmm(t_w, stack_heads(pw))
        span *= 2
        if span < base:
            pw = bmm(pw, stack_heads(pw))
    size = base
    while size < c_len:
        off = jnp.where(jnp.logical_and(same(2 * size), jnp.logical_not(same(size))), ab_w, 0.0)
        t_w = t_w - bmm(bmm(t_w, stack_heads(off)), stack_heads(t_w))
        size *= 2

    def head_diag(t):
        return jnp.where(first, t[:, :RW_HEAD_DIM], t[:, RW_HEAD_DIM:])

    akv = bmm(jnp.concatenate([ak_w, dk_w], axis=1), stack_heads(v_b))
    pq = bmm(t_w, jnp.concatenate([stack_heads(kk_abs), stack_heads(akv[:, :c_len])], axis=2))
    dpq = bmm(db_w, jnp.concatenate([stack_heads(pq[:, :, :LANES]), stack_heads(pq[:, :, LANES:])], axis=2))
    r_hat = r_abs - dpq[:, :, :LANES]
    y_loc = akv[:, c_len:] - dpq[:, :, LANES:]
    bpq = bmm_tn(b_bar, pq)
    m_w = jnp.where(eye, jnp.broadcast_to(gam, (n_b, RW_HEAD_DIM, LANES)), 0.0) - head_diag(bpq[:, :, :LANES])
    g_w = head_diag(bmm_tn(k_bar, v_b) - bpq[:, :, LANES:])

    zs = z_s[...]
    for step in range(n_chunks):
        c_f, c_b = step, n_chunks - 1 - step
        pick = lambda t: jnp.concatenate([t[c_f * n_pairs:(c_f + 1) * n_pairs],
                                          t[n_d + c_b * n_pairs:n_d + (c_b + 1) * n_pairs]], axis=0)
        both = bmm(jnp.concatenate([pick(r_hat), pick(m_w)], axis=1), stack_heads(zs))
        y_c = both[:, :c_len] + pick(y_loc)
        yf_ref[0, c_f * c_len:(c_f + 1) * c_len, :] = jnp.concatenate([y_c[j] for j in range(n_pairs)], axis=1)
        yb_ref[0, c_b * c_len:(c_b + 1) * c_len, :] = jnp.concatenate(
            [y_c[n_pairs + j] for j in range(n_pairs)], axis=1)
        zs = both[:, c_len:] + pick(g_w)
    z_s[...] = zs


def _rwkv_scan(kk_t, r_t, b_t, k_t, em, v, *, batch, seq, ctx_len):
    width = RW_HEADS * RW_HEAD_DIM
    ta = RW_TILE
    nct = ctx_len // ta
    nxt = seq // ta
    n_x = batch * seq
    c_len = RW_CHUNK

    def blk(reverse):
        def index(b, n):
            t_ctx = (nct - 1 - n) if reverse else n
            t_x = (nxt - 1 - (n - nct)) if reverse else (n - nct)
            return jnp.where(n < nct, batch * nxt + b * nct + t_ctx, b * nxt + t_x)
        return index

    def xblk(reverse):
        def index(b, n):
            m = jnp.maximum(n, nct) - nct
            return b * nxt + ((nxt - 1 - m) if reverse else m)
        return index

    t_i = np.arange(c_len)[:, None]
    s_i = np.arange(LANES)[None, :] % RW_HEAD_DIM
    strict = jnp.asarray(np.stack([s_i < t_i, s_i > t_i]), F32)
    incl = jnp.asarray(np.stack([s_i <= t_i, s_i >= t_i]), F32)

    def operands(d):
        index = blk(d == 1)
        tile = pl.BlockSpec((1, ta, width), lambda b, n: (d, index(b, n), 0))
        return [tile, tile, tile, tile,
                pl.BlockSpec((1, 1, ta // c_len, width), lambda b, n: (d, index(b, n), 0, 0)),
                pl.BlockSpec((ta, width), lambda b, n: (index(b, n), 0))]

    def out_spec(reverse):
        index = xblk(reverse)
        return pl.BlockSpec((1, ta, width), lambda b, n: (0, index(b, n), 0))

    y_shape = jax.ShapeDtypeStruct((1, n_x, width), F32)
    return pl.pallas_call(
        _rwkv_kernel,
        grid=(batch, nct + nxt),
        in_specs=operands(0) + operands(1) + [_resident(strict.shape), _resident(incl.shape)],
        out_specs=[out_spec(False), out_spec(True)],
        out_shape=[y_shape, y_shape],
        scratch_shapes=[pltpu.VMEM((2 * (width // LANES), RW_HEAD_DIM, LANES), F32)],
        compiler_params=_cparams(2),
    )(kk_t, r_t, b_t, k_t, em, v, kk_t, r_t, b_t, k_t, em, v, strict, incl)


def _merge_kernel(x_ref, mod_ref, ona_ref, yf_ref, yb_ref, bon_ref, g_ref, pg_ref, bg_ref, ln_ref,
                  pna_ref, prw_ref, wout_ref, gn_ref, wg_hbm, wu_hbm, wd_hbm, gf_ref, o_ref,
                  wg_s, wu_s, wd_s, stage_in, stage_out, sem, *, f_chunk):
    @pl.when(pl.program_id(0) == 0)
    def _():
        _ffn_load_weights(wg_hbm, wu_hbm, wd_hbm, wg_s, wu_s, wd_s, stage_in, stage_out, sem)

    d = x_ref.shape[1]
    y = yf_ref[0] + yb_ref[0]
    inv_n = 1.0 / RW_HEAD_DIM
    mean = _head_segsum(y) * inv_n
    yc = y - mean
    var = _head_segsum(yc * yc) * inv_n
    yn = yc * lax.rsqrt(var + RW_GN_EPS) * ln_ref[0:1, :] + ln_ref[1:2, :]
    o_rw = ((yn + bon_ref[...]) * g_ref[...]).astype(BF16)
    gates = jax.nn.sigmoid(pg_ref[...] + bg_ref[...])
    m = gates[:, :d] * _dot(ona_ref[...], pna_ref[...]) + gates[:, d:] * _dot(o_rw, prw_ref[...])
    x2 = x_ref[...] + mod_ref[0, 5:6, :] * _dot(m.astype(BF16), wout_ref[...])
    o_ref[...] = _swiglu_half_step(x2, mod_ref, gn_ref, wg_s, wu_s, wd_s, gf_ref,
                                   mod0=6, f_chunk=f_chunk, final=True)


def _merge_ffn(x1, mod, o_na, y_f, y_b, bon, g, p_gate, b_gate, ln, p_na, p_rw, w_out, gn, wg, wu, wd, gf, *,
               tiles_per_seq):
    n_x, d = o_na.shape[0], x1.shape[1]
    width = o_na.shape[1]
    tok = lambda i: (i, 0)
    tok3 = lambda i: (0, i, 0)
    return pl.pallas_call(
        functools.partial(_merge_kernel, f_chunk=FFN_CHUNK),
        grid=(n_x // TM,),
        in_specs=[pl.BlockSpec((TM, d), tok),
                  pl.BlockSpec((1, N_MOD, d), lambda i: (i // tiles_per_seq, 0, 0)),
                  pl.BlockSpec((TM, width), tok),
                  pl.BlockSpec((1, TM, width), tok3), pl.BlockSpec((1, TM, width), tok3),
                  pl.BlockSpec((TM, width), tok), pl.BlockSpec((TM, width), tok),
                  pl.BlockSpec((TM, 2 * d), tok),
                  _resident((1, 2 * d)), _resident((2, width)),
                  _resident(p_na.shape), _resident(p_rw.shape), _resident(w_out.shape),
                  _resident(gn.shape), _hbm_spec(), _hbm_spec(), _hbm_spec(), _resident(gf.shape)],
        out_specs=pl.BlockSpec((TM, d), tok),
        out_shape=jax.ShapeDtypeStruct((n_x, d), F32),
        scratch_shapes=_ffn_scratch(wg, wu, wd),
        compiler_params=_cparams(1),
    )(x1, mod, o_na, y_f, y_b, bon, g, p_gate, b_gate, ln, p_na, p_rw, w_out, gn, wg, wu, wd, gf)


def _pad_lora(w_up, direction):
    z = jnp.zeros_like(w_up[0])
    return jnp.concatenate([w_up[0] if direction == 0 else z, w_up[1] if direction == 1 else z], axis=0)


def kernel(x, c, ctx, c_ctx, w_ada, b_ada, norm_ffn1, norm_mix, norm_ffn2, norm_final, ffn1_wg, ffn1_wu, ffn1_wd, ffn2_wg, ffn2_wu, ffn2_wd, w_in, b_gate, na_rpb, rw_mu, rw_w0, rw_w_up, rw_a0, rw_a_up, rw_g_up, rw_k_k, rw_k_a, rw_r_k, rw_ln_w, rw_ln_b, p_na, p_rw, w_out):
    batch, seq, d = x.shape
    ctx_len = ctx.shape[1]
    n_x = batch * seq
    na_width = NA_HEADS * NA_HEAD_DIM
    rw_width = RW_HEADS * RW_HEAD_DIM
    c_rw = 3 * na_width
    c_gate = c_rw + 3 * rw_width + 4 * RW_LORA + RW_GATE_LORA
    assert w_ada.shape[0] == 1, "single layer"
    assert seq % TM == 0 and (batch * ctx_len) % TM == 0 and ctx_len % RW_TILE == 0 and batch + 1 <= 8
    assert seq % (NA_ROWS * GRID_W) == 0 and seq // (NA_ROWS * GRID_W) >= 2 and ctx_len == 4 * GRID_W
    row = lambda t: t.reshape(1, -1)

    cs = jnp.concatenate([c, c_ctx[None], jnp.zeros((8 - batch - 1, d), F32)], axis=0)
    mod = _ada_mod(cs, w_ada[0], b_ada[0])[:batch + 1].reshape(batch + 1, N_MOD, d)

    bf = lambda t: t.astype(BF16)

    x1 = _ffn(x.reshape(n_x, d), ctx.reshape(batch * ctx_len, d), mod, row(norm_ffn1[0]),
              ffn1_wg[0], ffn1_wu[0], ffn1_wd[0], row(norm_final),
              mod0=0, tiles_per_seq=seq // TM, n_seq=batch)
    par = jnp.concatenate([rw_w0[0], rw_a0[0], row(rw_k_k[0]), row(rw_k_a[0]), row(rw_r_k[0]),
                           jnp.zeros((1, rw_width), F32)], axis=0)
    wup = bf(jnp.stack([_pad_lora(rw_w_up[0], 0), _pad_lora(rw_w_up[0], 1)]))
    aup = bf(jnp.stack([_pad_lora(rw_a_up[0], 0), _pad_lora(rw_a_up[0], 1)]))
    qkv, p_gate, kk_t, r_t, b_t, k_t, em, v_rw, g, bon = _inproj(
        x1, mod, row(norm_mix[0]), w_in[0], rw_mu[0], par, wup, aup, bf(rw_g_up[0]),
        c_rw=c_rw, c_gate=c_gate, batch=batch, seq=seq, ctx_len=ctx_len)

    bias = _na_pair_bias(na_rpb[0])
    o_na = _na_attention(qkv, bias, batch=batch, seq=seq, ctx_len=ctx_len)

    y_f, y_b = _rwkv_scan(kk_t, r_t, b_t, k_t, em, v_rw, batch=batch, seq=seq, ctx_len=ctx_len)

    ln = jnp.stack([rw_ln_w[0], rw_ln_b[0]], axis=0)
    out = _merge_ffn(x1, mod, o_na, y_f, y_b, bon, g, p_gate, row(b_gate[0]), ln,
                     bf(p_na[0]), bf(p_rw[0]), bf(w_out[0]), row(norm_ffn2[0]),
                     ffn2_wg[0], ffn2_wu[0], ffn2_wd[0], row(norm_final),
                     tiles_per_seq=seq // TM)
    return out.reshape(batch, seq, d)
```

```python
import functools
import math

import numpy as np
import jax
import jax.numpy as jnp
from jax import lax
from jax.experimental import pallas as pl
from jax.experimental.pallas import tpu as pltpu

F32 = jnp.float32
BF16 = jnp.bfloat16

NORM_EPS = 1e-6
RW_GN_EPS = 64e-5
N_MOD = 9
GRID_W = 64
NA_HEADS = 8
NA_HEAD_DIM = 64
NA_WIN_H = 8
NA_WIN_W = 16
RW_HEADS = 8
RW_HEAD_DIM = 64
RW_LORA = 64
RW_GATE_LORA = 128

LANES = 128
VMEM_LIMIT = 56 * 1024 * 1024

TM = 512
FFN_CHUNK = 256
TM_IN = 256
NA_ROWS = 8
RW_TILE = 256
RW_CHUNK = 64
RW_INV_BASE = 8
assert RW_CHUNK == RW_HEAD_DIM


def _cparams(n_axes):
    return pltpu.CompilerParams(dimension_semantics=("arbitrary",) * n_axes,
                                vmem_limit_bytes=VMEM_LIMIT)


def _resident(shape):
    nd = len(shape)
    return pl.BlockSpec(shape, lambda *_: (0,) * nd, pipeline_mode=pl.Buffered(1))


def _rmsnorm(x, g):
    return x * lax.rsqrt(jnp.mean(x * x, axis=-1, keepdims=True) + NORM_EPS) * g


def _split2(x):
    hi = x.astype(BF16)
    lo = (x - hi.astype(F32)).astype(BF16)
    return hi, lo


def _dot(a, b):
    return jnp.dot(a, b, preferred_element_type=F32)


def _dot_split_lhs(x, w_bf16, parts):
    acc = None
    rem = x
    for _ in range(parts):
        p = rem.astype(BF16)
        rem = rem - p.astype(F32)
        t = _dot(p, w_bf16)
        acc = t if acc is None else acc + t
    return acc


def _dot_split_rhs(w_bf16, x, parts):
    acc = None
    rem = x
    for _ in range(parts):
        p = rem.astype(BF16)
        rem = rem - p.astype(F32)
        t = _dot(w_bf16, p)
        acc = t if acc is None else acc + t
    return acc


def _dot3(a, b):
    ah, al = _split2(a)
    bh, bl = _split2(b)
    return _dot(ah, bh) + (_dot(ah, bl) + _dot(al, bh))


WEIGHT_CHUNKS = 16


def _hbm_spec():
    return pl.BlockSpec(memory_space=pl.ANY)


WEIGHT_SLOTS = 4


def _stage_shape(w):
    return pltpu.VMEM((WEIGHT_SLOTS, w.shape[0] // WEIGHT_CHUNKS, w.shape[1]), F32)


def _stage_sems():
    return pltpu.SemaphoreType.DMA((WEIGHT_SLOTS,))


def _fetch_as_bf16(src_hbm, dst, stage, sem):
    rows = stage.shape[1]

    def chunk_copy(c):
        slot = c % WEIGHT_SLOTS
        return pltpu.make_async_copy(src_hbm.at[c * rows:(c + 1) * rows, :], stage.at[slot], sem.at[slot])

    for c in range(WEIGHT_SLOTS):
        chunk_copy(c).start()
    for c in range(WEIGHT_CHUNKS):
        chunk_copy(c).wait()
        dst[c * rows:(c + 1) * rows, :] = stage[c % WEIGHT_SLOTS].astype(dst.dtype)
        if c + WEIGHT_SLOTS < WEIGHT_CHUNKS:
            chunk_copy(c + WEIGHT_SLOTS).start()


def _ada_kernel(c_ref, w_ref, b_ref, o_ref):
    c = c_ref[...]
    s = c * jax.nn.sigmoid(c)
    o_ref[...] = _dot3(s, w_ref[...]) + b_ref[...]


def _ada_mod(cs, w_ada, b_ada):
    d = cs.shape[1]
    nm = w_ada.shape[1] // d
    return pl.pallas_call(
        _ada_kernel,
        grid=(nm,),
        in_specs=[pl.BlockSpec((8, d), lambda j: (0, 0)),
                  pl.BlockSpec((d, d), lambda j: (0, j)),
                  pl.BlockSpec((1, d), lambda j: (0, j))],
        out_specs=pl.BlockSpec((8, d), lambda j: (0, j)),
        out_shape=jax.ShapeDtypeStruct((8, nm * d), F32),
        compiler_params=_cparams(1),
    )(cs, w_ada, b_ada.reshape(1, -1))


def _ffn_scratch(wg, wu, wd):
    return [pltpu.VMEM(wg.shape, BF16), pltpu.VMEM(wu.shape, BF16), pltpu.VMEM(wd.shape, BF16),
            _stage_shape(wg), _stage_shape(wd), _stage_sems()]


def _ffn_load_weights(wg_hbm, wu_hbm, wd_hbm, wg_s, wu_s, wd_s, stage_in, stage_out, sem):
    _fetch_as_bf16(wg_hbm, wg_s, stage_in, sem)
    _fetch_as_bf16(wu_hbm, wu_s, stage_in, sem)
    _fetch_as_bf16(wd_hbm, wd_s, stage_out, sem)


def _ffn_kernel(x_ref, tail_ref, mod_ref, g_ref, wg_hbm, wu_hbm, wd_hbm, gf_ref, o_ref,
                wg_s, wu_s, wd_s, stage_in, stage_out, sem, *, mod0, f_chunk, n_head_tiles):
    @pl.when(pl.program_id(0) == 0)
    def _():
        _ffn_load_weights(wg_hbm, wu_hbm, wd_hbm, wg_s, wu_s, wd_s, stage_in, stage_out, sem)

    x = jnp.where(pl.program_id(0) < n_head_tiles, x_ref[...], tail_ref[...])
    o_ref[...] = _swiglu_half_step(x, mod_ref, g_ref, wg_s, wu_s, wd_s, gf_ref,
                                   mod0=mod0, f_chunk=f_chunk, final=False)


def _swiglu_half_step(x, mod_ref, g_ref, wg_ref, wu_ref, wd_ref, gf_ref, *, mod0, f_chunk, final):
    shift = mod_ref[0, mod0:mod0 + 1, :]
    scale = mod_ref[0, mod0 + 1:mod0 + 2, :]
    gate = mod_ref[0, mod0 + 2:mod0 + 3, :]
    h = (_rmsnorm(x, g_ref[...]) * (1.0 + scale) + shift).astype(BF16)
    d_ff = wg_ref.shape[1]
    bounds = [(f0, min(f0 + f_chunk, d_ff)) for f0 in range(0, d_ff, f_chunk)]

    def gate_up(lo, hi):
        return _dot(h, wg_ref[:, lo:hi]), _dot(h, wu_ref[:, lo:hi])

    acc = None
    pending = gate_up(*bounds[0])
    for c, (lo, hi) in enumerate(bounds):
        gg, uu = pending
        if c + 1 < len(bounds):
            pending = gate_up(*bounds[c + 1])
        a = (gg * jax.nn.sigmoid(gg) * uu).astype(BF16)
        t = _dot(a, wd_ref[lo:hi, :])
        acc = t if acc is None else acc + t
    y = x + 0.5 * gate * acc
    if final:
        y = _rmsnorm(y, gf_ref[...])
    return y


def _ffn(xs, tail, mod, g, wg, wu, wd, gf, *, mod0, tiles_per_seq, n_seq):
    n_head, d = xs.shape
    n = n_head + tail.shape[0]
    seq_of = lambda i: jnp.minimum(i // tiles_per_seq, n_seq)
    head_tiles = n_head // TM
    return pl.pallas_call(
        functools.partial(_ffn_kernel, mod0=mod0, f_chunk=FFN_CHUNK, n_head_tiles=head_tiles),
        grid=(n // TM,),
        in_specs=[pl.BlockSpec((TM, d), lambda i: (jnp.minimum(i, head_tiles - 1), 0)),
                  pl.BlockSpec((TM, d), lambda i: (jnp.maximum(i - head_tiles, 0), 0)),
                  pl.BlockSpec((1, N_MOD, d), lambda i: (seq_of(i), 0, 0)),
                  _resident((1, d)), _hbm_spec(), _hbm_spec(), _hbm_spec(), _resident((1, d))],
        out_specs=pl.BlockSpec((TM, d), lambda i: (i, 0)),
        out_shape=jax.ShapeDtypeStruct((n, d), F32),
        scratch_shapes=_ffn_scratch(wg, wu, wd),
        compiler_params=_cparams(1),
    )(xs, tail, mod, g, wg, wu, wd, gf)


def _head_segsum(x):
    r = lax.broadcasted_iota(jnp.int32, (LANES, LANES), 0) // RW_HEAD_DIM
    c = lax.broadcasted_iota(jnp.int32, (LANES, LANES), 1) // RW_HEAD_DIM
    e = jnp.where(r == c, 1.0, 0.0).astype(BF16)
    xb = x.astype(BF16)
    return jnp.concatenate([_dot(xb[:, j:j + LANES], e) for j in range(0, x.shape[1], LANES)], axis=1)


def _inproj_kernel(x_ref, xprev_ref, xnext_ref, mod_ref, g_ref, w_hbm, mu_ref, par_ref, wup_ref, aup_ref,
                   gup_ref, tri_ref, qkv_ref, gate_ref, kk_ref, r_ref, b_ref, k_ref, em_ref, v_ref, g_out_ref,
                   bon_ref, p_s, w_ref, stage, sem, *, c_rw, c_gate, n_x_tiles, tiles_x, tiles_ctx):
    i = pl.program_id(0)

    @pl.when(i == 0)
    def _():
        _fetch_as_bf16(w_hbm, w_ref, stage, sem)

    ta = x_ref.shape[0]
    width = v_ref.shape[1]
    c_len = RW_CHUNK
    n_chunks = ta // c_len
    in_x = i < n_x_tiles
    tile = jnp.where(in_x, i % tiles_x, (i - n_x_tiles) % tiles_ctx)
    last_tile = jnp.where(in_x, tiles_x - 1, tiles_ctx - 1)

    x_all = jnp.concatenate([xprev_ref[...], x_ref[...], xnext_ref[...]], axis=0)
    h_all = (_rmsnorm(x_all, g_ref[...]) * (1.0 + mod_ref[0, 4:5, :]) + mod_ref[0, 3:4, :]).astype(BF16)
    h = h_all[8:ta + 8]
    zero = jnp.zeros((8, h.shape[1]), BF16)
    h_all = jnp.concatenate([jnp.where(tile == 0, zero, h_all[:8]), h,
                             jnp.where(tile == last_tile, zero, h_all[ta + 8:])], axis=0)
    p_s[...] = _dot(h_all, w_ref[:, c_rw:c_gate])

    piece = 4 * LANES
    pieces = ([(qkv_ref, c0, c0) for c0 in range(0, c_rw, piece)]
              + [(gate_ref, c0, c_gate + c0) for c0 in range(0, w_ref.shape[1] - c_gate, piece)])

    def emit_dense(count):
        for _ in range(min(count, len(pieces))):
            ref, dst, src = pieces.pop(0)
            ref[:, dst:dst + piece] = _dot(h, w_ref[:, src:src + piece]).astype(ref.dtype)

    p = p_s[8:ta + 8, :]
    f = p + mu_ref[0:1, :] * (p_s[7:ta + 7, :] - p) + mu_ref[1:2, :] * (p_s[9:ta + 9, :] - p)
    emit_dense(2)

    r = f[:, 0:width]
    k = f[:, width:2 * width]
    v = f[:, 2 * width:3 * width]
    o = 3 * width
    th = jnp.tanh(f[:, o:o + LANES]).astype(BF16)
    al = f[:, o + LANES:o + 2 * LANES].astype(BF16)
    gl = f[:, o + 2 * LANES:o + 3 * LANES]
    k_k = par_ref[4:5, :]
    k_a = par_ref[5:6, :]
    r_k = par_ref[6:7, :]

    v_ref[...] = v.astype(BF16)
    g_out_ref[...] = _dot(jax.nn.sigmoid(gl).astype(BF16), gup_ref[...])
    kk = k * k_k
    kk = kk * lax.rsqrt(jnp.maximum(_head_segsum(kk * kk), 1e-12))
    emit_dense(1)

    def chunks(t):
        return t.reshape(n_chunks, c_len, width)

    kd_sum = None
    for d in range(2):
        z = par_ref[d:d + 1, :] + _dot(th, wup_ref[d])
        lw = -math.exp(-0.5) * jax.nn.sigmoid(z)
        a = jax.nn.sigmoid(par_ref[2 + d:3 + d, :] + _dot(al, aup_ref[d]))
        kd = k * (1.0 + (a - 1.0) * k_a)
        kd_sum = kd if kd_sum is None else kd_sum + kd
        bb = kk * a
        emit_dense(1)
        ci3 = chunks(_dot(tri_ref[d], lw.astype(BF16)))
        ce3 = ci3 - chunks(lw)
        t_last = 0 if d == 1 else c_len - 1
        cm = 0.5 * ci3[:, t_last:t_last + 1, :]
        e_k = jnp.exp(cm - ci3)
        emit_dense(1)
        kk_ref[d] = (chunks(kk) * jnp.exp(ce3 - cm)).reshape(ta, width).astype(BF16)
        r_ref[d] = (chunks(r) * jnp.exp(ci3 - cm)).reshape(ta, width).astype(BF16)
        b_ref[d] = (chunks(bb) * e_k).reshape(ta, width).astype(BF16)
        k_ref[d] = (chunks(kd) * e_k).reshape(ta, width).astype(BF16)
        em_ref[d, 0] = jnp.exp(cm).reshape(n_chunks, width)
    emit_dense(len(pieces))
    bon_ref[...] = _head_segsum(r * kd_sum * r_k) * v


def _inproj(xs, mod, g, w_in, mu, par, wup, aup, gup, *, c_rw, c_gate, batch, seq, ctx_len):
    n, d = xs.shape
    cols = w_in.shape[1]
    width = RW_HEADS * RW_HEAD_DIM
    ta = RW_TILE
    c_len = RW_CHUNK
    n_tiles = n // ta
    tiles_x = seq // ta
    sub = ta // 8
    last8 = n // 8 - 1
    seq_of = lambda i: jnp.minimum(i // tiles_x, batch)
    idx = np.arange(ta)
    same_chunk = (idx[:, None] // c_len) == (idx[None, :] // c_len)
    tri = jnp.asarray(np.stack([same_chunk & (idx[None, :] <= idx[:, None]),
                                same_chunk & (idx[None, :] >= idx[:, None])]), BF16)
    tok2 = pl.BlockSpec((2, ta, width), lambda i: (0, i, 0))
    tok = lambda c: pl.BlockSpec((ta, c), lambda i: (i, 0))
    sds = jax.ShapeDtypeStruct
    return pl.pallas_call(
        functools.partial(_inproj_kernel, c_rw=c_rw, c_gate=c_gate, n_x_tiles=batch * tiles_x, tiles_x=tiles_x,
                          tiles_ctx=ctx_len // ta),
        grid=(n_tiles,),
        in_specs=[tok(d),
                  pl.BlockSpec((8, d), lambda i: (jnp.maximum(i * sub - 1, 0), 0)),
                  pl.BlockSpec((8, d), lambda i: (jnp.minimum((i + 1) * sub, last8), 0)),
                  pl.BlockSpec((1, N_MOD, d), lambda i: (seq_of(i), 0, 0)),
                  _resident((1, d)), _hbm_spec(),
                  _resident(mu.shape), _resident(par.shape), _resident(wup.shape), _resident(aup.shape),
                  _resident(gup.shape), _resident(tri.shape)],
        out_specs=[tok(c_rw), tok(cols - c_gate), tok2, tok2, tok2, tok2,
                   pl.BlockSpec((2, 1, ta // c_len, width), lambda i: (0, i, 0, 0)),
                   tok(width), tok(width), tok(width)],
        out_shape=[sds((n, c_rw), BF16), sds((n, cols - c_gate), BF16)]
                  + [sds((2, n, width), BF16)] * 4
                  + [sds((2, n_tiles, ta // c_len, width), F32), sds((n, width), BF16),
                     sds((n, width), F32), sds((n, width), F32)],
        scratch_shapes=[pltpu.VMEM((ta + 16, c_gate - c_rw), F32), pltpu.VMEM(w_in.shape, BF16),
                        _stage_shape(w_in), _stage_sems()],
        compiler_params=_cparams(1),
    )(xs, xs, xs, mod, g, w_in, mu, par, wup, aup, gup, tri)


NA_KEY_ROWS = 2 * NA_ROWS
NA_PAIRS = NA_KEY_ROWS // 2


def _na_pair_bias(rpb):
    w = GRID_W
    qc = np.arange(w)[:, None]
    kc = np.arange(w)[None, :]
    s_c = np.clip(qc - NA_WIN_W // 2, 0, w - NA_WIN_W)
    col_ok = (kc >= s_c) & (kc < s_c + NA_WIN_W)
    dc = np.clip(kc - qc + NA_WIN_W - 1, 0, 2 * NA_WIN_W - 2)
    onehot = jnp.asarray(dc[None] == np.arange(2 * NA_WIN_W - 1)[:, None, None], F32)
    b = jnp.einsum('hrd,dqk->hrqk', rpb.astype(F32), onehot, precision=lax.Precision.HIGHEST)
    b = jnp.where(col_ok[None, None], b, -jnp.inf)
    ninf = jnp.full((rpb.shape[0], 1, w, w), -jnp.inf, F32)
    b = jnp.concatenate([ninf, b, ninf], axis=1)
    return jnp.concatenate([b[:, :-1], b[:, 1:]], axis=-1)


def _na_kernel(q_ref, k0_ref, k1_ref, k2_ref, k3_ref, kc_ref, v0_ref, v1_ref, v2_ref, v3_ref, vc_ref,
               bias_ref, o_ref, kcat, vcat, *, n_blocks):
    j = pl.program_id(2)
    blk = k0_ref.shape[0]
    for s, (kr, vr) in enumerate(((k0_ref, v0_ref), (k1_ref, v1_ref), (k2_ref, v2_ref), (k3_ref, v3_ref),
                                  (kc_ref, vc_ref))):
        kcat[s * blk:s * blk + kr.shape[0], :] = kr[...]
        vcat[s * blk:s * blk + vr.shape[0], :] = vr[...]
    n_loc = 4 * blk
    w = GRID_W
    half = NA_WIN_H // 2
    sub_rows = NA_ROWS // 2
    sub_pairs = (sub_rows + NA_WIN_H) // 2
    tq = sub_rows * w
    lane = lax.broadcasted_iota(jnp.int32, (1, LANES), 1)
    first = lane < NA_HEAD_DIM
    nt = (((1,), (1,)), ((), ()))

    items = [(pp, h, sb) for pp in range(q_ref.shape[1] // LANES) for sb in range(2) for h in range(2)]

    def scores(item):
        pp, h, sb = item
        cols = slice(pp * LANES, (pp + 1) * LANES)
        q = q_ref[sb * tq:(sb + 1) * tq, cols] * (NA_HEAD_DIM ** -0.5)
        qh = jnp.where(first if h == 0 else jnp.logical_not(first), q, jnp.zeros_like(q))
        k0 = 2 * sb * LANES
        s_loc = lax.dot_general(qh, kcat[k0:k0 + sub_pairs * LANES, cols], nt, preferred_element_type=F32)
        s_ctx = lax.dot_general(qh, kcat[n_loc:, cols], nt, preferred_element_type=F32)
        return s_loc, s_ctx

    def attend(item, s_loc, s_ctx):
        pp, h, sb = item
        cols = slice(pp * LANES, (pp + 1) * LANES)
        p_rows = []
        l_rows = []
        for ql in range(sub_rows):
            qi = sb * sub_rows + ql
            lo = jnp.where(j == 0, max(qi, half), jnp.where(j == n_blocks - 1, min(qi, half), qi))
            m_lo = min(qi, half) // 2
            m_hi = (max(qi, half) + NA_WIN_H - 1) // 2
            n_m = m_hi - m_lo + 1
            key_row = 2 * m_lo + lax.broadcasted_iota(jnp.int32, (1, n_m * LANES), 1) // w
            seen = (key_row >= lo) & (key_row < lo + NA_WIN_H)
            bias = jnp.concatenate(
                [bias_ref[2 * pp + h, min(max(2 * m - qi + half, 0), 2 * NA_WIN_H - 1)]
                 for m in range(m_lo, m_hi + 1)], axis=1)
            rows = slice(ql * w, (ql + 1) * w)
            t_lo, t_hi = m_lo - 2 * sb, m_hi - 2 * sb
            s_q = jnp.where(seen, s_loc[rows, t_lo * LANES:(t_hi + 1) * LANES] + bias, -jnp.inf)
            s_c = s_ctx[rows]
            mx = jnp.maximum(jnp.max(s_q, axis=-1, keepdims=True), jnp.max(s_c, axis=-1, keepdims=True))
            p_q = jnp.exp(s_q - mx)
            p_c = jnp.exp(s_c - mx)
            l_rows.append(jnp.sum(p_q, axis=-1, keepdims=True) + jnp.sum(p_c, axis=-1, keepdims=True))
            pieces = []
            if t_lo > 0:
                pieces.append(jnp.zeros((w, t_lo * LANES), BF16))
            pieces.append(p_q.astype(BF16))
            if t_hi < sub_pairs - 1:
                pieces.append(jnp.zeros((w, (sub_pairs - 1 - t_hi) * LANES), BF16))
            pieces.append(p_c.astype(BF16))
            p_rows.append(jnp.concatenate(pieces, axis=1))
        p = jnp.concatenate(p_rows, axis=0)
        l = jnp.concatenate(l_rows, axis=0)
        k0 = 2 * sb * LANES
        o = (_dot(p[:, :sub_pairs * LANES], vcat[k0:k0 + sub_pairs * LANES, cols])
             + _dot(p[:, sub_pairs * LANES:], vcat[n_loc:, cols]))
        return o / l

    pending = scores(items[0])
    outs = {}
    for n, item in enumerate(items):
        current = pending
        if n + 1 < len(items):
            pending = scores(items[n + 1])
        outs[item] = attend(item, *current)
        pp, h, sb = item
        if h == 1:
            o_ref[sb * tq:(sb + 1) * tq, pp * LANES:(pp + 1) * LANES] = jnp.where(
                first, outs[(pp, 0, sb)], outs[(pp, 1, sb)]).astype(o_ref.dtype)


def _na_attention(qkv, bias, *, batch, seq, ctx_len):
    rows = seq // GRID_W
    nj = rows // NA_ROWS
    tq = NA_ROWS * GRID_W
    tk = 4 * GRID_W
    width = NA_HEADS * NA_HEAD_DIM
    n_pairs = width // LANES
    kblocks = seq // tk
    ctx_blk0 = (batch * seq) // ctx_len

    lanes = 2 * LANES
    n_groups = width // lanes

    def kv_spec(slot, col0):
        def imap(hg, b, j):
            return (b * kblocks + jnp.clip(2 * j - 1 + slot, 0, kblocks - 1), col0 + hg)
        return pl.BlockSpec((tk, lanes), imap)

    def ctx_spec(col0):
        return pl.BlockSpec((ctx_len, lanes), lambda hg, b, j: (ctx_blk0 + b, col0 + hg))

    in_specs = ([pl.BlockSpec((tq, lanes), lambda hg, b, j: (b * nj + j, hg))]
                + [kv_spec(s, n_groups) for s in range(4)] + [ctx_spec(n_groups)]
                + [kv_spec(s, 2 * n_groups) for s in range(4)] + [ctx_spec(2 * n_groups)]
                + [pl.BlockSpec((4,) + bias.shape[1:], lambda hg, b, j: (hg, 0, 0, 0))])
    return pl.pallas_call(
        functools.partial(_na_kernel, n_blocks=nj),
        grid=(n_groups, batch, nj),
        in_specs=in_specs,
        out_specs=pl.BlockSpec((tq, lanes), lambda hg, b, j: (b * nj + j, hg)),
        out_shape=jax.ShapeDtypeStruct((batch * seq, width), BF16),
        scratch_shapes=[pltpu.VMEM((4 * tk + ctx_len, lanes), BF16),
                        pltpu.VMEM((4 * tk + ctx_len, lanes), BF16)],
        compiler_params=_cparams(3),
    )(*([qkv] * 11), bias)


def _rwkv_kernel(kkf_ref, rf_ref, bf_ref, kf_ref, emf_ref, vf_ref, kkb_ref, rb_ref, bb_ref, kb_ref, emb_ref, vb_ref,
                 strict_ref, incl_ref, yf_ref, yb_ref, z_s):
    n = pl.program_id(1)
    ta = vf_ref.shape[0]
    width = vf_ref.shape[1]
    n_pairs = width // LANES
    c_len = RW_CHUNK
    n_chunks = ta // c_len
    n_d = n_chunks * n_pairs
    n_b = 2 * n_d

    @pl.when(n == 0)
    def _():
        z_s[...] = jnp.zeros_like(z_s)

    def chunks(t):
        return t.reshape(n_chunks, c_len, width)

    def to_batch(t):
        parts = [t[:, :, j * LANES:(j + 1) * LANES] for j in range(n_pairs)]
        return jnp.stack(parts, axis=1).reshape(n_d, t.shape[1], LANES)

    def load_direction(kk_ref, r_ref, b_ref, k_ref, em_ref, v_ref):
        em = em_ref[0, 0].reshape(n_chunks, 1, width)
        kk_t = chunks(kk_ref[0].astype(F32))
        r_t = chunks(r_ref[0].astype(F32))
        b_t = chunks(b_ref[0].astype(F32))
        k_t = chunks(k_ref[0].astype(F32))
        return (to_batch(kk_t * em),
                to_batch(r_t * em),
                to_batch(k_t * em),
                to_batch(b_t * em),
                to_batch(em * em),
                to_batch(kk_t), to_batch(r_t), to_batch(b_t), to_batch(k_t),
                to_batch(chunks(v_ref[...].astype(F32))))

    fwd = load_direction(kkf_ref, rf_ref, bf_ref, kf_ref, emf_ref, vf_ref)
    bwd = load_direction(kkb_ref, rb_ref, bb_ref, kb_ref, emb_ref, vb_ref)
    kk_abs, r_abs, k_bar, b_bar, gam, kk_t, r_t, b_t, k_t, v_b = [
        jnp.concatenate([f, b], axis=0) for f, b in zip(fwd, bwd)]

    lane = lax.broadcasted_iota(jnp.int32, (1, 1, LANES), 2)
    first = lane < RW_HEAD_DIM
    ri = lax.broadcasted_iota(jnp.int32, (c_len, LANES), 0)
    cj = lax.broadcasted_iota(jnp.int32, (c_len, LANES), 1) % RW_HEAD_DIM
    eye = ri == cj

    def causal(mask_ref, t):
        return jnp.concatenate([jnp.where(mask_ref[0] > 0.0, t[:n_d], 0.0),
                                jnp.where(mask_ref[1] > 0.0, t[n_d:], 0.0)], axis=0)

    def stack_heads(t):
        zero = jnp.zeros_like(t)
        return jnp.concatenate([jnp.where(first, t, zero), jnp.where(first, zero, t)], axis=1)

    def bmm(x, y):
        return lax.dot_general(x.astype(BF16), y.astype(BF16), (((2,), (1,)), ((0,), (0,))),
                               preferred_element_type=F32)

    def bmm_nt(x, y):
        return lax.dot_general(x.astype(BF16), y.astype(BF16), (((2,), (2,)), ((0,), (0,))),
                               preferred_element_type=F32)

    def bmm_tn(x, y):
        return lax.dot_general(x.astype(BF16), y.astype(BF16), (((1,), (1,)), ((0,), (0,))),
                               preferred_element_type=F32)

    gm = bmm_nt(jnp.concatenate([kk_t, r_t], axis=1),
                jnp.concatenate([stack_heads(b_t), stack_heads(k_t)], axis=1))
    ab_w = causal(strict_ref, gm[:, :c_len, :LANES])
    ak_w = causal(strict_ref, gm[:, :c_len, LANES:])
    db_w = causal(incl_ref, gm[:, c_len:, :LANES])
    dk_w = causal(incl_ref, gm[:, c_len:, LANES:])

    base = RW_INV_BASE
    same = lambda s: (ri // s) == (cj // s)
    l_base = jnp.where(same(base), ab_w, 0.0)
    t_w = jnp.where(eye, 1.0, 0.0) - l_base
    pw = bmm(l_base, stack_heads(l_base))
    span = 2
    while span < base:
        t_w = t_w + bmm(t_w, stack_heads(pw))
        span *= 2
        if span < base:
            pw = bmm(pw, stack_heads(pw))
    size = base
    while size < c_len:
        off = jnp.where(jnp.logical_and(same(2 * size), jnp.logical_not(same(size))), ab_w, 0.0)
        t_w = t_w - bmm(bmm(t_w, stack_heads(off)), stack_heads(t_w))
        size *= 2

    def head_diag(t):
        return jnp.where(first, t[:, :RW_HEAD_DIM], t[:, RW_HEAD_DIM:])

    akv = bmm(jnp.concatenate([ak_w, dk_w], axis=1), stack_heads(v_b))
    pq = bmm(t_w, jnp.concatenate([stack_heads(kk_abs), stack_heads(akv[:, :c_len])], axis=2))
    dpq = bmm(db_w, jnp.concatenate([stack_heads(pq[:, :, :LANES]), stack_heads(pq[:, :, LANES:])], axis=2))
    r_hat = r_abs - dpq[:, :, :LANES]
    y_loc = akv[:, c_len:] - dpq[:, :, LANES:]
    bpq = bmm_tn(b_bar, pq)
    m_w = jnp.where(eye, jnp.broadcast_to(gam, (n_b, RW_HEAD_DIM, LANES)), 0.0) - head_diag(bpq[:, :, :LANES])
    g_w = head_diag(bmm_tn(k_bar, v_b) - bpq[:, :, LANES:])

    zs = z_s[...]
    for step in range(n_chunks):
        c_f, c_b = step, n_chunks - 1 - step
        pick = lambda t: jnp.concatenate([t[c_f * n_pairs:(c_f + 1) * n_pairs],
                                          t[n_d + c_b * n_pairs:n_d + (c_b + 1) * n_pairs]], axis=0)
        both = bmm(jnp.concatenate([pick(r_hat), pick(m_w)], axis=1), stack_heads(zs))
        y_c = both[:, :c_len] + pick(y_loc)
        yf_ref[0, c_f * c_len:(c_f + 1) * c_len, :] = jnp.concatenate([y_c[j] for j in range(n_pairs)], axis=1)
        yb_ref[0, c_b * c_len:(c_b + 1) * c_len, :] = jnp.concatenate(
            [y_c[n_pairs + j] for j in range(n_pairs)], axis=1)
        zs = both[:, c_len:] + pick(g_w)
    z_s[...] = zs


def _rwkv_scan(kk_t, r_t, b_t, k_t, em, v, *, batch, seq, ctx_len):
    width = RW_HEADS * RW_HEAD_DIM
    ta = RW_TILE
    nct = ctx_len // ta
    nxt = seq // ta
    n_x = batch * seq
    c_len = RW_CHUNK

    def blk(reverse):
        def index(b, n):
            t_ctx = (nct - 1 - n) if reverse else n
            t_x = (nxt - 1 - (n - nct)) if reverse else (n - nct)
            return jnp.where(n < nct, batch * nxt + b * nct + t_ctx, b * nxt + t_x)
        return index

    def xblk(reverse):
        def index(b, n):
            m = jnp.maximum(n, nct) - nct
            return b * nxt + ((nxt - 1 - m) if reverse else m)
        return index

    t_i = np.arange(c_len)[:, None]
    s_i = np.arange(LANES)[None, :] % RW_HEAD_DIM
    strict = jnp.asarray(np.stack([s_i < t_i, s_i > t_i]), F32)
    incl = jnp.asarray(np.stack([s_i <= t_i, s_i >= t_i]), F32)

    def operands(d):
        index = blk(d == 1)
        tile = pl.BlockSpec((1, ta, width), lambda b, n: (d, index(b, n), 0))
        return [tile, tile, tile, tile,
                pl.BlockSpec((1, 1, ta // c_len, width), lambda b, n: (d, index(b, n), 0, 0)),
                pl.BlockSpec((ta, width), lambda b, n: (index(b, n), 0))]

    def out_spec(reverse):
        index = xblk(reverse)
        return pl.BlockSpec((1, ta, width), lambda b, n: (0, index(b, n), 0))

    y_shape = jax.ShapeDtypeStruct((1, n_x, width), F32)
    return pl.pallas_call(
        _rwkv_kernel,
        grid=(batch, nct + nxt),
        in_specs=operands(0) + operands(1) + [_resident(strict.shape), _resident(incl.shape)],
        out_specs=[out_spec(False), out_spec(True)],
        out_shape=[y_shape, y_shape],
        scratch_shapes=[pltpu.VMEM((2 * (width // LANES), RW_HEAD_DIM, LANES), F32)],
        compiler_params=_cparams(2),
    )(kk_t, r_t, b_t, k_t, em, v, kk_t, r_t, b_t, k_t, em, v, strict, incl)


def _merge_kernel(x_ref, mod_ref, ona_ref, yf_ref, yb_ref, bon_ref, g_ref, pg_ref, bg_ref, ln_ref,
                  pna_ref, prw_ref, wout_ref, gn_ref, wg_hbm, wu_hbm, wd_hbm, gf_ref, o_ref,
                  wg_s, wu_s, wd_s, stage_in, stage_out, sem, *, f_chunk):
    @pl.when(pl.program_id(0) == 0)
    def _():
        _ffn_load_weights(wg_hbm, wu_hbm, wd_hbm, wg_s, wu_s, wd_s, stage_in, stage_out, sem)

    d = x_ref.shape[1]
    y = yf_ref[0] + yb_ref[0]
    inv_n = 1.0 / RW_HEAD_DIM
    mean = _head_segsum(y) * inv_n
    yc = y - mean
    var = _head_segsum(yc * yc) * inv_n
    yn = yc * lax.rsqrt(var + RW_GN_EPS) * ln_ref[0:1, :] + ln_ref[1:2, :]
    o_rw = ((yn + bon_ref[...]) * g_ref[...]).astype(BF16)
    gates = jax.nn.sigmoid(pg_ref[...] + bg_ref[...])
    m = gates[:, :d] * _dot(ona_ref[...], pna_ref[...]) + gates[:, d:] * _dot(o_rw, prw_ref[...])
    x2 = x_ref[...] + mod_ref[0, 5:6, :] * _dot(m.astype(BF16), wout_ref[...])
    o_ref[...] = _swiglu_half_step(x2, mod_ref, gn_ref, wg_s, wu_s, wd_s, gf_ref,
                                   mod0=6, f_chunk=f_chunk, final=True)


def _merge_ffn(x1, mod, o_na, y_f, y_b, bon, g, p_gate, b_gate, ln, p_na, p_rw, w_out, gn, wg, wu, wd, gf, *,
               tiles_per_seq):
    n_x, d = o_na.shape[0], x1.shape[1]
    width = o_na.shape[1]
    tok = lambda i: (i, 0)
    tok3 = lambda i: (0, i, 0)
    return pl.pallas_call(
        functools.partial(_merge_kernel, f_chunk=FFN_CHUNK),
        grid=(n_x // TM,),
        in_specs=[pl.BlockSpec((TM, d), tok),
                  pl.BlockSpec((1, N_MOD, d), lambda i: (i // tiles_per_seq, 0, 0)),
                  pl.BlockSpec((TM, width), tok),
                  pl.BlockSpec((1, TM, width), tok3), pl.BlockSpec((1, TM, width), tok3),
                  pl.BlockSpec((TM, width), tok), pl.BlockSpec((TM, width), tok),
                  pl.BlockSpec((TM, 2 * d), tok),
                  _resident((1, 2 * d)), _resident((2, width)),
                  _resident(p_na.shape), _resident(p_rw.shape), _resident(w_out.shape),
                  _resident(gn.shape), _hbm_spec(), _hbm_spec(), _hbm_spec(), _resident(gf.shape)],
        out_specs=pl.BlockSpec((TM, d), tok),
        out_shape=jax.ShapeDtypeStruct((n_x, d), F32),
        scratch_shapes=_ffn_scratch(wg, wu, wd),
        compiler_params=_cparams(1),
    )(x1, mod, o_na, y_f, y_b, bon, g, p_gate, b_gate, ln, p_na, p_rw, w_out, gn, wg, wu, wd, gf)


def _pad_lora(w_up, direction):
    z = jnp.zeros_like(w_up[0])
    return jnp.concatenate([w_up[0] if direction == 0 else z, w_up[1] if direction == 1 else z], axis=0)


def kernel(x, c, ctx, c_ctx, w_ada, b_ada, norm_ffn1, norm_mix, norm_ffn2, norm_final, ffn1_wg, ffn1_wu, ffn1_wd, ffn2_wg, ffn2_wu, ffn2_wd, w_in, b_gate, na_rpb, rw_mu, rw_w0, rw_w_up, rw_a0, rw_a_up, rw_g_up, rw_k_k, rw_k_a, rw_r_k, rw_ln_w, rw_ln_b, p_na, p_rw, w_out):
    batch, seq, d = x.shape
    ctx_len = ctx.shape[1]
    n_x = batch * seq
    na_width = NA_HEADS * NA_HEAD_DIM
    rw_width = RW_HEADS * RW_HEAD_DIM
    c_rw = 3 * na_width
    c_gate = c_rw + 3 * rw_width + 4 * RW_LORA + RW_GATE_LORA
    assert w_ada.shape[0] == 1, "single layer"
    assert seq % TM == 0 and (batch * ctx_len) % TM == 0 and ctx_len % RW_TILE == 0 and batch + 1 <= 8
    assert seq % (NA_ROWS * GRID_W) == 0 and seq // (NA_ROWS * GRID_W) >= 2 and ctx_len == 4 * GRID_W
    row = lambda t: t.reshape(1, -1)

    cs = jnp.concatenate([c, c_ctx[None], jnp.zeros((8 - batch - 1, d), F32)], axis=0)
    mod = _ada_mod(cs, w_ada[0], b_ada[0])[:batch + 1].reshape(batch + 1, N_MOD, d)

    bf = lambda t: t.astype(BF16)

    x1 = _ffn(x.reshape(n_x, d), ctx.reshape(batch * ctx_len, d), mod, row(norm_ffn1[0]),
              ffn1_wg[0], ffn1_wu[0], ffn1_wd[0], row(norm_final),
              mod0=0, tiles_per_seq=seq // TM, n_seq=batch)
    par = jnp.concatenate([rw_w0[0], rw_a0[0], row(rw_k_k[0]), row(rw_k_a[0]), row(rw_r_k[0]),
                           jnp.zeros((1, rw_width), F32)], axis=0)
    wup = bf(jnp.stack([_pad_lora(rw_w_up[0], 0), _pad_lora(rw_w_up[0], 1)]))
    aup = bf(jnp.stack([_pad_lora(rw_a_up[0], 0), _pad_lora(rw_a_up[0], 1)]))
    qkv, p_gate, kk_t, r_t, b_t, k_t, em, v_rw, g, bon = _inproj(
        x1, mod, row(norm_mix[0]), w_in[0], rw_mu[0], par, wup, aup, bf(rw_g_up[0]),
        c_rw=c_rw, c_gate=c_gate, batch=batch, seq=seq, ctx_len=ctx_len)

    bias = _na_pair_bias(na_rpb[0])
    o_na = _na_attention(qkv, bias, batch=batch, seq=seq, ctx_len=ctx_len)

    y_f, y_b = _rwkv_scan(kk_t, r_t, b_t, k_t, em, v_rw, batch=batch, seq=seq, ctx_len=ctx_len)

    ln = jnp.stack([rw_ln_w[0], rw_ln_b[0]], axis=0)
    out = _merge_ffn(x1, mod, o_na, y_f, y_b, bon, g, p_gate, row(b_gate[0]), ln,
                     bf(p_na[0]), bf(p_rw[0]), bf(w_out[0]), row(norm_ffn2[0]),
                     ffn2_wg[0], ffn2_wu[0], ffn2_wd[0], row(norm_final),
                     tiles_per_seq=seq // TM)
    return out.reshape(batch, seq, d)
```

```python
import functools
import math

import numpy as np
import jax
import jax.numpy as jnp
from jax import lax
from jax.experimental import pallas as pl
from jax.experimental.pallas import tpu as pltpu

F32 = jnp.float32
BF16 = jnp.bfloat16

NORM_EPS = 1e-6
RW_GN_EPS = 64e-5
N_MOD = 9
GRID_W = 64
NA_HEADS = 8
NA_HEAD_DIM = 64
NA_WIN_H = 8
NA_WIN_W = 16
RW_HEADS = 8
RW_HEAD_DIM = 64
RW_LORA = 64
RW_GATE_LORA = 128

LANES = 128
VMEM_LIMIT = 56 * 1024 * 1024

TM = 512
FFN_CHUNK = 256
NA_ROWS = 8
RW_TILE = 256
RW_CHUNK = 64
RW_INV_BASE = 8
assert RW_CHUNK == RW_HEAD_DIM


def _cparams(n_axes):
    return pltpu.CompilerParams(dimension_semantics=("arbitrary",) * n_axes,
                                vmem_limit_bytes=VMEM_LIMIT)


def _resident(shape):
    nd = len(shape)
    return pl.BlockSpec(shape, lambda *_: (0,) * nd, pipeline_mode=pl.Buffered(1))


def _rmsnorm(x, g):
    return x * lax.rsqrt(jnp.mean(x * x, axis=-1, keepdims=True) + NORM_EPS) * g


def _dot(a, b):
    return jnp.dot(a, b, preferred_element_type=F32)


WEIGHT_CHUNKS = 16


def _hbm_spec():
    return pl.BlockSpec(memory_space=pl.ANY)


WEIGHT_SLOTS = 4


def _stage_shape(w):
    return pltpu.VMEM((WEIGHT_SLOTS, w.shape[0] // WEIGHT_CHUNKS, w.shape[1]), F32)


def _stage_sems():
    return pltpu.SemaphoreType.DMA((WEIGHT_SLOTS,))


def _fetch_as_bf16(src_hbm, dst, stage, sem):
    rows = stage.shape[1]

    def chunk_copy(c):
        slot = c % WEIGHT_SLOTS
        return pltpu.make_async_copy(src_hbm.at[c * rows:(c + 1) * rows, :], stage.at[slot], sem.at[slot])

    for c in range(WEIGHT_SLOTS):
        chunk_copy(c).start()
    for c in range(WEIGHT_CHUNKS):
        chunk_copy(c).wait()
        dst[c * rows:(c + 1) * rows, :] = stage[c % WEIGHT_SLOTS].astype(dst.dtype)
        if c + WEIGHT_SLOTS < WEIGHT_CHUNKS:
            chunk_copy(c + WEIGHT_SLOTS).start()


def _ada_kernel(c_ref, w_ref, b_ref, o_ref):
    c = c_ref[...]
    s = c * jax.nn.sigmoid(c)
    o_ref[...] = _dot(s.astype(BF16), w_ref[...].astype(BF16)) + b_ref[...]


def _ada_mod(cs, w_ada, b_ada):
    d = cs.shape[1]
    nm = w_ada.shape[1] // d
    return pl.pallas_call(
        _ada_kernel,
        grid=(nm,),
        in_specs=[pl.BlockSpec((8, d), lambda j: (0, 0)),
                  pl.BlockSpec((d, d), lambda j: (0, j)),
                  pl.BlockSpec((1, d), lambda j: (0, j))],
        out_specs=pl.BlockSpec((8, d), lambda j: (0, j)),
        out_shape=jax.ShapeDtypeStruct((8, nm * d), F32),
        compiler_params=_cparams(1),
    )(cs, w_ada, b_ada.reshape(1, -1))


def _ffn_scratch(wg, wu, wd):
    return [pltpu.VMEM(wg.shape, BF16), pltpu.VMEM(wu.shape, BF16), pltpu.VMEM(wd.shape, BF16),
            _stage_shape(wg), _stage_shape(wd), _stage_sems()]


def _ffn_load_weights(wg_hbm, wu_hbm, wd_hbm, wg_s, wu_s, wd_s, stage_in, stage_out, sem):
    _fetch_as_bf16(wg_hbm, wg_s, stage_in, sem)
    _fetch_as_bf16(wu_hbm, wu_s, stage_in, sem)
    _fetch_as_bf16(wd_hbm, wd_s, stage_out, sem)


def _ffn_kernel(x_ref, tail_ref, mod_ref, g_ref, wg_hbm, wu_hbm, wd_hbm, gf_ref, o_ref,
                wg_s, wu_s, wd_s, stage_in, stage_out, sem, *, mod0, f_chunk, n_head_tiles):
    @pl.when(pl.program_id(0) == 0)
    def _():
        _ffn_load_weights(wg_hbm, wu_hbm, wd_hbm, wg_s, wu_s, wd_s, stage_in, stage_out, sem)

    x = jnp.where(pl.program_id(0) < n_head_tiles, x_ref[...], tail_ref[...])
    o_ref[...] = _swiglu_half_step(x, mod_ref, g_ref, wg_s, wu_s, wd_s, gf_ref,
                                   mod0=mod0, f_chunk=f_chunk, final=False)


def _swiglu_half_step(x, mod_ref, g_ref, wg_ref, wu_ref, wd_ref, gf_ref, *, mod0, f_chunk, final):
    shift = mod_ref[0, mod0:mod0 + 1, :]
    scale = mod_ref[0, mod0 + 1:mod0 + 2, :]
    gate = mod_ref[0, mod0 + 2:mod0 + 3, :]
    h = (_rmsnorm(x, g_ref[...]) * (1.0 + scale) + shift).astype(BF16)
    d_ff = wg_ref.shape[1]
    bounds = [(f0, min(f0 + f_chunk, d_ff)) for f0 in range(0, d_ff, f_chunk)]

    def gate_up(lo, hi):
        return _dot(h, wg_ref[:, lo:hi]), _dot(h, wu_ref[:, lo:hi])

    acc = None
    pending = gate_up(*bounds[0])
    for c, (lo, hi) in enumerate(bounds):
        gg, uu = pending
        if c + 1 < len(bounds):
            pending = gate_up(*bounds[c + 1])
        a = (gg * jax.nn.sigmoid(gg) * uu).astype(BF16)
        t = _dot(a, wd_ref[lo:hi, :])
        acc = t if acc is None else acc + t
    y = x + 0.5 * gate * acc
    if final:
        y = _rmsnorm(y, gf_ref[...])
    return y


def _ffn(xs, tail, mod, g, wg, wu, wd, gf, *, mod0, tiles_per_seq, n_seq):
    n_head, d = xs.shape
    n = n_head + tail.shape[0]
    seq_of = lambda i: jnp.minimum(i // tiles_per_seq, n_seq)
    head_tiles = n_head // TM
    return pl.pallas_call(
        functools.partial(_ffn_kernel, mod0=mod0, f_chunk=FFN_CHUNK, n_head_tiles=head_tiles),
        grid=(n // TM,),
        in_specs=[pl.BlockSpec((TM, d), lambda i: (jnp.minimum(i, head_tiles - 1), 0)),
                  pl.BlockSpec((TM, d), lambda i: (jnp.maximum(i - head_tiles, 0), 0)),
                  pl.BlockSpec((1, N_MOD, d), lambda i: (seq_of(i), 0, 0)),
                  _resident((1, d)), _hbm_spec(), _hbm_spec(), _hbm_spec(), _resident((1, d))],
        out_specs=pl.BlockSpec((TM, d), lambda i: (i, 0)),
        out_shape=jax.ShapeDtypeStruct((n, d), F32),
        scratch_shapes=_ffn_scratch(wg, wu, wd),
        compiler_params=_cparams(1),
    )(xs, tail, mod, g, wg, wu, wd, gf)


def _head_segsum(x):
    r = lax.broadcasted_iota(jnp.int32, (LANES, LANES), 0) // RW_HEAD_DIM
    c = lax.broadcasted_iota(jnp.int32, (LANES, LANES), 1) // RW_HEAD_DIM
    e = jnp.where(r == c, 1.0, 0.0).astype(BF16)
    xb = x.astype(BF16)
    return jnp.concatenate([_dot(xb[:, j:j + LANES], e) for j in range(0, x.shape[1], LANES)], axis=1)


def _inproj_kernel(x_ref, xprev_ref, xnext_ref, mod_ref, g_ref, w_hbm, mu_ref, par_ref, wup_ref, aup_ref,
                   gup_ref, tri_ref, qkv_ref, gate_ref, kk_ref, r_ref, b_ref, k_ref, em_ref, v_ref, g_out_ref,
                   bon_ref, p_s, w_ref, stage, sem, *, c_rw, c_gate, n_x_tiles, tiles_x, tiles_ctx):
    i = pl.program_id(0)

    @pl.when(i == 0)
    def _():
        _fetch_as_bf16(w_hbm, w_ref, stage, sem)

    ta = x_ref.shape[0]
    width = v_ref.shape[1]
    c_len = RW_CHUNK
    n_chunks = ta // c_len
    in_x = i < n_x_tiles
    tile = jnp.where(in_x, i % tiles_x, (i - n_x_tiles) % tiles_ctx)
    last_tile = jnp.where(in_x, tiles_x - 1, tiles_ctx - 1)

    x_all = jnp.concatenate([xprev_ref[...], x_ref[...], xnext_ref[...]], axis=0)
    h_all = (_rmsnorm(x_all, g_ref[...]) * (1.0 + mod_ref[0, 4:5, :]) + mod_ref[0, 3:4, :]).astype(BF16)
    h = h_all[8:ta + 8]
    zero = jnp.zeros((8, h.shape[1]), BF16)
    h_all = jnp.concatenate([jnp.where(tile == 0, zero, h_all[:8]), h,
                             jnp.where(tile == last_tile, zero, h_all[ta + 8:])], axis=0)

    piece = 4 * LANES
    pieces = ([(qkv_ref, c0, c0) for c0 in range(0, c_rw, piece)]
              + [(gate_ref, c0, c_gate + c0) for c0 in range(0, w_ref.shape[1] - c_gate, piece)])

    def emit_dense(count):
        for _ in range(min(count, len(pieces))):
            ref, dst, src = pieces.pop(0)
            ref[:, dst:dst + piece] = _dot(h, w_ref[:, src:src + piece]).astype(ref.dtype)

    def shifted(lo, hi):
        p_s[:, lo:hi] = _dot(h_all, w_ref[:, c_rw + lo:c_rw + hi])
        p = p_s[8:ta + 8, lo:hi]
        return (p + mu_ref[0:1, lo:hi] * (p_s[7:ta + 7, lo:hi] - p)
                + mu_ref[1:2, lo:hi] * (p_s[9:ta + 9, lo:hi] - p))

    o = 3 * width
    lora = shifted(o, o + 3 * LANES)
    k = shifted(width, 2 * width)
    r = shifted(0, width)
    v = shifted(2 * width, 3 * width)
    emit_dense(2)

    th = jnp.tanh(lora[:, :LANES]).astype(BF16)
    al = lora[:, LANES:2 * LANES].astype(BF16)
    gl = lora[:, 2 * LANES:]
    k_k = par_ref[4:5, :]
    k_a = par_ref[5:6, :]
    r_k = par_ref[6:7, :]

    v_ref[...] = v.astype(BF16)
    g_out_ref[...] = _dot(jax.nn.sigmoid(gl).astype(BF16), gup_ref[...])
    kk = k * k_k
    kk = kk * lax.rsqrt(jnp.maximum(_head_segsum(kk * kk), 1e-12))
    emit_dense(1)

    def chunks(t):
        return t.reshape(n_chunks, c_len, width)

    kd_sum = None
    for d in range(2):
        z = par_ref[d:d + 1, :] + _dot(th, wup_ref[d])
        lw = -math.exp(-0.5) * jax.nn.sigmoid(z)
        a = jax.nn.sigmoid(par_ref[2 + d:3 + d, :] + _dot(al, aup_ref[d]))
        kd = k * (1.0 + (a - 1.0) * k_a)
        kd_sum = kd if kd_sum is None else kd_sum + kd
        bb = kk * a
        emit_dense(1)
        ci3 = chunks(_dot(tri_ref[d], lw.astype(BF16)))
        ce3 = ci3 - chunks(lw)
        t_last = 0 if d == 1 else c_len - 1
        cm = 0.5 * ci3[:, t_last:t_last + 1, :]
        e_k = jnp.exp(cm - ci3)
        emit_dense(1)
        kk_ref[d] = (chunks(kk) * jnp.exp(ce3 - cm)).reshape(ta, width).astype(BF16)
        r_ref[d] = (chunks(r) * jnp.exp(ci3 - cm)).reshape(ta, width).astype(BF16)
        b_ref[d] = (chunks(bb) * e_k).reshape(ta, width).astype(BF16)
        k_ref[d] = (chunks(kd) * e_k).reshape(ta, width).astype(BF16)
        em_ref[d, 0] = jnp.exp(cm).reshape(n_chunks, width)
    emit_dense(len(pieces))
    bon_ref[...] = _head_segsum(r * kd_sum * r_k) * v


def _inproj(xs, mod, g, w_in, mu, par, wup, aup, gup, *, c_rw, c_gate, batch, seq, ctx_len):
    n, d = xs.shape
    cols = w_in.shape[1]
    width = RW_HEADS * RW_HEAD_DIM
    ta = RW_TILE
    c_len = RW_CHUNK
    n_tiles = n // ta
    tiles_x = seq // ta
    sub = ta // 8
    last8 = n // 8 - 1
    seq_of = lambda i: jnp.minimum(i // tiles_x, batch)
    idx = np.arange(ta)
    same_chunk = (idx[:, None] // c_len) == (idx[None, :] // c_len)
    tri = jnp.asarray(np.stack([same_chunk & (idx[None, :] <= idx[:, None]),
                                same_chunk & (idx[None, :] >= idx[:, None])]), BF16)
    tok2 = pl.BlockSpec((2, ta, width), lambda i: (0, i, 0))
    tok = lambda c: pl.BlockSpec((ta, c), lambda i: (i, 0))
    sds = jax.ShapeDtypeStruct
    return pl.pallas_call(
        functools.partial(_inproj_kernel, c_rw=c_rw, c_gate=c_gate, n_x_tiles=batch * tiles_x, tiles_x=tiles_x,
                          tiles_ctx=ctx_len // ta),
        grid=(n_tiles,),
        in_specs=[tok(d),
                  pl.BlockSpec((8, d), lambda i: (jnp.maximum(i * sub - 1, 0), 0)),
                  pl.BlockSpec((8, d), lambda i: (jnp.minimum((i + 1) * sub, last8), 0)),
                  pl.BlockSpec((1, N_MOD, d), lambda i: (seq_of(i), 0, 0)),
                  _resident((1, d)), _hbm_spec(),
                  _resident(mu.shape), _resident(par.shape), _resident(wup.shape), _resident(aup.shape),
                  _resident(gup.shape), _resident(tri.shape)],
        out_specs=[tok(c_rw), tok(cols - c_gate), tok2, tok2, tok2, tok2,
                   pl.BlockSpec((2, 1, ta // c_len, width), lambda i: (0, i, 0, 0)),
                   tok(width), tok(width), tok(width)],
        out_shape=[sds((n, c_rw), BF16), sds((n, cols - c_gate), BF16)]
                  + [sds((2, n, width), BF16)] * 4
                  + [sds((2, n_tiles, ta // c_len, width), F32), sds((n, width), BF16),
                     sds((n, width), F32), sds((n, width), F32)],
        scratch_shapes=[pltpu.VMEM((ta + 16, c_gate - c_rw), F32), pltpu.VMEM(w_in.shape, BF16),
                        _stage_shape(w_in), _stage_sems()],
        compiler_params=_cparams(1),
    )(xs, xs, xs, mod, g, w_in, mu, par, wup, aup, gup, tri)


LOG2E = math.log2(math.e)


def _na_pair_bias(rpb):
    w = GRID_W
    qc = np.arange(w)[:, None]
    kc = np.arange(w)[None, :]
    s_c = np.clip(qc - NA_WIN_W // 2, 0, w - NA_WIN_W)
    col_ok = (kc >= s_c) & (kc < s_c + NA_WIN_W)
    dc = np.clip(kc - qc + NA_WIN_W - 1, 0, 2 * NA_WIN_W - 2)
    onehot = jnp.asarray(dc[None] == np.arange(2 * NA_WIN_W - 1)[:, None, None], F32)
    b = jnp.einsum('hrd,dqk->hrqk', rpb.astype(F32), onehot, precision=lax.Precision.HIGHEST)
    b = jnp.where(col_ok[None, None], b * LOG2E, -jnp.inf)
    ninf = jnp.full((rpb.shape[0], 1, w, w), -jnp.inf, F32)
    b = jnp.concatenate([ninf, b, ninf], axis=1)
    n_e = 2 * NA_WIN_H
    return pl.pallas_call(
        _na_pair_kernel,
        grid=(rpb.shape[0],),
        in_specs=[pl.BlockSpec((1, n_e + 1, w, w), lambda h: (h, 0, 0, 0))],
        out_specs=pl.BlockSpec((2, 1, n_e, w, 2 * w), lambda h: (0, h, 0, 0, 0)),
        out_shape=jax.ShapeDtypeStruct((2, rpb.shape[0], n_e, w, 2 * w), F32),
        compiler_params=_cparams(1),
    )(b)


def _na_pair_kernel(b_ref, o_ref):
    b = b_ref[0]
    pair = jnp.concatenate([b[:-1], b[1:]], axis=-1)
    o_ref[0, 0] = pair
    dr = (lax.broadcasted_iota(jnp.int32, pair.shape, 0) - 1
          + (lax.broadcasted_iota(jnp.int32, pair.shape, 2) >= GRID_W).astype(jnp.int32))
    first_dr = NA_WIN_H // 2 - 1
    inside = jnp.logical_and(dr >= first_dr, dr < first_dr + NA_WIN_H)
    o_ref[1, 0] = jnp.where(inside, pair, -jnp.inf)


def _na_kernel(q_ref, k0_ref, k1_ref, k2_ref, k3_ref, kc_ref, v0_ref, v1_ref, v2_ref, v3_ref, vc_ref,
               bias_ref, o_ref, kcat, vcat, *, n_blocks):
    j = pl.program_id(2)
    blk = k0_ref.shape[0]
    for s, (kr, vr) in enumerate(((k0_ref, v0_ref), (k1_ref, v1_ref), (k2_ref, v2_ref), (k3_ref, v3_ref),
                                  (kc_ref, vc_ref))):
        kcat[s * blk:s * blk + kr.shape[0], :] = kr[...]
        vcat[s * blk:s * blk + vr.shape[0], :] = vr[...]
    n_loc = 4 * blk
    w = GRID_W
    half = NA_WIN_H // 2
    sub_rows = NA_ROWS // 2
    sub_pairs = (sub_rows + NA_WIN_H) // 2
    tq = sub_rows * w
    lane = lax.broadcasted_iota(jnp.int32, (1, LANES), 1)
    first = lane < NA_HEAD_DIM
    nt = (((1,), (1,)), ((), ()))

    items = [(pp, h, sb) for pp in range(q_ref.shape[1] // LANES) for sb in range(2) for h in range(2)]

    def scores(item):
        pp, h, sb = item
        cols = slice(pp * LANES, (pp + 1) * LANES)
        q = (q_ref[sb * tq:(sb + 1) * tq, cols].astype(F32) * (NA_HEAD_DIM ** -0.5 * LOG2E)).astype(BF16)
        qh = jnp.where(first if h == 0 else jnp.logical_not(first), q, jnp.zeros_like(q))
        k0 = 2 * sb * LANES
        s_loc = lax.dot_general(qh, kcat[k0:k0 + sub_pairs * LANES, cols], nt, preferred_element_type=F32)
        s_ctx = lax.dot_general(qh, kcat[n_loc:, cols], nt, preferred_element_type=F32)
        return s_loc, s_ctx

    def attend(item, s_loc, s_ctx, interior):
        pp, h, sb = item
        cols = slice(pp * LANES, (pp + 1) * LANES)
        p_rows = []
        l_rows = []
        for ql in range(sub_rows):
            qi = sb * sub_rows + ql
            if interior:
                m_lo, m_hi = qi // 2, (qi + NA_WIN_H - 1) // 2
            else:
                m_lo, m_hi = min(qi, half) // 2, (max(qi, half) + NA_WIN_H - 1) // 2
            bias = jnp.concatenate(
                [bias_ref[1 if interior else 0, 2 * pp + h, min(max(2 * m - qi + half, 0), 2 * NA_WIN_H - 1)]
                 for m in range(m_lo, m_hi + 1)], axis=1)
            rows = slice(ql * w, (ql + 1) * w)
            t_lo, t_hi = m_lo - 2 * sb, m_hi - 2 * sb
            s_q = s_loc[rows, t_lo * LANES:(t_hi + 1) * LANES] + bias
            if not interior:
                lo = jnp.where(j == 0, max(qi, half), min(qi, half))
                key_row = 2 * m_lo + lax.broadcasted_iota(jnp.int32, (1, (m_hi - m_lo + 1) * LANES), 1) // w
                s_q = jnp.where((key_row >= lo) & (key_row < lo + NA_WIN_H), s_q, -jnp.inf)
            s_c = s_ctx[rows]
            mx = jnp.maximum(jnp.max(s_q, axis=-1, keepdims=True), jnp.max(s_c, axis=-1, keepdims=True))
            p_q = jnp.exp2(s_q - mx)
            p_c = jnp.exp2(s_c - mx)
            l_rows.append(jnp.sum(p_q, axis=-1, keepdims=True) + jnp.sum(p_c, axis=-1, keepdims=True))
            pieces = []
            if t_lo > 0:
                pieces.append(jnp.zeros((w, t_lo * LANES), BF16))
            pieces.append(p_q.astype(BF16))
            if t_hi < sub_pairs - 1:
                pieces.append(jnp.zeros((w, (sub_pairs - 1 - t_hi) * LANES), BF16))
            pieces.append(p_c.astype(BF16))
            p_rows.append(jnp.concatenate(pieces, axis=1))
        p = jnp.concatenate(p_rows, axis=0)
        l = jnp.concatenate(l_rows, axis=0)
        k0 = 2 * sb * LANES
        o = (_dot(p[:, :sub_pairs * LANES], vcat[k0:k0 + sub_pairs * LANES, cols])
             + _dot(p[:, sub_pairs * LANES:], vcat[n_loc:, cols]))
        return o / l

    def run(interior):
        pending = scores(items[0])
        outs = {}
        for n, item in enumerate(items):
            current = pending
            if n + 1 < len(items):
                pending = scores(items[n + 1])
            outs[item] = attend(item, *current, interior)
            pp, h, sb = item
            if h == 1:
                o_ref[sb * tq:(sb + 1) * tq, pp * LANES:(pp + 1) * LANES] = jnp.where(
                    first, outs[(pp, 0, sb)], outs[(pp, 1, sb)]).astype(o_ref.dtype)

    at_edge = jnp.logical_or(j == 0, j == n_blocks - 1)
    pl.when(at_edge)(lambda: run(False))
    pl.when(jnp.logical_not(at_edge))(lambda: run(True))


def _na_attention(qkv, bias, *, batch, seq, ctx_len):
    rows = seq // GRID_W
    nj = rows // NA_ROWS
    tq = NA_ROWS * GRID_W
    tk = 4 * GRID_W
    width = NA_HEADS * NA_HEAD_DIM
    n_pairs = width // LANES
    kblocks = seq // tk
    ctx_blk0 = (batch * seq) // ctx_len

    lanes = 2 * LANES
    n_groups = width // lanes

    def kv_spec(slot, col0):
        def imap(hg, b, j):
            return (b * kblocks + jnp.clip(2 * j - 1 + slot, 0, kblocks - 1), col0 + hg)
        return pl.BlockSpec((tk, lanes), imap)

    def ctx_spec(col0):
        return pl.BlockSpec((ctx_len, lanes), lambda hg, b, j: (ctx_blk0 + b, col0 + hg))

    in_specs = ([pl.BlockSpec((tq, lanes), lambda hg, b, j: (b * nj + j, hg))]
                + [kv_spec(s, n_groups) for s in range(4)] + [ctx_spec(n_groups)]
                + [kv_spec(s, 2 * n_groups) for s in range(4)] + [ctx_spec(2 * n_groups)]
                + [pl.BlockSpec((2, 4) + bias.shape[2:], lambda hg, b, j: (0, hg, 0, 0, 0))])
    return pl.pallas_call(
        functools.partial(_na_kernel, n_blocks=nj),
        grid=(n_groups, batch, nj),
        in_specs=in_specs,
        out_specs=pl.BlockSpec((tq, lanes), lambda hg, b, j: (b * nj + j, hg)),
        out_shape=jax.ShapeDtypeStruct((batch * seq, width), BF16),
        scratch_shapes=[pltpu.VMEM((4 * tk + ctx_len, lanes), BF16),
                        pltpu.VMEM((4 * tk + ctx_len, lanes), BF16)],
        compiler_params=_cparams(3),
    )(*([qkv] * 11), bias)


def _rwkv_kernel(kkf_ref, rf_ref, bf_ref, kf_ref, emf_ref, vf_ref, kkb_ref, rb_ref, bb_ref, kb_ref, emb_ref, vb_ref,
                 strict_ref, incl_ref, yf_ref, yb_ref, z_s):
    n = pl.program_id(1)
    ta = vf_ref.shape[0]
    width = vf_ref.shape[1]
    n_pairs = width // LANES
    c_len = RW_CHUNK
    n_chunks = ta // c_len
    n_d = n_chunks * n_pairs
    n_b = 2 * n_d

    @pl.when(n == 0)
    def _():
        z_s[...] = jnp.zeros_like(z_s)

    def chunks(t):
        return t.reshape(n_chunks, c_len, width)

    def to_batch(t):
        parts = [t[:, :, j * LANES:(j + 1) * LANES] for j in range(n_pairs)]
        return jnp.stack(parts, axis=1).reshape(n_d, t.shape[1], LANES)

    def load_direction(kk_ref, r_ref, b_ref, k_ref, em_ref, v_ref):
        em = em_ref[0, 0].reshape(n_chunks, 1, width)
        kk_t = chunks(kk_ref[0].astype(F32))
        r_t = chunks(r_ref[0].astype(F32))
        b_t = chunks(b_ref[0].astype(F32))
        k_t = chunks(k_ref[0].astype(F32))
        return (to_batch(kk_t * em),
                to_batch(r_t * em),
                to_batch(k_t * em),
                to_batch(b_t * em),
                to_batch(em * em),
                to_batch(kk_t), to_batch(r_t), to_batch(b_t), to_batch(k_t),
                to_batch(chunks(v_ref[...].astype(F32))))

    fwd = load_direction(kkf_ref, rf_ref, bf_ref, kf_ref, emf_ref, vf_ref)
    bwd = load_direction(kkb_ref, rb_ref, bb_ref, kb_ref, emb_ref, vb_ref)
    kk_abs, r_abs, k_bar, b_bar, gam, kk_t, r_t, b_t, k_t, v_b = [
        jnp.concatenate([f, b], axis=0) for f, b in zip(fwd, bwd)]

    lane = lax.broadcasted_iota(jnp.int32, (1, 1, LANES), 2)
    first = lane < RW_HEAD_DIM
    ri = lax.broadcasted_iota(jnp.int32, (c_len, LANES), 0)
    cj = lax.broadcasted_iota(jnp.int32, (c_len, LANES), 1) % RW_HEAD_DIM
    eye = ri == cj

    def causal(mask_ref, t):
        return jnp.concatenate([jnp.where(mask_ref[0] > 0.0, t[:n_d], 0.0),
                                jnp.where(mask_ref[1] > 0.0, t[n_d:], 0.0)], axis=0)

    def stack_heads(t):
        zero = jnp.zeros_like(t)
        return jnp.concatenate([jnp.where(first, t, zero), jnp.where(first, zero, t)], axis=1)

    def bmm(x, y):
        return lax.dot_general(x.astype(BF16), y.astype(BF16), (((2,), (1,)), ((0,), (0,))),
                               preferred_element_type=F32)

    def bmm_nt(x, y):
        return lax.dot_general(x.astype(BF16), y.astype(BF16), (((2,), (2,)), ((0,), (0,))),
                               preferred_element_type=F32)

    def bmm_tn(x, y):
        return lax.dot_general(x.astype(BF16), y.astype(BF16), (((1,), (1,)), ((0,), (0,))),
                               preferred_element_type=F32)

    gm = bmm_nt(jnp.concatenate([kk_t, r_t], axis=1),
                jnp.concatenate([stack_heads(b_t), stack_heads(k_t)], axis=1))
    ab_w = causal(strict_ref, gm[:, :c_len, :LANES])
    ak_w = causal(strict_ref, gm[:, :c_len, LANES:])
    db_w = causal(incl_ref, gm[:, c_len:, :LANES])
    dk_w = causal(incl_ref, gm[:, c_len:, LANES:])

    base = RW_INV_BASE
    same = lambda s: (ri // s) == (cj // s)
    l_base = jnp.where(same(base), ab_w, 0.0)
    t_w = jnp.where(eye, 1.0, 0.0) - l_base
    pw = bmm(l_base, stack_heads(l_base))
    span = 2
    while span < base:
        t_w = t_w + bmm(t_w, stack_heads(pw))
        span *= 2
        if span < base:
            pw = bmm(pw, stack_heads(pw))
    size = base
    while size < c_len:
        off = jnp.where(jnp.logical_and(same(2 * size), jnp.logical_not(same(size))), ab_w, 0.0)
        t_w = t_w - bmm(bmm(t_w, stack_heads(off)), stack_heads(t_w))
        size *= 2

    def head_diag(t):
        return jnp.where(first, t[:, :RW_HEAD_DIM], t[:, RW_HEAD_DIM:])

    akv = bmm(jnp.concatenate([ak_w, dk_w], axis=1), stack_heads(v_b))
    pq = bmm(t_w, jnp.concatenate([stack_heads(kk_abs), stack_heads(akv[:, :c_len])], axis=2))
    dpq = bmm(db_w, jnp.concatenate([stack_heads(pq[:, :, :LANES]), stack_heads(pq[:, :, LANES:])], axis=2))
    r_hat = r_abs - dpq[:, :, :LANES]
    y_loc = akv[:, c_len:] - dpq[:, :, LANES:]
    bpq = bmm_tn(b_bar, pq)
    m_w = jnp.where(eye, jnp.broadcast_to(gam, (n_b, RW_HEAD_DIM, LANES)), 0.0) - head_diag(bpq[:, :, :LANES])
    g_w = head_diag(bmm_tn(k_bar, v_b) - bpq[:, :, LANES:])

    zs = z_s[...]
    for step in range(n_chunks):
        c_f, c_b = step, n_chunks - 1 - step
        pick = lambda t: jnp.concatenate([t[c_f * n_pairs:(c_f + 1) * n_pairs],
                                          t[n_d + c_b * n_pairs:n_d + (c_b + 1) * n_pairs]], axis=0)
        both = bmm(jnp.concatenate([pick(r_hat), pick(m_w)], axis=1), stack_heads(zs))
        y_c = both[:, :c_len] + pick(y_loc)
        yf_ref[0, c_f * c_len:(c_f + 1) * c_len, :] = jnp.concatenate([y_c[j] for j in range(n_pairs)], axis=1)
        yb_ref[0, c_b * c_len:(c_b + 1) * c_len, :] = jnp.concatenate(
            [y_c[n_pairs + j] for j in range(n_pairs)], axis=1)
        zs = both[:, c_len:] + pick(g_w)
    z_s[...] = zs


def _rwkv_scan(kk_t, r_t, b_t, k_t, em, v, *, batch, seq, ctx_len):
    width = RW_HEADS * RW_HEAD_DIM
    ta = RW_TILE
    nct = ctx_len // ta
    nxt = seq // ta
    n_x = batch * seq
    c_len = RW_CHUNK

    def blk(reverse):
        def index(b, n):
            t_ctx = (nct - 1 - n) if reverse else n
            t_x = (nxt - 1 - (n - nct)) if reverse else (n - nct)
            return jnp.where(n < nct, batch * nxt + b * nct + t_ctx, b * nxt + t_x)
        return index

    def xblk(reverse):
        def index(b, n):
            m = jnp.maximum(n, nct) - nct
            return b * nxt + ((nxt - 1 - m) if reverse else m)
        return index

    t_i = np.arange(c_len)[:, None]
    s_i = np.arange(LANES)[None, :] % RW_HEAD_DIM
    strict = jnp.asarray(np.stack([s_i < t_i, s_i > t_i]), F32)
    incl = jnp.asarray(np.stack([s_i <= t_i, s_i >= t_i]), F32)

    def operands(d):
        index = blk(d == 1)
        tile = pl.BlockSpec((1, ta, width), lambda b, n: (d, index(b, n), 0))
        return [tile, tile, tile, tile,
                pl.BlockSpec((1, 1, ta // c_len, width), lambda b, n: (d, index(b, n), 0, 0)),
                pl.BlockSpec((ta, width), lambda b, n: (index(b, n), 0))]

    def out_spec(reverse):
        index = xblk(reverse)
        return pl.BlockSpec((1, ta, width), lambda b, n: (0, index(b, n), 0))

    y_shape = jax.ShapeDtypeStruct((1, n_x, width), F32)
    return pl.pallas_call(
        _rwkv_kernel,
        grid=(batch, nct + nxt),
        in_specs=operands(0) + operands(1) + [_resident(strict.shape), _resident(incl.shape)],
        out_specs=[out_spec(False), out_spec(True)],
        out_shape=[y_shape, y_shape],
        scratch_shapes=[pltpu.VMEM((2 * (width // LANES), RW_HEAD_DIM, LANES), F32)],
        compiler_params=_cparams(2),
    )(kk_t, r_t, b_t, k_t, em, v, kk_t, r_t, b_t, k_t, em, v, strict, incl)


def _merge_kernel(x_ref, mod_ref, ona_ref, yf_ref, yb_ref, bon_ref, g_ref, pg_ref, bg_ref, ln_ref,
                  pna_ref, prw_ref, wout_ref, gn_ref, wg_hbm, wu_hbm, wd_hbm, gf_ref, o_ref,
                  wg_s, wu_s, wd_s, stage_in, stage_out, sem, *, f_chunk):
    @pl.when(pl.program_id(0) == 0)
    def _():
        _ffn_load_weights(wg_hbm, wu_hbm, wd_hbm, wg_s, wu_s, wd_s, stage_in, stage_out, sem)

    d = x_ref.shape[1]
    y = yf_ref[0] + yb_ref[0]
    inv_n = 1.0 / RW_HEAD_DIM
    mean = _head_segsum(y) * inv_n
    yc = y - mean
    var = _head_segsum(yc * yc) * inv_n
    yn = yc * lax.rsqrt(var + RW_GN_EPS) * ln_ref[0:1, :] + ln_ref[1:2, :]
    o_rw = ((yn + bon_ref[...]) * g_ref[...]).astype(BF16)
    gates = jax.nn.sigmoid(pg_ref[...] + bg_ref[...])
    m = gates[:, :d] * _dot(ona_ref[...], pna_ref[...]) + gates[:, d:] * _dot(o_rw, prw_ref[...])
    x2 = x_ref[...] + mod_ref[0, 5:6, :] * _dot(m.astype(BF16), wout_ref[...])
    o_ref[...] = _swiglu_half_step(x2, mod_ref, gn_ref, wg_s, wu_s, wd_s, gf_ref,
                                   mod0=6, f_chunk=f_chunk, final=True)


def _merge_ffn(x1, mod, o_na, y_f, y_b, bon, g, p_gate, b_gate, ln, p_na, p_rw, w_out, gn, wg, wu, wd, gf, *,
               tiles_per_seq):
    n_x, d = o_na.shape[0], x1.shape[1]
    width = o_na.shape[1]
    tok = lambda i: (i, 0)
    tok3 = lambda i: (0, i, 0)
    return pl.pallas_call(
        functools.partial(_merge_kernel, f_chunk=FFN_CHUNK),
        grid=(n_x // TM,),
        in_specs=[pl.BlockSpec((TM, d), tok),
                  pl.BlockSpec((1, N_MOD, d), lambda i: (i // tiles_per_seq, 0, 0)),
                  pl.BlockSpec((TM, width), tok),
                  pl.BlockSpec((1, TM, width), tok3), pl.BlockSpec((1, TM, width), tok3),
                  pl.BlockSpec((TM, width), tok), pl.BlockSpec((TM, width), tok),
                  pl.BlockSpec((TM, 2 * d), tok),
                  _resident((1, 2 * d)), _resident((2, width)),
                  _resident(p_na.shape), _resident(p_rw.shape), _resident(w_out.shape),
                  _resident(gn.shape), _hbm_spec(), _hbm_spec(), _hbm_spec(), _resident(gf.shape)],
        out_specs=pl.BlockSpec((TM, d), tok),
        out_shape=jax.ShapeDtypeStruct((n_x, d), F32),
        scratch_shapes=_ffn_scratch(wg, wu, wd),
        compiler_params=_cparams(1),
    )(x1, mod, o_na, y_f, y_b, bon, g, p_gate, b_gate, ln, p_na, p_rw, w_out, gn, wg, wu, wd, gf)


def _pad_lora(w_up, direction):
    z = jnp.zeros_like(w_up[0])
    return jnp.concatenate([w_up[0] if direction == 0 else z, w_up[1] if direction == 1 else z], axis=0)


def kernel(x, c, ctx, c_ctx, w_ada, b_ada, norm_ffn1, norm_mix, norm_ffn2, norm_final, ffn1_wg, ffn1_wu, ffn1_wd, ffn2_wg, ffn2_wu, ffn2_wd, w_in, b_gate, na_rpb, rw_mu, rw_w0, rw_w_up, rw_a0, rw_a_up, rw_g_up, rw_k_k, rw_k_a, rw_r_k, rw_ln_w, rw_ln_b, p_na, p_rw, w_out):
    batch, seq, d = x.shape
    ctx_len = ctx.shape[1]
    n_x = batch * seq
    na_width = NA_HEADS * NA_HEAD_DIM
    rw_width = RW_HEADS * RW_HEAD_DIM
    c_rw = 3 * na_width
    c_gate = c_rw + 3 * rw_width + 4 * RW_LORA + RW_GATE_LORA
    assert w_ada.shape[0] == 1, "single layer"
    assert seq % TM == 0 and (batch * ctx_len) % TM == 0 and ctx_len % RW_TILE == 0 and batch + 1 <= 8
    assert seq % (NA_ROWS * GRID_W) == 0 and seq // (NA_ROWS * GRID_W) >= 2 and ctx_len == 4 * GRID_W
    row = lambda t: t.reshape(1, -1)

    cs = jnp.concatenate([c, c_ctx[None], jnp.zeros((8 - batch - 1, d), F32)], axis=0)
    mod = _ada_mod(cs, w_ada[0], b_ada[0])[:batch + 1].reshape(batch + 1, N_MOD, d)

    bf = lambda t: t.astype(BF16)

    x1 = _ffn(x.reshape(n_x, d), ctx.reshape(batch * ctx_len, d), mod, row(norm_ffn1[0]),
              ffn1_wg[0], ffn1_wu[0], ffn1_wd[0], row(norm_final),
              mod0=0, tiles_per_seq=seq // TM, n_seq=batch)
    par = jnp.concatenate([rw_w0[0], rw_a0[0], row(rw_k_k[0]), row(rw_k_a[0]), row(rw_r_k[0]),
                           jnp.zeros((1, rw_width), F32)], axis=0)
    wup = bf(jnp.stack([_pad_lora(rw_w_up[0], 0), _pad_lora(rw_w_up[0], 1)]))
    aup = bf(jnp.stack([_pad_lora(rw_a_up[0], 0), _pad_lora(rw_a_up[0], 1)]))
    qkv, p_gate, kk_t, r_t, b_t, k_t, em, v_rw, g, bon = _inproj(
        x1, mod, row(norm_mix[0]), w_in[0], rw_mu[0], par, wup, aup, bf(rw_g_up[0]),
        c_rw=c_rw, c_gate=c_gate, batch=batch, seq=seq, ctx_len=ctx_len)

    bias = _na_pair_bias(na_rpb[0])
    o_na = _na_attention(qkv, bias, batch=batch, seq=seq, ctx_len=ctx_len)

    y_f, y_b = _rwkv_scan(kk_t, r_t, b_t, k_t, em, v_rw, batch=batch, seq=seq, ctx_len=ctx_len)

    ln = jnp.stack([rw_ln_w[0], rw_ln_b[0]], axis=0)
    out = _merge_ffn(x1, mod, o_na, y_f, y_b, bon, g, p_gate, row(b_gate[0]), ln,
                     bf(p_na[0]), bf(p_rw[0]), bf(w_out[0]), row(norm_ffn2[0]),
                     ffn2_wg[0], ffn2_wu[0], ffn2_wd[0], row(norm_final),
                     tiles_per_seq=seq // TM)
    return out.reshape(batch, seq, d)
```

```python
import functools
import math

import numpy as np
import jax
import jax.numpy as jnp
from jax import lax
from jax.experimental import pallas as pl
from jax.experimental.pallas import tpu as pltpu

F32 = jnp.float32
BF16 = jnp.bfloat16

NORM_EPS = 1e-6
RW_GN_EPS = 64e-5
N_MOD = 9
GRID_W = 64
NA_HEADS = 8
NA_HEAD_DIM = 64
NA_WIN_H = 8
NA_WIN_W = 16
RW_HEADS = 8
RW_HEAD_DIM = 64
RW_LORA = 64
RW_GATE_LORA = 128

LANES = 128
VMEM_LIMIT = 56 * 1024 * 1024
VMEM_LIMIT_MERGE = 60 * 1024 * 1024

TM = 512
FFN_CHUNK = 256
NA_ROWS = 8
RW_TILE = 256
RW_CHUNK = 64
RW_INV_BASE = 8
assert RW_CHUNK == RW_HEAD_DIM


def _cparams(n_axes, vmem_limit=VMEM_LIMIT):
    return pltpu.CompilerParams(dimension_semantics=("arbitrary",) * n_axes,
                                vmem_limit_bytes=vmem_limit)


def _resident(shape):
    nd = len(shape)
    return pl.BlockSpec(shape, lambda *_: (0,) * nd, pipeline_mode=pl.Buffered(1))


def _rmsnorm(x, g):
    return x * lax.rsqrt(jnp.mean(x * x, axis=-1, keepdims=True) + NORM_EPS) * g


def _dot(a, b):
    return jnp.dot(a, b, preferred_element_type=F32)


WEIGHT_CHUNKS = 16


def _hbm_spec():
    return pl.BlockSpec(memory_space=pl.ANY)


WEIGHT_SLOTS = 4


def _stage_shape(w):
    return pltpu.VMEM((WEIGHT_SLOTS, w.shape[0] // WEIGHT_CHUNKS, w.shape[1]), F32)


def _stage_sems():
    return pltpu.SemaphoreType.DMA((WEIGHT_SLOTS,))


def _fetch_as_bf16(src_hbm, dst, stage, sem):
    rows = stage.shape[1]

    def chunk_copy(c):
        slot = c % WEIGHT_SLOTS
        return pltpu.make_async_copy(src_hbm.at[c * rows:(c + 1) * rows, :], stage.at[slot], sem.at[slot])

    for c in range(WEIGHT_SLOTS):
        chunk_copy(c).start()
    for c in range(WEIGHT_CHUNKS):
        chunk_copy(c).wait()
        dst[c * rows:(c + 1) * rows, :] = stage[c % WEIGHT_SLOTS].astype(dst.dtype)
        if c + WEIGHT_SLOTS < WEIGHT_CHUNKS:
            chunk_copy(c + WEIGHT_SLOTS).start()


def _ada_kernel(c_ref, w_ref, b_ref, o_ref):
    c = c_ref[...]
    s = c * jax.nn.sigmoid(c)
    o_ref[...] = _dot(s.astype(BF16), w_ref[...].astype(BF16)) + b_ref[...]


def _ada_mod(cs, w_ada, b_ada):
    d = cs.shape[1]
    nm = w_ada.shape[1] // d
    return pl.pallas_call(
        _ada_kernel,
        grid=(nm,),
        in_specs=[pl.BlockSpec((8, d), lambda j: (0, 0)),
                  pl.BlockSpec((d, d), lambda j: (0, j)),
                  pl.BlockSpec((1, d), lambda j: (0, j))],
        out_specs=pl.BlockSpec((8, d), lambda j: (0, j)),
        out_shape=jax.ShapeDtypeStruct((8, nm * d), F32),
        compiler_params=_cparams(1),
    )(cs, w_ada, b_ada.reshape(1, -1))


def _ffn_scratch(wg, wu, wd, slots):
    d = wg.shape[0]
    return [pltpu.VMEM(wg.shape, BF16), pltpu.VMEM(wu.shape, BF16), pltpu.VMEM(wd.shape, BF16),
            pltpu.VMEM((slots, d, FFN_CHUNK), F32), pltpu.VMEM((slots, d, FFN_CHUNK), F32),
            pltpu.VMEM((slots, FFN_CHUNK, d), F32), pltpu.SemaphoreType.DMA((3, slots))]


class _FfnWeightStream:
    def __init__(self, wg_hbm, wu_hbm, wd_hbm, wg_s, wu_s, wd_s, stage_g, stage_u, stage_d, sem):
        self.src = (wg_hbm, wu_hbm, wd_hbm)
        self.dst = (wg_s, wu_s, wd_s)
        self.stage = (stage_g, stage_u, stage_d)
        self.sem = sem
        self.n = wg_hbm.shape[1] // FFN_CHUNK
        self.slots = stage_g.shape[0]

    def _copies(self, c):
        cols = slice(c * FFN_CHUNK, (c + 1) * FFN_CHUNK)
        slot = c % self.slots
        views = (self.src[0].at[:, cols], self.src[1].at[:, cols], self.src[2].at[cols, :])
        return [pltpu.make_async_copy(v, self.stage[a].at[slot], self.sem.at[a, slot])
                for a, v in enumerate(views)]

    def start(self, c):
        if c < self.n:
            for cp in self._copies(c):
                cp.start()

    def prime(self):
        for c in range(self.slots):
            self.start(c)

    def land(self, c):
        cols = slice(c * FFN_CHUNK, (c + 1) * FFN_CHUNK)
        slot = c % self.slots
        for cp in self._copies(c):
            cp.wait()
        self.dst[0][:, cols] = self.stage[0][slot].astype(BF16)
        self.dst[1][:, cols] = self.stage[1][slot].astype(BF16)
        self.dst[2][cols, :] = self.stage[2][slot].astype(BF16)
        self.start(c + self.slots)


def _ffn_kernel(x_ref, tail_ref, mod_ref, g_ref, wg_hbm, wu_hbm, wd_hbm, gf_ref, o_ref,
                wg_s, wu_s, wd_s, stage_g, stage_u, stage_d, sem, *, mod0, f_chunk, n_head_tiles):
    step = pl.program_id(0)
    x = jnp.where(step < n_head_tiles, x_ref[...], tail_ref[...])

    def body(stream):
        o_ref[...] = _swiglu_half_step(x, mod_ref, g_ref, wg_s, wu_s, wd_s, gf_ref,
                                       mod0=mod0, f_chunk=f_chunk, final=False, stream=stream)

    @pl.when(step == 0)
    def _():
        stream = _FfnWeightStream(wg_hbm, wu_hbm, wd_hbm, wg_s, wu_s, wd_s, stage_g, stage_u, stage_d, sem)
        stream.prime()
        body(stream)

    @pl.when(step > 0)
    def _():
        body(None)


def _swiglu_half_step(x, mod_ref, g_ref, wg_ref, wu_ref, wd_ref, gf_ref, *, mod0, f_chunk, final, stream=None):
    shift = mod_ref[0, mod0:mod0 + 1, :]
    scale = mod_ref[0, mod0 + 1:mod0 + 2, :]
    gate = mod_ref[0, mod0 + 2:mod0 + 3, :]
    h = (_rmsnorm(x, g_ref[...]) * (1.0 + scale) + shift).astype(BF16)
    d_ff = wg_ref.shape[1]
    bounds = [(f0, min(f0 + f_chunk, d_ff)) for f0 in range(0, d_ff, f_chunk)]

    def gate_up(c):
        if stream is not None:
            stream.land(c)
        lo, hi = bounds[c]
        return _dot(h, wg_ref[:, lo:hi]), _dot(h, wu_ref[:, lo:hi])

    acc = None
    pending = gate_up(0)
    for c, (lo, hi) in enumerate(bounds):
        gg, uu = pending
        if c + 1 < len(bounds):
            pending = gate_up(c + 1)
        a = (gg * jax.nn.sigmoid(gg) * uu).astype(BF16)
        t = _dot(a, wd_ref[lo:hi, :])
        acc = t if acc is None else acc + t
    y = x + 0.5 * gate * acc
    if final:
        y = _rmsnorm(y, gf_ref[...])
    return y


def _ffn(xs, tail, mod, g, wg, wu, wd, gf, *, mod0, tiles_per_seq, n_seq):
    n_head, d = xs.shape
    n = n_head + tail.shape[0]
    seq_of = lambda i: jnp.minimum(i // tiles_per_seq, n_seq)
    head_tiles = n_head // TM
    return pl.pallas_call(
        functools.partial(_ffn_kernel, mod0=mod0, f_chunk=FFN_CHUNK, n_head_tiles=head_tiles),
        grid=(n // TM,),
        in_specs=[pl.BlockSpec((TM, d), lambda i: (jnp.minimum(i, head_tiles - 1), 0)),
                  pl.BlockSpec((TM, d), lambda i: (jnp.maximum(i - head_tiles, 0), 0)),
                  pl.BlockSpec((1, N_MOD, d), lambda i: (seq_of(i), 0, 0)),
                  _resident((1, d)), _hbm_spec(), _hbm_spec(), _hbm_spec(), _resident((1, d))],
        out_specs=pl.BlockSpec((TM, d), lambda i: (i, 0)),
        out_shape=jax.ShapeDtypeStruct((n, d), F32),
        scratch_shapes=_ffn_scratch(wg, wu, wd, slots=2),
        compiler_params=_cparams(1),
    )(xs, tail, mod, g, wg, wu, wd, gf)


def _head_segsum(x):
    r = lax.broadcasted_iota(jnp.int32, (LANES, LANES), 0) // RW_HEAD_DIM
    c = lax.broadcasted_iota(jnp.int32, (LANES, LANES), 1) // RW_HEAD_DIM
    e = jnp.where(r == c, 1.0, 0.0).astype(BF16)
    xb = x.astype(BF16)
    return jnp.concatenate([_dot(xb[:, j:j + LANES], e) for j in range(0, x.shape[1], LANES)], axis=1)


def _inproj_kernel(x_ref, xprev_ref, xnext_ref, mod_ref, g_ref, w_hbm, mu_ref, par_ref, wup_ref, aup_ref,
                   gup_ref, tri_ref, qkv_ref, gate_ref, kk_ref, r_ref, b_ref, k_ref, em_ref, v_ref, g_out_ref,
                   bon_ref, p_s, w_ref, stage, sem, *, c_rw, c_gate, n_x_tiles, tiles_x, tiles_ctx):
    i = pl.program_id(0)

    @pl.when(i == 0)
    def _():
        _fetch_as_bf16(w_hbm, w_ref, stage, sem)

    ta = x_ref.shape[0]
    width = v_ref.shape[1]
    c_len = RW_CHUNK
    n_chunks = ta // c_len
    in_x = i < n_x_tiles
    tile = jnp.where(in_x, i % tiles_x, (i - n_x_tiles) % tiles_ctx)
    last_tile = jnp.where(in_x, tiles_x - 1, tiles_ctx - 1)

    x_all = jnp.concatenate([xprev_ref[...], x_ref[...], xnext_ref[...]], axis=0)
    h_all = (_rmsnorm(x_all, g_ref[...]) * (1.0 + mod_ref[0, 4:5, :]) + mod_ref[0, 3:4, :]).astype(BF16)
    h = h_all[8:ta + 8]
    zero = jnp.zeros((8, h.shape[1]), BF16)
    h_all = jnp.concatenate([jnp.where(tile == 0, zero, h_all[:8]), h,
                             jnp.where(tile == last_tile, zero, h_all[ta + 8:])], axis=0)

    piece = 4 * LANES
    pieces = ([(qkv_ref, c0, c0) for c0 in range(0, c_rw, piece)]
              + [(gate_ref, c0, c_gate + c0) for c0 in range(0, w_ref.shape[1] - c_gate, piece)])

    def emit_dense(count):
        for _ in range(min(count, len(pieces))):
            ref, dst, src = pieces.pop(0)
            ref[:, dst:dst + piece] = _dot(h, w_ref[:, src:src + piece]).astype(ref.dtype)

    def shifted(lo, hi):
        p_s[:, lo:hi] = _dot(h_all, w_ref[:, c_rw + lo:c_rw + hi])
        p = p_s[8:ta + 8, lo:hi]
        return (p + mu_ref[0:1, lo:hi] * (p_s[7:ta + 7, lo:hi] - p)
                + mu_ref[1:2, lo:hi] * (p_s[9:ta + 9, lo:hi] - p))

    o = 3 * width
    lora = shifted(o, o + 3 * LANES)
    k = shifted(width, 2 * width)
    r = shifted(0, width)
    v = shifted(2 * width, 3 * width)
    emit_dense(2)

    th = jnp.tanh(lora[:, :LANES]).astype(BF16)
    al = lora[:, LANES:2 * LANES].astype(BF16)
    gl = lora[:, 2 * LANES:]
    k_k = par_ref[4:5, :]
    k_a = par_ref[5:6, :]
    r_k = par_ref[6:7, :]

    v_ref[...] = v.astype(BF16)
    g_out_ref[...] = _dot(jax.nn.sigmoid(gl).astype(BF16), gup_ref[...])
    kk = k * k_k
    kk = kk * lax.rsqrt(jnp.maximum(_head_segsum(kk * kk), 1e-12))
    emit_dense(1)

    def chunks(t):
        return t.reshape(n_chunks, c_len, width)

    kd_sum = None
    for d in range(2):
        z = par_ref[d:d + 1, :] + _dot(th, wup_ref[d])
        lw = -math.exp(-0.5) * jax.nn.sigmoid(z)
        a = jax.nn.sigmoid(par_ref[2 + d:3 + d, :] + _dot(al, aup_ref[d]))
        kd = k * (1.0 + (a - 1.0) * k_a)
        kd_sum = kd if kd_sum is None else kd_sum + kd
        bb = kk * a
        emit_dense(1)
        ci3 = chunks(_dot(tri_ref[d], lw.astype(BF16)))
        ce3 = ci3 - chunks(lw)
        t_last = 0 if d == 1 else c_len - 1
        cm = 0.5 * ci3[:, t_last:t_last + 1, :]
        e_k = jnp.exp(cm - ci3)
        emit_dense(1)
        kk_ref[d] = (chunks(kk) * jnp.exp(ce3 - cm)).reshape(ta, width).astype(BF16)
        r_ref[d] = (chunks(r) * jnp.exp(ci3 - cm)).reshape(ta, width).astype(BF16)
        b_ref[d] = (chunks(bb) * e_k).reshape(ta, width).astype(BF16)
        k_ref[d] = (chunks(kd) * e_k).reshape(ta, width).astype(BF16)
        em_ref[d, 0] = jnp.exp(cm).reshape(n_chunks, width)
    emit_dense(len(pieces))
    bon_ref[...] = _head_segsum(r * kd_sum * r_k) * v


def _inproj(xs, mod, g, w_in, mu, par, wup, aup, gup, *, c_rw, c_gate, batch, seq, ctx_len):
    n, d = xs.shape
    cols = w_in.shape[1]
    width = RW_HEADS * RW_HEAD_DIM
    ta = RW_TILE
    c_len = RW_CHUNK
    n_tiles = n // ta
    tiles_x = seq // ta
    sub = ta // 8
    last8 = n // 8 - 1
    seq_of = lambda i: jnp.minimum(i // tiles_x, batch)
    idx = np.arange(ta)
    same_chunk = (idx[:, None] // c_len) == (idx[None, :] // c_len)
    tri = jnp.asarray(np.stack([same_chunk & (idx[None, :] <= idx[:, None]),
                                same_chunk & (idx[None, :] >= idx[:, None])]), BF16)
    tok2 = pl.BlockSpec((2, ta, width), lambda i: (0, i, 0))
    tok = lambda c: pl.BlockSpec((ta, c), lambda i: (i, 0))
    sds = jax.ShapeDtypeStruct
    return pl.pallas_call(
        functools.partial(_inproj_kernel, c_rw=c_rw, c_gate=c_gate, n_x_tiles=batch * tiles_x, tiles_x=tiles_x,
                          tiles_ctx=ctx_len // ta),
        grid=(n_tiles,),
        in_specs=[tok(d),
                  pl.BlockSpec((8, d), lambda i: (jnp.maximum(i * sub - 1, 0), 0)),
                  pl.BlockSpec((8, d), lambda i: (jnp.minimum((i + 1) * sub, last8), 0)),
                  pl.BlockSpec((1, N_MOD, d), lambda i: (seq_of(i), 0, 0)),
                  _resident((1, d)), _hbm_spec(),
                  _resident(mu.shape), _resident(par.shape), _resident(wup.shape), _resident(aup.shape),
                  _resident(gup.shape), _resident(tri.shape)],
        out_specs=[tok(c_rw), tok(cols - c_gate), tok2, tok2, tok2, tok2,
                   pl.BlockSpec((2, 1, ta // c_len, width), lambda i: (0, i, 0, 0)),
                   tok(width), tok(width), tok(width)],
        out_shape=[sds((n, c_rw), BF16), sds((n, cols - c_gate), BF16)]
                  + [sds((2, n, width), BF16)] * 4
                  + [sds((2, n_tiles, ta // c_len, width), F32), sds((n, width), BF16),
                     sds((n, width), F32), sds((n, width), F32)],
        scratch_shapes=[pltpu.VMEM((ta + 16, c_gate - c_rw), F32), pltpu.VMEM(w_in.shape, BF16),
                        _stage_shape(w_in), _stage_sems()],
        compiler_params=_cparams(1),
    )(xs, xs, xs, mod, g, w_in, mu, par, wup, aup, gup, tri)


LOG2E = math.log2(math.e)


def _na_pair_bias(rpb):
    w = GRID_W
    qc = np.arange(w)[:, None]
    kc = np.arange(w)[None, :]
    s_c = np.clip(qc - NA_WIN_W // 2, 0, w - NA_WIN_W)
    col_ok = (kc >= s_c) & (kc < s_c + NA_WIN_W)
    dc = np.clip(kc - qc + NA_WIN_W - 1, 0, 2 * NA_WIN_W - 2)
    onehot = jnp.asarray(dc[None] == np.arange(2 * NA_WIN_W - 1)[:, None, None], F32)
    b = jnp.einsum('hrd,dqk->hrqk', rpb.astype(F32), onehot, precision=lax.Precision.HIGHEST)
    b = jnp.where(col_ok[None, None], b * LOG2E, -jnp.inf)
    ninf = jnp.full((rpb.shape[0], 1, w, w), -jnp.inf, F32)
    b = jnp.concatenate([ninf, b, ninf], axis=1)
    n_e = 2 * NA_WIN_H
    return pl.pallas_call(
        _na_pair_kernel,
        grid=(rpb.shape[0],),
        in_specs=[pl.BlockSpec((1, n_e + 1, w, w), lambda h: (h, 0, 0, 0))],
        out_specs=pl.BlockSpec((2, 1, n_e, w, 2 * w), lambda h: (0, h, 0, 0, 0)),
        out_shape=jax.ShapeDtypeStruct((2, rpb.shape[0], n_e, w, 2 * w), F32),
        compiler_params=_cparams(1),
    )(b)


def _na_pair_kernel(b_ref, o_ref):
    b = b_ref[0]
    pair = jnp.concatenate([b[:-1], b[1:]], axis=-1)
    o_ref[0, 0] = pair
    dr = (lax.broadcasted_iota(jnp.int32, pair.shape, 0) - 1
          + (lax.broadcasted_iota(jnp.int32, pair.shape, 2) >= GRID_W).astype(jnp.int32))
    first_dr = NA_WIN_H // 2 - 1
    inside = jnp.logical_and(dr >= first_dr, dr < first_dr + NA_WIN_H)
    o_ref[1, 0] = jnp.where(inside, pair, -jnp.inf)


def _na_kernel(q_ref, k0_ref, k1_ref, k2_ref, k3_ref, kc_ref, v0_ref, v1_ref, v2_ref, v3_ref, vc_ref,
               bias_ref, o_ref, kcat, vcat, *, n_blocks):
    j = pl.program_id(2)
    blk = k0_ref.shape[0]
    for s, (kr, vr) in enumerate(((k0_ref, v0_ref), (k1_ref, v1_ref), (k2_ref, v2_ref), (k3_ref, v3_ref),
                                  (kc_ref, vc_ref))):
        kcat[s * blk:s * blk + kr.shape[0], :] = kr[...]
        vcat[s * blk:s * blk + vr.shape[0], :] = vr[...]
    n_loc = 4 * blk
    w = GRID_W
    half = NA_WIN_H // 2
    sub_rows = NA_ROWS // 2
    sub_pairs = (sub_rows + NA_WIN_H) // 2
    tq = sub_rows * w
    lane = lax.broadcasted_iota(jnp.int32, (1, LANES), 1)
    first = lane < NA_HEAD_DIM
    nt = (((1,), (1,)), ((), ()))

    items = [(pp, h, sb) for pp in range(q_ref.shape[1] // LANES) for sb in range(2) for h in range(2)]

    def scores(item):
        pp, h, sb = item
        cols = slice(pp * LANES, (pp + 1) * LANES)
        q = (q_ref[sb * tq:(sb + 1) * tq, cols].astype(F32) * (NA_HEAD_DIM ** -0.5 * LOG2E)).astype(BF16)
        qh = jnp.where(first if h == 0 else jnp.logical_not(first), q, jnp.zeros_like(q))
        k0 = 2 * sb * LANES
        s_loc = lax.dot_general(qh, kcat[k0:k0 + sub_pairs * LANES, cols], nt, preferred_element_type=F32)
        s_ctx = lax.dot_general(qh, kcat[n_loc:, cols], nt, preferred_element_type=F32)
        return s_loc, s_ctx

    def attend(item, s_loc, s_ctx, interior):
        pp, h, sb = item
        cols = slice(pp * LANES, (pp + 1) * LANES)
        p_rows = []
        l_rows = []
        for ql in range(sub_rows):
            qi = sb * sub_rows + ql
            if interior:
                m_lo, m_hi = qi // 2, (qi + NA_WIN_H - 1) // 2
            else:
                m_lo, m_hi = min(qi, half) // 2, (max(qi, half) + NA_WIN_H - 1) // 2
            bias = jnp.concatenate(
                [bias_ref[1 if interior else 0, 2 * pp + h, min(max(2 * m - qi + half, 0), 2 * NA_WIN_H - 1)]
                 for m in range(m_lo, m_hi + 1)], axis=1)
            rows = slice(ql * w, (ql + 1) * w)
            t_lo, t_hi = m_lo - 2 * sb, m_hi - 2 * sb
            s_q = s_loc[rows, t_lo * LANES:(t_hi + 1) * LANES] + bias
            if not interior:
                lo = jnp.where(j == 0, max(qi, half), min(qi, half))
                key_row = 2 * m_lo + lax.broadcasted_iota(jnp.int32, (1, (m_hi - m_lo + 1) * LANES), 1) // w
                s_q = jnp.where((key_row >= lo) & (key_row < lo + NA_WIN_H), s_q, -jnp.inf)
            s_c = s_ctx[rows]
            mx = jnp.maximum(jnp.max(s_q, axis=-1, keepdims=True), jnp.max(s_c, axis=-1, keepdims=True))
            p_q = jnp.exp2(s_q - mx)
            p_c = jnp.exp2(s_c - mx)
            l_rows.append(jnp.sum(p_q, axis=-1, keepdims=True) + jnp.sum(p_c, axis=-1, keepdims=True))
            pieces = []
            if t_lo > 0:
                pieces.append(jnp.zeros((w, t_lo * LANES), BF16))
            pieces.append(p_q.astype(BF16))
            if t_hi < sub_pairs - 1:
                pieces.append(jnp.zeros((w, (sub_pairs - 1 - t_hi) * LANES), BF16))
            pieces.append(p_c.astype(BF16))
            p_rows.append(jnp.concatenate(pieces, axis=1))
        p = jnp.concatenate(p_rows, axis=0)
        l = jnp.concatenate(l_rows, axis=0)
        k0 = 2 * sb * LANES
        o = (_dot(p[:, :sub_pairs * LANES], vcat[k0:k0 + sub_pairs * LANES, cols])
             + _dot(p[:, sub_pairs * LANES:], vcat[n_loc:, cols]))
        return o / l

    def run(interior):
        pending = scores(items[0])
        outs = {}
        for n, item in enumerate(items):
            current = pending
            if n + 1 < len(items):
                pending = scores(items[n + 1])
            outs[item] = attend(item, *current, interior)
            pp, h, sb = item
            if h == 1:
                o_ref[sb * tq:(sb + 1) * tq, pp * LANES:(pp + 1) * LANES] = jnp.where(
                    first, outs[(pp, 0, sb)], outs[(pp, 1, sb)]).astype(o_ref.dtype)

    at_edge = jnp.logical_or(j == 0, j == n_blocks - 1)
    pl.when(at_edge)(lambda: run(False))
    pl.when(jnp.logical_not(at_edge))(lambda: run(True))


def _na_attention(qkv, bias, *, batch, seq, ctx_len):
    rows = seq // GRID_W
    nj = rows // NA_ROWS
    tq = NA_ROWS * GRID_W
    tk = 4 * GRID_W
    width = NA_HEADS * NA_HEAD_DIM
    n_pairs = width // LANES
    kblocks = seq // tk
    ctx_blk0 = (batch * seq) // ctx_len

    lanes = 2 * LANES
    n_groups = width // lanes

    def kv_spec(slot, col0):
        def imap(hg, b, j):
            return (b * kblocks + jnp.clip(2 * j - 1 + slot, 0, kblocks - 1), col0 + hg)
        return pl.BlockSpec((tk, lanes), imap)

    def ctx_spec(col0):
        return pl.BlockSpec((ctx_len, lanes), lambda hg, b, j: (ctx_blk0 + b, col0 + hg))

    in_specs = ([pl.BlockSpec((tq, lanes), lambda hg, b, j: (b * nj + j, hg))]
                + [kv_spec(s, n_groups) for s in range(4)] + [ctx_spec(n_groups)]
                + [kv_spec(s, 2 * n_groups) for s in range(4)] + [ctx_spec(2 * n_groups)]
                + [pl.BlockSpec((2, 4) + bias.shape[2:], lambda hg, b, j: (0, hg, 0, 0, 0))])
    return pl.pallas_call(
        functools.partial(_na_kernel, n_blocks=nj),
        grid=(n_groups, batch, nj),
        in_specs=in_specs,
        out_specs=pl.BlockSpec((tq, lanes), lambda hg, b, j: (b * nj + j, hg)),
        out_shape=jax.ShapeDtypeStruct((batch * seq, width), BF16),
        scratch_shapes=[pltpu.VMEM((4 * tk + ctx_len, lanes), BF16),
                        pltpu.VMEM((4 * tk + ctx_len, lanes), BF16)],
        compiler_params=_cparams(3),
    )(*([qkv] * 11), bias)


def _rwkv_kernel(kkf_ref, rf_ref, bf_ref, kf_ref, emf_ref, vf_ref, kkb_ref, rb_ref, bb_ref, kb_ref, emb_ref, vb_ref,
                 strict_ref, incl_ref, yf_ref, yb_ref, z_s):
    n = pl.program_id(1)
    ta = vf_ref.shape[0]
    width = vf_ref.shape[1]
    n_pairs = width // LANES
    c_len = RW_CHUNK
    n_chunks = ta // c_len
    n_d = n_chunks * n_pairs
    n_b = 2 * n_d

    @pl.when(n == 0)
    def _():
        z_s[...] = jnp.zeros_like(z_s)

    def chunks(t):
        return t.reshape(n_chunks, c_len, width)

    def to_batch(t):
        parts = [t[:, :, j * LANES:(j + 1) * LANES] for j in range(n_pairs)]
        return jnp.stack(parts, axis=1).reshape(n_d, t.shape[1], LANES)

    def load_direction(kk_ref, r_ref, b_ref, k_ref, em_ref, v_ref):
        em = em_ref[0, 0].reshape(n_chunks, 1, width)
        kk_t = chunks(kk_ref[0].astype(F32))
        r_t = chunks(r_ref[0].astype(F32))
        b_t = chunks(b_ref[0].astype(F32))
        k_t = chunks(k_ref[0].astype(F32))
        return (to_batch(kk_t * em),
                to_batch(r_t * em),
                to_batch(k_t * em),
                to_batch(b_t * em),
                to_batch(em * em),
                to_batch(kk_t), to_batch(r_t), to_batch(b_t), to_batch(k_t),
                to_batch(chunks(v_ref[...].astype(F32))))

    fwd = load_direction(kkf_ref, rf_ref, bf_ref, kf_ref, emf_ref, vf_ref)
    bwd = load_direction(kkb_ref, rb_ref, bb_ref, kb_ref, emb_ref, vb_ref)
    kk_abs, r_abs, k_bar, b_bar, gam, kk_t, r_t, b_t, k_t, v_b = [
        jnp.concatenate([f, b], axis=0) for f, b in zip(fwd, bwd)]

    lane = lax.broadcasted_iota(jnp.int32, (1, 1, LANES), 2)
    first = lane < RW_HEAD_DIM
    ri = lax.broadcasted_iota(jnp.int32, (c_len, LANES), 0)
    cj = lax.broadcasted_iota(jnp.int32, (c_len, LANES), 1) % RW_HEAD_DIM
    eye = ri == cj

    def causal(mask_ref, t):
        return jnp.concatenate([jnp.where(mask_ref[0] > 0.0, t[:n_d], 0.0),
                                jnp.where(mask_ref[1] > 0.0, t[n_d:], 0.0)], axis=0)

    def stack_heads(t):
        zero = jnp.zeros_like(t)
        return jnp.concatenate([jnp.where(first, t, zero), jnp.where(first, zero, t)], axis=1)

    def bmm(x, y):
        return lax.dot_general(x.astype(BF16), y.astype(BF16), (((2,), (1,)), ((0,), (0,))),
                               preferred_element_type=F32)

    def bmm_nt(x, y):
        return lax.dot_general(x.astype(BF16), y.astype(BF16), (((2,), (2,)), ((0,), (0,))),
                               preferred_element_type=F32)

    def bmm_tn(x, y):
        return lax.dot_general(x.astype(BF16), y.astype(BF16), (((1,), (1,)), ((0,), (0,))),
                               preferred_element_type=F32)

    gm = bmm_nt(jnp.concatenate([kk_t, r_t], axis=1),
                jnp.concatenate([stack_heads(b_t), stack_heads(k_t)], axis=1))
    ab_w = causal(strict_ref, gm[:, :c_len, :LANES])
    ak_w = causal(strict_ref, gm[:, :c_len, LANES:])
    db_w = causal(incl_ref, gm[:, c_len:, :LANES])
    dk_w = causal(incl_ref, gm[:, c_len:, LANES:])

    base = RW_INV_BASE
    same = lambda s: (ri // s) == (cj // s)
    l_base = jnp.where(same(base), ab_w, 0.0)
    t_w = jnp.where(eye, 1.0, 0.0) - l_base
    pw = bmm(l_base, stack_heads(l_base))
    span = 2
    while span < base:
        t_w = t_w + bmm(t_w, stack_heads(pw))
        span *= 2
        if span < base:
            pw = bmm(pw, stack_heads(pw))
    size = base
    while size < c_len:
        off = jnp.where(jnp.logical_and(same(2 * size), jnp.logical_not(same(size))), ab_w, 0.0)
        t_w = t_w - bmm(bmm(t_w, stack_heads(off)), stack_heads(t_w))
        size *= 2

    def head_diag(t):
        return jnp.where(first, t[:, :RW_HEAD_DIM], t[:, RW_HEAD_DIM:])

    akv = bmm(jnp.concatenate([ak_w, dk_w], axis=1), stack_heads(v_b))
    pq = bmm(t_w, jnp.concatenate([stack_heads(kk_abs), stack_heads(akv[:, :c_len])], axis=2))
    dpq = bmm(db_w, jnp.concatenate([stack_heads(pq[:, :, :LANES]), stack_heads(pq[:, :, LANES:])], axis=2))
    r_hat = r_abs - dpq[:, :, :LANES]
    y_loc = akv[:, c_len:] - dpq[:, :, LANES:]
    bpq = bmm_tn(b_bar, pq)
    m_w = jnp.where(eye, jnp.broadcast_to(gam, (n_b, RW_HEAD_DIM, LANES)), 0.0) - head_diag(bpq[:, :, :LANES])
    g_w = head_diag(bmm_tn(k_bar, v_b) - bpq[:, :, LANES:])

    zs = z_s[...]
    for step in range(n_chunks):
        c_f, c_b = step, n_chunks - 1 - step
        pick = lambda t: jnp.concatenate([t[c_f * n_pairs:(c_f + 1) * n_pairs],
                                          t[n_d + c_b * n_pairs:n_d + (c_b + 1) * n_pairs]], axis=0)
        both = bmm(jnp.concatenate([pick(r_hat), pick(m_w)], axis=1), stack_heads(zs))
        y_c = both[:, :c_len] + pick(y_loc)
        yf_ref[0, c_f * c_len:(c_f + 1) * c_len, :] = jnp.concatenate([y_c[j] for j in range(n_pairs)], axis=1)
        yb_ref[0, c_b * c_len:(c_b + 1) * c_len, :] = jnp.concatenate(
            [y_c[n_pairs + j] for j in range(n_pairs)], axis=1)
        zs = both[:, c_len:] + pick(g_w)
    z_s[...] = zs


def _rwkv_scan(kk_t, r_t, b_t, k_t, em, v, *, batch, seq, ctx_len):
    width = RW_HEADS * RW_HEAD_DIM
    ta = RW_TILE
    nct = ctx_len // ta
    nxt = seq // ta
    n_x = batch * seq
    c_len = RW_CHUNK

    def blk(reverse):
        def index(b, n):
            t_ctx = (nct - 1 - n) if reverse else n
            t_x = (nxt - 1 - (n - nct)) if reverse else (n - nct)
            return jnp.where(n < nct, batch * nxt + b * nct + t_ctx, b * nxt + t_x)
        return index

    def xblk(reverse):
        def index(b, n):
            m = jnp.maximum(n, nct) - nct
            return b * nxt + ((nxt - 1 - m) if reverse else m)
        return index

    t_i = np.arange(c_len)[:, None]
    s_i = np.arange(LANES)[None, :] % RW_HEAD_DIM
    strict = jnp.asarray(np.stack([s_i < t_i, s_i > t_i]), F32)
    incl = jnp.asarray(np.stack([s_i <= t_i, s_i >= t_i]), F32)

    def operands(d):
        index = blk(d == 1)
        tile = pl.BlockSpec((1, ta, width), lambda b, n: (d, index(b, n), 0))
        return [tile, tile, tile, tile,
                pl.BlockSpec((1, 1, ta // c_len, width), lambda b, n: (d, index(b, n), 0, 0)),
                pl.BlockSpec((ta, width), lambda b, n: (index(b, n), 0))]

    def out_spec(reverse):
        index = xblk(reverse)
        return pl.BlockSpec((1, ta, width), lambda b, n: (0, index(b, n), 0))

    y_shape = jax.ShapeDtypeStruct((1, n_x, width), F32)
    return pl.pallas_call(
        _rwkv_kernel,
        grid=(batch, nct + nxt),
        in_specs=operands(0) + operands(1) + [_resident(strict.shape), _resident(incl.shape)],
        out_specs=[out_spec(False), out_spec(True)],
        out_shape=[y_shape, y_shape],
        scratch_shapes=[pltpu.VMEM((2 * (width // LANES), RW_HEAD_DIM, LANES), F32)],
        compiler_params=_cparams(2),
    )(kk_t, r_t, b_t, k_t, em, v, kk_t, r_t, b_t, k_t, em, v, strict, incl)


def _merge_kernel(x_ref, mod_ref, ona_ref, yf_ref, yb_ref, bon_ref, g_ref, pg_ref, bg_ref, ln_ref,
                  pna_ref, prw_ref, wout_ref, gn_ref, wg_hbm, wu_hbm, wd_hbm, gf_ref, o_ref,
                  wg_s, wu_s, wd_s, stage_g, stage_u, stage_d, sem, *, f_chunk):
    step = pl.program_id(0)
    stream = _FfnWeightStream(wg_hbm, wu_hbm, wd_hbm, wg_s, wu_s, wd_s, stage_g, stage_u, stage_d, sem)

    pl.when(step == 0)(stream.prime)

    d = x_ref.shape[1]
    y = yf_ref[0] + yb_ref[0]
    inv_n = 1.0 / RW_HEAD_DIM
    mean = _head_segsum(y) * inv_n
    yc = y - mean
    var = _head_segsum(yc * yc) * inv_n
    yn = yc * lax.rsqrt(var + RW_GN_EPS) * ln_ref[0:1, :] + ln_ref[1:2, :]
    o_rw = ((yn + bon_ref[...]) * g_ref[...]).astype(BF16)
    gates = jax.nn.sigmoid(pg_ref[...] + bg_ref[...])
    m = gates[:, :d] * _dot(ona_ref[...], pna_ref[...]) + gates[:, d:] * _dot(o_rw, prw_ref[...])
    x2 = x_ref[...] + mod_ref[0, 5:6, :] * _dot(m.astype(BF16), wout_ref[...])

    def closing_step(stream_or_none):
        o_ref[...] = _swiglu_half_step(x2, mod_ref, gn_ref, wg_s, wu_s, wd_s, gf_ref,
                                       mod0=6, f_chunk=f_chunk, final=True, stream=stream_or_none)

    pl.when(step == 0)(lambda: closing_step(stream))
    pl.when(step > 0)(lambda: closing_step(None))


def _merge_ffn(x1, mod, o_na, y_f, y_b, bon, g, p_gate, b_gate, ln, p_na, p_rw, w_out, gn, wg, wu, wd, gf, *,
               tiles_per_seq):
    n_x, d = o_na.shape[0], x1.shape[1]
    width = o_na.shape[1]
    tok = lambda i: (i, 0)
    tok3 = lambda i: (0, i, 0)
    return pl.pallas_call(
        functools.partial(_merge_kernel, f_chunk=FFN_CHUNK),
        grid=(n_x // TM,),
        in_specs=[pl.BlockSpec((TM, d), tok),
                  pl.BlockSpec((1, N_MOD, d), lambda i: (i // tiles_per_seq, 0, 0)),
                  pl.BlockSpec((TM, width), tok),
                  pl.BlockSpec((1, TM, width), tok3), pl.BlockSpec((1, TM, width), tok3),
                  pl.BlockSpec((TM, width), tok), pl.BlockSpec((TM, width), tok),
                  pl.BlockSpec((TM, 2 * d), tok),
                  _resident((1, 2 * d)), _resident((2, width)),
                  _resident(p_na.shape), _resident(p_rw.shape), _resident(w_out.shape),
                  _resident(gn.shape), _hbm_spec(), _hbm_spec(), _hbm_spec(), _resident(gf.shape)],
        out_specs=pl.BlockSpec((TM, d), tok),
        out_shape=jax.ShapeDtypeStruct((n_x, d), F32),
        scratch_shapes=_ffn_scratch(wg, wu, wd, slots=1),
        compiler_params=_cparams(1, VMEM_LIMIT_MERGE),
    )(x1, mod, o_na, y_f, y_b, bon, g, p_gate, b_gate, ln, p_na, p_rw, w_out, gn, wg, wu, wd, gf)


def _pad_lora(w_up, direction):
    z = jnp.zeros_like(w_up[0])
    return jnp.concatenate([w_up[0] if direction == 0 else z, w_up[1] if direction == 1 else z], axis=0)


def kernel(x, c, ctx, c_ctx, w_ada, b_ada, norm_ffn1, norm_mix, norm_ffn2, norm_final, ffn1_wg, ffn1_wu, ffn1_wd, ffn2_wg, ffn2_wu, ffn2_wd, w_in, b_gate, na_rpb, rw_mu, rw_w0, rw_w_up, rw_a0, rw_a_up, rw_g_up, rw_k_k, rw_k_a, rw_r_k, rw_ln_w, rw_ln_b, p_na, p_rw, w_out):
    batch, seq, d = x.shape
    ctx_len = ctx.shape[1]
    n_x = batch * seq
    na_width = NA_HEADS * NA_HEAD_DIM
    rw_width = RW_HEADS * RW_HEAD_DIM
    c_rw = 3 * na_width
    c_gate = c_rw + 3 * rw_width + 4 * RW_LORA + RW_GATE_LORA
    assert w_ada.shape[0] == 1, "single layer"
    assert seq % TM == 0 and (batch * ctx_len) % TM == 0 and ctx_len % RW_TILE == 0 and batch + 1 <= 8
    assert seq % (NA_ROWS * GRID_W) == 0 and seq // (NA_ROWS * GRID_W) >= 2 and ctx_len == 4 * GRID_W
    row = lambda t: t.reshape(1, -1)

    cs = jnp.concatenate([c, c_ctx[None], jnp.zeros((8 - batch - 1, d), F32)], axis=0)
    mod = _ada_mod(cs, w_ada[0], b_ada[0])[:batch + 1].reshape(batch + 1, N_MOD, d)

    bf = lambda t: t.astype(BF16)

    x1 = _ffn(x.reshape(n_x, d), ctx.reshape(batch * ctx_len, d), mod, row(norm_ffn1[0]),
              ffn1_wg[0], ffn1_wu[0], ffn1_wd[0], row(norm_final),
              mod0=0, tiles_per_seq=seq // TM, n_seq=batch)
    par = jnp.concatenate([rw_w0[0], rw_a0[0], row(rw_k_k[0]), row(rw_k_a[0]), row(rw_r_k[0]),
                           jnp.zeros((1, rw_width), F32)], axis=0)
    wup = bf(jnp.stack([_pad_lora(rw_w_up[0], 0), _pad_lora(rw_w_up[0], 1)]))
    aup = bf(jnp.stack([_pad_lora(rw_a_up[0], 0), _pad_lora(rw_a_up[0], 1)]))
    qkv, p_gate, kk_t, r_t, b_t, k_t, em, v_rw, g, bon = _inproj(
        x1, mod, row(norm_mix[0]), w_in[0], rw_mu[0], par, wup, aup, bf(rw_g_up[0]),
        c_rw=c_rw, c_gate=c_gate, batch=batch, seq=seq, ctx_len=ctx_len)

    bias = _na_pair_bias(na_rpb[0])
    o_na = _na_attention(qkv, bias, batch=batch, seq=seq, ctx_len=ctx_len)

    y_f, y_b = _rwkv_scan(kk_t, r_t, b_t, k_t, em, v_rw, batch=batch, seq=seq, ctx_len=ctx_len)

    ln = jnp.stack([rw_ln_w[0], rw_ln_b[0]], axis=0)
    out = _merge_ffn(x1, mod, o_na, y_f, y_b, bon, g, p_gate, row(b_gate[0]), ln,
                     bf(p_na[0]), bf(p_rw[0]), bf(w_out[0]), row(norm_ffn2[0]),
                     ffn2_wg[0], ffn2_wu[0], ffn2_wd[0], row(norm_final),
                     tiles_per_seq=seq // TM)
    return out.reshape(batch, seq, d)
```

```python
import functools
import math

import numpy as np
import jax
import jax.numpy as jnp
from jax import lax
from jax.experimental import pallas as pl
from jax.experimental.pallas import tpu as pltpu

F32 = jnp.float32
BF16 = jnp.bfloat16

NORM_EPS = 1e-6
RW_GN_EPS = 64e-5
N_MOD = 9
GRID_W = 64
NA_HEADS = 8
NA_HEAD_DIM = 64
NA_WIN_H = 8
NA_WIN_W = 16
RW_HEADS = 8
RW_HEAD_DIM = 64
RW_LORA = 64
RW_GATE_LORA = 128

LANES = 128
VMEM_LIMIT = 56 * 1024 * 1024

TM = 512
FFN_CHUNK = 256
NA_ROWS = 8
RW_TILE = 256
RW_CHUNK = 64
RW_INV_BASE = 8
assert RW_CHUNK == RW_HEAD_DIM


def _cparams(n_axes):
    return pltpu.CompilerParams(dimension_semantics=("arbitrary",) * n_axes,
                                vmem_limit_bytes=VMEM_LIMIT)


def _resident(shape):
    nd = len(shape)
    return pl.BlockSpec(shape, lambda *_: (0,) * nd, pipeline_mode=pl.Buffered(1))


def _rmsnorm(x, g):
    return x * lax.rsqrt(jnp.mean(x * x, axis=-1, keepdims=True) + NORM_EPS) * g


def _dot(a, b):
    return jnp.dot(a, b, preferred_element_type=F32)


WEIGHT_CHUNKS = 16


def _hbm_spec():
    return pl.BlockSpec(memory_space=pl.ANY)


WEIGHT_SLOTS = 4


def _stage_shape(w):
    return pltpu.VMEM((WEIGHT_SLOTS, w.shape[0] // WEIGHT_CHUNKS, w.shape[1]), F32)


def _stage_sems():
    return pltpu.SemaphoreType.DMA((WEIGHT_SLOTS,))


def _fetch_as_bf16(src_hbm, dst, stage, sem):
    rows = stage.shape[1]

    def chunk_copy(c):
        slot = c % WEIGHT_SLOTS
        return pltpu.make_async_copy(src_hbm.at[c * rows:(c + 1) * rows, :], stage.at[slot], sem.at[slot])

    for c in range(WEIGHT_SLOTS):
        chunk_copy(c).start()
    for c in range(WEIGHT_CHUNKS):
        chunk_copy(c).wait()
        dst[c * rows:(c + 1) * rows, :] = stage[c % WEIGHT_SLOTS].astype(dst.dtype)
        if c + WEIGHT_SLOTS < WEIGHT_CHUNKS:
            chunk_copy(c + WEIGHT_SLOTS).start()


def _ada_kernel(c_ref, w_ref, b_ref, o_ref):
    c = c_ref[...]
    s = c * jax.nn.sigmoid(c)
    o_ref[...] = _dot(s.astype(BF16), w_ref[...].astype(BF16)) + b_ref[...]


def _ada_mod(cs, w_ada, b_ada):
    d = cs.shape[1]
    nm = w_ada.shape[1] // d
    return pl.pallas_call(
        _ada_kernel,
        grid=(nm,),
        in_specs=[pl.BlockSpec((8, d), lambda j: (0, 0)),
                  pl.BlockSpec((d, d), lambda j: (0, j)),
                  pl.BlockSpec((1, d), lambda j: (0, j))],
        out_specs=pl.BlockSpec((8, d), lambda j: (0, j)),
        out_shape=jax.ShapeDtypeStruct((8, nm * d), F32),
        compiler_params=_cparams(1),
    )(cs, w_ada, b_ada.reshape(1, -1))


def _ffn_scratch(wg, wu, wd):
    return [pltpu.VMEM(wg.shape, BF16), pltpu.VMEM(wu.shape, BF16), pltpu.VMEM(wd.shape, BF16),
            _stage_shape(wg), _stage_shape(wd), _stage_sems()]


def _ffn_load_weights(wg_hbm, wu_hbm, wd_hbm, wg_s, wu_s, wd_s, stage_in, stage_out, sem):
    _fetch_as_bf16(wg_hbm, wg_s, stage_in, sem)
    _fetch_as_bf16(wu_hbm, wu_s, stage_in, sem)
    _fetch_as_bf16(wd_hbm, wd_s, stage_out, sem)


def _ffn_kernel(x_ref, tail_ref, mod_ref, g_ref, wg_hbm, wu_hbm, wd_hbm, gf_ref, o_ref,
                wg_s, wu_s, wd_s, stage_in, stage_out, sem, *, mod0, f_chunk, n_head_tiles):
    @pl.when(pl.program_id(0) == 0)
    def _():
        _ffn_load_weights(wg_hbm, wu_hbm, wd_hbm, wg_s, wu_s, wd_s, stage_in, stage_out, sem)

    x = jnp.where(pl.program_id(0) < n_head_tiles, x_ref[...], tail_ref[...])
    o_ref[...] = _swiglu_half_step(x, mod_ref, g_ref, wg_s, wu_s, wd_s, gf_ref,
                                   mod0=mod0, f_chunk=f_chunk, final=False)


def _swiglu_half_step(x, mod_ref, g_ref, wg_ref, wu_ref, wd_ref, gf_ref, *, mod0, f_chunk, final):
    shift = mod_ref[0, mod0:mod0 + 1, :]
    scale = mod_ref[0, mod0 + 1:mod0 + 2, :]
    gate = mod_ref[0, mod0 + 2:mod0 + 3, :]
    h = (_rmsnorm(x, g_ref[...]) * (1.0 + scale) + shift).astype(BF16)
    d_ff = wg_ref.shape[1]
    bounds = [(f0, min(f0 + f_chunk, d_ff)) for f0 in range(0, d_ff, f_chunk)]

    def gate_up(lo, hi):
        return _dot(h, wg_ref[:, lo:hi]), _dot(h, wu_ref[:, lo:hi])

    acc = None
    pending = gate_up(*bounds[0])
    for c, (lo, hi) in enumerate(bounds):
        gg, uu = pending
        if c + 1 < len(bounds):
            pending = gate_up(*bounds[c + 1])
        a = (gg * jax.nn.sigmoid(gg) * uu).astype(BF16)
        t = _dot(a, wd_ref[lo:hi, :])
        acc = t if acc is None else acc + t
    y = x + 0.5 * gate * acc
    if final:
        y = _rmsnorm(y, gf_ref[...])
    return y


def _ffn(xs, tail, mod, g, wg, wu, wd, gf, *, mod0, tiles_per_seq, n_seq):
    n_head, d = xs.shape
    n = n_head + tail.shape[0]
    seq_of = lambda i: jnp.minimum(i // tiles_per_seq, n_seq)
    head_tiles = n_head // TM
    return pl.pallas_call(
        functools.partial(_ffn_kernel, mod0=mod0, f_chunk=FFN_CHUNK, n_head_tiles=head_tiles),
        grid=(n // TM,),
        in_specs=[pl.BlockSpec((TM, d), lambda i: (jnp.minimum(i, head_tiles - 1), 0)),
                  pl.BlockSpec((TM, d), lambda i: (jnp.maximum(i - head_tiles, 0), 0)),
                  pl.BlockSpec((1, N_MOD, d), lambda i: (seq_of(i), 0, 0)),
                  _resident((1, d)), _hbm_spec(), _hbm_spec(), _hbm_spec(), _resident((1, d))],
        out_specs=pl.BlockSpec((TM, d), lambda i: (i, 0)),
        out_shape=jax.ShapeDtypeStruct((n, d), F32),
        scratch_shapes=_ffn_scratch(wg, wu, wd),
        compiler_params=_cparams(1),
    )(xs, tail, mod, g, wg, wu, wd, gf)


def _head_segsum(x):
    r = lax.broadcasted_iota(jnp.int32, (LANES, LANES), 0) // RW_HEAD_DIM
    c = lax.broadcasted_iota(jnp.int32, (LANES, LANES), 1) // RW_HEAD_DIM
    e = jnp.where(r == c, 1.0, 0.0).astype(BF16)
    xb = x.astype(BF16)
    return jnp.concatenate([_dot(xb[:, j:j + LANES], e) for j in range(0, x.shape[1], LANES)], axis=1)


def _inproj_kernel(x_ref, xprev_ref, xnext_ref, mod_ref, g_ref, w_hbm, mu_ref, par_ref, wup_ref, aup_ref,
                   gup_ref, tri_ref, qkv_ref, gate_ref, kk_ref, r_ref, b_ref, k_ref, em_ref, v_ref, g_out_ref,
                   bon_ref, p_s, w_ref, stage, sem, *, c_rw, c_gate, n_x_tiles, tiles_x, tiles_ctx):
    i = pl.program_id(0)

    @pl.when(i == 0)
    def _():
        _fetch_as_bf16(w_hbm, w_ref, stage, sem)

    ta = x_ref.shape[0]
    width = v_ref.shape[1]
    c_len = RW_CHUNK
    n_chunks = ta // c_len
    in_x = i < n_x_tiles
    tile = jnp.where(in_x, i % tiles_x, (i - n_x_tiles) % tiles_ctx)
    last_tile = jnp.where(in_x, tiles_x - 1, tiles_ctx - 1)

    x_all = jnp.concatenate([xprev_ref[...], x_ref[...], xnext_ref[...]], axis=0)
    h_all = (_rmsnorm(x_all, g_ref[...]) * (1.0 + mod_ref[0, 4:5, :]) + mod_ref[0, 3:4, :]).astype(BF16)
    h = h_all[8:ta + 8]
    zero = jnp.zeros((8, h.shape[1]), BF16)
    h_all = jnp.concatenate([jnp.where(tile == 0, zero, h_all[:8]), h,
                             jnp.where(tile == last_tile, zero, h_all[ta + 8:])], axis=0)

    piece = 4 * LANES
    pieces = ([(qkv_ref, c0, c0) for c0 in range(0, c_rw, piece)]
              + [(gate_ref, c0, c_gate + c0) for c0 in range(0, w_ref.shape[1] - c_gate, piece)])

    def emit_dense(count):
        for _ in range(min(count, len(pieces))):
            ref, dst, src = pieces.pop(0)
            ref[:, dst:dst + piece] = _dot(h, w_ref[:, src:src + piece]).astype(ref.dtype)

    def shifted(lo, hi):
        p_s[:, lo:hi] = _dot(h_all, w_ref[:, c_rw + lo:c_rw + hi])
        p = p_s[8:ta + 8, lo:hi]
        return (p + mu_ref[0:1, lo:hi] * (p_s[7:ta + 7, lo:hi] - p)
                + mu_ref[1:2, lo:hi] * (p_s[9:ta + 9, lo:hi] - p))

    o = 3 * width
    lora = shifted(o, o + 3 * LANES)
    k = shifted(width, 2 * width)
    r = shifted(0, width)
    v = shifted(2 * width, 3 * width)
    emit_dense(2)

    th = jnp.tanh(lora[:, :LANES]).astype(BF16)
    al = lora[:, LANES:2 * LANES].astype(BF16)
    gl = lora[:, 2 * LANES:]
    k_k = par_ref[4:5, :]
    k_a = par_ref[5:6, :]
    r_k = par_ref[6:7, :]

    v_ref[...] = v.astype(BF16)
    g_out_ref[...] = _dot(jax.nn.sigmoid(gl).astype(BF16), gup_ref[...])
    kk = k * k_k
    kk = kk * lax.rsqrt(jnp.maximum(_head_segsum(kk * kk), 1e-12))
    emit_dense(1)

    def chunks(t):
        return t.reshape(n_chunks, c_len, width)

    kd_sum = None
    for d in range(2):
        z = par_ref[d:d + 1, :] + _dot(th, wup_ref[d])
        lw = -math.exp(-0.5) * jax.nn.sigmoid(z)
        a = jax.nn.sigmoid(par_ref[2 + d:3 + d, :] + _dot(al, aup_ref[d]))
        kd = k * (1.0 + (a - 1.0) * k_a)
        kd_sum = kd if kd_sum is None else kd_sum + kd
        bb = kk * a
        emit_dense(1)
        ci3 = chunks(_dot(tri_ref[d], lw.astype(BF16)))
        ce3 = ci3 - chunks(lw)
        t_last = 0 if d == 1 else c_len - 1
        cm = 0.5 * ci3[:, t_last:t_last + 1, :]
        e_k = jnp.exp(cm - ci3)
        emit_dense(1)
        kk_ref[d] = (chunks(kk) * jnp.exp(ce3 - cm)).reshape(ta, width).astype(BF16)
        r_ref[d] = (chunks(r) * jnp.exp(ci3 - cm)).reshape(ta, width).astype(BF16)
        b_ref[d] = (chunks(bb) * e_k).reshape(ta, width).astype(BF16)
        k_ref[d] = (chunks(kd) * e_k).reshape(ta, width).astype(BF16)
        em_ref[d, 0] = jnp.exp(cm).reshape(n_chunks, width)
    emit_dense(len(pieces))
    bon_ref[...] = _head_segsum(r * kd_sum * r_k) * v


def _inproj(xs, mod, g, w_in, mu, par, wup, aup, gup, *, c_rw, c_gate, batch, seq, ctx_len):
    n, d = xs.shape
    cols = w_in.shape[1]
    width = RW_HEADS * RW_HEAD_DIM
    ta = RW_TILE
    c_len = RW_CHUNK
    n_tiles = n // ta
    tiles_x = seq // ta
    sub = ta // 8
    last8 = n // 8 - 1
    seq_of = lambda i: jnp.minimum(i // tiles_x, batch)
    idx = np.arange(ta)
    same_chunk = (idx[:, None] // c_len) == (idx[None, :] // c_len)
    tri = jnp.asarray(np.stack([same_chunk & (idx[None, :] <= idx[:, None]),
                                same_chunk & (idx[None, :] >= idx[:, None])]), BF16)
    tok2 = pl.BlockSpec((2, ta, width), lambda i: (0, i, 0))
    tok = lambda c: pl.BlockSpec((ta, c), lambda i: (i, 0))
    sds = jax.ShapeDtypeStruct
    return pl.pallas_call(
        functools.partial(_inproj_kernel, c_rw=c_rw, c_gate=c_gate, n_x_tiles=batch * tiles_x, tiles_x=tiles_x,
                          tiles_ctx=ctx_len // ta),
        grid=(n_tiles,),
        in_specs=[tok(d),
                  pl.BlockSpec((8, d), lambda i: (jnp.maximum(i * sub - 1, 0), 0)),
                  pl.BlockSpec((8, d), lambda i: (jnp.minimum((i + 1) * sub, last8), 0)),
                  pl.BlockSpec((1, N_MOD, d), lambda i: (seq_of(i), 0, 0)),
                  _resident((1, d)), _hbm_spec(),
                  _resident(mu.shape), _resident(par.shape), _resident(wup.shape), _resident(aup.shape),
                  _resident(gup.shape), _resident(tri.shape)],
        out_specs=[tok(c_rw), tok(cols - c_gate), tok2, tok2, tok2, tok2,
                   pl.BlockSpec((2, 1, ta // c_len, width), lambda i: (0, i, 0, 0)),
                   tok(width), tok(width), tok(width)],
        out_shape=[sds((n, c_rw), BF16), sds((n, cols - c_gate), BF16)]
                  + [sds((2, n, width), BF16)] * 4
                  + [sds((2, n_tiles, ta // c_len, width), F32), sds((n, width), BF16),
                     sds((n, width), F32), sds((n, width), F32)],
        scratch_shapes=[pltpu.VMEM((ta + 16, c_gate - c_rw), F32), pltpu.VMEM(w_in.shape, BF16),
                        _stage_shape(w_in), _stage_sems()],
        compiler_params=_cparams(1),
    )(xs, xs, xs, mod, g, w_in, mu, par, wup, aup, gup, tri)


LOG2E = math.log2(math.e)


def _na_pair_bias(rpb):
    w = GRID_W
    qc = np.arange(w)[:, None]
    kc = np.arange(w)[None, :]
    s_c = np.clip(qc - NA_WIN_W // 2, 0, w - NA_WIN_W)
    col_ok = (kc >= s_c) & (kc < s_c + NA_WIN_W)
    dc = np.clip(kc - qc + NA_WIN_W - 1, 0, 2 * NA_WIN_W - 2)
    onehot = jnp.asarray(dc[None] == np.arange(2 * NA_WIN_W - 1)[:, None, None], F32)
    b = jnp.einsum('hrd,dqk->hrqk', rpb.astype(F32), onehot, precision=lax.Precision.HIGHEST)
    b = jnp.where(col_ok[None, None], b * LOG2E, -jnp.inf)
    ninf = jnp.full((rpb.shape[0], 1, w, w), -jnp.inf, F32)
    b = jnp.concatenate([ninf, b, ninf], axis=1)
    n_e = 2 * NA_WIN_H
    return pl.pallas_call(
        _na_pair_kernel,
        grid=(rpb.shape[0],),
        in_specs=[pl.BlockSpec((1, n_e + 1, w, w), lambda h: (h, 0, 0, 0))],
        out_specs=pl.BlockSpec((2, 1, n_e, w, 2 * w), lambda h: (0, h, 0, 0, 0)),
        out_shape=jax.ShapeDtypeStruct((2, rpb.shape[0], n_e, w, 2 * w), F32),
        compiler_params=_cparams(1),
    )(b)


def _na_pair_kernel(b_ref, o_ref):
    b = b_ref[0]
    pair = jnp.concatenate([b[:-1], b[1:]], axis=-1)
    o_ref[0, 0] = pair
    dr = (lax.broadcasted_iota(jnp.int32, pair.shape, 0) - 1
          + (lax.broadcasted_iota(jnp.int32, pair.shape, 2) >= GRID_W).astype(jnp.int32))
    first_dr = NA_WIN_H // 2 - 1
    inside = jnp.logical_and(dr >= first_dr, dr < first_dr + NA_WIN_H)
    o_ref[1, 0] = jnp.where(inside, pair, -jnp.inf)


def _na_kernel(q_ref, k0_ref, k1_ref, k2_ref, k3_ref, k4_ref, k5_ref, kc_ref,
               v0_ref, v1_ref, v2_ref, v3_ref, v4_ref, v5_ref, vc_ref,
               bias_ref, o_ref, kcat, vcat, *, n_steps):
    j = pl.program_id(2)
    key_refs = (k0_ref, k1_ref, k2_ref, k3_ref, k4_ref, k5_ref, kc_ref)
    val_refs = (v0_ref, v1_ref, v2_ref, v3_ref, v4_ref, v5_ref, vc_ref)
    blk = k0_ref.shape[0]
    for s, (kr, vr) in enumerate(zip(key_refs, val_refs)):
        kcat[s * blk:s * blk + kr.shape[0], :] = kr[...]
        vcat[s * blk:s * blk + vr.shape[0], :] = vr[...]
    n_loc = (len(key_refs) - 1) * blk
    w = GRID_W
    half = NA_WIN_H // 2
    sub_rows = NA_ROWS // 2
    sub_pairs = (sub_rows + NA_WIN_H) // 2
    tq = sub_rows * w
    lane = lax.broadcasted_iota(jnp.int32, (1, LANES), 1)
    first = lane < NA_HEAD_DIM
    nt = (((1,), (1,)), ((), ()))

    items = [(pp, h, sb) for pp in range(q_ref.shape[1] // LANES) for sb in range(2) for h in range(2)]

    def scores(hb, item):
        pp, h, sb = item
        cols = slice(pp * LANES, (pp + 1) * LANES)
        q0 = (2 * hb + sb) * tq
        q = (q_ref[q0:q0 + tq, cols].astype(F32) * (NA_HEAD_DIM ** -0.5 * LOG2E)).astype(BF16)
        qh = jnp.where(first if h == 0 else jnp.logical_not(first), q, jnp.zeros_like(q))
        k0 = (2 * hb * 2 + 2 * sb) * LANES
        s_loc = lax.dot_general(qh, kcat[k0:k0 + sub_pairs * LANES, cols], nt, preferred_element_type=F32)
        s_ctx = lax.dot_general(qh, kcat[n_loc:, cols], nt, preferred_element_type=F32)
        return s_loc, s_ctx

    def attend(hb, item, s_loc, s_ctx, interior):
        pp, h, sb = item
        cols = slice(pp * LANES, (pp + 1) * LANES)
        p_rows = []
        l_rows = []
        for ql in range(sub_rows):
            qi = sb * sub_rows + ql
            if interior:
                m_lo, m_hi = qi // 2, (qi + NA_WIN_H - 1) // 2
            else:
                m_lo, m_hi = min(qi, half) // 2, (max(qi, half) + NA_WIN_H - 1) // 2
            bias = jnp.concatenate(
                [bias_ref[1 if interior else 0, 2 * pp + h, min(max(2 * m - qi + half, 0), 2 * NA_WIN_H - 1)]
                 for m in range(m_lo, m_hi + 1)], axis=1)
            rows = slice(ql * w, (ql + 1) * w)
            t_lo, t_hi = m_lo - 2 * sb, m_hi - 2 * sb
            s_q = s_loc[rows, t_lo * LANES:(t_hi + 1) * LANES] + bias
            if not interior:
                lo = max(qi, half) if hb == 0 else min(qi, half)
                key_row = 2 * m_lo + lax.broadcasted_iota(jnp.int32, (1, (m_hi - m_lo + 1) * LANES), 1) // w
                s_q = jnp.where((key_row >= lo) & (key_row < lo + NA_WIN_H), s_q, -jnp.inf)
            s_c = s_ctx[rows]
            mx = jnp.maximum(jnp.max(s_q, axis=-1, keepdims=True), jnp.max(s_c, axis=-1, keepdims=True))
            p_q = jnp.exp2(s_q - mx)
            p_c = jnp.exp2(s_c - mx)
            l_rows.append(jnp.sum(p_q, axis=-1, keepdims=True) + jnp.sum(p_c, axis=-1, keepdims=True))
            pieces = []
            if t_lo > 0:
                pieces.append(jnp.zeros((w, t_lo * LANES), BF16))
            pieces.append(p_q.astype(BF16))
            if t_hi < sub_pairs - 1:
                pieces.append(jnp.zeros((w, (sub_pairs - 1 - t_hi) * LANES), BF16))
            pieces.append(p_c.astype(BF16))
            p_rows.append(jnp.concatenate(pieces, axis=1))
        p = jnp.concatenate(p_rows, axis=0)
        l = jnp.concatenate(l_rows, axis=0)
        k0 = (4 * hb + 2 * sb) * LANES
        o = (_dot(p[:, :sub_pairs * LANES], vcat[k0:k0 + sub_pairs * LANES, cols])
             + _dot(p[:, sub_pairs * LANES:], vcat[n_loc:, cols]))
        return o / l

    def run(hb, interior):
        pending = scores(hb, items[0])
        outs = {}
        for n, item in enumerate(items):
            current = pending
            if n + 1 < len(items):
                pending = scores(hb, items[n + 1])
            outs[item] = attend(hb, item, *current, interior)
            pp, h, sb = item
            if h == 1:
                q0 = (2 * hb + sb) * tq
                o_ref[q0:q0 + tq, pp * LANES:(pp + 1) * LANES] = jnp.where(
                    first, outs[(pp, 0, sb)], outs[(pp, 1, sb)]).astype(o_ref.dtype)

    pl.when(j == 0)(lambda: run(0, False))
    pl.when(j > 0)(lambda: run(0, True))
    pl.when(j == n_steps - 1)(lambda: run(1, False))
    pl.when(j < n_steps - 1)(lambda: run(1, True))


def _na_attention(qkv, bias, *, batch, seq, ctx_len):
    rows = seq // GRID_W
    nj = rows // (2 * NA_ROWS)
    tq = 2 * NA_ROWS * GRID_W
    tk = 4 * GRID_W
    n_slots = 6
    width = NA_HEADS * NA_HEAD_DIM
    kblocks = seq // tk
    ctx_blk0 = (batch * seq) // ctx_len

    lanes = 2 * LANES
    n_groups = width // lanes

    def kv_spec(slot, col0):
        def imap(hg, b, j):
            return (b * kblocks + jnp.clip(4 * j - 1 + slot, 0, kblocks - 1), col0 + hg)
        return pl.BlockSpec((tk, lanes), imap)

    def ctx_spec(col0):
        return pl.BlockSpec((ctx_len, lanes), lambda hg, b, j: (ctx_blk0 + b, col0 + hg))

    in_specs = ([pl.BlockSpec((tq, lanes), lambda hg, b, j: (b * nj + j, hg))]
                + [kv_spec(s, n_groups) for s in range(n_slots)] + [ctx_spec(n_groups)]
                + [kv_spec(s, 2 * n_groups) for s in range(n_slots)] + [ctx_spec(2 * n_groups)]
                + [pl.BlockSpec((2, 4) + bias.shape[2:], lambda hg, b, j: (0, hg, 0, 0, 0))])
    return pl.pallas_call(
        functools.partial(_na_kernel, n_steps=nj),
        grid=(n_groups, batch, nj),
        in_specs=in_specs,
        out_specs=pl.BlockSpec((tq, lanes), lambda hg, b, j: (b * nj + j, hg)),
        out_shape=jax.ShapeDtypeStruct((batch * seq, width), BF16),
        scratch_shapes=[pltpu.VMEM((n_slots * tk + ctx_len, lanes), BF16),
                        pltpu.VMEM((n_slots * tk + ctx_len, lanes), BF16)],
        compiler_params=_cparams(3),
    )(*([qkv] * (3 + 2 * n_slots)), bias)


def _rwkv_kernel(kkf_ref, rf_ref, bf_ref, kf_ref, emf_ref, vf_ref, kkb_ref, rb_ref, bb_ref, kb_ref, emb_ref, vb_ref,
                 strict_ref, incl_ref, yf_ref, yb_ref, z_s):
    n = pl.program_id(1)
    ta = vf_ref.shape[0]
    width = vf_ref.shape[1]
    n_pairs = width // LANES
    c_len = RW_CHUNK
    n_chunks = ta // c_len
    n_d = n_chunks * n_pairs
    n_b = 2 * n_d

    @pl.when(n == 0)
    def _():
        z_s[...] = jnp.zeros_like(z_s)

    def chunks(t):
        return t.reshape(n_chunks, c_len, width)

    def to_batch(t):
        parts = [t[:, :, j * LANES:(j + 1) * LANES] for j in range(n_pairs)]
        return jnp.stack(parts, axis=1).reshape(n_d, t.shape[1], LANES)

    def load_direction(kk_ref, r_ref, b_ref, k_ref, em_ref, v_ref):
        em = em_ref[0, 0].reshape(n_chunks, 1, width)
        kk_t = chunks(kk_ref[0].astype(F32))
        r_t = chunks(r_ref[0].astype(F32))
        b_t = chunks(b_ref[0].astype(F32))
        k_t = chunks(k_ref[0].astype(F32))
        return (to_batch(kk_t * em),
                to_batch(r_t * em),
                to_batch(k_t * em),
                to_batch(b_t * em),
                to_batch(em * em),
                to_batch(kk_t), to_batch(r_t), to_batch(b_t), to_batch(k_t),
                to_batch(chunks(v_ref[...].astype(F32))))

    fwd = load_direction(kkf_ref, rf_ref, bf_ref, kf_ref, emf_ref, vf_ref)
    bwd = load_direction(kkb_ref, rb_ref, bb_ref, kb_ref, emb_ref, vb_ref)
    kk_abs, r_abs, k_bar, b_bar, gam, kk_t, r_t, b_t, k_t, v_b = [
        jnp.concatenate([f, b], axis=0) for f, b in zip(fwd, bwd)]

    lane = lax.broadcasted_iota(jnp.int32, (1, 1, LANES), 2)
    first = lane < RW_HEAD_DIM
    ri = lax.broadcasted_iota(jnp.int32, (c_len, LANES), 0)
    cj = lax.broadcasted_iota(jnp.int32, (c_len, LANES), 1) % RW_HEAD_DIM
    eye = ri == cj

    def causal(mask_ref, t):
        return jnp.concatenate([jnp.where(mask_ref[0] > 0.0, t[:n_d], 0.0),
                                jnp.where(mask_ref[1] > 0.0, t[n_d:], 0.0)], axis=0)

    def stack_heads(t):
        zero = jnp.zeros_like(t)
        return jnp.concatenate([jnp.where(first, t, zero), jnp.where(first, zero, t)], axis=1)

    def bmm(x, y):
        return lax.dot_general(x.astype(BF16), y.astype(BF16), (((2,), (1,)), ((0,), (0,))),
                               preferred_element_type=F32)

    def bmm_nt(x, y):
        return lax.dot_general(x.astype(BF16), y.astype(BF16), (((2,), (2,)), ((0,), (0,))),
                               preferred_element_type=F32)

    def bmm_tn(x, y):
        return lax.dot_general(x.astype(BF16), y.astype(BF16), (((1,), (1,)), ((0,), (0,))),
                               preferred_element_type=F32)

    gm = bmm_nt(jnp.concatenate([kk_t, r_t], axis=1),
                jnp.concatenate([stack_heads(b_t), stack_heads(k_t)], axis=1))
    ab_w = causal(strict_ref, gm[:, :c_len, :LANES])
    ak_w = causal(strict_ref, gm[:, :c_len, LANES:])
    db_w = causal(incl_ref, gm[:, c_len:, :LANES])
    dk_w = causal(incl_ref, gm[:, c_len:, LANES:])

    base = RW_INV_BASE
    same = lambda s: (ri // s) == (cj // s)
    l_base = jnp.where(same(base), ab_w, 0.0)
    t_w = jnp.where(eye, 1.0, 0.0) - l_base
    pw = bmm(l_base, stack_heads(l_base))
    span = 2
    while span < base:
        t_w = t_w + bmm(t_w, stack_heads(pw))
        span *= 2
        if span < base:
            pw = bmm(pw, stack_heads(pw))
    size = base
    while size < c_len:
        off = jnp.where(jnp.logical_and(same(2 * size), jnp.logical_not(same(size))), ab_w, 0.0)
        t_w = t_w - bmm(bmm(t_w, stack_heads(off)), stack_heads(t_w))
        size *= 2

    def head_diag(t):
        return jnp.where(first, t[:, :RW_HEAD_DIM], t[:, RW_HEAD_DIM:])

    akv = bmm(jnp.concatenate([ak_w, dk_w], axis=1), stack_heads(v_b))
    pq = bmm(t_w, jnp.concatenate([stack_heads(kk_abs), stack_heads(akv[:, :c_len])], axis=2))
    dpq = bmm(db_w, jnp.concatenate([stack_heads(pq[:, :, :LANES]), stack_heads(pq[:, :, LANES:])], axis=2))
    r_hat = r_abs - dpq[:, :, :LANES]
    y_loc = akv[:, c_len:] - dpq[:, :, LANES:]
    bpq = bmm_tn(b_bar, pq)
    m_w = jnp.where(eye, jnp.broadcast_to(gam, (n_b, RW_HEAD_DIM, LANES)), 0.0) - head_diag(bpq[:, :, :LANES])
    g_w = head_diag(bmm_tn(k_bar, v_b) - bpq[:, :, LANES:])

    zs = z_s[...]
    for step in range(n_chunks):
        c_f, c_b = step, n_chunks - 1 - step
        pick = lambda t: jnp.concatenate([t[c_f * n_pairs:(c_f + 1) * n_pairs],
                                          t[n_d + c_b * n_pairs:n_d + (c_b + 1) * n_pairs]], axis=0)
        both = bmm(jnp.concatenate([pick(r_hat), pick(m_w)], axis=1), stack_heads(zs))
        y_c = both[:, :c_len] + pick(y_loc)
        yf_ref[0, c_f * c_len:(c_f + 1) * c_len, :] = jnp.concatenate([y_c[j] for j in range(n_pairs)], axis=1)
        yb_ref[0, c_b * c_len:(c_b + 1) * c_len, :] = jnp.concatenate(
            [y_c[n_pairs + j] for j in range(n_pairs)], axis=1)
        zs = both[:, c_len:] + pick(g_w)
    z_s[...] = zs


def _rwkv_scan(kk_t, r_t, b_t, k_t, em, v, *, batch, seq, ctx_len):
    width = RW_HEADS * RW_HEAD_DIM
    ta = RW_TILE
    nct = ctx_len // ta
    nxt = seq // ta
    n_x = batch * seq
    c_len = RW_CHUNK

    def blk(reverse):
        def index(b, n):
            t_ctx = (nct - 1 - n) if reverse else n
            t_x = (nxt - 1 - (n - nct)) if reverse else (n - nct)
            return jnp.where(n < nct, batch * nxt + b * nct + t_ctx, b * nxt + t_x)
        return index

    def xblk(reverse):
        def index(b, n):
            m = jnp.maximum(n, nct) - nct
            return b * nxt + ((nxt - 1 - m) if reverse else m)
        return index

    t_i = np.arange(c_len)[:, None]
    s_i = np.arange(LANES)[None, :] % RW_HEAD_DIM
    strict = jnp.asarray(np.stack([s_i < t_i, s_i > t_i]), F32)
    incl = jnp.asarray(np.stack([s_i <= t_i, s_i >= t_i]), F32)

    def operands(d):
        index = blk(d == 1)
        tile = pl.BlockSpec((1, ta, width), lambda b, n: (d, index(b, n), 0))
        return [tile, tile, tile, tile,
                pl.BlockSpec((1, 1, ta // c_len, width), lambda b, n: (d, index(b, n), 0, 0)),
                pl.BlockSpec((ta, width), lambda b, n: (index(b, n), 0))]

    def out_spec(reverse):
        index = xblk(reverse)
        return pl.BlockSpec((1, ta, width), lambda b, n: (0, index(b, n), 0))

    y_shape = jax.ShapeDtypeStruct((1, n_x, width), F32)
    return pl.pallas_call(
        _rwkv_kernel,
        grid=(batch, nct + nxt),
        in_specs=operands(0) + operands(1) + [_resident(strict.shape), _resident(incl.shape)],
        out_specs=[out_spec(False), out_spec(True)],
        out_shape=[y_shape, y_shape],
        scratch_shapes=[pltpu.VMEM((2 * (width // LANES), RW_HEAD_DIM, LANES), F32)],
        compiler_params=_cparams(2),
    )(kk_t, r_t, b_t, k_t, em, v, kk_t, r_t, b_t, k_t, em, v, strict, incl)


def _merge_kernel(x_ref, mod_ref, ona_ref, yf_ref, yb_ref, bon_ref, g_ref, pg_ref, bg_ref, ln_ref,
                  pna_ref, prw_ref, wout_ref, gn_ref, wg_hbm, wu_hbm, wd_hbm, gf_ref, o_ref,
                  wg_s, wu_s, wd_s, stage_in, stage_out, sem, *, f_chunk):
    @pl.when(pl.program_id(0) == 0)
    def _():
        _ffn_load_weights(wg_hbm, wu_hbm, wd_hbm, wg_s, wu_s, wd_s, stage_in, stage_out, sem)

    d = x_ref.shape[1]
    y = yf_ref[0] + yb_ref[0]
    inv_n = 1.0 / RW_HEAD_DIM
    mean = _head_segsum(y) * inv_n
    yc = y - mean
    var = _head_segsum(yc * yc) * inv_n
    yn = yc * lax.rsqrt(var + RW_GN_EPS) * ln_ref[0:1, :] + ln_ref[1:2, :]
    o_rw = ((yn + bon_ref[...]) * g_ref[...]).astype(BF16)
    gates = jax.nn.sigmoid(pg_ref[...] + bg_ref[...])
    m = gates[:, :d] * _dot(ona_ref[...], pna_ref[...]) + gates[:, d:] * _dot(o_rw, prw_ref[...])
    x2 = x_ref[...] + mod_ref[0, 5:6, :] * _dot(m.astype(BF16), wout_ref[...])
    o_ref[...] = _swiglu_half_step(x2, mod_ref, gn_ref, wg_s, wu_s, wd_s, gf_ref,
                                   mod0=6, f_chunk=f_chunk, final=True)


def _merge_ffn(x1, mod, o_na, y_f, y_b, bon, g, p_gate, b_gate, ln, p_na, p_rw, w_out, gn, wg, wu, wd, gf, *,
               tiles_per_seq):
    n_x, d = o_na.shape[0], x1.shape[1]
    width = o_na.shape[1]
    tok = lambda i: (i, 0)
    tok3 = lambda i: (0, i, 0)
    return pl.pallas_call(
        functools.partial(_merge_kernel, f_chunk=FFN_CHUNK),
        grid=(n_x // TM,),
        in_specs=[pl.BlockSpec((TM, d), tok),
                  pl.BlockSpec((1, N_MOD, d), lambda i: (i // tiles_per_seq, 0, 0)),
                  pl.BlockSpec((TM, width), tok),
                  pl.BlockSpec((1, TM, width), tok3), pl.BlockSpec((1, TM, width), tok3),
                  pl.BlockSpec((TM, width), tok), pl.BlockSpec((TM, width), tok),
                  pl.BlockSpec((TM, 2 * d), tok),
                  _resident((1, 2 * d)), _resident((2, width)),
                  _resident(p_na.shape), _resident(p_rw.shape), _resident(w_out.shape),
                  _resident(gn.shape), _hbm_spec(), _hbm_spec(), _hbm_spec(), _resident(gf.shape)],
        out_specs=pl.BlockSpec((TM, d), tok),
        out_shape=jax.ShapeDtypeStruct((n_x, d), F32),
        scratch_shapes=_ffn_scratch(wg, wu, wd),
        compiler_params=_cparams(1),
    )(x1, mod, o_na, y_f, y_b, bon, g, p_gate, b_gate, ln, p_na, p_rw, w_out, gn, wg, wu, wd, gf)


def _pad_lora(w_up, direction):
    z = jnp.zeros_like(w_up[0])
    return jnp.concatenate([w_up[0] if direction == 0 else z, w_up[1] if direction == 1 else z], axis=0)


def kernel(x, c, ctx, c_ctx, w_ada, b_ada, norm_ffn1, norm_mix, norm_ffn2, norm_final, ffn1_wg, ffn1_wu, ffn1_wd, ffn2_wg, ffn2_wu, ffn2_wd, w_in, b_gate, na_rpb, rw_mu, rw_w0, rw_w_up, rw_a0, rw_a_up, rw_g_up, rw_k_k, rw_k_a, rw_r_k, rw_ln_w, rw_ln_b, p_na, p_rw, w_out):
    batch, seq, d = x.shape
    ctx_len = ctx.shape[1]
    n_x = batch * seq
    na_width = NA_HEADS * NA_HEAD_DIM
    rw_width = RW_HEADS * RW_HEAD_DIM
    c_rw = 3 * na_width
    c_gate = c_rw + 3 * rw_width + 4 * RW_LORA + RW_GATE_LORA
    assert w_ada.shape[0] == 1, "single layer"
    assert seq % TM == 0 and (batch * ctx_len) % TM == 0 and ctx_len % RW_TILE == 0 and batch + 1 <= 8
    assert seq % (2 * NA_ROWS * GRID_W) == 0 and ctx_len == 4 * GRID_W
    row = lambda t: t.reshape(1, -1)

    cs = jnp.concatenate([c, c_ctx[None], jnp.zeros((8 - batch - 1, d), F32)], axis=0)
    mod = _ada_mod(cs, w_ada[0], b_ada[0])[:batch + 1].reshape(batch + 1, N_MOD, d)

    bf = lambda t: t.astype(BF16)

    x1 = _ffn(x.reshape(n_x, d), ctx.reshape(batch * ctx_len, d), mod, row(norm_ffn1[0]),
              ffn1_wg[0], ffn1_wu[0], ffn1_wd[0], row(norm_final),
              mod0=0, tiles_per_seq=seq // TM, n_seq=batch)
    par = jnp.concatenate([rw_w0[0], rw_a0[0], row(rw_k_k[0]), row(rw_k_a[0]), row(rw_r_k[0]),
                           jnp.zeros((1, rw_width), F32)], axis=0)
    wup = bf(jnp.stack([_pad_lora(rw_w_up[0], 0), _pad_lora(rw_w_up[0], 1)]))
    aup = bf(jnp.stack([_pad_lora(rw_a_up[0], 0), _pad_lora(rw_a_up[0], 1)]))
    qkv, p_gate, kk_t, r_t, b_t, k_t, em, v_rw, g, bon = _inproj(
        x1, mod, row(norm_mix[0]), w_in[0], rw_mu[0], par, wup, aup, bf(rw_g_up[0]),
        c_rw=c_rw, c_gate=c_gate, batch=batch, seq=seq, ctx_len=ctx_len)

    bias = _na_pair_bias(na_rpb[0])
    o_na = _na_attention(qkv, bias, batch=batch, seq=seq, ctx_len=ctx_len)

    y_f, y_b = _rwkv_scan(kk_t, r_t, b_t, k_t, em, v_rw, batch=batch, seq=seq, ctx_len=ctx_len)

    ln = jnp.stack([rw_ln_w[0], rw_ln_b[0]], axis=0)
    out = _merge_ffn(x1, mod, o_na, y_f, y_b, bon, g, p_gate, row(b_gate[0]), ln,
                     bf(p_na[0]), bf(p_rw[0]), bf(w_out[0]), row(norm_ffn2[0]),
                     ffn2_wg[0], ffn2_wu[0], ffn2_wd[0], row(norm_final),
                     tiles_per_seq=seq // TM)
    return out.reshape(batch, seq, d)
```

```python
import functools
import math

import numpy as np
import jax
import jax.numpy as jnp
from jax import lax
from jax.experimental import pallas as pl
from jax.experimental.pallas import tpu as pltpu

F32 = jnp.float32
BF16 = jnp.bfloat16

NORM_EPS = 1e-6
RW_GN_EPS = 64e-5
N_MOD = 9
GRID_W = 64
NA_HEADS = 8
NA_HEAD_DIM = 64
NA_WIN_H = 8
NA_WIN_W = 16
RW_HEADS = 8
RW_HEAD_DIM = 64
RW_LORA = 64
RW_GATE_LORA = 128

LANES = 128
VMEM_LIMIT = 56 * 1024 * 1024

TM = 512
FFN_CHUNK = 256
NA_ROWS = 8
RW_TILE = 256
RW_CHUNK = 64
RW_INV_BASE = 8
assert RW_CHUNK == RW_HEAD_DIM


def _cparams(n_axes):
    return pltpu.CompilerParams(dimension_semantics=("arbitrary",) * n_axes,
                                vmem_limit_bytes=VMEM_LIMIT)


def _resident(shape):
    nd = len(shape)
    return pl.BlockSpec(shape, lambda *_: (0,) * nd, pipeline_mode=pl.Buffered(1))


def _rmsnorm(x, g):
    return x * lax.rsqrt(jnp.mean(x * x, axis=-1, keepdims=True) + NORM_EPS) * g


def _dot(a, b):
    return jnp.dot(a, b, preferred_element_type=F32)


WEIGHT_CHUNKS = 16


def _hbm_spec():
    return pl.BlockSpec(memory_space=pl.ANY)


WEIGHT_SLOTS = 4


def _stage_shape(w):
    return pltpu.VMEM((WEIGHT_SLOTS, w.shape[0] // WEIGHT_CHUNKS, w.shape[1]), F32)


def _stage_sems():
    return pltpu.SemaphoreType.DMA((WEIGHT_SLOTS,))


def _fetch_as_bf16(src_hbm, dst, stage, sem):
    rows = stage.shape[1]

    def chunk_copy(c):
        slot = c % WEIGHT_SLOTS
        return pltpu.make_async_copy(src_hbm.at[c * rows:(c + 1) * rows, :], stage.at[slot], sem.at[slot])

    for c in range(WEIGHT_SLOTS):
        chunk_copy(c).start()
    for c in range(WEIGHT_CHUNKS):
        chunk_copy(c).wait()
        dst[c * rows:(c + 1) * rows, :] = stage[c % WEIGHT_SLOTS].astype(dst.dtype)
        if c + WEIGHT_SLOTS < WEIGHT_CHUNKS:
            chunk_copy(c + WEIGHT_SLOTS).start()


def _ada_kernel(c_ref, w_ref, b_ref, o_ref):
    c = c_ref[...]
    s = c * jax.nn.sigmoid(c)
    o_ref[...] = _dot(s.astype(BF16), w_ref[...].astype(BF16)) + b_ref[...]


def _ada_mod(cs, w_ada, b_ada):
    d = cs.shape[1]
    nm = w_ada.shape[1] // d
    return pl.pallas_call(
        _ada_kernel,
        grid=(nm,),
        in_specs=[pl.BlockSpec((8, d), lambda j: (0, 0)),
                  pl.BlockSpec((d, d), lambda j: (0, j)),
                  pl.BlockSpec((1, d), lambda j: (0, j))],
        out_specs=pl.BlockSpec((8, d), lambda j: (0, j)),
        out_shape=jax.ShapeDtypeStruct((8, nm * d), F32),
        compiler_params=_cparams(1),
    )(cs, w_ada, b_ada.reshape(1, -1))


def _ffn_scratch(wg, wu, wd):
    return [pltpu.VMEM(wg.shape, BF16), pltpu.VMEM(wu.shape, BF16), pltpu.VMEM(wd.shape, BF16),
            _stage_shape(wg), _stage_shape(wd), _stage_sems()]


def _ffn_load_weights(wg_hbm, wu_hbm, wd_hbm, wg_s, wu_s, wd_s, stage_in, stage_out, sem):
    _fetch_as_bf16(wg_hbm, wg_s, stage_in, sem)
    _fetch_as_bf16(wu_hbm, wu_s, stage_in, sem)
    _fetch_as_bf16(wd_hbm, wd_s, stage_out, sem)


def _ffn_kernel(x_ref, tail_ref, mod_ref, g_ref, wg_hbm, wu_hbm, wd_hbm, gf_ref, o_ref,
                wg_s, wu_s, wd_s, stage_in, stage_out, sem, *, mod0, f_chunk, n_head_tiles):
    @pl.when(pl.program_id(0) == 0)
    def _():
        _ffn_load_weights(wg_hbm, wu_hbm, wd_hbm, wg_s, wu_s, wd_s, stage_in, stage_out, sem)

    x = jnp.where(pl.program_id(0) < n_head_tiles, x_ref[...], tail_ref[...])
    o_ref[...] = _swiglu_half_step(x, mod_ref, g_ref, wg_s, wu_s, wd_s, gf_ref,
                                   mod0=mod0, f_chunk=f_chunk, final=False)


def _swiglu_half_step(x, mod_ref, g_ref, wg_ref, wu_ref, wd_ref, gf_ref, *, mod0, f_chunk, final):
    shift = mod_ref[0, mod0:mod0 + 1, :]
    scale = mod_ref[0, mod0 + 1:mod0 + 2, :]
    gate = mod_ref[0, mod0 + 2:mod0 + 3, :]
    h = (_rmsnorm(x, g_ref[...]) * (1.0 + scale) + shift).astype(BF16)
    d_ff = wg_ref.shape[1]
    bounds = [(f0, min(f0 + f_chunk, d_ff)) for f0 in range(0, d_ff, f_chunk)]

    def gate_up(lo, hi):
        return _dot(h, wg_ref[:, lo:hi]), _dot(h, wu_ref[:, lo:hi])

    acc = None
    pending = gate_up(*bounds[0])
    for c, (lo, hi) in enumerate(bounds):
        gg, uu = pending
        if c + 1 < len(bounds):
            pending = gate_up(*bounds[c + 1])
        a = (gg * jax.nn.sigmoid(gg) * uu).astype(BF16)
        t = _dot(a, wd_ref[lo:hi, :])
        acc = t if acc is None else acc + t
    y = x + 0.5 * gate * acc
    if final:
        y = _rmsnorm(y, gf_ref[...])
    return y


def _ffn(xs, tail, mod, g, wg, wu, wd, gf, *, mod0, tiles_per_seq, n_seq):
    n_head, d = xs.shape
    n = n_head + tail.shape[0]
    seq_of = lambda i: jnp.minimum(i // tiles_per_seq, n_seq)
    head_tiles = n_head // TM
    return pl.pallas_call(
        functools.partial(_ffn_kernel, mod0=mod0, f_chunk=FFN_CHUNK, n_head_tiles=head_tiles),
        grid=(n // TM,),
        in_specs=[pl.BlockSpec((TM, d), lambda i: (jnp.minimum(i, head_tiles - 1), 0)),
                  pl.BlockSpec((TM, d), lambda i: (jnp.maximum(i - head_tiles, 0), 0)),
                  pl.BlockSpec((1, N_MOD, d), lambda i: (seq_of(i), 0, 0)),
                  _resident((1, d)), _hbm_spec(), _hbm_spec(), _hbm_spec(), _resident((1, d))],
        out_specs=pl.BlockSpec((TM, d), lambda i: (i, 0)),
        out_shape=jax.ShapeDtypeStruct((n, d), F32),
        scratch_shapes=_ffn_scratch(wg, wu, wd),
        compiler_params=_cparams(1),
    )(xs, tail, mod, g, wg, wu, wd, gf)


def _head_segsum(x):
    r = lax.broadcasted_iota(jnp.int32, (LANES, LANES), 0) // RW_HEAD_DIM
    c = lax.broadcasted_iota(jnp.int32, (LANES, LANES), 1) // RW_HEAD_DIM
    e = jnp.where(r == c, 1.0, 0.0).astype(BF16)
    xb = x.astype(BF16)
    return jnp.concatenate([_dot(xb[:, j:j + LANES], e) for j in range(0, x.shape[1], LANES)], axis=1)


def _inproj_kernel(x_ref, xprev_ref, xnext_ref, mod_ref, g_ref, w_hbm, mu_ref, par_ref, wup_ref, aup_ref,
                   gup_ref, tri_ref, qkv_ref, gate_ref, ops_ref, em_ref, v_ref, g_out_ref,
                   bon_ref, p_s, w_ref, stage, sem, *, c_rw, c_gate, n_x_tiles, tiles_x, tiles_ctx):
    i = pl.program_id(0)

    @pl.when(i == 0)
    def _():
        _fetch_as_bf16(w_hbm, w_ref, stage, sem)

    ta = x_ref.shape[0]
    width = v_ref.shape[1]
    c_len = RW_CHUNK
    n_chunks = ta // c_len
    in_x = i < n_x_tiles
    tile = jnp.where(in_x, i % tiles_x, (i - n_x_tiles) % tiles_ctx)
    last_tile = jnp.where(in_x, tiles_x - 1, tiles_ctx - 1)

    x_all = jnp.concatenate([xprev_ref[...], x_ref[...], xnext_ref[...]], axis=0)
    h_all = (_rmsnorm(x_all, g_ref[...]) * (1.0 + mod_ref[0, 4:5, :]) + mod_ref[0, 3:4, :]).astype(BF16)
    h = h_all[8:ta + 8]
    zero = jnp.zeros((8, h.shape[1]), BF16)
    h_all = jnp.concatenate([jnp.where(tile == 0, zero, h_all[:8]), h,
                             jnp.where(tile == last_tile, zero, h_all[ta + 8:])], axis=0)

    piece = 4 * LANES
    pieces = ([(qkv_ref, c0, c0) for c0 in range(0, c_rw, piece)]
              + [(gate_ref, c0, c_gate + c0) for c0 in range(0, w_ref.shape[1] - c_gate, piece)])

    def emit_dense(count):
        for _ in range(min(count, len(pieces))):
            ref, dst, src = pieces.pop(0)
            ref[:, dst:dst + piece] = _dot(h, w_ref[:, src:src + piece]).astype(ref.dtype)

    def shifted(lo, hi):
        p_s[:, lo:hi] = _dot(h_all, w_ref[:, c_rw + lo:c_rw + hi])
        p = p_s[8:ta + 8, lo:hi]
        return (p + mu_ref[0:1, lo:hi] * (p_s[7:ta + 7, lo:hi] - p)
                + mu_ref[1:2, lo:hi] * (p_s[9:ta + 9, lo:hi] - p))

    o = 3 * width
    lora = shifted(o, o + 3 * LANES)
    k = shifted(width, 2 * width)
    r = shifted(0, width)
    v = shifted(2 * width, 3 * width)
    emit_dense(2)

    th = jnp.tanh(lora[:, :LANES]).astype(BF16)
    al = lora[:, LANES:2 * LANES].astype(BF16)
    gl = lora[:, 2 * LANES:]
    k_k = par_ref[4:5, :]
    k_a = par_ref[5:6, :]
    r_k = par_ref[6:7, :]

    v_ref[...] = v.astype(BF16)
    g_out_ref[...] = _dot(jax.nn.sigmoid(gl).astype(BF16), gup_ref[...])
    kk = k * k_k
    kk = kk * lax.rsqrt(jnp.maximum(_head_segsum(kk * kk), 1e-12))
    emit_dense(1)

    def chunks(t):
        return t.reshape(n_chunks, c_len, width)

    kd_sum = None
    for d in range(2):
        z = par_ref[d:d + 1, :] + _dot(th, wup_ref[d])
        lw = -math.exp(-0.5) * jax.nn.sigmoid(z)
        a = jax.nn.sigmoid(par_ref[2 + d:3 + d, :] + _dot(al, aup_ref[d]))
        kd = k * (1.0 + (a - 1.0) * k_a)
        kd_sum = kd if kd_sum is None else kd_sum + kd
        bb = kk * a
        emit_dense(1)
        ci3 = chunks(_dot(tri_ref[d], lw.astype(BF16)))
        ce3 = ci3 - chunks(lw)
        t_last = 0 if d == 1 else c_len - 1
        cm = 0.5 * ci3[:, t_last:t_last + 1, :]
        e_k = jnp.exp(cm - ci3)
        emit_dense(1)
        scaled = (chunks(kk) * jnp.exp(ce3 - cm), chunks(r) * jnp.exp(ci3 - cm), chunks(bb) * e_k, chunks(kd) * e_k)
        for slot, t in enumerate(scaled):
            ops_ref[d, :, slot * width:(slot + 1) * width] = t.reshape(ta, width).astype(BF16)
        em_ref[d, 0] = jnp.exp(cm).reshape(n_chunks, width)
    emit_dense(len(pieces))
    bon_ref[...] = _head_segsum(r * kd_sum * r_k) * v


def _inproj(xs, mod, g, w_in, mu, par, wup, aup, gup, *, c_rw, c_gate, batch, seq, ctx_len):
    n, d = xs.shape
    cols = w_in.shape[1]
    width = RW_HEADS * RW_HEAD_DIM
    ta = RW_TILE
    c_len = RW_CHUNK
    n_tiles = n // ta
    tiles_x = seq // ta
    sub = ta // 8
    last8 = n // 8 - 1
    seq_of = lambda i: jnp.minimum(i // tiles_x, batch)
    idx = np.arange(ta)
    same_chunk = (idx[:, None] // c_len) == (idx[None, :] // c_len)
    tri = jnp.asarray(np.stack([same_chunk & (idx[None, :] <= idx[:, None]),
                                same_chunk & (idx[None, :] >= idx[:, None])]), BF16)
    tok2 = pl.BlockSpec((2, ta, 4 * width), lambda i: (0, i, 0))
    tok = lambda c: pl.BlockSpec((ta, c), lambda i: (i, 0))
    sds = jax.ShapeDtypeStruct
    return pl.pallas_call(
        functools.partial(_inproj_kernel, c_rw=c_rw, c_gate=c_gate, n_x_tiles=batch * tiles_x, tiles_x=tiles_x,
                          tiles_ctx=ctx_len // ta),
        grid=(n_tiles,),
        in_specs=[tok(d),
                  pl.BlockSpec((8, d), lambda i: (jnp.maximum(i * sub - 1, 0), 0)),
                  pl.BlockSpec((8, d), lambda i: (jnp.minimum((i + 1) * sub, last8), 0)),
                  pl.BlockSpec((1, N_MOD, d), lambda i: (seq_of(i), 0, 0)),
                  _resident((1, d)), _hbm_spec(),
                  _resident(mu.shape), _resident(par.shape), _resident(wup.shape), _resident(aup.shape),
                  _resident(gup.shape), _resident(tri.shape)],
        out_specs=[tok(c_rw), tok(cols - c_gate), tok2,
                   pl.BlockSpec((2, 1, ta // c_len, width), lambda i: (0, i, 0, 0)),
                   tok(width), tok(width), tok(width)],
        out_shape=[sds((n, c_rw), BF16), sds((n, cols - c_gate), BF16), sds((2, n, 4 * width), BF16),
                   sds((2, n_tiles, ta // c_len, width), F32), sds((n, width), BF16),
                   sds((n, width), F32), sds((n, width), F32)],
        scratch_shapes=[pltpu.VMEM((ta + 16, c_gate - c_rw), F32), pltpu.VMEM(w_in.shape, BF16),
                        _stage_shape(w_in), _stage_sems()],
        compiler_params=_cparams(1),
    )(xs, xs, xs, mod, g, w_in, mu, par, wup, aup, gup, tri)


LOG2E = math.log2(math.e)


def _na_pair_bias(rpb):
    w = GRID_W
    qc = np.arange(w)[:, None]
    kc = np.arange(w)[None, :]
    s_c = np.clip(qc - NA_WIN_W // 2, 0, w - NA_WIN_W)
    col_ok = (kc >= s_c) & (kc < s_c + NA_WIN_W)
    dc = np.clip(kc - qc + NA_WIN_W - 1, 0, 2 * NA_WIN_W - 2)
    onehot = jnp.asarray(dc[None] == np.arange(2 * NA_WIN_W - 1)[:, None, None], F32)
    b = jnp.einsum('hrd,dqk->hrqk', rpb.astype(F32), onehot, precision=lax.Precision.HIGHEST)
    b = jnp.where(col_ok[None, None], b * LOG2E, -jnp.inf)
    ninf = jnp.full((rpb.shape[0], 1, w, w), -jnp.inf, F32)
    b = jnp.concatenate([ninf, b, ninf], axis=1)
    n_e = 2 * NA_WIN_H
    return pl.pallas_call(
        _na_pair_kernel,
        grid=(rpb.shape[0],),
        in_specs=[pl.BlockSpec((1, n_e + 1, w, w), lambda h: (h, 0, 0, 0))],
        out_specs=pl.BlockSpec((2, 1, n_e, w, 2 * w), lambda h: (0, h, 0, 0, 0)),
        out_shape=jax.ShapeDtypeStruct((2, rpb.shape[0], n_e, w, 2 * w), F32),
        compiler_params=_cparams(1),
    )(b)


def _na_pair_kernel(b_ref, o_ref):
    b = b_ref[0]
    pair = jnp.concatenate([b[:-1], b[1:]], axis=-1)
    o_ref[0, 0] = pair
    dr = (lax.broadcasted_iota(jnp.int32, pair.shape, 0) - 1
          + (lax.broadcasted_iota(jnp.int32, pair.shape, 2) >= GRID_W).astype(jnp.int32))
    first_dr = NA_WIN_H // 2 - 1
    inside = jnp.logical_and(dr >= first_dr, dr < first_dr + NA_WIN_H)
    o_ref[1, 0] = jnp.where(inside, pair, -jnp.inf)


def _na_kernel(q_ref, k0_ref, k1_ref, k2_ref, k3_ref, k4_ref, k5_ref, kc_ref,
               v0_ref, v1_ref, v2_ref, v3_ref, v4_ref, v5_ref, vc_ref,
               bias_ref, o_ref, kcat, vcat, *, n_steps):
    j = pl.program_id(2)
    key_refs = (k0_ref, k1_ref, k2_ref, k3_ref, k4_ref, k5_ref, kc_ref)
    val_refs = (v0_ref, v1_ref, v2_ref, v3_ref, v4_ref, v5_ref, vc_ref)
    blk = k0_ref.shape[0]
    for s, (kr, vr) in enumerate(zip(key_refs, val_refs)):
        kcat[s * blk:s * blk + kr.shape[0], :] = kr[...]
        vcat[s * blk:s * blk + vr.shape[0], :] = vr[...]
    n_loc = (len(key_refs) - 1) * blk
    w = GRID_W
    half = NA_WIN_H // 2
    sub_rows = NA_ROWS // 2
    sub_pairs = (sub_rows + NA_WIN_H) // 2
    tq = sub_rows * w
    lane = lax.broadcasted_iota(jnp.int32, (1, LANES), 1)
    first = lane < NA_HEAD_DIM
    nt = (((1,), (1,)), ((), ()))

    items = [(pp, h, sb) for pp in range(q_ref.shape[1] // LANES) for sb in range(2) for h in range(2)]

    def scores(hb, item):
        pp, h, sb = item
        cols = slice(pp * LANES, (pp + 1) * LANES)
        q0 = (2 * hb + sb) * tq
        q = (q_ref[q0:q0 + tq, cols].astype(F32) * (NA_HEAD_DIM ** -0.5 * LOG2E)).astype(BF16)
        qh = jnp.where(first if h == 0 else jnp.logical_not(first), q, jnp.zeros_like(q))
        k0 = (4 * hb + 2 * sb) * LANES
        s_loc = lax.dot_general(qh, kcat[k0:k0 + sub_pairs * LANES, cols], nt, preferred_element_type=F32)
        s_ctx = lax.dot_general(qh, kcat[n_loc:, cols], nt, preferred_element_type=F32)
        return s_loc, s_ctx

    def attend(hb, item, s_loc, s_ctx, interior):
        pp, h, sb = item
        cols = slice(pp * LANES, (pp + 1) * LANES)
        p_rows = []
        l_rows = []
        for ql in range(sub_rows):
            qi = sb * sub_rows + ql
            if interior:
                m_lo, m_hi = qi // 2, (qi + NA_WIN_H - 1) // 2
            else:
                m_lo, m_hi = min(qi, half) // 2, (max(qi, half) + NA_WIN_H - 1) // 2
            bias = jnp.concatenate(
                [bias_ref[1 if interior else 0, 2 * pp + h, min(max(2 * m - qi + half, 0), 2 * NA_WIN_H - 1)]
                 for m in range(m_lo, m_hi + 1)], axis=1)
            rows = slice(ql * w, (ql + 1) * w)
            t_lo, t_hi = m_lo - 2 * sb, m_hi - 2 * sb
            s_q = s_loc[rows, t_lo * LANES:(t_hi + 1) * LANES] + bias
            if not interior:
                lo = max(qi, half) if hb == 0 else min(qi, half)
                key_row = 2 * m_lo + lax.broadcasted_iota(jnp.int32, (1, (m_hi - m_lo + 1) * LANES), 1) // w
                s_q = jnp.where((key_row >= lo) & (key_row < lo + NA_WIN_H), s_q, -jnp.inf)
            s_c = s_ctx[rows]
            mx = jnp.maximum(jnp.max(s_q, axis=-1, keepdims=True), jnp.max(s_c, axis=-1, keepdims=True))
            p_q = jnp.exp2(s_q - mx)
            p_c = jnp.exp2(s_c - mx)
            l_rows.append(jnp.sum(p_q, axis=-1, keepdims=True) + jnp.sum(p_c, axis=-1, keepdims=True))
            pieces = []
            if t_lo > 0:
                pieces.append(jnp.zeros((w, t_lo * LANES), BF16))
            pieces.append(p_q.astype(BF16))
            if t_hi < sub_pairs - 1:
                pieces.append(jnp.zeros((w, (sub_pairs - 1 - t_hi) * LANES), BF16))
            pieces.append(p_c.astype(BF16))
            p_rows.append(jnp.concatenate(pieces, axis=1))
        p = jnp.concatenate(p_rows, axis=0)
        l = jnp.concatenate(l_rows, axis=0)
        k0 = (4 * hb + 2 * sb) * LANES
        o = (_dot(p[:, :sub_pairs * LANES], vcat[k0:k0 + sub_pairs * LANES, cols])
             + _dot(p[:, sub_pairs * LANES:], vcat[n_loc:, cols]))
        return o / l

    def run(hb, interior):
        pending = scores(hb, items[0])
        outs = {}
        for n, item in enumerate(items):
            current = pending
            if n + 1 < len(items):
                pending = scores(hb, items[n + 1])
            outs[item] = attend(hb, item, *current, interior)
            pp, h, sb = item
            if h == 1:
                q0 = (2 * hb + sb) * tq
                o_ref[q0:q0 + tq, pp * LANES:(pp + 1) * LANES] = jnp.where(
                    first, outs[(pp, 0, sb)], outs[(pp, 1, sb)]).astype(o_ref.dtype)

    pl.when(j == 0)(lambda: run(0, False))
    pl.when(j > 0)(lambda: run(0, True))
    pl.when(j == n_steps - 1)(lambda: run(1, False))
    pl.when(j < n_steps - 1)(lambda: run(1, True))


def _na_attention(qkv, bias, *, batch, seq, ctx_len):
    rows = seq // GRID_W
    nj = rows // (2 * NA_ROWS)
    tq = 2 * NA_ROWS * GRID_W
    slot_rows = NA_WIN_H // 2
    tk = slot_rows * GRID_W
    n_slots = (2 * NA_ROWS + NA_WIN_H) // slot_rows
    width = NA_HEADS * NA_HEAD_DIM
    kblocks = seq // tk
    ctx_blk0 = (batch * seq) // ctx_len

    lanes = width
    n_groups = width // lanes

    def kv_spec(slot, col0):
        def imap(hg, b, j):
            first_slot = (2 * NA_ROWS // slot_rows) * j - 1
            return (b * kblocks + jnp.clip(first_slot + slot, 0, kblocks - 1), col0 + hg)
        return pl.BlockSpec((tk, lanes), imap)

    def ctx_spec(col0):
        return pl.BlockSpec((ctx_len, lanes), lambda hg, b, j: (ctx_blk0 + b, col0 + hg))

    in_specs = ([pl.BlockSpec((tq, lanes), lambda hg, b, j: (b * nj + j, hg))]
                + [kv_spec(s, n_groups) for s in range(n_slots)] + [ctx_spec(n_groups)]
                + [kv_spec(s, 2 * n_groups) for s in range(n_slots)] + [ctx_spec(2 * n_groups)]
                + [pl.BlockSpec((2, 2 * lanes // LANES) + bias.shape[2:], lambda hg, b, j: (0, hg, 0, 0, 0))])
    return pl.pallas_call(
        functools.partial(_na_kernel, n_steps=nj),
        grid=(n_groups, batch, nj),
        in_specs=in_specs,
        out_specs=pl.BlockSpec((tq, lanes), lambda hg, b, j: (b * nj + j, hg)),
        out_shape=jax.ShapeDtypeStruct((batch * seq, width), BF16),
        scratch_shapes=[pltpu.VMEM((n_slots * tk + ctx_len, lanes), BF16),
                        pltpu.VMEM((n_slots * tk + ctx_len, lanes), BF16)],
        compiler_params=_cparams(3),
    )(*([qkv] * (3 + 2 * n_slots)), bias)


def _rwkv_kernel(opsf_ref, emf_ref, vf_ref, opsb_ref, emb_ref, vb_ref, strict_ref, incl_ref, yf_ref, yb_ref, z_s):
    n = pl.program_id(1)
    ta = vf_ref.shape[0]
    width = vf_ref.shape[1]
    n_pairs = width // LANES
    c_len = RW_CHUNK
    n_chunks = ta // c_len
    n_d = n_chunks * n_pairs
    n_b = 2 * n_d

    @pl.when(n == 0)
    def _():
        z_s[...] = jnp.zeros_like(z_s)

    def chunks(t):
        return t.reshape(n_chunks, c_len, width)

    def to_batch(t):
        parts = [t[:, :, j * LANES:(j + 1) * LANES] for j in range(n_pairs)]
        return jnp.stack(parts, axis=1).reshape(n_d, t.shape[1], LANES)

    def load_direction(ops_ref, em_ref, v_ref):
        em = em_ref[0, 0].reshape(n_chunks, 1, width)
        kk_t, r_t, b_t, k_t = [chunks(ops_ref[0, :, s * width:(s + 1) * width].astype(F32)) for s in range(4)]
        return (to_batch(kk_t * em),
                to_batch(r_t * em),
                to_batch(k_t * em),
                to_batch(b_t * em),
                to_batch(em * em),
                to_batch(kk_t), to_batch(r_t), to_batch(b_t), to_batch(k_t),
                to_batch(chunks(v_ref[...].astype(F32))))

    fwd = load_direction(opsf_ref, emf_ref, vf_ref)
    bwd = load_direction(opsb_ref, emb_ref, vb_ref)
    kk_abs, r_abs, k_bar, b_bar, gam, kk_t, r_t, b_t, k_t, v_b = [
        jnp.concatenate([f, b], axis=0) for f, b in zip(fwd, bwd)]

    lane = lax.broadcasted_iota(jnp.int32, (1, 1, LANES), 2)
    first = lane < RW_HEAD_DIM
    ri = lax.broadcasted_iota(jnp.int32, (c_len, LANES), 0)
    cj = lax.broadcasted_iota(jnp.int32, (c_len, LANES), 1) % RW_HEAD_DIM
    eye = ri == cj

    def causal(mask_ref, t):
        return jnp.concatenate([jnp.where(mask_ref[0] > 0.0, t[:n_d], 0.0),
                                jnp.where(mask_ref[1] > 0.0, t[n_d:], 0.0)], axis=0)

    def stack_heads(t):
        zero = jnp.zeros_like(t)
        return jnp.concatenate([jnp.where(first, t, zero), jnp.where(first, zero, t)], axis=1)

    def bmm(x, y):
        return lax.dot_general(x.astype(BF16), y.astype(BF16), (((2,), (1,)), ((0,), (0,))),
                               preferred_element_type=F32)

    def bmm_nt(x, y):
        return lax.dot_general(x.astype(BF16), y.astype(BF16), (((2,), (2,)), ((0,), (0,))),
                               preferred_element_type=F32)

    def bmm_tn(x, y):
        return lax.dot_general(x.astype(BF16), y.astype(BF16), (((1,), (1,)), ((0,), (0,))),
                               preferred_element_type=F32)

    gm = bmm_nt(jnp.concatenate([kk_t, r_t], axis=1),
                jnp.concatenate([stack_heads(b_t), stack_heads(k_t)], axis=1))
    ab_w = causal(strict_ref, gm[:, :c_len, :LANES])
    ak_w = causal(strict_ref, gm[:, :c_len, LANES:])
    db_w = causal(incl_ref, gm[:, c_len:, :LANES])
    dk_w = causal(incl_ref, gm[:, c_len:, LANES:])

    base = RW_INV_BASE
    same = lambda s: (ri // s) == (cj // s)
    l_base = jnp.where(same(base), ab_w, 0.0)
    t_w = jnp.where(eye, 1.0, 0.0) - l_base
    pw = bmm(l_base, stack_heads(l_base))
    span = 2
    while span < base:
        t_w = t_w + bmm(t_w, stack_heads(pw))
        span *= 2
        if span < base:
            pw = bmm(pw, stack_heads(pw))
    size = base
    while size < c_len:
        off = jnp.where(jnp.logical_and(same(2 * size), jnp.logical_not(same(size))), ab_w, 0.0)
        t_w = t_w - bmm(bmm(t_w, stack_heads(off)), stack_heads(t_w))
        size *= 2

    def head_diag(t):
        return jnp.where(first, t[:, :RW_HEAD_DIM], t[:, RW_HEAD_DIM:])

    akv = bmm(jnp.concatenate([ak_w, dk_w], axis=1), stack_heads(v_b))
    pq = bmm(t_w, jnp.concatenate([stack_heads(kk_abs), stack_heads(akv[:, :c_len])], axis=2))
    dpq = bmm(db_w, jnp.concatenate([stack_heads(pq[:, :, :LANES]), stack_heads(pq[:, :, LANES:])], axis=2))
    r_hat = r_abs - dpq[:, :, :LANES]
    y_loc = akv[:, c_len:] - dpq[:, :, LANES:]
    bpq = bmm_tn(b_bar, pq)
    m_w = jnp.where(eye, jnp.broadcast_to(gam, (n_b, RW_HEAD_DIM, LANES)), 0.0) - head_diag(bpq[:, :, :LANES])
    g_w = head_diag(bmm_tn(k_bar, v_b) - bpq[:, :, LANES:])

    zs = z_s[...]
    for step in range(n_chunks):
        c_f, c_b = step, n_chunks - 1 - step
        pick = lambda t: jnp.concatenate([t[c_f * n_pairs:(c_f + 1) * n_pairs],
                                          t[n_d + c_b * n_pairs:n_d + (c_b + 1) * n_pairs]], axis=0)
        both = bmm(jnp.concatenate([pick(r_hat), pick(m_w)], axis=1), stack_heads(zs))
        y_c = both[:, :c_len] + pick(y_loc)
        yf_ref[0, c_f * c_len:(c_f + 1) * c_len, :] = jnp.concatenate([y_c[j] for j in range(n_pairs)], axis=1)
        yb_ref[0, c_b * c_len:(c_b + 1) * c_len, :] = jnp.concatenate(
            [y_c[n_pairs + j] for j in range(n_pairs)], axis=1)
        zs = both[:, c_len:] + pick(g_w)
    z_s[...] = zs


def _rwkv_scan(ops, em, v, *, batch, seq, ctx_len):
    width = RW_HEADS * RW_HEAD_DIM
    ta = RW_TILE
    nct = ctx_len // ta
    nxt = seq // ta
    n_x = batch * seq
    c_len = RW_CHUNK

    def blk(reverse):
        def index(b, n):
            t_ctx = (nct - 1 - n) if reverse else n
            t_x = (nxt - 1 - (n - nct)) if reverse else (n - nct)
            return jnp.where(n < nct, batch * nxt + b * nct + t_ctx, b * nxt + t_x)
        return index

    def xblk(reverse):
        def index(b, n):
            m = jnp.maximum(n, nct) - nct
            return b * nxt + ((nxt - 1 - m) if reverse else m)
        return index

    t_i = np.arange(c_len)[:, None]
    s_i = np.arange(LANES)[None, :] % RW_HEAD_DIM
    strict = jnp.asarray(np.stack([s_i < t_i, s_i > t_i]), F32)
    incl = jnp.asarray(np.stack([s_i <= t_i, s_i >= t_i]), F32)

    def operands(d):
        index = blk(d == 1)
        return [pl.BlockSpec((1, ta, ops.shape[2]), lambda b, n: (d, index(b, n), 0)),
                pl.BlockSpec((1, 1, ta // c_len, width), lambda b, n: (d, index(b, n), 0, 0)),
                pl.BlockSpec((ta, width), lambda b, n: (index(b, n), 0))]

    def out_spec(reverse):
        index = xblk(reverse)
        return pl.BlockSpec((1, ta, width), lambda b, n: (0, index(b, n), 0))

    y_shape = jax.ShapeDtypeStruct((1, n_x, width), F32)
    return pl.pallas_call(
        _rwkv_kernel,
        grid=(batch, nct + nxt),
        in_specs=operands(0) + operands(1) + [_resident(strict.shape), _resident(incl.shape)],
        out_specs=[out_spec(False), out_spec(True)],
        out_shape=[y_shape, y_shape],
        scratch_shapes=[pltpu.VMEM((2 * (width // LANES), RW_HEAD_DIM, LANES), F32)],
        compiler_params=_cparams(2),
    )(ops, em, v, ops, em, v, strict, incl)


def _merge_kernel(x_ref, mod_ref, ona_ref, yf_ref, yb_ref, bon_ref, g_ref, pg_ref, bg_ref, ln_ref,
                  pna_ref, prw_ref, wout_ref, gn_ref, wg_hbm, wu_hbm, wd_hbm, gf_ref, o_ref,
                  wg_s, wu_s, wd_s, stage_in, stage_out, sem, *, f_chunk):
    @pl.when(pl.program_id(0) == 0)
    def _():
        _ffn_load_weights(wg_hbm, wu_hbm, wd_hbm, wg_s, wu_s, wd_s, stage_in, stage_out, sem)

    d = x_ref.shape[1]
    y = yf_ref[0] + yb_ref[0]
    inv_n = 1.0 / RW_HEAD_DIM
    mean = _head_segsum(y) * inv_n
    yc = y - mean
    var = _head_segsum(yc * yc) * inv_n
    yn = yc * lax.rsqrt(var + RW_GN_EPS) * ln_ref[0:1, :] + ln_ref[1:2, :]
    o_rw = ((yn + bon_ref[...]) * g_ref[...]).astype(BF16)
    gates = jax.nn.sigmoid(pg_ref[...] + bg_ref[...])
    m = gates[:, :d] * _dot(ona_ref[...], pna_ref[...]) + gates[:, d:] * _dot(o_rw, prw_ref[...])
    x2 = x_ref[...] + mod_ref[0, 5:6, :] * _dot(m.astype(BF16), wout_ref[...])
    o_ref[...] = _swiglu_half_step(x2, mod_ref, gn_ref, wg_s, wu_s, wd_s, gf_ref,
                                   mod0=6, f_chunk=f_chunk, final=True)


def _merge_ffn(x1, mod, o_na, y_f, y_b, bon, g, p_gate, b_gate, ln, p_na, p_rw, w_out, gn, wg, wu, wd, gf, *,
               tiles_per_seq):
    n_x, d = o_na.shape[0], x1.shape[1]
    width = o_na.shape[1]
    tok = lambda i: (i, 0)
    tok3 = lambda i: (0, i, 0)
    return pl.pallas_call(
        functools.partial(_merge_kernel, f_chunk=FFN_CHUNK),
        grid=(n_x // TM,),
        in_specs=[pl.BlockSpec((TM, d), tok),
                  pl.BlockSpec((1, N_MOD, d), lambda i: (i // tiles_per_seq, 0, 0)),
                  pl.BlockSpec((TM, width), tok),
                  pl.BlockSpec((1, TM, width), tok3), pl.BlockSpec((1, TM, width), tok3),
                  pl.BlockSpec((TM, width), tok), pl.BlockSpec((TM, width), tok),
                  pl.BlockSpec((TM, 2 * d), tok),
                  _resident((1, 2 * d)), _resident((2, width)),
                  _resident(p_na.shape), _resident(p_rw.shape), _resident(w_out.shape),
                  _resident(gn.shape), _hbm_spec(), _hbm_spec(), _hbm_spec(), _resident(gf.shape)],
        out_specs=pl.BlockSpec((TM, d), tok),
        out_shape=jax.ShapeDtypeStruct((n_x, d), F32),
        scratch_shapes=_ffn_scratch(wg, wu, wd),
        compiler_params=_cparams(1),
    )(x1, mod, o_na, y_f, y_b, bon, g, p_gate, b_gate, ln, p_na, p_rw, w_out, gn, wg, wu, wd, gf)


def _pad_lora(w_up, direction):
    z = jnp.zeros_like(w_up[0])
    return jnp.concatenate([w_up[0] if direction == 0 else z, w_up[1] if direction == 1 else z], axis=0)


def kernel(x, c, ctx, c_ctx, w_ada, b_ada, norm_ffn1, norm_mix, norm_ffn2, norm_final, ffn1_wg, ffn1_wu, ffn1_wd, ffn2_wg, ffn2_wu, ffn2_wd, w_in, b_gate, na_rpb, rw_mu, rw_w0, rw_w_up, rw_a0, rw_a_up, rw_g_up, rw_k_k, rw_k_a, rw_r_k, rw_ln_w, rw_ln_b, p_na, p_rw, w_out):
    batch, seq, d = x.shape
    ctx_len = ctx.shape[1]
    n_x = batch * seq
    na_width = NA_HEADS * NA_HEAD_DIM
    rw_width = RW_HEADS * RW_HEAD_DIM
    c_rw = 3 * na_width
    c_gate = c_rw + 3 * rw_width + 4 * RW_LORA + RW_GATE_LORA
    assert w_ada.shape[0] == 1, "single layer"
    assert seq % TM == 0 and (batch * ctx_len) % TM == 0 and ctx_len % RW_TILE == 0 and batch + 1 <= 8
    assert seq % (2 * NA_ROWS * GRID_W) == 0 and ctx_len == 4 * GRID_W
    row = lambda t: t.reshape(1, -1)

    cs = jnp.concatenate([c, c_ctx[None], jnp.zeros((8 - batch - 1, d), F32)], axis=0)
    mod = _ada_mod(cs, w_ada[0], b_ada[0])[:batch + 1].reshape(batch + 1, N_MOD, d)

    bf = lambda t: t.astype(BF16)

    x1 = _ffn(x.reshape(n_x, d), ctx.reshape(batch * ctx_len, d), mod, row(norm_ffn1[0]),
              ffn1_wg[0], ffn1_wu[0], ffn1_wd[0], row(norm_final),
              mod0=0, tiles_per_seq=seq // TM, n_seq=batch)
    par = jnp.concatenate([rw_w0[0], rw_a0[0], row(rw_k_k[0]), row(rw_k_a[0]), row(rw_r_k[0]),
                           jnp.zeros((1, rw_width), F32)], axis=0)
    wup = bf(jnp.stack([_pad_lora(rw_w_up[0], 0), _pad_lora(rw_w_up[0], 1)]))
    aup = bf(jnp.stack([_pad_lora(rw_a_up[0], 0), _pad_lora(rw_a_up[0], 1)]))
    qkv, p_gate, scan_ops, em, v_rw, g, bon = _inproj(
        x1, mod, row(norm_mix[0]), w_in[0], rw_mu[0], par, wup, aup, bf(rw_g_up[0]),
        c_rw=c_rw, c_gate=c_gate, batch=batch, seq=seq, ctx_len=ctx_len)

    bias = _na_pair_bias(na_rpb[0])
    o_na = _na_attention(qkv, bias, batch=batch, seq=seq, ctx_len=ctx_len)

    y_f, y_b = _rwkv_scan(scan_ops, em, v_rw, batch=batch, seq=seq, ctx_len=ctx_len)

    ln = jnp.stack([rw_ln_w[0], rw_ln_b[0]], axis=0)
    out = _merge_ffn(x1, mod, o_na, y_f, y_b, bon, g, p_gate, row(b_gate[0]), ln,
                     bf(p_na[0]), bf(p_rw[0]), bf(w_out[0]), row(norm_ffn2[0]),
                     ffn2_wg[0], ffn2_wu[0], ffn2_wd[0], row(norm_final),
                     tiles_per_seq=seq // TM)
    return out.reshape(batch, seq, d)
```

```python
import functools
import math

import numpy as np
import jax
import jax.numpy as jnp
from jax import lax
from jax.experimental import pallas as pl
from jax.experimental.pallas import tpu as pltpu

F32 = jnp.float32
BF16 = jnp.bfloat16

NORM_EPS = 1e-6
RW_GN_EPS = 64e-5
N_MOD = 9
GRID_W = 64
NA_HEADS = 8
NA_HEAD_DIM = 64
NA_WIN_H = 8
NA_WIN_W = 16
RW_HEADS = 8
RW_HEAD_DIM = 64
RW_LORA = 64
RW_GATE_LORA = 128

LANES = 128
VMEM_LIMIT = 56 * 1024 * 1024

TM = 512
FFN_CHUNK = 256
NA_ROWS = 8
RW_TILE = 256
RW_CHUNK = 64
RW_INV_BASE = 8
assert RW_CHUNK == RW_HEAD_DIM


def _cparams(n_axes):
    return pltpu.CompilerParams(dimension_semantics=("arbitrary",) * n_axes,
                                vmem_limit_bytes=VMEM_LIMIT)


def _resident(shape):
    nd = len(shape)
    return pl.BlockSpec(shape, lambda *_: (0,) * nd, pipeline_mode=pl.Buffered(1))


def _rmsnorm(x, g):
    return x * lax.rsqrt(jnp.mean(x * x, axis=-1, keepdims=True) + NORM_EPS) * g


def _dot(a, b):
    return jnp.dot(a, b, preferred_element_type=F32)


WEIGHT_CHUNKS = 16


def _hbm_spec():
    return pl.BlockSpec(memory_space=pl.ANY)


WEIGHT_SLOTS = 4


def _stage_shape(w):
    return pltpu.VMEM((WEIGHT_SLOTS, w.shape[0] // WEIGHT_CHUNKS, w.shape[1]), F32)


def _stage_sems():
    return pltpu.SemaphoreType.DMA((WEIGHT_SLOTS,))


def _fetch_as_bf16(src_hbm, dst, stage, sem):
    rows = stage.shape[1]

    def chunk_copy(c):
        slot = c % WEIGHT_SLOTS
        return pltpu.make_async_copy(src_hbm.at[c * rows:(c + 1) * rows, :], stage.at[slot], sem.at[slot])

    for c in range(WEIGHT_SLOTS):
        chunk_copy(c).start(priority=c % 2)
    for c in range(WEIGHT_CHUNKS):
        chunk_copy(c).wait()
        dst[c * rows:(c + 1) * rows, :] = stage[c % WEIGHT_SLOTS].astype(dst.dtype)
        if c + WEIGHT_SLOTS < WEIGHT_CHUNKS:
            chunk_copy(c + WEIGHT_SLOTS).start(priority=c % 2)


def _ada_kernel(c_ref, w_ref, b_ref, o_ref):
    c = c_ref[...]
    s = c * jax.nn.sigmoid(c)
    o_ref[...] = _dot(s.astype(BF16), w_ref[...].astype(BF16)) + b_ref[...]


def _ada_mod(cs, w_ada, b_ada):
    d = cs.shape[1]
    nm = w_ada.shape[1] // d
    return pl.pallas_call(
        _ada_kernel,
        grid=(nm,),
        in_specs=[pl.BlockSpec((8, d), lambda j: (0, 0)),
                  pl.BlockSpec((d, d), lambda j: (0, j)),
                  pl.BlockSpec((1, d), lambda j: (0, j))],
        out_specs=pl.BlockSpec((8, d), lambda j: (0, j)),
        out_shape=jax.ShapeDtypeStruct((8, nm * d), F32),
        compiler_params=_cparams(1),
    )(cs, w_ada, b_ada.reshape(1, -1))


def _ffn_scratch(wg, wu, wd):
    return [pltpu.VMEM(wg.shape, BF16), pltpu.VMEM(wu.shape, BF16), pltpu.VMEM(wd.shape, BF16),
            _stage_shape(wg), _stage_shape(wd), _stage_sems()]


def _ffn_load_weights(wg_hbm, wu_hbm, wd_hbm, wg_s, wu_s, wd_s, stage_in, stage_out, sem):
    _fetch_as_bf16(wg_hbm, wg_s, stage_in, sem)
    _fetch_as_bf16(wu_hbm, wu_s, stage_in, sem)
    _fetch_as_bf16(wd_hbm, wd_s, stage_out, sem)


def _ffn_kernel(x_ref, tail_ref, mod_ref, g_ref, wg_hbm, wu_hbm, wd_hbm, gf_ref, o_ref,
                wg_s, wu_s, wd_s, stage_in, stage_out, sem, *, mod0, f_chunk, n_head_tiles):
    @pl.when(pl.program_id(0) == 0)
    def _():
        _ffn_load_weights(wg_hbm, wu_hbm, wd_hbm, wg_s, wu_s, wd_s, stage_in, stage_out, sem)

    x = jnp.where(pl.program_id(0) < n_head_tiles, x_ref[...], tail_ref[...])
    o_ref[...] = _swiglu_half_step(x, mod_ref, g_ref, wg_s, wu_s, wd_s, gf_ref,
                                   mod0=mod0, f_chunk=f_chunk, final=False)


def _swiglu_half_step(x, mod_ref, g_ref, wg_ref, wu_ref, wd_ref, gf_ref, *, mod0, f_chunk, final):
    shift = mod_ref[0, mod0:mod0 + 1, :]
    scale = mod_ref[0, mod0 + 1:mod0 + 2, :]
    gate = mod_ref[0, mod0 + 2:mod0 + 3, :]
    h = (_rmsnorm(x, g_ref[...]) * (1.0 + scale) + shift).astype(BF16)
    d_ff = wg_ref.shape[1]
    bounds = [(f0, min(f0 + f_chunk, d_ff)) for f0 in range(0, d_ff, f_chunk)]

    def gate_up(lo, hi):
        return _dot(h, wg_ref[:, lo:hi]), _dot(h, wu_ref[:, lo:hi])

    acc = None
    pending = gate_up(*bounds[0])
    for c, (lo, hi) in enumerate(bounds):
        gg, uu = pending
        if c + 1 < len(bounds):
            pending = gate_up(*bounds[c + 1])
        a = (gg * jax.nn.sigmoid(gg) * uu).astype(BF16)
        t = _dot(a, wd_ref[lo:hi, :])
        acc = t if acc is None else acc + t
    y = x + 0.5 * gate * acc
    if final:
        y = _rmsnorm(y, gf_ref[...])
    return y


def _ffn(xs, tail, mod, g, wg, wu, wd, gf, *, mod0, tiles_per_seq, n_seq):
    n_head, d = xs.shape
    n = n_head + tail.shape[0]
    seq_of = lambda i: jnp.minimum(i // tiles_per_seq, n_seq)
    head_tiles = n_head // TM
    return pl.pallas_call(
        functools.partial(_ffn_kernel, mod0=mod0, f_chunk=FFN_CHUNK, n_head_tiles=head_tiles),
        grid=(n // TM,),
        in_specs=[pl.BlockSpec((TM, d), lambda i: (jnp.minimum(i, head_tiles - 1), 0)),
                  pl.BlockSpec((TM, d), lambda i: (jnp.maximum(i - head_tiles, 0), 0)),
                  pl.BlockSpec((1, N_MOD, d), lambda i: (seq_of(i), 0, 0)),
                  _resident((1, d)), _hbm_spec(), _hbm_spec(), _hbm_spec(), _resident((1, d))],
        out_specs=pl.BlockSpec((TM, d), lambda i: (i, 0)),
        out_shape=jax.ShapeDtypeStruct((n, d), F32),
        scratch_shapes=_ffn_scratch(wg, wu, wd),
        compiler_params=_cparams(1),
    )(xs, tail, mod, g, wg, wu, wd, gf)


def _head_segsum(x):
    r = lax.broadcasted_iota(jnp.int32, (LANES, LANES), 0) // RW_HEAD_DIM
    c = lax.broadcasted_iota(jnp.int32, (LANES, LANES), 1) // RW_HEAD_DIM
    e = jnp.where(r == c, 1.0, 0.0).astype(BF16)
    xb = x.astype(BF16)
    return jnp.concatenate([_dot(xb[:, j:j + LANES], e) for j in range(0, x.shape[1], LANES)], axis=1)


def _inproj_kernel(x_ref, xprev_ref, xnext_ref, mod_ref, g_ref, w_hbm, mu_ref, par_ref, wup_ref, aup_ref,
                   gup_ref, tri_ref, qkv_ref, gate_ref, ops_ref, em_ref, v_ref, g_out_ref,
                   bon_ref, p_s, w_ref, stage, sem, *, c_rw, c_gate, n_x_tiles, tiles_x, tiles_ctx):
    i = pl.program_id(0)

    @pl.when(i == 0)
    def _():
        _fetch_as_bf16(w_hbm, w_ref, stage, sem)

    ta = x_ref.shape[0]
    width = v_ref.shape[1]
    c_len = RW_CHUNK
    n_chunks = ta // c_len
    in_x = i < n_x_tiles
    tile = jnp.where(in_x, i % tiles_x, (i - n_x_tiles) % tiles_ctx)
    last_tile = jnp.where(in_x, tiles_x - 1, tiles_ctx - 1)

    x_all = jnp.concatenate([xprev_ref[...], x_ref[...], xnext_ref[...]], axis=0)
    h_all = (_rmsnorm(x_all, g_ref[...]) * (1.0 + mod_ref[0, 4:5, :]) + mod_ref[0, 3:4, :]).astype(BF16)
    h = h_all[8:ta + 8]
    zero = jnp.zeros((8, h.shape[1]), BF16)
    h_all = jnp.concatenate([jnp.where(tile == 0, zero, h_all[:8]), h,
                             jnp.where(tile == last_tile, zero, h_all[ta + 8:])], axis=0)

    piece = 4 * LANES
    pieces = ([(qkv_ref, c0, c0) for c0 in range(0, c_rw, piece)]
              + [(gate_ref, c0, c_gate + c0) for c0 in range(0, w_ref.shape[1] - c_gate, piece)])

    def emit_dense(count):
        for _ in range(min(count, len(pieces))):
            ref, dst, src = pieces.pop(0)
            ref[:, dst:dst + piece] = _dot(h, w_ref[:, src:src + piece]).astype(ref.dtype)

    def shifted(lo, hi):
        p_s[:, lo:hi] = _dot(h_all, w_ref[:, c_rw + lo:c_rw + hi])
        p = p_s[8:ta + 8, lo:hi]
        return (p + mu_ref[0:1, lo:hi] * (p_s[7:ta + 7, lo:hi] - p)
                + mu_ref[1:2, lo:hi] * (p_s[9:ta + 9, lo:hi] - p))

    o = 3 * width
    lora = shifted(o, o + 3 * LANES)
    k = shifted(width, 2 * width)
    r = shifted(0, width)
    v = shifted(2 * width, 3 * width)
    emit_dense(2)

    th = jnp.tanh(lora[:, :LANES]).astype(BF16)
    al = lora[:, LANES:2 * LANES].astype(BF16)
    gl = lora[:, 2 * LANES:]
    k_k = par_ref[4:5, :]
    k_a = par_ref[5:6, :]
    r_k = par_ref[6:7, :]

    v_ref[...] = v.astype(BF16)
    g_out_ref[...] = _dot(jax.nn.sigmoid(gl).astype(BF16), gup_ref[...])
    kk = k * k_k
    kk = kk * lax.rsqrt(jnp.maximum(_head_segsum(kk * kk), 1e-12))
    emit_dense(1)

    def chunks(t):
        return t.reshape(n_chunks, c_len, width)

    kd_sum = None
    for d in range(2):
        z = par_ref[d:d + 1, :] + _dot(th, wup_ref[d])
        lw = -math.exp(-0.5) * jax.nn.sigmoid(z)
        a = jax.nn.sigmoid(par_ref[2 + d:3 + d, :] + _dot(al, aup_ref[d]))
        kd = k * (1.0 + (a - 1.0) * k_a)
        kd_sum = kd if kd_sum is None else kd_sum + kd
        bb = kk * a
        emit_dense(1)
        ci3 = chunks(_dot(tri_ref[d], lw.astype(BF16)))
        ce3 = ci3 - chunks(lw)
        t_last = 0 if d == 1 else c_len - 1
        cm = 0.5 * ci3[:, t_last:t_last + 1, :]
        e_k = jnp.exp(cm - ci3)
        emit_dense(1)
        scaled = (chunks(kk) * jnp.exp(ce3 - cm), chunks(r) * jnp.exp(ci3 - cm), chunks(bb) * e_k, chunks(kd) * e_k)
        for slot, t in enumerate(scaled):
            ops_ref[d, :, slot * width:(slot + 1) * width] = t.reshape(ta, width).astype(BF16)
        em_ref[d, 0] = jnp.exp(cm).reshape(n_chunks, width)
    emit_dense(len(pieces))
    bon_ref[...] = _head_segsum(r * kd_sum * r_k) * v


def _inproj(xs, mod, g, w_in, mu, par, wup, aup, gup, *, c_rw, c_gate, batch, seq, ctx_len):
    n, d = xs.shape
    cols = w_in.shape[1]
    width = RW_HEADS * RW_HEAD_DIM
    ta = RW_TILE
    c_len = RW_CHUNK
    n_tiles = n // ta
    tiles_x = seq // ta
    sub = ta // 8
    last8 = n // 8 - 1
    seq_of = lambda i: jnp.minimum(i // tiles_x, batch)
    idx = np.arange(ta)
    same_chunk = (idx[:, None] // c_len) == (idx[None, :] // c_len)
    tri = jnp.asarray(np.stack([same_chunk & (idx[None, :] <= idx[:, None]),
                                same_chunk & (idx[None, :] >= idx[:, None])]), BF16)
    tok2 = pl.BlockSpec((2, ta, 4 * width), lambda i: (0, i, 0))
    tok = lambda c: pl.BlockSpec((ta, c), lambda i: (i, 0))
    sds = jax.ShapeDtypeStruct
    return pl.pallas_call(
        functools.partial(_inproj_kernel, c_rw=c_rw, c_gate=c_gate, n_x_tiles=batch * tiles_x, tiles_x=tiles_x,
                          tiles_ctx=ctx_len // ta),
        grid=(n_tiles,),
        in_specs=[tok(d),
                  pl.BlockSpec((8, d), lambda i: (jnp.maximum(i * sub - 1, 0), 0)),
                  pl.BlockSpec((8, d), lambda i: (jnp.minimum((i + 1) * sub, last8), 0)),
                  pl.BlockSpec((1, N_MOD, d), lambda i: (seq_of(i), 0, 0)),
                  _resident((1, d)), _hbm_spec(),
                  _resident(mu.shape), _resident(par.shape), _resident(wup.shape), _resident(aup.shape),
                  _resident(gup.shape), _resident(tri.shape)],
        out_specs=[tok(c_rw), tok(cols - c_gate), tok2,
                   pl.BlockSpec((2, 1, ta // c_len, width), lambda i: (0, i, 0, 0)),
                   tok(width), tok(width), tok(width)],
        out_shape=[sds((n, c_rw), BF16), sds((n, cols - c_gate), BF16), sds((2, n, 4 * width), BF16),
                   sds((2, n_tiles, ta // c_len, width), F32), sds((n, width), BF16),
                   sds((n, width), F32), sds((n, width), F32)],
        scratch_shapes=[pltpu.VMEM((ta + 16, c_gate - c_rw), F32), pltpu.VMEM(w_in.shape, BF16),
                        _stage_shape(w_in), _stage_sems()],
        compiler_params=_cparams(1),
    )(xs, xs, xs, mod, g, w_in, mu, par, wup, aup, gup, tri)


LOG2E = math.log2(math.e)


def _na_pair_bias(rpb):
    w = GRID_W
    qc = np.arange(w)[:, None]
    kc = np.arange(w)[None, :]
    s_c = np.clip(qc - NA_WIN_W // 2, 0, w - NA_WIN_W)
    col_ok = (kc >= s_c) & (kc < s_c + NA_WIN_W)
    dc = np.clip(kc - qc + NA_WIN_W - 1, 0, 2 * NA_WIN_W - 2)
    onehot = jnp.asarray(dc[None] == np.arange(2 * NA_WIN_W - 1)[:, None, None], F32)
    b = jnp.einsum('hrd,dqk->hrqk', rpb.astype(F32), onehot, precision=lax.Precision.HIGHEST)
    b = jnp.where(col_ok[None, None], b * LOG2E, -jnp.inf)
    ninf = jnp.full((rpb.shape[0], 1, w, w), -jnp.inf, F32)
    b = jnp.concatenate([ninf, b, ninf], axis=1)
    n_e = 2 * NA_WIN_H
    return pl.pallas_call(
        _na_pair_kernel,
        grid=(rpb.shape[0],),
        in_specs=[pl.BlockSpec((1, n_e + 1, w, w), lambda h: (h, 0, 0, 0))],
        out_specs=pl.BlockSpec((2, 1, n_e, w, 2 * w), lambda h: (0, h, 0, 0, 0)),
        out_shape=jax.ShapeDtypeStruct((2, rpb.shape[0], n_e, w, 2 * w), F32),
        compiler_params=_cparams(1),
    )(b)


def _na_pair_kernel(b_ref, o_ref):
    b = b_ref[0]
    pair = jnp.concatenate([b[:-1], b[1:]], axis=-1)
    o_ref[0, 0] = pair
    dr = (lax.broadcasted_iota(jnp.int32, pair.shape, 0) - 1
          + (lax.broadcasted_iota(jnp.int32, pair.shape, 2) >= GRID_W).astype(jnp.int32))
    first_dr = NA_WIN_H // 2 - 1
    inside = jnp.logical_and(dr >= first_dr, dr < first_dr + NA_WIN_H)
    o_ref[1, 0] = jnp.where(inside, pair, -jnp.inf)


def _na_kernel(q_ref, k0_ref, k1_ref, k2_ref, k3_ref, k4_ref, k5_ref, kc_ref,
               v0_ref, v1_ref, v2_ref, v3_ref, v4_ref, v5_ref, vc_ref,
               bias_ref, o_ref, kcat, vcat, *, n_steps):
    j = pl.program_id(2)
    key_refs = (k0_ref, k1_ref, k2_ref, k3_ref, k4_ref, k5_ref, kc_ref)
    val_refs = (v0_ref, v1_ref, v2_ref, v3_ref, v4_ref, v5_ref, vc_ref)
    blk = k0_ref.shape[0]
    for s, (kr, vr) in enumerate(zip(key_refs, val_refs)):
        kcat[s * blk:s * blk + kr.shape[0], :] = kr[...]
        vcat[s * blk:s * blk + vr.shape[0], :] = vr[...]
    n_loc = (len(key_refs) - 1) * blk
    w = GRID_W
    half = NA_WIN_H // 2
    sub_rows = NA_ROWS // 2
    sub_pairs = (sub_rows + NA_WIN_H) // 2
    tq = sub_rows * w
    lane = lax.broadcasted_iota(jnp.int32, (1, LANES), 1)
    first = lane < NA_HEAD_DIM
    nt = (((1,), (1,)), ((), ()))

    items = [(pp, h, sb) for pp in range(q_ref.shape[1] // LANES) for sb in range(2) for h in range(2)]

    def scores(hb, item):
        pp, h, sb = item
        cols = slice(pp * LANES, (pp + 1) * LANES)
        q0 = (2 * hb + sb) * tq
        q = (q_ref[q0:q0 + tq, cols].astype(F32) * (NA_HEAD_DIM ** -0.5 * LOG2E)).astype(BF16)
        qh = jnp.where(first if h == 0 else jnp.logical_not(first), q, jnp.zeros_like(q))
        k0 = (4 * hb + 2 * sb) * LANES
        s_loc = lax.dot_general(qh, kcat[k0:k0 + sub_pairs * LANES, cols], nt, preferred_element_type=F32)
        s_ctx = lax.dot_general(qh, kcat[n_loc:, cols], nt, preferred_element_type=F32)
        return s_loc, s_ctx

    def attend(hb, item, s_loc, s_ctx, interior):
        pp, h, sb = item
        cols = slice(pp * LANES, (pp + 1) * LANES)
        p_rows = []
        l_rows = []
        for ql in range(sub_rows):
            qi = sb * sub_rows + ql
            if interior:
                m_lo, m_hi = qi // 2, (qi + NA_WIN_H - 1) // 2
            else:
                m_lo, m_hi = min(qi, half) // 2, (max(qi, half) + NA_WIN_H - 1) // 2
            bias = jnp.concatenate(
                [bias_ref[1 if interior else 0, 2 * pp + h, min(max(2 * m - qi + half, 0), 2 * NA_WIN_H - 1)]
                 for m in range(m_lo, m_hi + 1)], axis=1)
            rows = slice(ql * w, (ql + 1) * w)
            t_lo, t_hi = m_lo - 2 * sb, m_hi - 2 * sb
            s_q = s_loc[rows, t_lo * LANES:(t_hi + 1) * LANES] + bias
            if not interior:
                lo = max(qi, half) if hb == 0 else min(qi, half)
                key_row = 2 * m_lo + lax.broadcasted_iota(jnp.int32, (1, (m_hi - m_lo + 1) * LANES), 1) // w
                s_q = jnp.where((key_row >= lo) & (key_row < lo + NA_WIN_H), s_q, -jnp.inf)
            s_c = s_ctx[rows]
            mx = jnp.maximum(jnp.max(s_q, axis=-1, keepdims=True), jnp.max(s_c, axis=-1, keepdims=True))
            p_q = jnp.exp2(s_q - mx)
            p_c = jnp.exp2(s_c - mx)
            l_rows.append(jnp.sum(p_q, axis=-1, keepdims=True) + jnp.sum(p_c, axis=-1, keepdims=True))
            pieces = []
            if t_lo > 0:
                pieces.append(jnp.zeros((w, t_lo * LANES), BF16))
            pieces.append(p_q.astype(BF16))
            if t_hi < sub_pairs - 1:
                pieces.append(jnp.zeros((w, (sub_pairs - 1 - t_hi) * LANES), BF16))
            pieces.append(p_c.astype(BF16))
            p_rows.append(jnp.concatenate(pieces, axis=1))
        p = jnp.concatenate(p_rows, axis=0)
        l = jnp.concatenate(l_rows, axis=0)
        k0 = (4 * hb + 2 * sb) * LANES
        o = (_dot(p[:, :sub_pairs * LANES], vcat[k0:k0 + sub_pairs * LANES, cols])
             + _dot(p[:, sub_pairs * LANES:], vcat[n_loc:, cols]))
        return o / l

    def run(hb, interior):
        pending = scores(hb, items[0])
        outs = {}
        for n, item in enumerate(items):
            current = pending
            if n + 1 < len(items):
                pending = scores(hb, items[n + 1])
            outs[item] = attend(hb, item, *current, interior)
            pp, h, sb = item
            if h == 1:
                q0 = (2 * hb + sb) * tq
                o_ref[q0:q0 + tq, pp * LANES:(pp + 1) * LANES] = jnp.where(
                    first, outs[(pp, 0, sb)], outs[(pp, 1, sb)]).astype(o_ref.dtype)

    pl.when(j == 0)(lambda: run(0, False))
    pl.when(j > 0)(lambda: run(0, True))
    pl.when(j == n_steps - 1)(lambda: run(1, False))
    pl.when(j < n_steps - 1)(lambda: run(1, True))


def _na_attention(qkv, bias, *, batch, seq, ctx_len):
    rows = seq // GRID_W
    nj = rows // (2 * NA_ROWS)
    tq = 2 * NA_ROWS * GRID_W
    slot_rows = NA_WIN_H // 2
    tk = slot_rows * GRID_W
    n_slots = (2 * NA_ROWS + NA_WIN_H) // slot_rows
    width = NA_HEADS * NA_HEAD_DIM
    kblocks = seq // tk
    ctx_blk0 = (batch * seq) // ctx_len

    lanes = 2 * LANES
    n_groups = width // lanes

    def kv_spec(slot, col0):
        def imap(hg, b, j):
            first_slot = (2 * NA_ROWS // slot_rows) * j - 1
            return (b * kblocks + jnp.clip(first_slot + slot, 0, kblocks - 1), col0 + hg)
        return pl.BlockSpec((tk, lanes), imap)

    def ctx_spec(col0):
        return pl.BlockSpec((ctx_len, lanes), lambda hg, b, j: (ctx_blk0 + b, col0 + hg))

    in_specs = ([pl.BlockSpec((tq, lanes), lambda hg, b, j: (b * nj + j, hg))]
                + [kv_spec(s, n_groups) for s in range(n_slots)] + [ctx_spec(n_groups)]
                + [kv_spec(s, 2 * n_groups) for s in range(n_slots)] + [ctx_spec(2 * n_groups)]
                + [pl.BlockSpec((2, 4) + bias.shape[2:], lambda hg, b, j: (0, hg, 0, 0, 0))])
    return pl.pallas_call(
        functools.partial(_na_kernel, n_steps=nj),
        grid=(n_groups, batch, nj),
        in_specs=in_specs,
        out_specs=pl.BlockSpec((tq, lanes), lambda hg, b, j: (b * nj + j, hg)),
        out_shape=jax.ShapeDtypeStruct((batch * seq, width), BF16),
        scratch_shapes=[pltpu.VMEM((n_slots * tk + ctx_len, lanes), BF16),
                        pltpu.VMEM((n_slots * tk + ctx_len, lanes), BF16)],
        compiler_params=_cparams(3),
    )(*([qkv] * (3 + 2 * n_slots)), bias)


def _rwkv_kernel(opsf_ref, emf_ref, vf_ref, opsb_ref, emb_ref, vb_ref, strict_ref, incl_ref, yf_ref, yb_ref, z_s):
    n = pl.program_id(1)
    ta = vf_ref.shape[0]
    width = vf_ref.shape[1]
    n_pairs = width // LANES
    c_len = RW_CHUNK
    n_chunks = ta // c_len
    n_d = n_chunks * n_pairs
    n_b = 2 * n_d

    @pl.when(n == 0)
    def _():
        z_s[...] = jnp.zeros_like(z_s)

    def chunks(t):
        return t.reshape(n_chunks, c_len, width)

    def to_batch(t):
        parts = [t[:, :, j * LANES:(j + 1) * LANES] for j in range(n_pairs)]
        return jnp.stack(parts, axis=1).reshape(n_d, t.shape[1], LANES)

    def load_direction(ops_ref, em_ref, v_ref):
        em = em_ref[0, 0].reshape(n_chunks, 1, width)
        kk_t, r_t, b_t, k_t = [chunks(ops_ref[0, :, s * width:(s + 1) * width].astype(F32)) for s in range(4)]
        return (to_batch(kk_t * em),
                to_batch(r_t * em),
                to_batch(k_t * em),
                to_batch(b_t * em),
                to_batch(em * em),
                to_batch(kk_t), to_batch(r_t), to_batch(b_t), to_batch(k_t),
                to_batch(chunks(v_ref[...].astype(F32))))

    fwd = load_direction(opsf_ref, emf_ref, vf_ref)
    bwd = load_direction(opsb_ref, emb_ref, vb_ref)
    kk_abs, r_abs, k_bar, b_bar, gam, kk_t, r_t, b_t, k_t, v_b = [
        jnp.concatenate([f, b], axis=0) for f, b in zip(fwd, bwd)]

    lane = lax.broadcasted_iota(jnp.int32, (1, 1, LANES), 2)
    first = lane < RW_HEAD_DIM
    ri = lax.broadcasted_iota(jnp.int32, (c_len, LANES), 0)
    cj = lax.broadcasted_iota(jnp.int32, (c_len, LANES), 1) % RW_HEAD_DIM
    eye = ri == cj

    def causal(mask_ref, t):
        return jnp.concatenate([jnp.where(mask_ref[0] > 0.0, t[:n_d], 0.0),
                                jnp.where(mask_ref[1] > 0.0, t[n_d:], 0.0)], axis=0)

    def stack_heads(t):
        zero = jnp.zeros_like(t)
        return jnp.concatenate([jnp.where(first, t, zero), jnp.where(first, zero, t)], axis=1)

    def bmm(x, y):
        return lax.dot_general(x.astype(BF16), y.astype(BF16), (((2,), (1,)), ((0,), (0,))),
                               preferred_element_type=F32)

    def bmm_nt(x, y):
        return lax.dot_general(x.astype(BF16), y.astype(BF16), (((2,), (2,)), ((0,), (0,))),
                               preferred_element_type=F32)

    def bmm_tn(x, y):
        return lax.dot_general(x.astype(BF16), y.astype(BF16), (((1,), (1,)), ((0,), (0,))),
                               preferred_element_type=F32)

    gm = bmm_nt(jnp.concatenate([kk_t, r_t], axis=1),
                jnp.concatenate([stack_heads(b_t), stack_heads(k_t)], axis=1))
    ab_w = causal(strict_ref, gm[:, :c_len, :LANES])
    ak_w = causal(strict_ref, gm[:, :c_len, LANES:])
    db_w = causal(incl_ref, gm[:, c_len:, :LANES])
    dk_w = causal(incl_ref, gm[:, c_len:, LANES:])

    base = RW_INV_BASE
    same = lambda s: (ri // s) == (cj // s)
    l_base = jnp.where(same(base), ab_w, 0.0)
    t_w = jnp.where(eye, 1.0, 0.0) - l_base
    pw = bmm(l_base, stack_heads(l_base))
    span = 2
    while span < base:
        t_w = t_w + bmm(t_w, stack_heads(pw))
        span *= 2
        if span < base:
            pw = bmm(pw, stack_heads(pw))
    size = base
    while size < c_len:
        off = jnp.where(jnp.logical_and(same(2 * size), jnp.logical_not(same(size))), ab_w, 0.0)
        t_w = t_w - bmm(bmm(t_w, stack_heads(off)), stack_heads(t_w))
        size *= 2

    def head_diag(t):
        return jnp.where(first, t[:, :RW_HEAD_DIM], t[:, RW_HEAD_DIM:])

    akv = bmm(jnp.concatenate([ak_w, dk_w], axis=1), stack_heads(v_b))
    pq = bmm(t_w, jnp.concatenate([stack_heads(kk_abs), stack_heads(akv[:, :c_len])], axis=2))
    dpq = bmm(db_w, jnp.concatenate([stack_heads(pq[:, :, :LANES]), stack_heads(pq[:, :, LANES:])], axis=2))
    r_hat = r_abs - dpq[:, :, :LANES]
    y_loc = akv[:, c_len:] - dpq[:, :, LANES:]
    bpq = bmm_tn(b_bar, pq)
    m_w = jnp.where(eye, jnp.broadcast_to(gam, (n_b, RW_HEAD_DIM, LANES)), 0.0) - head_diag(bpq[:, :, :LANES])
    g_w = head_diag(bmm_tn(k_bar, v_b) - bpq[:, :, LANES:])

    zs = z_s[...]
    for step in range(n_chunks):
        c_f, c_b = step, n_chunks - 1 - step
        pick = lambda t: jnp.concatenate([t[c_f * n_pairs:(c_f + 1) * n_pairs],
                                          t[n_d + c_b * n_pairs:n_d + (c_b + 1) * n_pairs]], axis=0)
        both = bmm(jnp.concatenate([pick(r_hat), pick(m_w)], axis=1), stack_heads(zs))
        y_c = both[:, :c_len] + pick(y_loc)
        yf_ref[0, c_f * c_len:(c_f + 1) * c_len, :] = jnp.concatenate([y_c[j] for j in range(n_pairs)], axis=1)
        yb_ref[0, c_b * c_len:(c_b + 1) * c_len, :] = jnp.concatenate(
            [y_c[n_pairs + j] for j in range(n_pairs)], axis=1)
        zs = both[:, c_len:] + pick(g_w)
    z_s[...] = zs


def _rwkv_scan(ops, em, v, *, batch, seq, ctx_len):
    width = RW_HEADS * RW_HEAD_DIM
    ta = RW_TILE
    nct = ctx_len // ta
    nxt = seq // ta
    n_x = batch * seq
    c_len = RW_CHUNK

    def blk(reverse):
        def index(b, n):
            t_ctx = (nct - 1 - n) if reverse else n
            t_x = (nxt - 1 - (n - nct)) if reverse else (n - nct)
            return jnp.where(n < nct, batch * nxt + b * nct + t_ctx, b * nxt + t_x)
        return index

    def xblk(reverse):
        def index(b, n):
            m = jnp.maximum(n, nct) - nct
            return b * nxt + ((nxt - 1 - m) if reverse else m)
        return index

    t_i = np.arange(c_len)[:, None]
    s_i = np.arange(LANES)[None, :] % RW_HEAD_DIM
    strict = jnp.asarray(np.stack([s_i < t_i, s_i > t_i]), F32)
    incl = jnp.asarray(np.stack([s_i <= t_i, s_i >= t_i]), F32)

    def operands(d):
        index = blk(d == 1)
        return [pl.BlockSpec((1, ta, ops.shape[2]), lambda b, n: (d, index(b, n), 0)),
                pl.BlockSpec((1, 1, ta // c_len, width), lambda b, n: (d, index(b, n), 0, 0)),
                pl.BlockSpec((ta, width), lambda b, n: (index(b, n), 0))]

    def out_spec(reverse):
        index = xblk(reverse)
        return pl.BlockSpec((1, ta, width), lambda b, n: (0, index(b, n), 0))

    y_shape = jax.ShapeDtypeStruct((1, n_x, width), F32)
    return pl.pallas_call(
        _rwkv_kernel,
        grid=(batch, nct + nxt),
        in_specs=operands(0) + operands(1) + [_resident(strict.shape), _resident(incl.shape)],
        out_specs=[out_spec(False), out_spec(True)],
        out_shape=[y_shape, y_shape],
        scratch_shapes=[pltpu.VMEM((2 * (width // LANES), RW_HEAD_DIM, LANES), F32)],
        compiler_params=_cparams(2),
    )(ops, em, v, ops, em, v, strict, incl)


def _merge_kernel(x_ref, mod_ref, ona_ref, yf_ref, yb_ref, bon_ref, g_ref, pg_ref, bg_ref, ln_ref,
                  pna_ref, prw_ref, wout_ref, gn_ref, wg_hbm, wu_hbm, wd_hbm, gf_ref, o_ref,
                  wg_s, wu_s, wd_s, stage_in, stage_out, sem, *, f_chunk):
    @pl.when(pl.program_id(0) == 0)
    def _():
        _ffn_load_weights(wg_hbm, wu_hbm, wd_hbm, wg_s, wu_s, wd_s, stage_in, stage_out, sem)

    d = x_ref.shape[1]
    y = yf_ref[0] + yb_ref[0]
    inv_n = 1.0 / RW_HEAD_DIM
    mean = _head_segsum(y) * inv_n
    yc = y - mean
    var = _head_segsum(yc * yc) * inv_n
    yn = yc * lax.rsqrt(var + RW_GN_EPS) * ln_ref[0:1, :] + ln_ref[1:2, :]
    o_rw = ((yn + bon_ref[...]) * g_ref[...]).astype(BF16)
    gates = jax.nn.sigmoid(pg_ref[...] + bg_ref[...])
    m = gates[:, :d] * _dot(ona_ref[...], pna_ref[...]) + gates[:, d:] * _dot(o_rw, prw_ref[...])
    x2 = x_ref[...] + mod_ref[0, 5:6, :] * _dot(m.astype(BF16), wout_ref[...])
    o_ref[...] = _swiglu_half_step(x2, mod_ref, gn_ref, wg_s, wu_s, wd_s, gf_ref,
                                   mod0=6, f_chunk=f_chunk, final=True)


def _merge_ffn(x1, mod, o_na, y_f, y_b, bon, g, p_gate, b_gate, ln, p_na, p_rw, w_out, gn, wg, wu, wd, gf, *,
               tiles_per_seq):
    n_x, d = o_na.shape[0], x1.shape[1]
    width = o_na.shape[1]
    tok = lambda i: (i, 0)
    tok3 = lambda i: (0, i, 0)
    return pl.pallas_call(
        functools.partial(_merge_kernel, f_chunk=FFN_CHUNK),
        grid=(n_x // TM,),
        in_specs=[pl.BlockSpec((TM, d), tok),
                  pl.BlockSpec((1, N_MOD, d), lambda i: (i // tiles_per_seq, 0, 0)),
                  pl.BlockSpec((TM, width), tok),
                  pl.BlockSpec((1, TM, width), tok3), pl.BlockSpec((1, TM, width), tok3),
                  pl.BlockSpec((TM, width), tok), pl.BlockSpec((TM, width), tok),
                  pl.BlockSpec((TM, 2 * d), tok),
                  _resident((1, 2 * d)), _resident((2, width)),
                  _resident(p_na.shape), _resident(p_rw.shape), _resident(w_out.shape),
                  _resident(gn.shape), _hbm_spec(), _hbm_spec(), _hbm_spec(), _resident(gf.shape)],
        out_specs=pl.BlockSpec((TM, d), tok),
        out_shape=jax.ShapeDtypeStruct((n_x, d), F32),
        scratch_shapes=_ffn_scratch(wg, wu, wd),
        compiler_params=_cparams(1),
    )(x1, mod, o_na, y_f, y_b, bon, g, p_gate, b_gate, ln, p_na, p_rw, w_out, gn, wg, wu, wd, gf)


def _pad_lora(w_up, direction):
    z = jnp.zeros_like(w_up[0])
    return jnp.concatenate([w_up[0] if direction == 0 else z, w_up[1] if direction == 1 else z], axis=0)


def kernel(x, c, ctx, c_ctx, w_ada, b_ada, norm_ffn1, norm_mix, norm_ffn2, norm_final, ffn1_wg, ffn1_wu, ffn1_wd, ffn2_wg, ffn2_wu, ffn2_wd, w_in, b_gate, na_rpb, rw_mu, rw_w0, rw_w_up, rw_a0, rw_a_up, rw_g_up, rw_k_k, rw_k_a, rw_r_k, rw_ln_w, rw_ln_b, p_na, p_rw, w_out):
    batch, seq, d = x.shape
    ctx_len = ctx.shape[1]
    n_x = batch * seq
    na_width = NA_HEADS * NA_HEAD_DIM
    rw_width = RW_HEADS * RW_HEAD_DIM
    c_rw = 3 * na_width
    c_gate = c_rw + 3 * rw_width + 4 * RW_LORA + RW_GATE_LORA
    assert w_ada.shape[0] == 1, "single layer"
    assert seq % TM == 0 and (batch * ctx_len) % TM == 0 and ctx_len % RW_TILE == 0 and batch + 1 <= 8
    assert seq % (2 * NA_ROWS * GRID_W) == 0 and ctx_len == 4 * GRID_W
    row = lambda t: t.reshape(1, -1)

    cs = jnp.concatenate([c, c_ctx[None], jnp.zeros((8 - batch - 1, d), F32)], axis=0)
    mod = _ada_mod(cs, w_ada[0], b_ada[0])[:batch + 1].reshape(batch + 1, N_MOD, d)

    bf = lambda t: t.astype(BF16)

    x1 = _ffn(x.reshape(n_x, d), ctx.reshape(batch * ctx_len, d), mod, row(norm_ffn1[0]),
              ffn1_wg[0], ffn1_wu[0], ffn1_wd[0], row(norm_final),
              mod0=0, tiles_per_seq=seq // TM, n_seq=batch)
    par = jnp.concatenate([rw_w0[0], rw_a0[0], row(rw_k_k[0]), row(rw_k_a[0]), row(rw_r_k[0]),
                           jnp.zeros((1, rw_width), F32)], axis=0)
    wup = bf(jnp.stack([_pad_lora(rw_w_up[0], 0), _pad_lora(rw_w_up[0], 1)]))
    aup = bf(jnp.stack([_pad_lora(rw_a_up[0], 0), _pad_lora(rw_a_up[0], 1)]))
    qkv, p_gate, scan_ops, em, v_rw, g, bon = _inproj(
        x1, mod, row(norm_mix[0]), w_in[0], rw_mu[0], par, wup, aup, bf(rw_g_up[0]),
        c_rw=c_rw, c_gate=c_gate, batch=batch, seq=seq, ctx_len=ctx_len)

    bias = _na_pair_bias(na_rpb[0])
    o_na = _na_attention(qkv, bias, batch=batch, seq=seq, ctx_len=ctx_len)

    y_f, y_b = _rwkv_scan(scan_ops, em, v_rw, batch=batch, seq=seq, ctx_len=ctx_len)

    ln = jnp.stack([rw_ln_w[0], rw_ln_b[0]], axis=0)
    out = _merge_ffn(x1, mod, o_na, y_f, y_b, bon, g, p_gate, row(b_gate[0]), ln,
                     bf(p_na[0]), bf(p_rw[0]), bf(w_out[0]), row(norm_ffn2[0]),
                     ffn2_wg[0], ffn2_wu[0], ffn2_wd[0], row(norm_final),
                     tiles_per_seq=seq // TM)
    return out.reshape(batch, seq, d)
```
